```python
import math
import jax
import jax.numpy as jnp
from jax import lax
import numpy as np

D_MODEL = 1024
BATCH = 8
SEQ = 4096
DEPTH = 4

CTX_LEN = 256
GRID_W = 64
Q_BLOCK = 128
ROPE_THETA = 10000.0
NORM_EPS = 1e-6

HY_WIDTH = D_MODEL // 4
HY_SHORT = 3
HY_EMB = 33
HY_FILTER_HIDDEN = 64
HY_FAST_DECAY = 0.3
HY_SLOW_DECAY = 1.5
HY_DECAY_TARGET = 1e-2

DIFF_HEADS = 4
DIFF_WIDTH = D_MODEL // 4
DIFF_V_DIM = DIFF_WIDTH // DIFF_HEADS
DIFF_QK_DIM = DIFF_V_DIM // 2

GQA_HEAD_DIM = 64
GQA_WIDTH = D_MODEL // 2
GQA_HEADS = GQA_WIDTH // GQA_HEAD_DIM
GQA_KV_HEADS = GQA_HEADS // 4
GQA_REP = GQA_HEADS // GQA_KV_HEADS

MIX_WIDTH = HY_WIDTH + DIFF_WIDTH + GQA_WIDTH

OFF_DQ = 3 * HY_WIDTH
OFF_DK = OFF_DQ + 2 * DIFF_HEADS * DIFF_QK_DIM
OFF_DV = OFF_DK + 2 * DIFF_HEADS * DIFF_QK_DIM
OFF_GQ = OFF_DV + DIFF_WIDTH
OFF_GK = OFF_GQ + GQA_WIDTH
OFF_GV = OFF_GK + GQA_KV_HEADS * GQA_HEAD_DIM
IN_COLS = OFF_GV + GQA_KV_HEADS * GQA_HEAD_DIM
IN_SPLITS = [OFF_DQ, OFF_DK, OFF_DV, OFF_GQ, OFF_GK, OFF_GV]

D_FF = 256 * math.ceil(8 * D_MODEL / 3 / 256)
N_EXPERTS = 8
TOP_K = 2
D_FF_EXPERT = 7 * D_MODEL // 2

kernel_name = 'hybrid_diffusion_trunk'


def rms_norm(x, g):
    xf = x.astype(jnp.float32)
    y = xf * lax.rsqrt(jnp.mean(xf * xf, axis=-1, keepdims=True) + NORM_EPS)
    return y.astype(x.dtype) * g


def axial_rope_tables(length, head_dim, dtype):
    rows = length // GRID_W
    t = jnp.arange(rows * GRID_W)
    row = (t // GRID_W).astype(jnp.float32)
    col = (t % GRID_W).astype(jnp.float32)
    n = head_dim // 4
    inv = ROPE_THETA ** (-jnp.arange(n, dtype=jnp.float32) / n)
    ang = jnp.concatenate([row[:, None] * inv, col[:, None] * inv], axis=-1)
    return jnp.cos(ang).astype(dtype), jnp.sin(ang).astype(dtype)


def apply_rope(x, cos, sin):
    shape = (1, cos.shape[0]) + (1,) * (x.ndim - 3) + (cos.shape[1],)
    cos = cos.reshape(shape)
    sin = sin.reshape(shape)
    x1, x2 = jnp.split(x, 2, axis=-1)
    return jnp.concatenate([x1 * cos - x2 * sin, x1 * sin + x2 * cos], axis=-1)


def sweep_query_blocks(fn, q):
    b, l = q.shape[:2]
    nb = l // Q_BLOCK
    qb = jnp.moveaxis(q.reshape((b, nb, Q_BLOCK) + q.shape[2:]), 1, 0)
    ob = lax.map(fn, qb)
    return jnp.moveaxis(ob, 0, 1).reshape((b, l) + ob.shape[3:])


def diff_attention(q, k, v, lam):
    scale = DIFF_QK_DIM ** -0.5

    def block(qb):
        s = jnp.einsum('bqhmd,bkhmd->bhmqk', qb, k) * scale
        p = jax.nn.softmax(s.astype(jnp.float32), axis=-1)
        a = (p[:, :, 0] - lam * p[:, :, 1]).astype(v.dtype)
        return jnp.einsum('bhqk,bkhd->bqhd', a, v)

    return sweep_query_blocks(block, q)


def gqa_attention(q, k, v):
    scale = GQA_HEAD_DIM ** -0.5

    def block(qb):
        s = jnp.einsum('bqgrd,bkgd->bgrqk', qb, k) * scale
        p = jax.nn.softmax(s.astype(jnp.float32), axis=-1).astype(v.dtype)
        return jnp.einsum('bgrqk,bkgd->bqgrd', p, v)

    return sweep_query_blocks(block, q)


def short_conv(z, w, b):
    l = z.shape[1]
    pad = HY_SHORT // 2
    zp = jnp.pad(z, ((0, 0), (pad, HY_SHORT - 1 - pad), (0, 0)))
    y = b
    for j in range(HY_SHORT):
        y = y + zp[:, j:j + l] * w[j]
    return y


def hyena_filter(length, w1, b1, w2, b2, w3, freq):
    t = jnp.linspace(0.0, 1.0, length, dtype=jnp.float32)[:, None]
    bands = (HY_EMB - 1) // 2
    ang = (2.0 * math.pi / length) * jnp.arange(length, dtype=jnp.float32)[:, None] \
        * jnp.linspace(1e-4, bands - 1, bands, dtype=jnp.float32)
    feats = jnp.concatenate([t, jnp.cos(ang), -jnp.sin(ang)], axis=-1)
    a = jnp.sin(freq * (feats @ w1 + b1))
    a = jnp.sin(freq * (a @ w2 + b2))
    hfilt = (a @ w3).astype(jnp.float32)
    max_decay = math.log(HY_DECAY_TARGET) / HY_FAST_DECAY
    min_decay = math.log(HY_DECAY_TARGET) / HY_SLOW_DECAY
    deltas = jnp.abs(jnp.linspace(min_decay, max_decay, HY_WIDTH, dtype=jnp.float32))
    decay = jnp.exp(-t * deltas)
    h_fwd = hfilt[:, :HY_WIDTH] * decay
    h_bwd = hfilt[:, HY_WIDTH:] * decay
    zero = jnp.zeros((1, HY_WIDTH), jnp.float32)
    return jnp.concatenate([h_fwd, zero, h_bwd[1:][::-1]], axis=0)


def bidir_long_conv(v, h_circ):
    l = v.shape[1]
    vf = jnp.fft.rfft(v.astype(jnp.float32), n=2 * l, axis=1)
    hf = jnp.fft.rfft(h_circ, n=2 * l, axis=0)
    y = jnp.fft.irfft(vf * hf[None], n=2 * l, axis=1)[:, :l]
    return y.astype(v.dtype)


def hyena_mixer(z, conv_w, conv_b, w1, b1, w2, b2, w3, freq, skip):
    z = short_conv(z, conv_w, conv_b)
    x0, x1, v = jnp.split(z, 3, axis=-1)
    h_circ = hyena_filter(z.shape[1], w1, b1, w2, b2, w3, freq)
    v = v * x1
    v = bidir_long_conv(v, h_circ) + v * skip
    return v * x0


def merge_head_groups(y_hy, o_diff, o_gqa, subln_g, lam_init):
    b, l = y_hy.shape[:2]
    o_diff = rms_norm(o_diff, subln_g) * (1.0 - lam_init)
    return jnp.concatenate([y_hy, o_diff.reshape(b, l, -1), o_gqa.reshape(b, l, -1)], axis=-1)


def swiglu(t, wg, wu, wd):
    return (jax.nn.silu(t @ wg) * (t @ wu)) @ wd


def moe_swiglu(t, router, wg, wu, wd):
    logits = (t @ router).astype(jnp.float32)
    top_val, top_idx = lax.top_k(logits, TOP_K)
    top_w = jax.nn.softmax(top_val, axis=-1)
    gates = jnp.sum(jax.nn.one_hot(top_idx, N_EXPERTS, dtype=jnp.float32) * top_w[..., None],
                    axis=1).astype(t.dtype)
    out = jnp.zeros_like(t)
    for e in range(N_EXPERTS):
        out = out + gates[:, e:e + 1] * swiglu(t, wg[e], wu[e], wd[e])
    return out


def setup_inputs(seed: int = 0) -> dict:
    key = jax.random.key(seed)
    keys = jax.random.split(key, 48)
    count = [0]

    def normal(shape, scale):
        k = keys[count[0]]
        count[0] += 1
        return jax.random.normal(k, shape, jnp.float32) * scale

    def gain(shape):
        return 1.0 + normal(shape, 0.05)

    d = D_MODEL
    n_dense = (DEPTH + 1) // 2
    n_moe = DEPTH // 2
    return {
        'x': normal((BATCH, SEQ, d), 1.0),
        'c': normal((BATCH, d), 1.0),
        'ctx': normal((BATCH, CTX_LEN, d), 1.0),
        'c_ctx': normal((d,), 1.0),
        'w_ada': normal((DEPTH, d, 6 * d), 0.5 * d ** -0.5),
        'b_ada': normal((DEPTH, 6 * d), 0.02),
        'g_mix': gain((DEPTH, d)),
        'g_ffn': gain((DEPTH, d)),
        'w_in': normal((DEPTH, d, IN_COLS), d ** -0.5),
        'w_out': normal((DEPTH, MIX_WIDTH, d), MIX_WIDTH ** -0.5),
        'hy_conv_w': normal((DEPTH, HY_SHORT, 3 * HY_WIDTH), HY_SHORT ** -0.5),
        'hy_conv_b': normal((DEPTH, 3 * HY_WIDTH), 0.02),
        'hy_w1': normal((DEPTH, HY_EMB, HY_FILTER_HIDDEN), HY_EMB ** -0.5),
        'hy_b1': normal((DEPTH, HY_FILTER_HIDDEN), 0.1),
        'hy_w2': normal((DEPTH, HY_FILTER_HIDDEN, HY_FILTER_HIDDEN), HY_FILTER_HIDDEN ** -0.5),
        'hy_b2': normal((DEPTH, HY_FILTER_HIDDEN), 0.1),
        'hy_w3': normal((DEPTH, HY_FILTER_HIDDEN, 2 * HY_WIDTH), 0.005),
        'hy_freq': gain((DEPTH, HY_FILTER_HIDDEN)),
        'hy_skip': normal((DEPTH, HY_WIDTH), 1.0),
        'diff_lq1': normal((DEPTH, DIFF_QK_DIM), 0.1),
        'diff_lk1': normal((DEPTH, DIFF_QK_DIM), 0.1),
        'diff_lq2': normal((DEPTH, DIFF_QK_DIM), 0.1),
        'diff_lk2': normal((DEPTH, DIFF_QK_DIM), 0.1),
        'diff_subln': gain((DEPTH, DIFF_V_DIM)),
        'gqa_qnorm': gain((DEPTH, GQA_HEAD_DIM)),
        'gqa_knorm': gain((DEPTH, GQA_HEAD_DIM)),
        'ffn_wg': normal((n_dense, d, D_FF), d ** -0.5),
        'ffn_wu': normal((n_dense, d, D_FF), d ** -0.5),
        'ffn_wd': normal((n_dense, D_FF, d), D_FF ** -0.5),
        'moe_router': normal((n_moe, d, N_EXPERTS), d ** -0.5),
        'moe_wg': normal((n_moe, N_EXPERTS, d, D_FF_EXPERT), d ** -0.5),
        'moe_wu': normal((n_moe, N_EXPERTS, d, D_FF_EXPERT), d ** -0.5),
        'moe_wd': normal((n_moe, N_EXPERTS, D_FF_EXPERT, d), D_FF_EXPERT ** -0.5),
        'g_final': gain((d,)),
    }


def reference(x, c, ctx, c_ctx, w_ada, b_ada, g_mix, g_ffn, w_in, w_out,
              hy_conv_w, hy_conv_b, hy_w1, hy_b1, hy_w2, hy_b2, hy_w3, hy_freq, hy_skip,
              diff_lq1, diff_lk1, diff_lq2, diff_lk2, diff_subln, gqa_qnorm, gqa_knorm,
              ffn_wg, ffn_wu, ffn_wd, moe_router, moe_wg, moe_wu, moe_wd, g_final):
    b, seq, d = x.shape
    n_ctx = ctx.shape[1]
    cos_d, sin_d = axial_rope_tables(seq, DIFF_QK_DIM, x.dtype)
    cos_g, sin_g = axial_rope_tables(seq, GQA_HEAD_DIM, x.dtype)
    silu_c = jax.nn.silu(c)
    silu_cc = jax.nn.silu(c_ctx)
    h, hc = x, ctx
    for i in range(DEPTH):
        last = i == DEPTH - 1
        lam_init = 0.8 - 0.6 * math.exp(-0.3 * i)
        lam = (jnp.exp(jnp.sum(diff_lq1[i].astype(jnp.float32) * diff_lk1[i].astype(jnp.float32)))
               - jnp.exp(jnp.sum(diff_lq2[i].astype(jnp.float32) * diff_lk2[i].astype(jnp.float32)))
               + lam_init)
        hy_params = (hy_conv_w[i], hy_conv_b[i], hy_w1[i], hy_b1[i], hy_w2[i], hy_b2[i],
                     hy_w3[i], hy_freq[i], hy_skip[i])

        sh_m, sc_m, gt_m, sh_f, sc_f, gt_f = jnp.split(
            (silu_c @ w_ada[i] + b_ada[i])[:, None, :], 6, axis=-1)
        csh_m, csc_m, cgt_m, csh_f, csc_f, cgt_f = jnp.split(
            silu_cc @ w_ada[i] + b_ada[i], 6, axis=-1)

        u = rms_norm(h, g_mix[i]) * (1.0 + sc_m) + sh_m
        uc = rms_norm(hc, g_mix[i]) * (1.0 + csc_m) + csh_m
        z_hy, z_dq, z_dk, z_dv, z_gq, z_gk, z_gv = jnp.split(u @ w_in[i], IN_SPLITS, axis=-1)
        if last:
            zc_dk, zc_dv = jnp.split(uc @ w_in[i][:, OFF_DK:OFF_GQ], 2, axis=-1)
            zc_gk, zc_gv = jnp.split(uc @ w_in[i][:, OFF_GK:], 2, axis=-1)
        else:
            zc_hy, zc_dq, zc_dk, zc_dv, zc_gq, zc_gk, zc_gv = jnp.split(
                uc @ w_in[i], IN_SPLITS, axis=-1)

        kc_d = zc_dk.reshape(b, n_ctx, DIFF_HEADS, 2, DIFF_QK_DIM)
        vc_d = zc_dv.reshape(b, n_ctx, DIFF_HEADS, DIFF_V_DIM)
        kc_g = rms_norm(zc_gk.reshape(b, n_ctx, GQA_KV_HEADS, GQA_HEAD_DIM), gqa_knorm[i])
        vc_g = zc_gv.reshape(b, n_ctx, GQA_KV_HEADS, GQA_HEAD_DIM)

        q_d = apply_rope(z_dq.reshape(b, seq, DIFF_HEADS, 2, DIFF_QK_DIM), cos_d, sin_d)
        k_d = apply_rope(z_dk.reshape(b, seq, DIFF_HEADS, 2, DIFF_QK_DIM), cos_d, sin_d)
        v_d = z_dv.reshape(b, seq, DIFF_HEADS, DIFF_V_DIM)
        q_g = apply_rope(rms_norm(z_gq.reshape(b, seq, GQA_KV_HEADS, GQA_REP, GQA_HEAD_DIM),
                                  gqa_qnorm[i]), cos_g, sin_g)
        k_g = apply_rope(rms_norm(z_gk.reshape(b, seq, GQA_KV_HEADS, GQA_HEAD_DIM),
                                  gqa_knorm[i]), cos_g, sin_g)
        v_g = z_gv.reshape(b, seq, GQA_KV_HEADS, GQA_HEAD_DIM)

        y_hy = hyena_mixer(z_hy, *hy_params)
        o_d = diff_attention(q_d, jnp.concatenate([k_d, kc_d], axis=1),
                             jnp.concatenate([v_d, vc_d], axis=1), lam)
        o_g = gqa_attention(q_g, jnp.concatenate([k_g, kc_g], axis=1),
                            jnp.concatenate([v_g, vc_g], axis=1))
        y = merge_head_groups(y_hy, o_d, o_g, diff_subln[i], lam_init) @ w_out[i]
        h = h + gt_m * y
        if not last:
            qc_d = zc_dq.reshape(b, n_ctx, DIFF_HEADS, 2, DIFF_QK_DIM)
            qc_g = rms_norm(zc_gq.reshape(b, n_ctx, GQA_KV_HEADS, GQA_REP, GQA_HEAD_DIM),
                            gqa_qnorm[i])
            yc = merge_head_groups(hyena_mixer(zc_hy, *hy_params),
                                   diff_attention(qc_d, kc_d, vc_d, lam),
                                   gqa_attention(qc_g, kc_g, vc_g),
                                   diff_subln[i], lam_init) @ w_out[i]
            hc = hc + cgt_m * yc

        u = rms_norm(h, g_ffn[i]) * (1.0 + sc_f) + sh_f
        tokens = u.reshape(-1, d)
        n_c = 0
        if not last:
            uc = rms_norm(hc, g_ffn[i]) * (1.0 + csc_f) + csh_f
            tokens = jnp.concatenate([uc.reshape(-1, d), tokens], axis=0)
            n_c = b * n_ctx
        if i % 2 == 0:
            out = swiglu(tokens, ffn_wg[i // 2], ffn_wu[i // 2], ffn_wd[i // 2])
        else:
            out = moe_swiglu(tokens, moe_router[i // 2], moe_wg[i // 2], moe_wu[i // 2],
                             moe_wd[i // 2])
        h = h + gt_f * out[n_c:].reshape(h.shape)
        if not last:
            hc = hc + cgt_f * out[:n_c].reshape(hc.shape)
    return rms_norm(h, g_final)
```

```python
import functools
import math

import jax
import jax.numpy as jnp
from jax import lax
from jax.experimental import pallas as pl
from jax.experimental.pallas import tpu as pltpu

F32 = jnp.float32
BF16 = jnp.bfloat16
HIGHEST = lax.Precision.HIGHEST

D_MODEL = 1024
GRID_W = 64
ROPE_THETA = 10000.0
NORM_EPS = 1e-6

HY_WIDTH = 256
HY_EMB = 33
HY_FAST_DECAY = 0.3
HY_SLOW_DECAY = 1.5
HY_DECAY_TARGET = 1e-2

DIFF_HEADS = 4
DIFF_QK_DIM = 32
DIFF_V_DIM = 64
DIFF_WIDTH = 256
GQA_HEAD_DIM = 64
GQA_WIDTH = 512
GQA_KV_WIDTH = 128
GQA_REP = 4

OFF_DQ = 3 * HY_WIDTH
OFF_DK = OFF_DQ + DIFF_WIDTH
OFF_DV = OFF_DK + DIFF_WIDTH
OFF_GQ = OFF_DV + DIFF_WIDTH
OFF_GK = OFF_GQ + GQA_WIDTH
OFF_GV = OFF_GK + GQA_KV_WIDTH
IN_COLS = OFF_GV + GQA_KV_WIDTH

N_EXPERTS = 8
LANES = 128
VMEM_LIMIT = 56 * 1024 * 1024


def _params(*sem):
    return pltpu.CompilerParams(dimension_semantics=sem, vmem_limit_bytes=VMEM_LIMIT)


def _dot(a, b):
    return jnp.dot(a, b, preferred_element_type=F32)


def _dot_hi(a, b):
    return jnp.dot(a, b, preferred_element_type=F32, precision=HIGHEST)


def _rms(x, g):
    ms = jnp.mean(x * x, axis=-1, keepdims=True)
    return x * lax.rsqrt(ms + NORM_EPS) * g


def _silu(x):
    return x * jax.nn.sigmoid(x)


def _group_mean_sq(x, ones_bd, width):
    return _dot((x * x).astype(BF16), ones_bd) * (1.0 / width)


def _mods_kernel(c_ref, w_ref, b_ref, o_ref):
    o_ref[0] = _dot_hi(_silu(c_ref[...]), w_ref[0]) + b_ref[0]


def _mods(cc, w_ada, b_ada):
    depth, d, n = w_ada.shape
    tn = 1536
    return pl.pallas_call(
        _mods_kernel,
        grid=(depth, n // tn),
        in_specs=[pl.BlockSpec(cc.shape, lambda i, j: (0, 0)),
                  pl.BlockSpec((1, d, tn), lambda i, j: (i, 0, j)),
                  pl.BlockSpec((1, 1, tn), lambda i, j: (i, 0, j))],
        out_specs=pl.BlockSpec((1, cc.shape[0], tn), lambda i, j: (i, 0, j)),
        out_shape=jax.ShapeDtypeStruct((depth, cc.shape[0], n), F32),
        compiler_params=_params("arbitrary", "arbitrary"),
    )(cc, w_ada, b_ada.reshape(depth, 1, n))


def _rope128(x, cos, sin_signed, half):
    lane = lax.broadcasted_iota(jnp.int32, x.shape, 1)
    first = (lane & (2 * half - 1)) < half
    swapped = jnp.where(first, pltpu.roll(x, LANES - half, 1), pltpu.roll(x, half, 1))
    return x * cos + swapped * sin_signed


def _inproj_kernel(*refs, rope):
    h_ref, sc_ref, sh_ref, g_ref, w_ref, qn_ref, kn_ref, bd_ref = refs[:8]
    if rope:
        cd_ref, sd_ref, cg_ref, sg_ref = refs[8:12]
    zhy_ref, qd_ref, kd_ref, vd_ref, qg_ref, kg_ref, vg_ref = refs[-7:]
    u = _rms(h_ref[0], g_ref[...]) * (1.0 + sc_ref[0]) + sh_ref[0]
    z = _dot(u.astype(BF16), w_ref[...])
    zhy_ref[0] = z[:, :OFF_DQ]

    def piece(off, j):
        return z[:, off + LANES * j: off + LANES * (j + 1)]

    for j in range(DIFF_WIDTH // LANES):
        q, k = piece(OFF_DQ, j), piece(OFF_DK, j)
        if rope:
            q = _rope128(q, cd_ref[...], sd_ref[...], DIFF_QK_DIM // 2)
            k = _rope128(k, cd_ref[...], sd_ref[...], DIFF_QK_DIM // 2)
        qd_ref[0, :, LANES * j: LANES * (j + 1)] = (q * DIFF_QK_DIM ** -0.5).astype(BF16)
        kd_ref[0, :, LANES * j: LANES * (j + 1)] = k.astype(BF16)
    vd_ref[0] = z[:, OFF_DV:OFF_GQ].astype(BF16)

    def gqa_piece(x, gain):
        ms = _group_mean_sq(x, bd_ref[...], GQA_HEAD_DIM)
        x = x * lax.rsqrt(ms + NORM_EPS) * gain
        if rope:
            x = _rope128(x, cg_ref[...], sg_ref[...], GQA_HEAD_DIM // 2)
        return x

    for j in range(GQA_WIDTH // LANES):
        q = gqa_piece(piece(OFF_GQ, j), qn_ref[...])
        qg_ref[0, :, LANES * j: LANES * (j + 1)] = (q * GQA_HEAD_DIM ** -0.5).astype(BF16)
    kg_ref[0] = gqa_piece(piece(OFF_GK, 0), kn_ref[...]).astype(BF16)
    vg_ref[0] = z[:, OFF_GV:].astype(BF16)


def _inproj(h, sc, sh, g, w_bf, qn, kn, bd, rope_tabs):
    b, l, d = h.shape
    tm = min(l, 512)
    row = lambda bi, i: (bi, i, 0)
    vec = lambda bi, i: (bi, 0, 0)
    const = lambda bi, i: (0, 0)
    in_specs = [pl.BlockSpec((1, tm, d), row), pl.BlockSpec((1, 1, d), vec), pl.BlockSpec((1, 1, d), vec),
                pl.BlockSpec((1, d), const), pl.BlockSpec((d, IN_COLS), const),
                pl.BlockSpec((1, LANES), const), pl.BlockSpec((1, LANES), const),
                pl.BlockSpec((LANES, LANES), const)]
    args = [h, sc, sh, g, w_bf, qn, kn, bd]
    if rope_tabs is not None:
        in_specs += [pl.BlockSpec((tm, LANES), lambda bi, i: (i, 0))] * 4
        args += list(rope_tabs)
    widths = [(OFF_DQ, F32), (DIFF_WIDTH, BF16), (DIFF_WIDTH, BF16), (DIFF_WIDTH, BF16),
              (GQA_WIDTH, BF16), (GQA_KV_WIDTH, BF16), (GQA_KV_WIDTH, BF16)]
    return pl.pallas_call(
        functools.partial(_inproj_kernel, rope=rope_tabs is not None),
        grid=(b, l // tm),
        in_specs=in_specs,
        out_specs=[pl.BlockSpec((1, tm, w), row) for w, _ in widths],
        out_shape=[jax.ShapeDtypeStruct((b, l, w), dt) for w, dt in widths],
        compiler_params=_params("arbitrary", "arbitrary"),
    )(*args)


def _hy_filter_kernel(f_ref, w1_ref, b1_ref, w2_ref, b2_ref, w3_ref, fr_ref, dl_ref, hs_ref, hd_ref):
    f = f_ref[...]
    fr = fr_ref[...]
    a = jnp.sin(fr * (_dot_hi(f, w1_ref[...]) + b1_ref[...]))
    a = jnp.sin(fr * (_dot_hi(a, w2_ref[...]) + b2_ref[...]))
    hf = _dot_hi(a, w3_ref[...])
    decay = jnp.exp(-f[:, 0:1] * dl_ref[...])
    h_fwd = hf[:, :HY_WIDTH] * decay
    h_bwd = hf[:, HY_WIDTH:] * decay
    row = lax.broadcasted_iota(jnp.int32, h_bwd.shape, 0) + pl.program_id(0) * f.shape[0]
    h_bwd = jnp.where(row == 0, 0.0, h_bwd)
    hs_ref[...] = (h_fwd + h_bwd).astype(BF16)
    hd_ref[...] = (h_bwd - h_fwd).astype(BF16)


def _hy_spectrum_kernel(c_ref, s_ref, hs_ref, hd_ref, gre_ref, gim_ref, *, scale):
    gre_ref[...] = _dot(c_ref[...], hs_ref[...]) * scale
    gim_ref[...] = _dot(s_ref[...], hd_ref[...]) * scale


def _hy_pre_kernel(z_ref, zp_ref, zn_ref, cw_ref, cb_ref, wb_ref, wf_ref, x0_ref, *, nt):
    ti = pl.program_id(1)
    z = z_ref[0]
    tl = z.shape[0]
    row = lax.broadcasted_iota(jnp.int32, z.shape, 0)
    prev_row = jnp.where(ti > 0, zp_ref[0, 7:8, :], 0.0)
    next_row = jnp.where(ti < nt - 1, zn_ref[0, 0:1, :], 0.0)
    z_prev = jnp.where(row == 0, prev_row, pltpu.roll(z, 1, 0))
    z_next = jnp.where(row == tl - 1, next_row, pltpu.roll(z, tl - 1, 0))
    cw = cw_ref[...]
    y = cb_ref[...] + z_prev * cw[0:1] + z * cw[1:2] + z_next * cw[2:3]
    x0, x1, v = y[:, :HY_WIDTH], y[:, HY_WIDTH:2 * HY_WIDTH], y[:, 2 * HY_WIDTH:]
    w = v * x1
    wb_ref[...] = w.astype(BF16)
    wf_ref[0] = w
    x0_ref[0] = x0


def _hy_fwd_kernel(c_ref, s_ref, x_ref, gre_ref, gim_ref, yre_ref, yim_ref, *, reps):
    x = x_ref[...]
    a = _dot(c_ref[...], x)
    b = _dot(s_ref[...], x)
    gre = jnp.concatenate([gre_ref[...]] * reps, axis=1)
    gim = jnp.concatenate([gim_ref[...]] * reps, axis=1)
    yre_ref[...] = (a * gre + b * gim).astype(BF16)
    yim_ref[...] = (a * gim - b * gre).astype(BF16)


def _hy_inv_kernel(ci_ref, si_ref, yre_ref, yim_ref, wf_ref, x0_ref, skip_ref, o_ref, *, reps):
    y = _dot(ci_ref[...], yre_ref[...]) - _dot(si_ref[...], yim_ref[...])
    for r in range(reps):
        yr = y[:, r * HY_WIDTH:(r + 1) * HY_WIDTH]
        o_ref[r] = ((yr + wf_ref[r] * skip_ref[...]) * x0_ref[r]).astype(BF16)


def _hyena(zhy, p, tabs):
    conv_w, conv_b, w1, b1, w2, b2, w3, freq, skip = p
    c_tab, s_tab, ci_tab, si_tab, feats, deltas = tabs
    b, l, _ = zhy.shape
    c = HY_WIDTH
    tl = min(l, 512)
    nt = l // tl
    hid = w2.shape[0]
    const1 = lambda i: (0, 0)
    w1p = jnp.zeros((LANES, hid), F32).at[:HY_EMB].set(w1)
    hs, hd = pl.pallas_call(
        _hy_filter_kernel,
        grid=(nt,),
        in_specs=[pl.BlockSpec((tl, LANES), lambda i: (i, 0)),
                  pl.BlockSpec((LANES, hid), const1), pl.BlockSpec((1, hid), const1),
                  pl.BlockSpec((hid, hid), const1), pl.BlockSpec((1, hid), const1),
                  pl.BlockSpec((hid, 2 * c), const1), pl.BlockSpec((1, hid), const1),
                  pl.BlockSpec((1, c), const1)],
        out_specs=[pl.BlockSpec((tl, c), lambda i: (i, 0))] * 2,
        out_shape=[jax.ShapeDtypeStruct((l, c), BF16)] * 2,
        compiler_params=_params("arbitrary"),
    )(feats, w1p, b1.reshape(1, hid), w2, b2.reshape(1, hid), w3, freq.reshape(1, hid), deltas)

    gre, gim = pl.pallas_call(
        functools.partial(_hy_spectrum_kernel, scale=1.0 / l),
        grid=(nt,),
        in_specs=[pl.BlockSpec((tl, l), lambda i: (i, 0)), pl.BlockSpec((tl, l), lambda i: (i, 0)),
                  pl.BlockSpec((l, c), const1), pl.BlockSpec((l, c), const1)],
        out_specs=[pl.BlockSpec((tl, c), lambda i: (i, 0))] * 2,
        out_shape=[jax.ShapeDtypeStruct((l, c), F32)] * 2,
        compiler_params=_params("arbitrary"),
    )(c_tab, s_tab, hs, hd)

    halo = 8
    wb, wf, x0 = pl.pallas_call(
        functools.partial(_hy_pre_kernel, nt=nt),
        grid=(b, nt),
        in_specs=[pl.BlockSpec((1, tl, 3 * c), lambda bi, i: (bi, i, 0)),
                  pl.BlockSpec((1, halo, 3 * c), lambda bi, i: (bi, jnp.maximum(i * (tl // halo) - 1, 0), 0)),
                  pl.BlockSpec((1, halo, 3 * c),
                               lambda bi, i: (bi, jnp.minimum((i + 1) * (tl // halo), l // halo - 1), 0)),
                  pl.BlockSpec((3, 3 * c), lambda bi, i: (0, 0)),
                  pl.BlockSpec((1, 3 * c), lambda bi, i: (0, 0))],
        out_specs=[pl.BlockSpec((tl, c), lambda bi, i: (i, bi)),
                   pl.BlockSpec((1, tl, c), lambda bi, i: (bi, i, 0)),
                   pl.BlockSpec((1, tl, c), lambda bi, i: (bi, i, 0))],
        out_shape=[jax.ShapeDtypeStruct((l, b * c), BF16),
                   jax.ShapeDtypeStruct((b, l, c), F32),
                   jax.ShapeDtypeStruct((b, l, c), F32)],
        compiler_params=_params("arbitrary", "arbitrary"),
    )(zhy, zhy, zhy, conv_w, conv_b.reshape(1, 3 * c))

    reps = 2
    tn = reps * c
    nj = b * c // tn
    yre, yim = pl.pallas_call(
        functools.partial(_hy_fwd_kernel, reps=reps),
        grid=(nj, nt),
        in_specs=[pl.BlockSpec((tl, l), lambda j, i: (i, 0)), pl.BlockSpec((tl, l), lambda j, i: (i, 0)),
                  pl.BlockSpec((l, tn), lambda j, i: (0, j)),
                  pl.BlockSpec((tl, c), lambda j, i: (i, 0)), pl.BlockSpec((tl, c), lambda j, i: (i, 0))],
        out_specs=[pl.BlockSpec((tl, tn), lambda j, i: (i, j))] * 2,
        out_shape=[jax.ShapeDtypeStruct((l, b * c), BF16)] * 2,
        compiler_params=_params("arbitrary", "arbitrary"),
    )(c_tab, s_tab, wb, gre, gim)

    return pl.pallas_call(
        functools.partial(_hy_inv_kernel, reps=reps),
        grid=(nj, nt),
        in_specs=[pl.BlockSpec((tl, l), lambda j, i: (i, 0)), pl.BlockSpec((tl, l), lambda j, i: (i, 0)),
                  pl.BlockSpec((l, tn), lambda j, i: (0, j)), pl.BlockSpec((l, tn), lambda j, i: (0, j)),
                  pl.BlockSpec((reps, tl, c), lambda j, i: (j, i, 0)),
                  pl.BlockSpec((reps, tl, c), lambda j, i: (j, i, 0)),
                  pl.BlockSpec((1, c), lambda j, i: (0, 0))],
        out_specs=pl.BlockSpec((reps, tl, c), lambda j, i: (j, i, 0)),
        out_shape=jax.ShapeDtypeStruct((b, l, c), BF16),
        compiler_params=_params("arbitrary", "arbitrary"),
    )(ci_tab, si_tab, yre, yim, wf, x0, skip.reshape(1, c))


def _lane_mask(width, shift, idx):
    lane = lax.broadcasted_iota(jnp.int32, (1, width), 1)
    return jnp.where((lane >> shift) == idx, 1.0, 0.0).astype(BF16)


def _softmax_parts(qm, kt_refs):
    ss = [_dot(qm, kt[0]) for kt in kt_refs]
    mx = functools.reduce(jnp.maximum, [jnp.max(s, axis=-1, keepdims=True) for s in ss])
    es = [jnp.exp(s - mx) for s in ss]
    l = functools.reduce(jnp.add, [jnp.sum(e, axis=-1, keepdims=True) for e in es])
    return es, 1.0 / l


def _diff_kernel(*refs, n_kv, lam_init):
    lqk_ref, q_ref = refs[:2]
    kt_refs = refs[2:2 + n_kv]
    v_refs = refs[2 + n_kv:2 + 2 * n_kv]
    g_ref, bd_ref, o_ref = refs[2 + 2 * n_kv:]
    x = lqk_ref[...]
    lam = (jnp.exp(jnp.sum(x[0:1] * x[1:2], axis=-1, keepdims=True))
           - jnp.exp(jnp.sum(x[2:3] * x[3:4], axis=-1, keepdims=True)) + lam_init)
    q = q_ref[0]
    lane = lax.broadcasted_iota(jnp.int32, (q.shape[0], DIFF_WIDTH), 1)

    def head(h, o_acc):
        es1, inv1 = _softmax_parts(q * _lane_mask(DIFF_WIDTH, 5, 2 * h), kt_refs)
        es2, inv2 = _softmax_parts(q * _lane_mask(DIFF_WIDTH, 5, 2 * h + 1), kt_refs)
        inv2 = inv2 * lam
        r = None
        for e1, e2, v in zip(es1, es2, v_refs):
            a = (e1 * inv1 - e2 * inv2).astype(BF16)
            pv = _dot(a, v[0])
            r = pv if r is None else r + pv
        return jnp.where((lane >> 6) == h, r, o_acc)

    o = lax.fori_loop(0, DIFF_HEADS, head, jnp.zeros(lane.shape, F32))
    ms = _group_mean_sq(o, bd_ref[...], DIFF_V_DIM)
    o_ref[0] = (o * lax.rsqrt(ms + NORM_EPS) * (g_ref[...] * (1.0 - lam_init))).astype(BF16)


def _gqa_kernel(*refs, n_kv):
    q_ref = refs[0]
    kt_refs = refs[1:1 + n_kv]
    v_refs = refs[1 + n_kv:1 + 2 * n_kv]
    o_ref = refs[-1]
    q = q_ref[0]
    lane = lax.broadcasted_iota(jnp.int32, (q.shape[0], GQA_REP * GQA_HEAD_DIM), 1)

    def head(r, o_acc):
        es, inv = _softmax_parts(q * _lane_mask(GQA_REP * GQA_HEAD_DIM, 6, r), [kt.at[0] for kt in kt_refs])
        pv = None
        for e, v in zip(es, v_refs):
            t = _dot(e.astype(BF16), v[0, 0])
            pv = t if pv is None else pv + t
        return jnp.where((lane >> 6) == r, pv * inv, o_acc)

    o_ref[0] = lax.fori_loop(0, GQA_REP, head, jnp.zeros(lane.shape, F32)).astype(BF16)


def _diff_attention(lqk, q, kts, vs, g256, bd256, lam_init):
    b, lq, w = q.shape
    tq = min(lq, 256)
    in_specs = [pl.BlockSpec(lqk.shape, lambda bi, i: (0, 0)),
                pl.BlockSpec((1, tq, w), lambda bi, i: (bi, i, 0))]
    in_specs += [pl.BlockSpec((1, w, kt.shape[2]), lambda bi, i: (bi, 0, 0)) for kt in kts]
    in_specs += [pl.BlockSpec((1, v.shape[1], w), lambda bi, i: (bi, 0, 0)) for v in vs]
    in_specs += [pl.BlockSpec((1, w), lambda bi, i: (0, 0)), pl.BlockSpec((w, w), lambda bi, i: (0, 0))]
    return pl.pallas_call(
        functools.partial(_diff_kernel, n_kv=len(kts), lam_init=lam_init),
        grid=(b, lq // tq),
        in_specs=in_specs,
        out_specs=pl.BlockSpec((1, tq, w), lambda bi, i: (bi, i, 0)),
        out_shape=jax.ShapeDtypeStruct((b, lq, w), BF16),
        compiler_params=_params("arbitrary", "arbitrary"),
    )(lqk, q, *kts, *vs, g256, bd256)


def _gqa_attention(q, kts, vs):
    b, lq, _ = q.shape
    w = GQA_REP * GQA_HEAD_DIM
    groups = GQA_WIDTH // w
    tq = min(lq, 256)
    in_specs = [pl.BlockSpec((1, tq, w), lambda bi, g, i: (bi, i, g))]
    in_specs += [pl.BlockSpec((1, 1, w, kt.shape[3]), lambda bi, g, i: (bi, g, 0, 0)) for kt in kts]
    in_specs += [pl.BlockSpec((1, 1, v.shape[2], w), lambda bi, g, i: (bi, g, 0, 0)) for v in vs]
    return pl.pallas_call(
        functools.partial(_gqa_kernel, n_kv=len(kts)),
        grid=(b, groups, lq // tq),
        in_specs=in_specs,
        out_specs=pl.BlockSpec((1, tq, w), lambda bi, g, i: (bi, i, g)),
        out_shape=jax.ShapeDtypeStruct((b, lq, GQA_WIDTH), BF16),
        compiler_params=_params("arbitrary", "arbitrary", "arbitrary"),
    )(q, *kts, *vs)


def _top2_gates(logits):
    lane = lax.broadcasted_iota(jnp.int32, logits.shape, 1).astype(F32)
    neg = jnp.float32(-jnp.inf)
    lg = jnp.where(lane < N_EXPERTS, logits, neg)
    m1 = jnp.max(lg, axis=-1, keepdims=True)
    i1 = jnp.min(jnp.where(lg == m1, lane, float(LANES)), axis=-1, keepdims=True)
    lg2 = jnp.where(lane == i1, neg, lg)
    m2 = jnp.max(lg2, axis=-1, keepdims=True)
    i2 = jnp.min(jnp.where(lg2 == m2, lane, float(LANES)), axis=-1, keepdims=True)
    e2 = jnp.exp(m2 - m1)
    w1 = 1.0 / (1.0 + e2)
    return jnp.where(lane == i1, w1, 0.0) + jnp.where(lane == i2, e2 * w1, 0.0)


def _outproj_kernel(*refs, moe):
    yhy_ref, od_ref, og_ref, h_ref, gt_ref, w_ref, gf_ref, sc_ref, sh_ref = refs[:9]
    if moe:
        rt_ref, hn_ref, u_ref, gates_ref = refs[9:]
    else:
        hn_ref, u_ref = refs[9:]
    y = (_dot(yhy_ref[0], w_ref[0:HY_WIDTH, :])
         + _dot(od_ref[0], w_ref[HY_WIDTH:HY_WIDTH + DIFF_WIDTH, :])
         + _dot(og_ref[0], w_ref[HY_WIDTH + DIFF_WIDTH:, :]))
    hn = h_ref[0] + gt_ref[0] * y
    hn_ref[0] = hn
    u = _rms(hn, gf_ref[...]) * (1.0 + sc_ref[0]) + sh_ref[0]
    u_ref[0] = u.astype(BF16)
    if moe:
        gates_ref[0] = _top2_gates(_dot_hi(u, rt_ref[...]))


def _outproj(yhy, od, og, h, gt, w_bf, gf, sc, sh, router_pad):
    b, l, d = h.shape
    tm = min(l, 512)
    row = lambda bi, i: (bi, i, 0)
    vec = lambda bi, i: (bi, 0, 0)
    const = lambda bi, i: (0, 0)
    moe = router_pad is not None
    in_specs = [pl.BlockSpec((1, tm, HY_WIDTH), row), pl.BlockSpec((1, tm, DIFF_WIDTH), row),
                pl.BlockSpec((1, tm, GQA_WIDTH), row), pl.BlockSpec((1, tm, d), row),
                pl.BlockSpec((1, 1, d), vec), pl.BlockSpec((d, d), const), pl.BlockSpec((1, d), const),
                pl.BlockSpec((1, 1, d), vec), pl.BlockSpec((1, 1, d), vec)]
    args = [yhy, od, og, h, gt, w_bf, gf, sc, sh]
    out_specs = [pl.BlockSpec((1, tm, d), row), pl.BlockSpec((1, tm, d), row)]
    out_shape = [jax.ShapeDtypeStruct((b, l, d), F32), jax.ShapeDtypeStruct((b, l, d), BF16)]
    if moe:
        in_specs.append(pl.BlockSpec((d, LANES), const))
        args.append(router_pad)
        out_specs.append(pl.BlockSpec((1, tm, LANES), row))
        out_shape.append(jax.ShapeDtypeStruct((b, l, LANES), F32))
    return pl.pallas_call(
        functools.partial(_outproj_kernel, moe=moe),
        grid=(b, l // tm), in_specs=in_specs, out_specs=out_specs, out_shape=out_shape,
        compiler_params=_params("arbitrary", "arbitrary"),
    )(*args)


def _ffn_kernel(u_ref, wg_ref, wu_ref, wd_ref, h_ref, gt_ref, o_ref, acc_ref):
    f = pl.program_id(2)

    @pl.when(f == 0)
    def _():
        acc_ref[...] = jnp.zeros_like(acc_ref)

    u = u_ref[0]
    hid = _silu(_dot(u, wg_ref[...])) * _dot(u, wu_ref[...])
    acc_ref[...] += _dot(hid.astype(BF16), wd_ref[...])

    @pl.when(f == pl.num_programs(2) - 1)
    def _():
        o_ref[0] = h_ref[0] + gt_ref[0] * acc_ref[...]


def _ffn(u, wg, wu, wd, h, gt):
    b, l, d = h.shape
    dff = wg.shape[1]
    tm = min(l, 512)
    tf = dff // 2
    row = lambda bi, i, f: (bi, i, 0)
    return pl.pallas_call(
        _ffn_kernel,
        grid=(b, l // tm, dff // tf),
        in_specs=[pl.BlockSpec((1, tm, d), row),
                  pl.BlockSpec((d, tf), lambda bi, i, f: (0, f)), pl.BlockSpec((d, tf), lambda bi, i, f: (0, f)),
                  pl.BlockSpec((tf, d), lambda bi, i, f: (f, 0)),
                  pl.BlockSpec((1, tm, d), row), pl.BlockSpec((1, 1, d), lambda bi, i, f: (bi, 0, 0))],
        out_specs=pl.BlockSpec((1, tm, d), row),
        out_shape=jax.ShapeDtypeStruct((b, l, d), F32),
        scratch_shapes=[pltpu.VMEM((tm, d), F32)],
        compiler_params=_params("arbitrary", "arbitrary", "arbitrary"),
    )(u, wg, wu, wd, h, gt)


def _moe_kernel(u_ref, gates_ref, wg_ref, wu_ref, wd_ref, h_ref, gt_ref, o_ref, acc_ref):
    e, f = pl.program_id(2), pl.program_id(3)

    @pl.when((e == 0) & (f == 0))
    def _():
        acc_ref[...] = jnp.zeros_like(acc_ref)

    u = u_ref[0]
    gates = gates_ref[0]
    lane = lax.broadcasted_iota(jnp.int32, gates.shape, 1)
    gate = jnp.sum(jnp.where(lane == e, gates, 0.0), axis=-1, keepdims=True)
    hid = _silu(_dot(u, wg_ref[0])) * _dot(u, wu_ref[0]) * gate
    acc_ref[...] += _dot(hid.astype(BF16), wd_ref[0])

    @pl.when((e == pl.num_programs(2) - 1) & (f == pl.num_programs(3) - 1))
    def _():
        o_ref[0] = h_ref[0] + gt_ref[0] * acc_ref[...]


def _moe(u, gates, wg, wu, wd, h, gt):
    b, l, d = h.shape
    n_e, _, dff = wg.shape
    tm = min(l, 1024)
    tf = dff // 4
    row = lambda bi, i, e, f: (bi, i, 0)
    return pl.pallas_call(
        _moe_kernel,
        grid=(b, l // tm, n_e, dff // tf),
        in_specs=[pl.BlockSpec((1, tm, d), row), pl.BlockSpec((1, tm, LANES), row),
                  pl.BlockSpec((1, d, tf), lambda bi, i, e, f: (e, 0, f)),
                  pl.BlockSpec((1, d, tf), lambda bi, i, e, f: (e, 0, f)),
                  pl.BlockSpec((1, tf, d), lambda bi, i, e, f: (e, f, 0)),
                  pl.BlockSpec((1, tm, d), row), pl.BlockSpec((1, 1, d), lambda bi, i, e, f: (bi, 0, 0))],
        out_specs=pl.BlockSpec((1, tm, d), row),
        out_shape=jax.ShapeDtypeStruct((b, l, d), F32),
        scratch_shapes=[pltpu.VMEM((tm, d), F32)],
        compiler_params=_params("arbitrary", "arbitrary", "arbitrary", "arbitrary"),
    )(u, gates, wg, wu, wd, h, gt)


def _final_norm_kernel(h_ref, g_ref, o_ref):
    o_ref[0] = _rms(h_ref[0], g_ref[...])


def _final_norm(h, g):
    b, l, d = h.shape
    tm = min(l, 1024)
    return pl.pallas_call(
        _final_norm_kernel,
        grid=(b, l // tm),
        in_specs=[pl.BlockSpec((1, tm, d), lambda bi, i: (bi, i, 0)), pl.BlockSpec((1, d), lambda bi, i: (0, 0))],
        out_specs=pl.BlockSpec((1, tm, d), lambda bi, i: (bi, i, 0)),
        out_shape=jax.ShapeDtypeStruct((b, l, d), F32),
        compiler_params=_params("arbitrary", "arbitrary"),
    )(h, g.reshape(1, d))


def _rope_tables(length, head_dim):
    t = jnp.arange(length)
    row = (t // GRID_W).astype(F32)
    col = (t % GRID_W).astype(F32)
    n = head_dim // 4
    inv = ROPE_THETA ** (-jnp.arange(n, dtype=F32) / n)
    ang = jnp.concatenate([row[:, None] * inv, col[:, None] * inv], axis=-1)
    cos = jnp.concatenate([jnp.cos(ang)] * 2, axis=-1)
    sin = jnp.concatenate([-jnp.sin(ang), jnp.sin(ang)], axis=-1)
    reps = LANES // head_dim
    return jnp.tile(cos, (1, reps)), jnp.tile(sin, (1, reps))


def _hyena_tables(length):
    k = jnp.arange(length, dtype=jnp.int32)
    period = 4 * length
    step = 2.0 * math.pi / period

    def tab(a, bidx, fn):
        return fn(((a * bidx) % period).astype(F32) * step).astype(BF16)

    odd = 2 * k + 1
    c_tab = tab(odd[:, None], k[None, :], jnp.cos)
    s_tab = tab(odd[:, None], k[None, :], jnp.sin)
    ci_tab = tab(k[:, None], odd[None, :], jnp.cos)
    si_tab = tab(k[:, None], odd[None, :], jnp.sin)
    t = jnp.linspace(0.0, 1.0, length, dtype=F32)[:, None]
    bands = (HY_EMB - 1) // 2
    ang = (2.0 * math.pi / length) * jnp.arange(length, dtype=F32)[:, None] \
        * jnp.linspace(1e-4, bands - 1, bands, dtype=F32)
    feats = jnp.concatenate([t, jnp.cos(ang), -jnp.sin(ang),
                             jnp.zeros((length, LANES - HY_EMB), F32)], axis=-1)
    max_decay = math.log(HY_DECAY_TARGET) / HY_FAST_DECAY
    min_decay = math.log(HY_DECAY_TARGET) / HY_SLOW_DECAY
    deltas = jnp.abs(jnp.linspace(min_decay, max_decay, HY_WIDTH, dtype=F32)).reshape(1, HY_WIDTH)
    return c_tab, s_tab, ci_tab, si_tab, feats, deltas


def _block_diag_ones(n, group):
    i = jnp.arange(n) // group
    return (i[:, None] == i[None, :]).astype(BF16)


def kernel(x, c, ctx, c_ctx, w_ada, b_ada, g_mix, g_ffn, w_in, w_out, hy_conv_w, hy_conv_b, hy_w1, hy_b1, hy_w2, hy_b2, hy_w3, hy_freq, hy_skip, diff_lq1, diff_lk1, diff_lq2, diff_lk2, diff_subln, gqa_qnorm, gqa_knorm, ffn_wg, ffn_wu, ffn_wd, moe_router, moe_wg, moe_wu, moe_wd, g_final):
    b, seq, d = x.shape
    n_ctx = ctx.shape[1]
    depth = w_ada.shape[0]

    rope_tabs = _rope_tables(seq, DIFF_QK_DIM) + _rope_tables(seq, GQA_HEAD_DIM)
    hy_tabs = _hyena_tables(seq)
    hy_tabs_c = _hyena_tables(n_ctx)
    bd128 = _block_diag_ones(LANES, GQA_HEAD_DIM)
    bd256 = _block_diag_ones(DIFF_WIDTH, DIFF_V_DIM)

    rows = 16
    cc = jnp.zeros((rows, d), F32).at[:b].set(c).at[b].set(c_ctx)
    mods = _mods(cc, w_ada, b_ada)

    def mod_vecs(i):
        lat = [mods[i, :b, j * d:(j + 1) * d].reshape(b, 1, d) for j in range(6)]
        cx = [jnp.broadcast_to(mods[i, b, j * d:(j + 1) * d].reshape(1, 1, d), (b, 1, d)) for j in range(6)]
        return lat, cx

    def kv_layouts(kd, vd, kg, vg):
        l = kd.shape[1]
        kdt = jnp.swapaxes(kd, 1, 2)
        kgt = jnp.swapaxes(kg, 1, 2).reshape(b, 2, 1, GQA_HEAD_DIM, l)
        kgt = jnp.broadcast_to(kgt, (b, 2, GQA_REP, GQA_HEAD_DIM, l)).reshape(b, 2, GQA_REP * GQA_HEAD_DIM, l)
        vgx = jnp.swapaxes(vg.reshape(b, l, 2, 1, GQA_HEAD_DIM), 1, 2)
        vgx = jnp.broadcast_to(vgx, (b, 2, l, GQA_REP, GQA_HEAD_DIM)).reshape(b, 2, l, GQA_REP * GQA_HEAD_DIM)
        return kdt, vd, kgt, vgx

    h, hc = x, ctx
    for i in range(depth):
        last = i == depth - 1
        lam_init = 0.8 - 0.6 * math.exp(-0.3 * i)
        (sh_m, sc_m, gt_m, sh_f, sc_f, gt_f), (csh_m, csc_m, cgt_m, csh_f, csc_f, cgt_f) = mod_vecs(i)
        w_in_bf = w_in[i].astype(BF16)
        w_out_bf = w_out[i].astype(BF16)
        g_m = g_mix[i].reshape(1, d)
        g_f = g_ffn[i].reshape(1, d)
        qn = jnp.tile(gqa_qnorm[i], LANES // GQA_HEAD_DIM).reshape(1, LANES)
        kn = jnp.tile(gqa_knorm[i], LANES // GQA_HEAD_DIM).reshape(1, LANES)
        subln = jnp.tile(diff_subln[i], DIFF_HEADS).reshape(1, DIFF_WIDTH)
        lqk = jnp.stack([diff_lq1[i], diff_lk1[i], diff_lq2[i], diff_lk2[i]]).astype(F32)
        hy_p = (hy_conv_w[i], hy_conv_b[i], hy_w1[i], hy_b1[i], hy_w2[i], hy_b2[i], hy_w3[i],
                hy_freq[i], hy_skip[i])
        moe = i % 2 == 1
        router_pad = None
        if moe:
            router_pad = jnp.zeros((d, LANES), F32).at[:, :N_EXPERTS].set(moe_router[i // 2])

        zhy, qd, kd, vd, qg, kg, vg = _inproj(h, sc_m, sh_m, g_m, w_in_bf, qn, kn, bd128, rope_tabs)
        czhy, cqd, ckd, cvd, cqg, ckg, cvg = _inproj(hc, csc_m, csh_m, g_m, w_in_bf, qn, kn, bd128, None)
        kdt, vdx, kgt, vgx = kv_layouts(kd, vd, kg, vg)
        ckdt, cvdx, ckgt, cvgx = kv_layouts(ckd, cvd, ckg, cvg)

        y_hy = _hyena(zhy, hy_p, hy_tabs)
        o_d = _diff_attention(lqk, qd, [kdt, ckdt], [vdx, cvdx], subln, bd256, lam_init)
        o_g = _gqa_attention(qg, [kgt, ckgt], [vgx, cvgx])
        mixed = _outproj(y_hy, o_d, o_g, h, gt_m, w_out_bf, g_f, sc_f, sh_f, router_pad)
        if not last:
            yc_hy = _hyena(czhy, hy_p, hy_tabs_c)
            oc_d = _diff_attention(lqk, cqd, [ckdt], [cvdx], subln, bd256, lam_init)
            oc_g = _gqa_attention(cqg, [ckgt], [cvgx])
            cmixed = _outproj(yc_hy, oc_d, oc_g, hc, cgt_m, w_out_bf, g_f, csc_f, csh_f, router_pad)

        if moe:
            wg, wu, wd = (w[i // 2].astype(BF16) for w in (moe_wg, moe_wu, moe_wd))
            h = _moe(mixed[1], mixed[2], wg, wu, wd, mixed[0], gt_f)
            if not last:
                hc = _moe(cmixed[1], cmixed[2], wg, wu, wd, cmixed[0], cgt_f)
        else:
            wg, wu, wd = (w[i // 2].astype(BF16) for w in (ffn_wg, ffn_wu, ffn_wd))
            h = _ffn(mixed[1], wg, wu, wd, mixed[0], gt_f)
            if not last:
                hc = _ffn(cmixed[1], wg, wu, wd, cmixed[0], cgt_f)
    return _final_norm(h, g_final)
```

```python
import functools
import math

import jax
import jax.numpy as jnp
from jax import lax
from jax.experimental import pallas as pl
from jax.experimental.pallas import tpu as pltpu

F32 = jnp.float32
BF16 = jnp.bfloat16
HIGHEST = lax.Precision.HIGHEST

D_MODEL = 1024
GRID_W = 64
ROPE_THETA = 10000.0
NORM_EPS = 1e-6

HY_WIDTH = 256
HY_EMB = 33
HY_FAST_DECAY = 0.3
HY_SLOW_DECAY = 1.5
HY_DECAY_TARGET = 1e-2

DIFF_HEADS = 4
DIFF_QK_DIM = 32
DIFF_V_DIM = 64
DIFF_WIDTH = 256
GQA_HEAD_DIM = 64
GQA_WIDTH = 512
GQA_KV_WIDTH = 128
GQA_REP = 4

OFF_DQ = 3 * HY_WIDTH
OFF_DK = OFF_DQ + DIFF_WIDTH
OFF_DV = OFF_DK + DIFF_WIDTH
OFF_GQ = OFF_DV + DIFF_WIDTH
OFF_GK = OFF_GQ + GQA_WIDTH
OFF_GV = OFF_GK + GQA_KV_WIDTH
IN_COLS = OFF_GV + GQA_KV_WIDTH

N_EXPERTS = 8
LOG2_E = math.log2(math.e)
LANES = 128
VMEM_LIMIT = 56 * 1024 * 1024


def _params(*sem):
    return pltpu.CompilerParams(dimension_semantics=sem, vmem_limit_bytes=VMEM_LIMIT)


def _dot(a, b):
    return jnp.dot(a, b, preferred_element_type=F32)


def _dot_hi(a, b):
    return jnp.dot(a, b, preferred_element_type=F32, precision=HIGHEST)


def _rms(x, g):
    ms = jnp.mean(x * x, axis=-1, keepdims=True)
    return x * lax.rsqrt(ms + NORM_EPS) * g


def _silu(x):
    return x * jax.nn.sigmoid(x)


def _group_mean_sq(x, ones_bd, width):
    return _dot((x * x).astype(BF16), ones_bd) * (1.0 / width)


def _mods_kernel(c_ref, w_ref, b_ref, o_ref):
    o_ref[0] = _dot_hi(_silu(c_ref[...]), w_ref[0]) + b_ref[0]


def _mods(cc, w_ada, b_ada):
    depth, d, n = w_ada.shape
    tn = 1536
    return pl.pallas_call(
        _mods_kernel,
        grid=(depth, n // tn),
        in_specs=[pl.BlockSpec(cc.shape, lambda i, j: (0, 0)),
                  pl.BlockSpec((1, d, tn), lambda i, j: (i, 0, j)),
                  pl.BlockSpec((1, 1, tn), lambda i, j: (i, 0, j))],
        out_specs=pl.BlockSpec((1, cc.shape[0], tn), lambda i, j: (i, 0, j)),
        out_shape=jax.ShapeDtypeStruct((depth, cc.shape[0], n), F32),
        compiler_params=_params("arbitrary", "arbitrary"),
    )(cc, w_ada, b_ada.reshape(depth, 1, n))


def _rope128(x, cos, sin_signed, half):
    lane = lax.broadcasted_iota(jnp.int32, x.shape, 1)
    first = (lane & (2 * half - 1)) < half
    swapped = jnp.where(first, pltpu.roll(x, LANES - half, 1), pltpu.roll(x, half, 1))
    return x * cos + swapped * sin_signed


def _inproj_kernel(*refs, rope):
    h_ref, sc_ref, sh_ref, g_ref, w_ref, qn_ref, kn_ref, bd_ref = refs[:8]
    if rope:
        cd_ref, sd_ref, cg_ref, sg_ref = refs[8:12]
    zhy_ref, qd_ref, kd_ref, vd_ref, qg_ref, kg_ref, vg_ref = refs[-7:]
    u = _rms(h_ref[0], g_ref[...]) * (1.0 + sc_ref[0]) + sh_ref[0]
    z = _dot(u.astype(BF16), w_ref[...])
    zhy_ref[0] = z[:, :OFF_DQ]

    def piece(off, j):
        return z[:, off + LANES * j: off + LANES * (j + 1)]

    for j in range(DIFF_WIDTH // LANES):
        q, k = piece(OFF_DQ, j), piece(OFF_DK, j)
        if rope:
            q = _rope128(q, cd_ref[...], sd_ref[...], DIFF_QK_DIM // 2)
            k = _rope128(k, cd_ref[...], sd_ref[...], DIFF_QK_DIM // 2)
        qd_ref[0, :, LANES * j: LANES * (j + 1)] = (q * (LOG2_E * DIFF_QK_DIM ** -0.5)).astype(BF16)
        kd_ref[0, :, LANES * j: LANES * (j + 1)] = k.astype(BF16)
    vd_ref[0] = z[:, OFF_DV:OFF_GQ].astype(BF16)

    def gqa_piece(x, gain):
        ms = _group_mean_sq(x, bd_ref[...], GQA_HEAD_DIM)
        x = x * lax.rsqrt(ms + NORM_EPS) * gain
        if rope:
            x = _rope128(x, cg_ref[...], sg_ref[...], GQA_HEAD_DIM // 2)
        return x

    for j in range(GQA_WIDTH // LANES):
        q = gqa_piece(piece(OFF_GQ, j), qn_ref[...])
        qg_ref[0, :, LANES * j: LANES * (j + 1)] = (q * (LOG2_E * GQA_HEAD_DIM ** -0.5)).astype(BF16)
    kg_ref[0] = gqa_piece(piece(OFF_GK, 0), kn_ref[...]).astype(BF16)
    vg_ref[0] = z[:, OFF_GV:].astype(BF16)


def _inproj(h, sc, sh, g, w_bf, qn, kn, bd, rope_tabs):
    b, l, d = h.shape
    tm = min(l, 512)
    row = lambda bi, i: (bi, i, 0)
    vec = lambda bi, i: (bi, 0, 0)
    const = lambda bi, i: (0, 0)
    in_specs = [pl.BlockSpec((1, tm, d), row), pl.BlockSpec((1, 1, d), vec), pl.BlockSpec((1, 1, d), vec),
                pl.BlockSpec((1, d), const), pl.BlockSpec((d, IN_COLS), const),
                pl.BlockSpec((1, LANES), const), pl.BlockSpec((1, LANES), const),
                pl.BlockSpec((LANES, LANES), const)]
    args = [h, sc, sh, g, w_bf, qn, kn, bd]
    if rope_tabs is not None:
        in_specs += [pl.BlockSpec((tm, LANES), lambda bi, i: (i, 0))] * 4
        args += list(rope_tabs)
    widths = [(OFF_DQ, F32), (DIFF_WIDTH, BF16), (DIFF_WIDTH, BF16), (DIFF_WIDTH, BF16),
              (GQA_WIDTH, BF16), (GQA_KV_WIDTH, BF16), (GQA_KV_WIDTH, BF16)]
    return pl.pallas_call(
        functools.partial(_inproj_kernel, rope=rope_tabs is not None),
        grid=(b, l // tm),
        in_specs=in_specs,
        out_specs=[pl.BlockSpec((1, tm, w), row) for w, _ in widths],
        out_shape=[jax.ShapeDtypeStruct((b, l, w), dt) for w, dt in widths],
        compiler_params=_params("arbitrary", "arbitrary"),
    )(*args)


def _hy_filter_kernel(f_ref, w1_ref, b1_ref, w2_ref, b2_ref, w3_ref, fr_ref, dl_ref, hs_ref, hd_ref):
    f = f_ref[...]
    fr = fr_ref[...]
    a = jnp.sin(fr * (_dot_hi(f, w1_ref[...]) + b1_ref[...]))
    a = jnp.sin(fr * (_dot_hi(a, w2_ref[...]) + b2_ref[...]))
    hf = _dot_hi(a, w3_ref[...])
    decay = jnp.exp(-f[:, 0:1] * dl_ref[...])
    h_fwd = hf[:, :HY_WIDTH] * decay
    h_bwd = hf[:, HY_WIDTH:] * decay
    row = lax.broadcasted_iota(jnp.int32, h_bwd.shape, 0) + pl.program_id(0) * f.shape[0]
    h_bwd = jnp.where(row == 0, 0.0, h_bwd)
    hs_ref[...] = (h_fwd + h_bwd).astype(BF16)
    hd_ref[...] = (h_bwd - h_fwd).astype(BF16)


def _hy_spectrum_kernel(c_ref, s_ref, hs_ref, hd_ref, gre_ref, gim_ref, *, scale):
    gre_ref[...] = _dot(c_ref[...], hs_ref[...]) * scale
    gim_ref[...] = _dot(s_ref[...], hd_ref[...]) * scale


def _hy_pre_kernel(z_ref, zp_ref, zn_ref, cw_ref, cb_ref, wb_ref, wf_ref, x0_ref, *, nt):
    ti = pl.program_id(1)
    z = z_ref[0]
    tl = z.shape[0]
    row = lax.broadcasted_iota(jnp.int32, z.shape, 0)
    prev_row = jnp.where(ti > 0, zp_ref[0, 7:8, :], 0.0)
    next_row = jnp.where(ti < nt - 1, zn_ref[0, 0:1, :], 0.0)
    z_prev = jnp.where(row == 0, prev_row, pltpu.roll(z, 1, 0))
    z_next = jnp.where(row == tl - 1, next_row, pltpu.roll(z, tl - 1, 0))
    cw = cw_ref[...]
    y = cb_ref[...] + z_prev * cw[0:1] + z * cw[1:2] + z_next * cw[2:3]
    x0, x1, v = y[:, :HY_WIDTH], y[:, HY_WIDTH:2 * HY_WIDTH], y[:, 2 * HY_WIDTH:]
    w = v * x1
    wb_ref[...] = w.astype(BF16)
    wf_ref[0] = w
    x0_ref[0] = x0


def _hy_fwd_kernel(c_ref, s_ref, x_ref, gre_ref, gim_ref, yre_ref, yim_ref, *, reps):
    x = x_ref[...]
    a = _dot(c_ref[...], x)
    b = _dot(s_ref[...], x)
    gre = jnp.concatenate([gre_ref[...]] * reps, axis=1)
    gim = jnp.concatenate([gim_ref[...]] * reps, axis=1)
    yre_ref[...] = (a * gre + b * gim).astype(BF16)
    yim_ref[...] = (a * gim - b * gre).astype(BF16)


def _hy_inv_kernel(ci_ref, si_ref, yre_ref, yim_ref, wf_ref, x0_ref, skip_ref, o_ref, *, reps):
    y = _dot(ci_ref[...], yre_ref[...]) - _dot(si_ref[...], yim_ref[...])
    for r in range(reps):
        yr = y[:, r * HY_WIDTH:(r + 1) * HY_WIDTH]
        o_ref[r] = ((yr + wf_ref[r] * skip_ref[...]) * x0_ref[r]).astype(BF16)


def _hyena(zhy, p, tabs):
    conv_w, conv_b, w1, b1, w2, b2, w3, freq, skip = p
    c_tab, s_tab, ci_tab, si_tab, feats, deltas = tabs
    b, l, _ = zhy.shape
    c = HY_WIDTH
    tl = min(l, 512)
    nt = l // tl
    hid = w2.shape[0]
    const1 = lambda i: (0, 0)
    w1p = jnp.zeros((LANES, hid), F32).at[:HY_EMB].set(w1)
    hs, hd = pl.pallas_call(
        _hy_filter_kernel,
        grid=(nt,),
        in_specs=[pl.BlockSpec((tl, LANES), lambda i: (i, 0)),
                  pl.BlockSpec((LANES, hid), const1), pl.BlockSpec((1, hid), const1),
                  pl.BlockSpec((hid, hid), const1), pl.BlockSpec((1, hid), const1),
                  pl.BlockSpec((hid, 2 * c), const1), pl.BlockSpec((1, hid), const1),
                  pl.BlockSpec((1, c), const1)],
        out_specs=[pl.BlockSpec((tl, c), lambda i: (i, 0))] * 2,
        out_shape=[jax.ShapeDtypeStruct((l, c), BF16)] * 2,
        compiler_params=_params("arbitrary"),
    )(feats, w1p, b1.reshape(1, hid), w2, b2.reshape(1, hid), w3, freq.reshape(1, hid), deltas)

    gre, gim = pl.pallas_call(
        functools.partial(_hy_spectrum_kernel, scale=1.0 / l),
        grid=(nt,),
        in_specs=[pl.BlockSpec((tl, l), lambda i: (i, 0)), pl.BlockSpec((tl, l), lambda i: (i, 0)),
                  pl.BlockSpec((l, c), const1), pl.BlockSpec((l, c), const1)],
        out_specs=[pl.BlockSpec((tl, c), lambda i: (i, 0))] * 2,
        out_shape=[jax.ShapeDtypeStruct((l, c), F32)] * 2,
        compiler_params=_params("arbitrary"),
    )(c_tab, s_tab, hs, hd)

    halo = 8
    wb, wf, x0 = pl.pallas_call(
        functools.partial(_hy_pre_kernel, nt=nt),
        grid=(b, nt),
        in_specs=[pl.BlockSpec((1, tl, 3 * c), lambda bi, i: (bi, i, 0)),
                  pl.BlockSpec((1, halo, 3 * c), lambda bi, i: (bi, jnp.maximum(i * (tl // halo) - 1, 0), 0)),
                  pl.BlockSpec((1, halo, 3 * c),
                               lambda bi, i: (bi, jnp.minimum((i + 1) * (tl // halo), l // halo - 1), 0)),
                  pl.BlockSpec((3, 3 * c), lambda bi, i: (0, 0)),
                  pl.BlockSpec((1, 3 * c), lambda bi, i: (0, 0))],
        out_specs=[pl.BlockSpec((tl, c), lambda bi, i: (i, bi)),
                   pl.BlockSpec((1, tl, c), lambda bi, i: (bi, i, 0)),
                   pl.BlockSpec((1, tl, c), lambda bi, i: (bi, i, 0))],
        out_shape=[jax.ShapeDtypeStruct((l, b * c), BF16),
                   jax.ShapeDtypeStruct((b, l, c), F32),
                   jax.ShapeDtypeStruct((b, l, c), F32)],
        compiler_params=_params("arbitrary", "arbitrary"),
    )(zhy, zhy, zhy, conv_w, conv_b.reshape(1, 3 * c))

    reps = 2
    tn = reps * c
    nj = b * c // tn
    yre, yim = pl.pallas_call(
        functools.partial(_hy_fwd_kernel, reps=reps),
        grid=(nj, nt),
        in_specs=[pl.BlockSpec((tl, l), lambda j, i: (i, 0)), pl.BlockSpec((tl, l), lambda j, i: (i, 0)),
                  pl.BlockSpec((l, tn), lambda j, i: (0, j)),
                  pl.BlockSpec((tl, c), lambda j, i: (i, 0)), pl.BlockSpec((tl, c), lambda j, i: (i, 0))],
        out_specs=[pl.BlockSpec((tl, tn), lambda j, i: (i, j))] * 2,
        out_shape=[jax.ShapeDtypeStruct((l, b * c), BF16)] * 2,
        compiler_params=_params("arbitrary", "arbitrary"),
    )(c_tab, s_tab, wb, gre, gim)

    return pl.pallas_call(
        functools.partial(_hy_inv_kernel, reps=reps),
        grid=(nj, nt),
        in_specs=[pl.BlockSpec((tl, l), lambda j, i: (i, 0)), pl.BlockSpec((tl, l), lambda j, i: (i, 0)),
                  pl.BlockSpec((l, tn), lambda j, i: (0, j)), pl.BlockSpec((l, tn), lambda j, i: (0, j)),
                  pl.BlockSpec((reps, tl, c), lambda j, i: (j, i, 0)),
                  pl.BlockSpec((reps, tl, c), lambda j, i: (j, i, 0)),
                  pl.BlockSpec((1, c), lambda j, i: (0, 0))],
        out_specs=pl.BlockSpec((reps, tl, c), lambda j, i: (j, i, 0)),
        out_shape=jax.ShapeDtypeStruct((b, l, c), BF16),
        compiler_params=_params("arbitrary", "arbitrary"),
    )(ci_tab, si_tab, yre, yim, wf, x0, skip.reshape(1, c))


def _lane_mask(width, shift, idx):
    lane = lax.broadcasted_iota(jnp.int32, (1, width), 1)
    return jnp.where((lane >> shift) == idx, 1.0, 0.0).astype(BF16)


ATTN_WIDTH = 256
ATTN_CHUNK = 512


def _key_chunks(lk):
    return [(c0, min(ATTN_CHUNK, lk - c0)) for c0 in range(0, lk, ATTN_CHUNK)]


ATTN_STACK = 4
ATTN_TQ = 128
ATTN_STEP_ROWS = 512


def _qk_stage(q, shift, first, kt_ref, qs_scr, s_scr):
    tq = q.shape[0]
    for i in range(ATTN_STACK):
        qs_scr[i * tq:(i + 1) * tq, :] = q * _lane_mask(ATTN_WIDTH, shift, first + i)
    qs = qs_scr[...]
    m = jnp.full((ATTN_STACK * tq, LANES), -jnp.inf, F32)
    for c0, ck in _key_chunks(kt_ref.shape[-1]):
        s = _dot(qs, kt_ref[:, c0:c0 + ck])
        s_scr[:, c0:c0 + ck] = s
        for j in range(ck // LANES):
            m = jnp.maximum(m, s[:, j * LANES:(j + 1) * LANES])
    return jnp.max(m, axis=-1, keepdims=True)


def _exp_stage(s_scr, mx, p_scr):
    l = jnp.zeros((s_scr.shape[0], LANES), F32)
    for c0, ck in _key_chunks(s_scr.shape[-1]):
        e = jnp.exp2(s_scr[:, c0:c0 + ck] - mx)
        p_scr[:, c0:c0 + ck] = e.astype(BF16)
        for j in range(ck // LANES):
            l = l + e[:, j * LANES:(j + 1) * LANES]
    return 1.0 / jnp.sum(l, axis=-1, keepdims=True)


def _attention_passes(passes, consume, kt_ref, v_ref, scratch):
    qs, ss, ps = scratch[0:2], scratch[2:4], scratch[4:6]
    tq = passes[0][0].shape[0]

    def qk(i):
        q, shift, first, g = passes[i]
        return _qk_stage(q, shift, first, kt_ref.at[0, g], qs[i % 2], ss[i % 2])

    mx = qk(0)
    for i in range(len(passes)):
        mx_next = qk(i + 1) if i + 1 < len(passes) else None
        inv = _exp_stage(ss[i % 2], mx, ps[i % 2])
        pv = _dot(ps[i % 2][...], v_ref[0, passes[i][3]]) * inv
        consume(i, [pv[r * tq:(r + 1) * tq] for r in range(ATTN_STACK)])
        mx = mx_next


def _diff_kernel(lqk_ref, q_ref, kt_ref, v_ref, g_ref, bd_ref, o_ref, *scratch, lam_init):
    x = lqk_ref[...]
    lam = (jnp.exp(jnp.sum(x[0:1] * x[1:2], axis=-1, keepdims=True))
           - jnp.exp(jnp.sum(x[2:3] * x[3:4], axis=-1, keepdims=True)) + lam_init)
    tq = min(ATTN_TQ, q_ref.shape[1])
    group = lax.broadcasted_iota(jnp.int32, (tq, ATTN_WIDTH), 1) >> 6
    halves = 2 * DIFF_HEADS // ATTN_STACK
    passes = []
    for t in range(q_ref.shape[1] // tq):
        q = q_ref[0, t * tq:(t + 1) * tq, :]
        passes += [(q, 5, half * ATTN_STACK, 0) for half in range(halves)]
    acc = {}

    def consume(i, pvs):
        t, half = divmod(i, halves)
        o1, o2 = acc.pop(t, (jnp.zeros(group.shape, F32),) * 2)
        for r, pv in enumerate(pvs):
            head = (half * ATTN_STACK + r) >> 1
            if r & 1 == 0:
                o1 = jnp.where(group == head, pv, o1)
            else:
                o2 = jnp.where(group == head, pv, o2)
        if half + 1 < halves:
            acc[t] = (o1, o2)
            return
        o = o1 - lam * o2
        ms = _group_mean_sq(o, bd_ref[...], DIFF_V_DIM)
        o = o * lax.rsqrt(ms + NORM_EPS) * (g_ref[...] * (1.0 - lam_init))
        o_ref[0, t * tq:(t + 1) * tq, :] = o.astype(BF16)

    _attention_passes(passes, consume, kt_ref, v_ref, scratch)


def _gqa_kernel(q_ref, kt_ref, v_ref, o_ref, *scratch):
    tq = min(ATTN_TQ, q_ref.shape[1])
    group = lax.broadcasted_iota(jnp.int32, (tq, ATTN_WIDTH), 1) >> 6
    groups = q_ref.shape[2] // ATTN_WIDTH
    passes = []
    for t in range(q_ref.shape[1] // tq):
        for g in range(groups):
            passes.append((q_ref[0, t * tq:(t + 1) * tq, g * ATTN_WIDTH:(g + 1) * ATTN_WIDTH], 6, 0, g))

    def consume(i, pvs):
        t, g = divmod(i, groups)
        o = jnp.zeros(group.shape, F32)
        for r, pv in enumerate(pvs):
            o = jnp.where(group == r, pv, o)
        o_ref[0, t * tq:(t + 1) * tq, g * ATTN_WIDTH:(g + 1) * ATTN_WIDTH] = o.astype(BF16)

    _attention_passes(passes, consume, kt_ref, v_ref, scratch)


def _attention(kernel_fn, q, kt, v, extra=()):
    b, lq, width = q.shape
    groups, w, lk = kt.shape[1:]
    rows = min(lq, ATTN_STEP_ROWS)
    pass_rows = ATTN_STACK * min(ATTN_TQ, rows)
    whole = lambda a: pl.BlockSpec(a.shape, lambda bi, i: (0, 0))
    return pl.pallas_call(
        kernel_fn,
        grid=(b, lq // rows),
        in_specs=[whole(a) for a in extra[:1]]
        + [pl.BlockSpec((1, rows, width), lambda bi, i: (bi, i, 0)),
           pl.BlockSpec((1, groups, w, lk), lambda bi, i: (bi, 0, 0, 0)),
           pl.BlockSpec((1, groups, lk, w), lambda bi, i: (bi, 0, 0, 0))]
        + [whole(a) for a in extra[1:]],
        out_specs=pl.BlockSpec((1, rows, width), lambda bi, i: (bi, i, 0)),
        out_shape=jax.ShapeDtypeStruct((b, lq, width), BF16),
        scratch_shapes=[pltpu.VMEM((pass_rows, w), BF16)] * 2 + [pltpu.VMEM((pass_rows, lk), F32)] * 2
        + [pltpu.VMEM((pass_rows, lk), BF16)] * 2,
        compiler_params=_params("arbitrary", "arbitrary"),
    )(*extra[:1], q, kt, v, *extra[1:])


def _diff_attention(lqk, q, kt, v, g256, bd256, lam_init):
    return _attention(functools.partial(_diff_kernel, lam_init=lam_init), q, kt, v, (lqk, g256, bd256))


def _gqa_attention(q, kt, v):
    return _attention(_gqa_kernel, q, kt, v)


def _top2_gates(logits):
    lane = lax.broadcasted_iota(jnp.int32, logits.shape, 1).astype(F32)
    neg = jnp.float32(-jnp.inf)
    lg = jnp.where(lane < N_EXPERTS, logits, neg)
    m1 = jnp.max(lg, axis=-1, keepdims=True)
    i1 = jnp.min(jnp.where(lg == m1, lane, float(LANES)), axis=-1, keepdims=True)
    lg2 = jnp.where(lane == i1, neg, lg)
    m2 = jnp.max(lg2, axis=-1, keepdims=True)
    i2 = jnp.min(jnp.where(lg2 == m2, lane, float(LANES)), axis=-1, keepdims=True)
    e2 = jnp.exp(m2 - m1)
    w1 = 1.0 / (1.0 + e2)
    return jnp.where(lane == i1, w1, 0.0) + jnp.where(lane == i2, e2 * w1, 0.0)


def _outproj_kernel(*refs, moe):
    yhy_ref, od_ref, og_ref, h_ref, gt_ref, w_ref, gf_ref, sc_ref, sh_ref = refs[:9]
    if moe:
        rt_ref, hn_ref, u_ref, gates_ref = refs[9:]
    else:
        hn_ref, u_ref = refs[9:]
    y = (_dot(yhy_ref[0], w_ref[0:HY_WIDTH, :])
         + _dot(od_ref[0], w_ref[HY_WIDTH:HY_WIDTH + DIFF_WIDTH, :])
         + _dot(og_ref[0], w_ref[HY_WIDTH + DIFF_WIDTH:, :]))
    hn = h_ref[0] + gt_ref[0] * y
    hn_ref[0] = hn
    u = _rms(hn, gf_ref[...]) * (1.0 + sc_ref[0]) + sh_ref[0]
    u_ref[0] = u.astype(BF16)
    if moe:
        gates_ref[0] = _top2_gates(_dot_hi(u, rt_ref[...]))


def _outproj(yhy, od, og, h, gt, w_bf, gf, sc, sh, router_pad):
    b, l, d = h.shape
    tm = min(l, 512)
    row = lambda bi, i: (bi, i, 0)
    vec = lambda bi, i: (bi, 0, 0)
    const = lambda bi, i: (0, 0)
    moe = router_pad is not None
    in_specs = [pl.BlockSpec((1, tm, HY_WIDTH), row), pl.BlockSpec((1, tm, DIFF_WIDTH), row),
                pl.BlockSpec((1, tm, GQA_WIDTH), row), pl.BlockSpec((1, tm, d), row),
                pl.BlockSpec((1, 1, d), vec), pl.BlockSpec((d, d), const), pl.BlockSpec((1, d), const),
                pl.BlockSpec((1, 1, d), vec), pl.BlockSpec((1, 1, d), vec)]
    args = [yhy, od, og, h, gt, w_bf, gf, sc, sh]
    out_specs = [pl.BlockSpec((1, tm, d), row), pl.BlockSpec((1, tm, d), row)]
    out_shape = [jax.ShapeDtypeStruct((b, l, d), F32), jax.ShapeDtypeStruct((b, l, d), BF16)]
    if moe:
        in_specs.append(pl.BlockSpec((d, LANES), const))
        args.append(router_pad)
        out_specs.append(pl.BlockSpec((1, tm, LANES), row))
        out_shape.append(jax.ShapeDtypeStruct((b, l, LANES), F32))
    return pl.pallas_call(
        functools.partial(_outproj_kernel, moe=moe),
        grid=(b, l // tm), in_specs=in_specs, out_specs=out_specs, out_shape=out_shape,
        compiler_params=_params("arbitrary", "arbitrary"),
    )(*args)


def _ffn_kernel(u_ref, wg_ref, wu_ref, wd_ref, h_ref, gt_ref, o_ref, acc_ref):
    f = pl.program_id(2)

    @pl.when(f == 0)
    def _():
        acc_ref[...] = jnp.zeros_like(acc_ref)

    u = u_ref[0]
    hid = _silu(_dot(u, wg_ref[...])) * _dot(u, wu_ref[...])
    acc_ref[...] += _dot(hid.astype(BF16), wd_ref[...])

    @pl.when(f == pl.num_programs(2) - 1)
    def _():
        o_ref[0] = h_ref[0] + gt_ref[0] * acc_ref[...]


def _ffn(u, wg, wu, wd, h, gt):
    b, l, d = h.shape
    dff = wg.shape[1]
    tm = min(l, 512)
    tf = dff // 2
    row = lambda bi, i, f: (bi, i, 0)
    return pl.pallas_call(
        _ffn_kernel,
        grid=(b, l // tm, dff // tf),
        in_specs=[pl.BlockSpec((1, tm, d), row),
                  pl.BlockSpec((d, tf), lambda bi, i, f: (0, f)), pl.BlockSpec((d, tf), lambda bi, i, f: (0, f)),
                  pl.BlockSpec((tf, d), lambda bi, i, f: (f, 0)),
                  pl.BlockSpec((1, tm, d), row), pl.BlockSpec((1, 1, d), lambda bi, i, f: (bi, 0, 0))],
        out_specs=pl.BlockSpec((1, tm, d), row),
        out_shape=jax.ShapeDtypeStruct((b, l, d), F32),
        scratch_shapes=[pltpu.VMEM((tm, d), F32)],
        compiler_params=_params("arbitrary", "arbitrary", "arbitrary"),
    )(u, wg, wu, wd, h, gt)


def _moe_kernel(u_ref, gates_ref, wg_ref, wu_ref, wd_ref, h_ref, gt_ref, o_ref, acc_ref):
    e, f = pl.program_id(2), pl.program_id(3)

    @pl.when((e == 0) & (f == 0))
    def _():
        acc_ref[...] = jnp.zeros_like(acc_ref)

    u = u_ref[0]
    gates = gates_ref[0]
    lane = lax.broadcasted_iota(jnp.int32, gates.shape, 1)
    gate = jnp.sum(jnp.where(lane == e, gates, 0.0), axis=-1, keepdims=True)
    hid = _silu(_dot(u, wg_ref[0])) * _dot(u, wu_ref[0]) * gate
    acc_ref[...] += _dot(hid.astype(BF16), wd_ref[0])

    @pl.when((e == pl.num_programs(2) - 1) & (f == pl.num_programs(3) - 1))
    def _():
        o_ref[0] = h_ref[0] + gt_ref[0] * acc_ref[...]


def _moe(u, gates, wg, wu, wd, h, gt):
    b, l, d = h.shape
    n_e, _, dff = wg.shape
    tm = min(l, 1024)
    tf = dff // 4
    row = lambda bi, i, e, f: (bi, i, 0)
    return pl.pallas_call(
        _moe_kernel,
        grid=(b, l // tm, n_e, dff // tf),
        in_specs=[pl.BlockSpec((1, tm, d), row), pl.BlockSpec((1, tm, LANES), row),
                  pl.BlockSpec((1, d, tf), lambda bi, i, e, f: (e, 0, f)),
                  pl.BlockSpec((1, d, tf), lambda bi, i, e, f: (e, 0, f)),
                  pl.BlockSpec((1, tf, d), lambda bi, i, e, f: (e, f, 0)),
                  pl.BlockSpec((1, tm, d), row), pl.BlockSpec((1, 1, d), lambda bi, i, e, f: (bi, 0, 0))],
        out_specs=pl.BlockSpec((1, tm, d), row),
        out_shape=jax.ShapeDtypeStruct((b, l, d), F32),
        scratch_shapes=[pltpu.VMEM((tm, d), F32)],
        compiler_params=_params("arbitrary", "arbitrary", "arbitrary", "arbitrary"),
    )(u, gates, wg, wu, wd, h, gt)


def _final_norm_kernel(h_ref, g_ref, o_ref):
    o_ref[0] = _rms(h_ref[0], g_ref[...])


def _final_norm(h, g):
    b, l, d = h.shape
    tm = min(l, 1024)
    return pl.pallas_call(
        _final_norm_kernel,
        grid=(b, l // tm),
        in_specs=[pl.BlockSpec((1, tm, d), lambda bi, i: (bi, i, 0)), pl.BlockSpec((1, d), lambda bi, i: (0, 0))],
        out_specs=pl.BlockSpec((1, tm, d), lambda bi, i: (bi, i, 0)),
        out_shape=jax.ShapeDtypeStruct((b, l, d), F32),
        compiler_params=_params("arbitrary", "arbitrary"),
    )(h, g.reshape(1, d))


def _rope_tables(length, head_dim):
    t = jnp.arange(length)
    row = (t // GRID_W).astype(F32)
    col = (t % GRID_W).astype(F32)
    n = head_dim // 4
    inv = ROPE_THETA ** (-jnp.arange(n, dtype=F32) / n)
    ang = jnp.concatenate([row[:, None] * inv, col[:, None] * inv], axis=-1)
    cos = jnp.concatenate([jnp.cos(ang)] * 2, axis=-1)
    sin = jnp.concatenate([-jnp.sin(ang), jnp.sin(ang)], axis=-1)
    reps = LANES // head_dim
    return jnp.tile(cos, (1, reps)), jnp.tile(sin, (1, reps))


def _hyena_tables(length):
    k = jnp.arange(length, dtype=jnp.int32)
    period = 4 * length
    step = 2.0 * math.pi / period

    def tab(a, bidx, fn):
        return fn(((a * bidx) % period).astype(F32) * step).astype(BF16)

    odd = 2 * k + 1
    c_tab = tab(odd[:, None], k[None, :], jnp.cos)
    s_tab = tab(odd[:, None], k[None, :], jnp.sin)
    ci_tab = tab(k[:, None], odd[None, :], jnp.cos)
    si_tab = tab(k[:, None], odd[None, :], jnp.sin)
    t = jnp.linspace(0.0, 1.0, length, dtype=F32)[:, None]
    bands = (HY_EMB - 1) // 2
    ang = (2.0 * math.pi / length) * jnp.arange(length, dtype=F32)[:, None] \
        * jnp.linspace(1e-4, bands - 1, bands, dtype=F32)
    feats = jnp.concatenate([t, jnp.cos(ang), -jnp.sin(ang),
                             jnp.zeros((length, LANES - HY_EMB), F32)], axis=-1)
    max_decay = math.log(HY_DECAY_TARGET) / HY_FAST_DECAY
    min_decay = math.log(HY_DECAY_TARGET) / HY_SLOW_DECAY
    deltas = jnp.abs(jnp.linspace(min_decay, max_decay, HY_WIDTH, dtype=F32)).reshape(1, HY_WIDTH)
    return c_tab, s_tab, ci_tab, si_tab, feats, deltas


def _block_diag_ones(n, group):
    i = jnp.arange(n) // group
    return (i[:, None] == i[None, :]).astype(BF16)


def kernel(x, c, ctx, c_ctx, w_ada, b_ada, g_mix, g_ffn, w_in, w_out, hy_conv_w, hy_conv_b, hy_w1, hy_b1, hy_w2, hy_b2, hy_w3, hy_freq, hy_skip, diff_lq1, diff_lk1, diff_lq2, diff_lk2, diff_subln, gqa_qnorm, gqa_knorm, ffn_wg, ffn_wu, ffn_wd, moe_router, moe_wg, moe_wu, moe_wd, g_final):
    b, seq, d = x.shape
    n_ctx = ctx.shape[1]
    depth = w_ada.shape[0]

    rope_tabs = _rope_tables(seq, DIFF_QK_DIM) + _rope_tables(seq, GQA_HEAD_DIM)
    hy_tabs = _hyena_tables(seq)
    hy_tabs_c = _hyena_tables(n_ctx)
    bd128 = _block_diag_ones(LANES, GQA_HEAD_DIM)
    bd256 = _block_diag_ones(DIFF_WIDTH, DIFF_V_DIM)

    rows = 16
    cc = jnp.zeros((rows, d), F32).at[:b].set(c).at[b].set(c_ctx)
    mods = _mods(cc, w_ada, b_ada)

    def mod_vecs(i):
        lat = [mods[i, :b, j * d:(j + 1) * d].reshape(b, 1, d) for j in range(6)]
        cx = [jnp.broadcast_to(mods[i, b, j * d:(j + 1) * d].reshape(1, 1, d), (b, 1, d)) for j in range(6)]
        return lat, cx

    def kv_layouts(kd, vd, kg, vg):
        l = kd.shape[1]
        kdt = jnp.swapaxes(kd, 1, 2)[:, None]
        vd = vd[:, None]
        kgt = jnp.swapaxes(kg, 1, 2).reshape(b, 2, 1, GQA_HEAD_DIM, l)
        kgt = jnp.broadcast_to(kgt, (b, 2, GQA_REP, GQA_HEAD_DIM, l)).reshape(b, 2, GQA_REP * GQA_HEAD_DIM, l)
        vgx = jnp.swapaxes(vg.reshape(b, l, 2, 1, GQA_HEAD_DIM), 1, 2)
        vgx = jnp.broadcast_to(vgx, (b, 2, l, GQA_REP, GQA_HEAD_DIM)).reshape(b, 2, l, GQA_REP * GQA_HEAD_DIM)
        return kdt, vd, kgt, vgx

    h, hc = x, ctx
    for i in range(depth):
        last = i == depth - 1
        lam_init = 0.8 - 0.6 * math.exp(-0.3 * i)
        (sh_m, sc_m, gt_m, sh_f, sc_f, gt_f), (csh_m, csc_m, cgt_m, csh_f, csc_f, cgt_f) = mod_vecs(i)
        w_in_bf = w_in[i].astype(BF16)
        w_out_bf = w_out[i].astype(BF16)
        g_m = g_mix[i].reshape(1, d)
        g_f = g_ffn[i].reshape(1, d)
        qn = jnp.tile(gqa_qnorm[i], LANES // GQA_HEAD_DIM).reshape(1, LANES)
        kn = jnp.tile(gqa_knorm[i], LANES // GQA_HEAD_DIM).reshape(1, LANES)
        subln = jnp.tile(diff_subln[i], DIFF_HEADS).reshape(1, DIFF_WIDTH)
        lqk = jnp.stack([diff_lq1[i], diff_lk1[i], diff_lq2[i], diff_lk2[i]]).astype(F32)
        hy_p = (hy_conv_w[i], hy_conv_b[i], hy_w1[i], hy_b1[i], hy_w2[i], hy_b2[i], hy_w3[i],
                hy_freq[i], hy_skip[i])
        moe = i % 2 == 1
        router_pad = None
        if moe:
            router_pad = jnp.zeros((d, LANES), F32).at[:, :N_EXPERTS].set(moe_router[i // 2])

        zhy, qd, kd, vd, qg, kg, vg = _inproj(h, sc_m, sh_m, g_m, w_in_bf, qn, kn, bd128, rope_tabs)
        czhy, cqd, ckd, cvd, cqg, ckg, cvg = _inproj(hc, csc_m, csh_m, g_m, w_in_bf, qn, kn, bd128, None)
        kdt, vdx, kgt, vgx = kv_layouts(kd, vd, kg, vg)
        ckdt, cvdx, ckgt, cvgx = kv_layouts(ckd, cvd, ckg, cvg)

        cat = lambda a, c_, axis: jnp.concatenate([a, c_], axis=axis)
        y_hy = _hyena(zhy, hy_p, hy_tabs)
        o_d = _diff_attention(lqk, qd, cat(kdt, ckdt, 3), cat(vdx, cvdx, 2), subln, bd256, lam_init)
        o_g = _gqa_attention(qg, cat(kgt, ckgt, 3), cat(vgx, cvgx, 2))
        mixed = _outproj(y_hy, o_d, o_g, h, gt_m, w_out_bf, g_f, sc_f, sh_f, router_pad)
        if not last:
            yc_hy = _hyena(czhy, hy_p, hy_tabs_c)
            oc_d = _diff_attention(lqk, cqd, ckdt, cvdx, subln, bd256, lam_init)
            oc_g = _gqa_attention(cqg, ckgt, cvgx)
            cmixed = _outproj(yc_hy, oc_d, oc_g, hc, cgt_m, w_out_bf, g_f, csc_f, csh_f, router_pad)

        if moe:
            wg, wu, wd = (w[i // 2].astype(BF16) for w in (moe_wg, moe_wu, moe_wd))
            h = _moe(mixed[1], mixed[2], wg, wu, wd, mixed[0], gt_f)
            if not last:
                hc = _moe(cmixed[1], cmixed[2], wg, wu, wd, cmixed[0], cgt_f)
        else:
            wg, wu, wd = (w[i // 2].astype(BF16) for w in (ffn_wg, ffn_wu, ffn_wd))
            h = _ffn(mixed[1], wg, wu, wd, mixed[0], gt_f)
            if not last:
                hc = _ffn(cmixed[1], wg, wu, wd, cmixed[0], cgt_f)
    return _final_norm(h, g_final)
```

```python
import functools
import math

import jax
import jax.numpy as jnp
from jax import lax
from jax.experimental import pallas as pl
from jax.experimental.pallas import tpu as pltpu
from jax.experimental.pallas import tpu_sc as plsc

F32 = jnp.float32
BF16 = jnp.bfloat16
HIGHEST = lax.Precision.HIGHEST

D_MODEL = 1024
GRID_W = 64
ROPE_THETA = 10000.0
NORM_EPS = 1e-6

HY_WIDTH = 256
HY_EMB = 33
HY_FAST_DECAY = 0.3
HY_SLOW_DECAY = 1.5
HY_DECAY_TARGET = 1e-2

DIFF_HEADS = 4
DIFF_QK_DIM = 32
DIFF_V_DIM = 64
DIFF_WIDTH = 256
GQA_HEAD_DIM = 64
GQA_WIDTH = 512
GQA_KV_WIDTH = 128
GQA_REP = 4

OFF_DQ = 3 * HY_WIDTH
OFF_DK = OFF_DQ + DIFF_WIDTH
OFF_DV = OFF_DK + DIFF_WIDTH
OFF_GQ = OFF_DV + DIFF_WIDTH
OFF_GK = OFF_GQ + GQA_WIDTH
OFF_GV = OFF_GK + GQA_KV_WIDTH
IN_COLS = OFF_GV + GQA_KV_WIDTH

N_EXPERTS = 8
LOG2_E = math.log2(math.e)
LANES = 128
VMEM_LIMIT = 56 * 1024 * 1024


def _params(*sem):
    return pltpu.CompilerParams(dimension_semantics=sem, vmem_limit_bytes=VMEM_LIMIT)


def _dot(a, b):
    return jnp.dot(a, b, preferred_element_type=F32)


def _dot_hi(a, b):
    return jnp.dot(a, b, preferred_element_type=F32, precision=HIGHEST)


def _rms(x, g):
    ms = jnp.mean(x * x, axis=-1, keepdims=True)
    return x * lax.rsqrt(ms + NORM_EPS) * g


def _silu(x):
    return x * jax.nn.sigmoid(x)


def _group_mean_sq(x, ones_bd, width):
    return _dot((x * x).astype(BF16), ones_bd) * (1.0 / width)


def _mods_kernel(c_ref, w_ref, b_ref, o_ref):
    o_ref[0] = _dot_hi(_silu(c_ref[...]), w_ref[0]) + b_ref[0]


def _mods(cc, w_ada, b_ada):
    depth, d, n = w_ada.shape
    tn = 1536
    return pl.pallas_call(
        _mods_kernel,
        grid=(depth, n // tn),
        in_specs=[pl.BlockSpec(cc.shape, lambda i, j: (0, 0)),
                  pl.BlockSpec((1, d, tn), lambda i, j: (i, 0, j)),
                  pl.BlockSpec((1, 1, tn), lambda i, j: (i, 0, j))],
        out_specs=pl.BlockSpec((1, cc.shape[0], tn), lambda i, j: (i, 0, j)),
        out_shape=jax.ShapeDtypeStruct((depth, cc.shape[0], n), F32),
        compiler_params=_params("arbitrary", "arbitrary"),
    )(cc, w_ada, b_ada.reshape(depth, 1, n))


def _rope128(x, cos, sin_signed, half):
    lane = lax.broadcasted_iota(jnp.int32, x.shape, 1)
    first = (lane & (2 * half - 1)) < half
    swapped = jnp.where(first, pltpu.roll(x, LANES - half, 1), pltpu.roll(x, half, 1))
    return x * cos + swapped * sin_signed


def _inproj_kernel(*refs, rope):
    h_ref, sc_ref, sh_ref, g_ref, w_ref, qn_ref, kn_ref, bd_ref = refs[:8]
    if rope:
        cd_ref, sd_ref, cg_ref, sg_ref = refs[8:12]
    zhy_ref, qd_ref, kd_ref, vd_ref, qg_ref, kg_ref, vg_ref = refs[-7:]
    u = _rms(h_ref[0], g_ref[...]) * (1.0 + sc_ref[0]) + sh_ref[0]
    z = _dot(u.astype(BF16), w_ref[...])
    zhy_ref[0] = z[:, :OFF_DQ]

    def piece(off, j):
        return z[:, off + LANES * j: off + LANES * (j + 1)]

    for j in range(DIFF_WIDTH // LANES):
        q, k = piece(OFF_DQ, j), piece(OFF_DK, j)
        if rope:
            q = _rope128(q, cd_ref[...], sd_ref[...], DIFF_QK_DIM // 2)
            k = _rope128(k, cd_ref[...], sd_ref[...], DIFF_QK_DIM // 2)
        qd_ref[0, :, LANES * j: LANES * (j + 1)] = (q * (LOG2_E * DIFF_QK_DIM ** -0.5)).astype(BF16)
        kd_ref[0, :, LANES * j: LANES * (j + 1)] = k.astype(BF16)
    vd_ref[0] = z[:, OFF_DV:OFF_GQ].astype(BF16)

    def gqa_piece(x, gain):
        ms = _group_mean_sq(x, bd_ref[...], GQA_HEAD_DIM)
        x = x * lax.rsqrt(ms + NORM_EPS) * gain
        if rope:
            x = _rope128(x, cg_ref[...], sg_ref[...], GQA_HEAD_DIM // 2)
        return x

    for j in range(GQA_WIDTH // LANES):
        q = gqa_piece(piece(OFF_GQ, j), qn_ref[...])
        qg_ref[0, :, LANES * j: LANES * (j + 1)] = (q * (LOG2_E * GQA_HEAD_DIM ** -0.5)).astype(BF16)
    kg_ref[0] = gqa_piece(piece(OFF_GK, 0), kn_ref[...]).astype(BF16)
    vg_ref[0] = z[:, OFF_GV:].astype(BF16)


def _inproj(h, sc, sh, g, w_bf, qn, kn, bd, rope_tabs):
    b, l, d = h.shape
    tm = min(l, 512)
    row = lambda bi, i: (bi, i, 0)
    vec = lambda bi, i: (bi, 0, 0)
    const = lambda bi, i: (0, 0)
    in_specs = [pl.BlockSpec((1, tm, d), row), pl.BlockSpec((1, 1, d), vec), pl.BlockSpec((1, 1, d), vec),
                pl.BlockSpec((1, d), const), pl.BlockSpec((d, IN_COLS), const),
                pl.BlockSpec((1, LANES), const), pl.BlockSpec((1, LANES), const),
                pl.BlockSpec((LANES, LANES), const)]
    args = [h, sc, sh, g, w_bf, qn, kn, bd]
    if rope_tabs is not None:
        in_specs += [pl.BlockSpec((tm, LANES), lambda bi, i: (i, 0))] * 4
        args += list(rope_tabs)
    widths = [(OFF_DQ, F32), (DIFF_WIDTH, BF16), (DIFF_WIDTH, BF16), (DIFF_WIDTH, BF16),
              (GQA_WIDTH, BF16), (GQA_KV_WIDTH, BF16), (GQA_KV_WIDTH, BF16)]
    return pl.pallas_call(
        functools.partial(_inproj_kernel, rope=rope_tabs is not None),
        grid=(b, l // tm),
        in_specs=in_specs,
        out_specs=[pl.BlockSpec((1, tm, w), row) for w, _ in widths],
        out_shape=[jax.ShapeDtypeStruct((b, l, w), dt) for w, dt in widths],
        compiler_params=_params("arbitrary", "arbitrary"),
    )(*args)


def _hy_filter_kernel(f_ref, w1_ref, b1_ref, w2_ref, b2_ref, w3_ref, fr_ref, dl_ref, hs_ref, hd_ref):
    f = f_ref[...]
    fr = fr_ref[...]
    a = jnp.sin(fr * (_dot_hi(f, w1_ref[...]) + b1_ref[...]))
    a = jnp.sin(fr * (_dot_hi(a, w2_ref[...]) + b2_ref[...]))
    hf = _dot_hi(a, w3_ref[...])
    decay = jnp.exp(-f[:, 0:1] * dl_ref[...])
    h_fwd = hf[:, :HY_WIDTH] * decay
    h_bwd = hf[:, HY_WIDTH:] * decay
    row = lax.broadcasted_iota(jnp.int32, h_bwd.shape, 0) + pl.program_id(0) * f.shape[0]
    h_bwd = jnp.where(row == 0, 0.0, h_bwd)
    hs_ref[...] = (h_fwd + h_bwd).astype(BF16)
    hd_ref[...] = (h_bwd - h_fwd).astype(BF16)


def _hy_spectrum_kernel(c_ref, s_ref, hs_ref, hd_ref, gre_ref, gim_ref, *, scale):
    gre_ref[...] = _dot(c_ref[...], hs_ref[...]) * scale
    gim_ref[...] = _dot(s_ref[...], hd_ref[...]) * scale


def _hy_pre_kernel(z_ref, zp_ref, zn_ref, cw_ref, cb_ref, wb_ref, wf_ref, x0_ref, *, nt):
    ti = pl.program_id(1)
    z = z_ref[0]
    tl = z.shape[0]
    row = lax.broadcasted_iota(jnp.int32, z.shape, 0)
    prev_row = jnp.where(ti > 0, zp_ref[0, 7:8, :], 0.0)
    next_row = jnp.where(ti < nt - 1, zn_ref[0, 0:1, :], 0.0)
    z_prev = jnp.where(row == 0, prev_row, pltpu.roll(z, 1, 0))
    z_next = jnp.where(row == tl - 1, next_row, pltpu.roll(z, tl - 1, 0))
    cw = cw_ref[...]
    y = cb_ref[...] + z_prev * cw[0:1] + z * cw[1:2] + z_next * cw[2:3]
    x0, x1, v = y[:, :HY_WIDTH], y[:, HY_WIDTH:2 * HY_WIDTH], y[:, 2 * HY_WIDTH:]
    w = v * x1
    wb_ref[...] = w.astype(BF16)
    wf_ref[0] = w
    x0_ref[0] = x0


def _hy_fwd_kernel(c_ref, s_ref, x_ref, gre_ref, gim_ref, yre_ref, yim_ref, *, reps):
    x = x_ref[...]
    a = _dot(c_ref[...], x)
    b = _dot(s_ref[...], x)
    gre = jnp.concatenate([gre_ref[...]] * reps, axis=1)
    gim = jnp.concatenate([gim_ref[...]] * reps, axis=1)
    yre_ref[...] = (a * gre + b * gim).astype(BF16)
    yim_ref[...] = (a * gim - b * gre).astype(BF16)


def _hy_inv_kernel(ci_ref, si_ref, yre_ref, yim_ref, wf_ref, x0_ref, skip_ref, o_ref, *, reps):
    y = _dot(ci_ref[...], yre_ref[...]) - _dot(si_ref[...], yim_ref[...])
    for r in range(reps):
        yr = y[:, r * HY_WIDTH:(r + 1) * HY_WIDTH]
        o_ref[r] = ((yr + wf_ref[r] * skip_ref[...]) * x0_ref[r]).astype(BF16)


def _hyena(zhy, p, tabs):
    conv_w, conv_b, w1, b1, w2, b2, w3, freq, skip = p
    c_tab, s_tab, ci_tab, si_tab, feats, deltas = tabs
    b, l, _ = zhy.shape
    c = HY_WIDTH
    tl = min(l, 512)
    nt = l // tl
    hid = w2.shape[0]
    const1 = lambda i: (0, 0)
    w1p = jnp.zeros((LANES, hid), F32).at[:HY_EMB].set(w1)
    hs, hd = pl.pallas_call(
        _hy_filter_kernel,
        grid=(nt,),
        in_specs=[pl.BlockSpec((tl, LANES), lambda i: (i, 0)),
                  pl.BlockSpec((LANES, hid), const1), pl.BlockSpec((1, hid), const1),
                  pl.BlockSpec((hid, hid), const1), pl.BlockSpec((1, hid), const1),
                  pl.BlockSpec((hid, 2 * c), const1), pl.BlockSpec((1, hid), const1),
                  pl.BlockSpec((1, c), const1)],
        out_specs=[pl.BlockSpec((tl, c), lambda i: (i, 0))] * 2,
        out_shape=[jax.ShapeDtypeStruct((l, c), BF16)] * 2,
        compiler_params=_params("arbitrary"),
    )(feats, w1p, b1.reshape(1, hid), w2, b2.reshape(1, hid), w3, freq.reshape(1, hid), deltas)

    gre, gim = pl.pallas_call(
        functools.partial(_hy_spectrum_kernel, scale=1.0 / l),
        grid=(nt,),
        in_specs=[pl.BlockSpec((tl, l), lambda i: (i, 0)), pl.BlockSpec((tl, l), lambda i: (i, 0)),
                  pl.BlockSpec((l, c), const1), pl.BlockSpec((l, c), const1)],
        out_specs=[pl.BlockSpec((tl, c), lambda i: (i, 0))] * 2,
        out_shape=[jax.ShapeDtypeStruct((l, c), F32)] * 2,
        compiler_params=_params("arbitrary"),
    )(c_tab, s_tab, hs, hd)

    halo = 8
    wb, wf, x0 = pl.pallas_call(
        functools.partial(_hy_pre_kernel, nt=nt),
        grid=(b, nt),
        in_specs=[pl.BlockSpec((1, tl, 3 * c), lambda bi, i: (bi, i, 0)),
                  pl.BlockSpec((1, halo, 3 * c), lambda bi, i: (bi, jnp.maximum(i * (tl // halo) - 1, 0), 0)),
                  pl.BlockSpec((1, halo, 3 * c),
                               lambda bi, i: (bi, jnp.minimum((i + 1) * (tl // halo), l // halo - 1), 0)),
                  pl.BlockSpec((3, 3 * c), lambda bi, i: (0, 0)),
                  pl.BlockSpec((1, 3 * c), lambda bi, i: (0, 0))],
        out_specs=[pl.BlockSpec((tl, c), lambda bi, i: (i, bi)),
                   pl.BlockSpec((1, tl, c), lambda bi, i: (bi, i, 0)),
                   pl.BlockSpec((1, tl, c), lambda bi, i: (bi, i, 0))],
        out_shape=[jax.ShapeDtypeStruct((l, b * c), BF16),
                   jax.ShapeDtypeStruct((b, l, c), F32),
                   jax.ShapeDtypeStruct((b, l, c), F32)],
        compiler_params=_params("arbitrary", "arbitrary"),
    )(zhy, zhy, zhy, conv_w, conv_b.reshape(1, 3 * c))

    reps = 2
    tn = reps * c
    nj = b * c // tn
    yre, yim = pl.pallas_call(
        functools.partial(_hy_fwd_kernel, reps=reps),
        grid=(nj, nt),
        in_specs=[pl.BlockSpec((tl, l), lambda j, i: (i, 0)), pl.BlockSpec((tl, l), lambda j, i: (i, 0)),
                  pl.BlockSpec((l, tn), lambda j, i: (0, j)),
                  pl.BlockSpec((tl, c), lambda j, i: (i, 0)), pl.BlockSpec((tl, c), lambda j, i: (i, 0))],
        out_specs=[pl.BlockSpec((tl, tn), lambda j, i: (i, j))] * 2,
        out_shape=[jax.ShapeDtypeStruct((l, b * c), BF16)] * 2,
        compiler_params=_params("arbitrary", "arbitrary"),
    )(c_tab, s_tab, wb, gre, gim)

    return pl.pallas_call(
        functools.partial(_hy_inv_kernel, reps=reps),
        grid=(nj, nt),
        in_specs=[pl.BlockSpec((tl, l), lambda j, i: (i, 0)), pl.BlockSpec((tl, l), lambda j, i: (i, 0)),
                  pl.BlockSpec((l, tn), lambda j, i: (0, j)), pl.BlockSpec((l, tn), lambda j, i: (0, j)),
                  pl.BlockSpec((reps, tl, c), lambda j, i: (j, i, 0)),
                  pl.BlockSpec((reps, tl, c), lambda j, i: (j, i, 0)),
                  pl.BlockSpec((1, c), lambda j, i: (0, 0))],
        out_specs=pl.BlockSpec((reps, tl, c), lambda j, i: (j, i, 0)),
        out_shape=jax.ShapeDtypeStruct((b, l, c), BF16),
        compiler_params=_params("arbitrary", "arbitrary"),
    )(ci_tab, si_tab, yre, yim, wf, x0, skip.reshape(1, c))


def _lane_mask(width, shift, idx):
    lane = lax.broadcasted_iota(jnp.int32, (1, width), 1)
    return jnp.where((lane >> shift) == idx, 1.0, 0.0).astype(BF16)


ATTN_WIDTH = 256
ATTN_CHUNK = 512


def _key_chunks(lk):
    return [(c0, min(ATTN_CHUNK, lk - c0)) for c0 in range(0, lk, ATTN_CHUNK)]


ATTN_STACK = 4
ATTN_TQ = 128
ATTN_STEP_ROWS = 512


def _qk_stage(q, shift, first, kt_ref, qs_scr, s_scr):
    tq = q.shape[0]
    for i in range(ATTN_STACK):
        qs_scr[i * tq:(i + 1) * tq, :] = q * _lane_mask(ATTN_WIDTH, shift, first + i)
    qs = qs_scr[...]
    m = jnp.full((ATTN_STACK * tq, LANES), -jnp.inf, F32)
    for c0, ck in _key_chunks(kt_ref.shape[-1]):
        s = _dot(qs, kt_ref[:, c0:c0 + ck])
        s_scr[:, c0:c0 + ck] = s
        for j in range(ck // LANES):
            m = jnp.maximum(m, s[:, j * LANES:(j + 1) * LANES])
    return jnp.max(m, axis=-1, keepdims=True)


def _exp_stage(s_scr, mx, p_scr):
    l = jnp.zeros((s_scr.shape[0], LANES), F32)
    for c0, ck in _key_chunks(s_scr.shape[-1]):
        e = jnp.exp2(s_scr[:, c0:c0 + ck] - mx)
        p_scr[:, c0:c0 + ck] = e.astype(BF16)
        for j in range(ck // LANES):
            l = l + e[:, j * LANES:(j + 1) * LANES]
    return 1.0 / jnp.sum(l, axis=-1, keepdims=True)


def _attention_passes(passes, consume, kt_ref, v_ref, scratch):
    qs, ss, ps = scratch[0:2], scratch[2:4], scratch[4:6]
    tq = passes[0][0].shape[0]

    def qk(i):
        q, shift, first, g = passes[i]
        return _qk_stage(q, shift, first, kt_ref.at[0, g], qs[i % 2], ss[i % 2])

    mx = qk(0)
    for i in range(len(passes)):
        mx_next = qk(i + 1) if i + 1 < len(passes) else None
        inv = _exp_stage(ss[i % 2], mx, ps[i % 2])
        pv = _dot(ps[i % 2][...], v_ref[0, passes[i][3]]) * inv
        consume(i, [pv[r * tq:(r + 1) * tq] for r in range(ATTN_STACK)])
        mx = mx_next


def _diff_kernel(lqk_ref, q_ref, kt_ref, v_ref, g_ref, bd_ref, o_ref, *scratch, lam_init):
    x = lqk_ref[...]
    lam = (jnp.exp(jnp.sum(x[0:1] * x[1:2], axis=-1, keepdims=True))
           - jnp.exp(jnp.sum(x[2:3] * x[3:4], axis=-1, keepdims=True)) + lam_init)
    tq = min(ATTN_TQ, q_ref.shape[1])
    group = lax.broadcasted_iota(jnp.int32, (tq, ATTN_WIDTH), 1) >> 6
    halves = 2 * DIFF_HEADS // ATTN_STACK
    passes = []
    for t in range(q_ref.shape[1] // tq):
        q = q_ref[0, t * tq:(t + 1) * tq, :]
        passes += [(q, 5, half * ATTN_STACK, 0) for half in range(halves)]
    acc = {}

    def consume(i, pvs):
        t, half = divmod(i, halves)
        o1, o2 = acc.pop(t, (jnp.zeros(group.shape, F32),) * 2)
        for r, pv in enumerate(pvs):
            head = (half * ATTN_STACK + r) >> 1
            if r & 1 == 0:
                o1 = jnp.where(group == head, pv, o1)
            else:
                o2 = jnp.where(group == head, pv, o2)
        if half + 1 < halves:
            acc[t] = (o1, o2)
            return
        o = o1 - lam * o2
        ms = _group_mean_sq(o, bd_ref[...], DIFF_V_DIM)
        o = o * lax.rsqrt(ms + NORM_EPS) * (g_ref[...] * (1.0 - lam_init))
        o_ref[0, t * tq:(t + 1) * tq, :] = o.astype(BF16)

    _attention_passes(passes, consume, kt_ref, v_ref, scratch)


def _gqa_kernel(q_ref, kt_ref, v_ref, o_ref, *scratch):
    tq = min(ATTN_TQ, q_ref.shape[1])
    group = lax.broadcasted_iota(jnp.int32, (tq, ATTN_WIDTH), 1) >> 6
    groups = q_ref.shape[2] // ATTN_WIDTH
    passes = []
    for t in range(q_ref.shape[1] // tq):
        for g in range(groups):
            passes.append((q_ref[0, t * tq:(t + 1) * tq, g * ATTN_WIDTH:(g + 1) * ATTN_WIDTH], 6, 0, g))

    def consume(i, pvs):
        t, g = divmod(i, groups)
        o = jnp.zeros(group.shape, F32)
        for r, pv in enumerate(pvs):
            o = jnp.where(group == r, pv, o)
        o_ref[0, t * tq:(t + 1) * tq, g * ATTN_WIDTH:(g + 1) * ATTN_WIDTH] = o.astype(BF16)

    _attention_passes(passes, consume, kt_ref, v_ref, scratch)


def _attention(kernel_fn, q, kt, v, extra=()):
    b, lq, width = q.shape
    groups, w, lk = kt.shape[1:]
    rows = min(lq, ATTN_STEP_ROWS)
    pass_rows = ATTN_STACK * min(ATTN_TQ, rows)
    whole = lambda a: pl.BlockSpec(a.shape, lambda bi, i: (0, 0))
    return pl.pallas_call(
        kernel_fn,
        grid=(b, lq // rows),
        in_specs=[whole(a) for a in extra[:1]]
        + [pl.BlockSpec((1, rows, width), lambda bi, i: (bi, i, 0)),
           pl.BlockSpec((1, groups, w, lk), lambda bi, i: (bi, 0, 0, 0)),
           pl.BlockSpec((1, groups, lk, w), lambda bi, i: (bi, 0, 0, 0))]
        + [whole(a) for a in extra[1:]],
        out_specs=pl.BlockSpec((1, rows, width), lambda bi, i: (bi, i, 0)),
        out_shape=jax.ShapeDtypeStruct((b, lq, width), BF16),
        scratch_shapes=[pltpu.VMEM((pass_rows, w), BF16)] * 2 + [pltpu.VMEM((pass_rows, lk), F32)] * 2
        + [pltpu.VMEM((pass_rows, lk), BF16)] * 2,
        compiler_params=_params("arbitrary", "arbitrary"),
    )(*extra[:1], q, kt, v, *extra[1:])


def _diff_attention(lqk, q, kt, v, g256, bd256, lam_init):
    return _attention(functools.partial(_diff_kernel, lam_init=lam_init), q, kt, v, (lqk, g256, bd256))


def _gqa_attention(q, kt, v):
    return _attention(_gqa_kernel, q, kt, v)


def _top2_route(logits):
    lane = lax.broadcasted_iota(jnp.int32, logits.shape, 1).astype(F32)
    neg = jnp.float32(-jnp.inf)
    lg = jnp.where(lane < N_EXPERTS, logits, neg)
    m1 = jnp.max(lg, axis=-1, keepdims=True)
    i1 = jnp.min(jnp.where(lg == m1, lane, float(LANES)), axis=-1, keepdims=True)
    lg2 = jnp.where(lane == i1, neg, lg)
    m2 = jnp.max(lg2, axis=-1, keepdims=True)
    i2 = jnp.min(jnp.where(lg2 == m2, lane, float(LANES)), axis=-1, keepdims=True)
    e2 = jnp.exp(m2 - m1)
    w1 = 1.0 / (1.0 + e2)
    return (jnp.where(lane == 0, i1, 0.0) + jnp.where(lane == 1, i2, 0.0)
            + jnp.where(lane == 2, w1, 0.0) + jnp.where(lane == 3, e2 * w1, 0.0))


def _outproj_kernel(*refs, moe):
    yhy_ref, od_ref, og_ref, h_ref, gt_ref, w_ref, gf_ref, sc_ref, sh_ref = refs[:9]
    if moe:
        rt_ref, hn_ref, u_ref, gates_ref = refs[9:]
    else:
        hn_ref, u_ref = refs[9:]
    y = (_dot(yhy_ref[0], w_ref[0:HY_WIDTH, :])
         + _dot(od_ref[0], w_ref[HY_WIDTH:HY_WIDTH + DIFF_WIDTH, :])
         + _dot(og_ref[0], w_ref[HY_WIDTH + DIFF_WIDTH:, :]))
    hn = h_ref[0] + gt_ref[0] * y
    hn_ref[0] = hn
    u = _rms(hn, gf_ref[...]) * (1.0 + sc_ref[0]) + sh_ref[0]
    u_ref[0] = u.astype(u_ref.dtype)
    if moe:
        gates_ref[0] = _top2_route(_dot_hi(u, rt_ref[...]))


def _outproj(yhy, od, og, h, gt, w_bf, gf, sc, sh, router_pad):
    b, l, d = h.shape
    tm = min(l, 512)
    row = lambda bi, i: (bi, i, 0)
    vec = lambda bi, i: (bi, 0, 0)
    const = lambda bi, i: (0, 0)
    moe = router_pad is not None
    in_specs = [pl.BlockSpec((1, tm, HY_WIDTH), row), pl.BlockSpec((1, tm, DIFF_WIDTH), row),
                pl.BlockSpec((1, tm, GQA_WIDTH), row), pl.BlockSpec((1, tm, d), row),
                pl.BlockSpec((1, 1, d), vec), pl.BlockSpec((d, d), const), pl.BlockSpec((1, d), const),
                pl.BlockSpec((1, 1, d), vec), pl.BlockSpec((1, 1, d), vec)]
    args = [yhy, od, og, h, gt, w_bf, gf, sc, sh]
    out_specs = [pl.BlockSpec((1, tm, d), row), pl.BlockSpec((1, tm, d), row)]
    out_shape = [jax.ShapeDtypeStruct((b, l, d), F32), jax.ShapeDtypeStruct((b, l, d), F32 if moe else BF16)]
    if moe:
        in_specs.append(pl.BlockSpec((d, LANES), const))
        args.append(router_pad)
        out_specs.append(pl.BlockSpec((1, tm, LANES), row))
        out_shape.append(jax.ShapeDtypeStruct((b, l, LANES), F32))
    return pl.pallas_call(
        functools.partial(_outproj_kernel, moe=moe),
        grid=(b, l // tm), in_specs=in_specs, out_specs=out_specs, out_shape=out_shape,
        compiler_params=_params("arbitrary", "arbitrary"),
    )(*args)


def _ffn_kernel(u_ref, wg_ref, wu_ref, wd_ref, h_ref, gt_ref, o_ref, acc_ref):
    f = pl.program_id(2)

    @pl.when(f == 0)
    def _():
        acc_ref[...] = jnp.zeros_like(acc_ref)

    u = u_ref[0]
    hid = _silu(_dot(u, wg_ref[...])) * _dot(u, wu_ref[...])
    acc_ref[...] += _dot(hid.astype(BF16), wd_ref[...])

    @pl.when(f == pl.num_programs(2) - 1)
    def _():
        o_ref[0] = h_ref[0] + gt_ref[0] * acc_ref[...]


def _ffn(u, wg, wu, wd, h, gt):
    b, l, d = h.shape
    dff = wg.shape[1]
    tm = min(l, 512)
    tf = dff // 2
    row = lambda bi, i, f: (bi, i, 0)
    return pl.pallas_call(
        _ffn_kernel,
        grid=(b, l // tm, dff // tf),
        in_specs=[pl.BlockSpec((1, tm, d), row),
                  pl.BlockSpec((d, tf), lambda bi, i, f: (0, f)), pl.BlockSpec((d, tf), lambda bi, i, f: (0, f)),
                  pl.BlockSpec((tf, d), lambda bi, i, f: (f, 0)),
                  pl.BlockSpec((1, tm, d), row), pl.BlockSpec((1, 1, d), lambda bi, i, f: (bi, 0, 0))],
        out_specs=pl.BlockSpec((1, tm, d), row),
        out_shape=jax.ShapeDtypeStruct((b, l, d), F32),
        scratch_shapes=[pltpu.VMEM((tm, d), F32)],
        compiler_params=_params("arbitrary", "arbitrary", "arbitrary"),
    )(u, wg, wu, wd, h, gt)


MOE_TILE = 512
SC_CHUNK = 64


def _sc_gather(table, idx):
    info = plsc.get_sparse_core_info()
    nc, ns = info.num_cores, info.num_subcores
    n, d = idx.shape[0], table.shape[1]
    per_worker = n // (nc * ns)
    assert per_worker * nc * ns == n and per_worker % SC_CHUNK == 0
    mesh = plsc.VectorSubcoreMesh(core_axis_name="c", subcore_axis_name="s")

    @functools.partial(
        pl.kernel, mesh=mesh,
        out_type=jax.ShapeDtypeStruct((n, d), table.dtype),
        scratch_types=[pltpu.VMEM((SC_CHUNK,), jnp.int32), pltpu.VMEM((SC_CHUNK, d), table.dtype),
                       pltpu.SemaphoreType.DMA],
    )
    def gather_kernel(table_hbm, idx_hbm, out_hbm, idx_v, rows_v, sem):
        base = (lax.axis_index("s") * nc + lax.axis_index("c")) * per_worker

        @pl.loop(0, per_worker // SC_CHUNK)
        def _(j):
            off = pl.multiple_of(base + j * SC_CHUNK, 8)
            pltpu.sync_copy(idx_hbm.at[pl.ds(off, SC_CHUNK)], idx_v)
            pltpu.async_copy(table_hbm.at[idx_v], rows_v, sem).wait()
            pltpu.sync_copy(rows_v, out_hbm.at[pl.ds(off, SC_CHUNK)])

    return gather_kernel(table, idx)


def _route_plan(route, n_rows):
    t = route.shape[0]
    e = route[:, :2].astype(jnp.int32).reshape(2 * t)
    onehot = (e[:, None] == jnp.arange(N_EXPERTS, dtype=jnp.int32)[None, :]).astype(jnp.int32)
    csum = jnp.cumsum(onehot, axis=0)
    counts = csum[-1]
    rank = jnp.take_along_axis(csum, e[:, None], axis=1)[:, 0] - 1
    padded = (counts + MOE_TILE - 1) // MOE_TILE * MOE_TILE
    ends = jnp.cumsum(padded)
    pos = (ends - padded)[e] + rank
    row_token = jnp.zeros((n_rows,), jnp.int32).at[pos].set(jnp.arange(2 * t, dtype=jnp.int32) // 2)
    tile_start = jnp.arange(n_rows // MOE_TILE, dtype=jnp.int32) * MOE_TILE
    tile_expert = jnp.minimum(jnp.sum((tile_start[:, None] >= ends[None, :]).astype(jnp.int32), axis=1),
                              N_EXPERTS - 1)
    n_used = (ends[-1] // MOE_TILE).reshape(1)
    return pos, row_token, tile_expert, n_used


def _gmm_kernel(te_ref, nu_ref, x_ref, wg_ref, wu_ref, wd_ref, o_ref, acc_ref):
    t, f = pl.program_id(0), pl.program_id(1)

    @pl.when(t < nu_ref[0])
    def _():
        @pl.when(f == 0)
        def _():
            acc_ref[...] = jnp.zeros_like(acc_ref)

        x = x_ref[...].astype(BF16)
        hid = _silu(_dot(x, wg_ref[0])) * _dot(x, wu_ref[0])
        acc_ref[...] += _dot(hid.astype(BF16), wd_ref[0])

        @pl.when(f == pl.num_programs(1) - 1)
        def _():
            o_ref[...] = acc_ref[...]


def _grouped_swiglu(x_sorted, tile_expert, n_used, wg, wu, wd):
    r, d = x_sorted.shape
    dff = wg.shape[2]
    n_f = 2
    tf = dff // n_f
    fidx = lambda t, f, te, nu: jnp.where(t < nu[0], f, n_f - 1)
    return pl.pallas_call(
        _gmm_kernel,
        grid_spec=pltpu.PrefetchScalarGridSpec(
            num_scalar_prefetch=2,
            grid=(r // MOE_TILE, n_f),
            in_specs=[pl.BlockSpec((MOE_TILE, d), lambda t, f, te, nu: (t, 0)),
                      pl.BlockSpec((1, d, tf), lambda t, f, te, nu: (te[t], 0, fidx(t, f, te, nu))),
                      pl.BlockSpec((1, d, tf), lambda t, f, te, nu: (te[t], 0, fidx(t, f, te, nu))),
                      pl.BlockSpec((1, tf, d), lambda t, f, te, nu: (te[t], fidx(t, f, te, nu), 0))],
            out_specs=pl.BlockSpec((MOE_TILE, d), lambda t, f, te, nu: (t, 0)),
            scratch_shapes=[pltpu.VMEM((MOE_TILE, d), F32)]),
        out_shape=jax.ShapeDtypeStruct((r, d), F32),
        compiler_params=_params("arbitrary", "arbitrary"),
    )(tile_expert, n_used, x_sorted, wg, wu, wd)


def _moe_combine_kernel(y_ref, rt_ref, h_ref, gt_ref, o_ref):
    d = h_ref.shape[1]
    rt = rt_ref[...]
    out = rt[:, 2:3] * y_ref[:, :d] + rt[:, 3:4] * y_ref[:, d:]
    o_ref[...] = h_ref[...] + gt_ref[0] * out


def _moe_combine(y_pair, route, h, gt_tab, tile_vec):
    t, d = h.shape
    tm = 512
    return pl.pallas_call(
        _moe_combine_kernel,
        grid=(t // tm,),
        in_specs=[pl.BlockSpec((tm, 2 * d), lambda i: (i, 0)), pl.BlockSpec((tm, LANES), lambda i: (i, 0)),
                  pl.BlockSpec((tm, d), lambda i: (i, 0)),
                  pl.BlockSpec((1, 1, d), lambda i: (tile_vec(i, tm), 0, 0))],
        out_specs=pl.BlockSpec((tm, d), lambda i: (i, 0)),
        out_shape=jax.ShapeDtypeStruct((t, d), F32),
        compiler_params=_params("arbitrary"),
    )(y_pair, route, h, gt_tab)


def _moe(u, route, wg, wu, wd, h, gt_tab, tile_vec):
    t, d = u.shape
    n_rows = 2 * t + N_EXPERTS * MOE_TILE
    pos, row_token, tile_expert, n_used = _route_plan(route, n_rows)
    x_sorted = _sc_gather(u, row_token)
    y_sorted = _grouped_swiglu(x_sorted, tile_expert, n_used, wg, wu, wd)
    y_pair = _sc_gather(y_sorted, pos).reshape(t, 2 * d)
    return _moe_combine(y_pair, route, h, gt_tab, tile_vec)


def _final_norm_kernel(h_ref, g_ref, o_ref):
    o_ref[0] = _rms(h_ref[0], g_ref[...])


def _final_norm(h, g):
    b, l, d = h.shape
    tm = min(l, 1024)
    return pl.pallas_call(
        _final_norm_kernel,
        grid=(b, l // tm),
        in_specs=[pl.BlockSpec((1, tm, d), lambda bi, i: (bi, i, 0)), pl.BlockSpec((1, d), lambda bi, i: (0, 0))],
        out_specs=pl.BlockSpec((1, tm, d), lambda bi, i: (bi, i, 0)),
        out_shape=jax.ShapeDtypeStruct((b, l, d), F32),
        compiler_params=_params("arbitrary", "arbitrary"),
    )(h, g.reshape(1, d))


def _rope_tables(length, head_dim):
    t = jnp.arange(length)
    row = (t // GRID_W).astype(F32)
    col = (t % GRID_W).astype(F32)
    n = head_dim // 4
    inv = ROPE_THETA ** (-jnp.arange(n, dtype=F32) / n)
    ang = jnp.concatenate([row[:, None] * inv, col[:, None] * inv], axis=-1)
    cos = jnp.concatenate([jnp.cos(ang)] * 2, axis=-1)
    sin = jnp.concatenate([-jnp.sin(ang), jnp.sin(ang)], axis=-1)
    reps = LANES // head_dim
    return jnp.tile(cos, (1, reps)), jnp.tile(sin, (1, reps))


def _hyena_tables(length):
    k = jnp.arange(length, dtype=jnp.int32)
    period = 4 * length
    step = 2.0 * math.pi / period

    def tab(a, bidx, fn):
        return fn(((a * bidx) % period).astype(F32) * step).astype(BF16)

    odd = 2 * k + 1
    c_tab = tab(odd[:, None], k[None, :], jnp.cos)
    s_tab = tab(odd[:, None], k[None, :], jnp.sin)
    ci_tab = tab(k[:, None], odd[None, :], jnp.cos)
    si_tab = tab(k[:, None], odd[None, :], jnp.sin)
    t = jnp.linspace(0.0, 1.0, length, dtype=F32)[:, None]
    bands = (HY_EMB - 1) // 2
    ang = (2.0 * math.pi / length) * jnp.arange(length, dtype=F32)[:, None] \
        * jnp.linspace(1e-4, bands - 1, bands, dtype=F32)
    feats = jnp.concatenate([t, jnp.cos(ang), -jnp.sin(ang),
                             jnp.zeros((length, LANES - HY_EMB), F32)], axis=-1)
    max_decay = math.log(HY_DECAY_TARGET) / HY_FAST_DECAY
    min_decay = math.log(HY_DECAY_TARGET) / HY_SLOW_DECAY
    deltas = jnp.abs(jnp.linspace(min_decay, max_decay, HY_WIDTH, dtype=F32)).reshape(1, HY_WIDTH)
    return c_tab, s_tab, ci_tab, si_tab, feats, deltas


def _block_diag_ones(n, group):
    i = jnp.arange(n) // group
    return (i[:, None] == i[None, :]).astype(BF16)


def kernel(x, c, ctx, c_ctx, w_ada, b_ada, g_mix, g_ffn, w_in, w_out, hy_conv_w, hy_conv_b, hy_w1, hy_b1, hy_w2, hy_b2, hy_w3, hy_freq, hy_skip, diff_lq1, diff_lk1, diff_lq2, diff_lk2, diff_subln, gqa_qnorm, gqa_knorm, ffn_wg, ffn_wu, ffn_wd, moe_router, moe_wg, moe_wu, moe_wd, g_final):
    b, seq, d = x.shape
    n_ctx = ctx.shape[1]
    depth = w_ada.shape[0]

    rope_tabs = _rope_tables(seq, DIFF_QK_DIM) + _rope_tables(seq, GQA_HEAD_DIM)
    hy_tabs = _hyena_tables(seq)
    hy_tabs_c = _hyena_tables(n_ctx)
    bd128 = _block_diag_ones(LANES, GQA_HEAD_DIM)
    bd256 = _block_diag_ones(DIFF_WIDTH, DIFF_V_DIM)

    rows = 16
    cc = jnp.zeros((rows, d), F32).at[:b].set(c).at[b].set(c_ctx)
    mods = _mods(cc, w_ada, b_ada)

    def mod_vecs(i):
        lat = [mods[i, :b, j * d:(j + 1) * d].reshape(b, 1, d) for j in range(6)]
        cx = [jnp.broadcast_to(mods[i, b, j * d:(j + 1) * d].reshape(1, 1, d), (b, 1, d)) for j in range(6)]
        return lat, cx

    def kv_layouts(kd, vd, kg, vg):
        l = kd.shape[1]
        kdt = jnp.swapaxes(kd, 1, 2)[:, None]
        vd = vd[:, None]
        kgt = jnp.swapaxes(kg, 1, 2).reshape(b, 2, 1, GQA_HEAD_DIM, l)
        kgt = jnp.broadcast_to(kgt, (b, 2, GQA_REP, GQA_HEAD_DIM, l)).reshape(b, 2, GQA_REP * GQA_HEAD_DIM, l)
        vgx = jnp.swapaxes(vg.reshape(b, l, 2, 1, GQA_HEAD_DIM), 1, 2)
        vgx = jnp.broadcast_to(vgx, (b, 2, l, GQA_REP, GQA_HEAD_DIM)).reshape(b, 2, l, GQA_REP * GQA_HEAD_DIM)
        return kdt, vd, kgt, vgx

    h, hc = x, ctx
    for i in range(depth):
        last = i == depth - 1
        lam_init = 0.8 - 0.6 * math.exp(-0.3 * i)
        (sh_m, sc_m, gt_m, sh_f, sc_f, gt_f), (csh_m, csc_m, cgt_m, csh_f, csc_f, cgt_f) = mod_vecs(i)
        w_in_bf = w_in[i].astype(BF16)
        w_out_bf = w_out[i].astype(BF16)
        g_m = g_mix[i].reshape(1, d)
        g_f = g_ffn[i].reshape(1, d)
        qn = jnp.tile(gqa_qnorm[i], LANES // GQA_HEAD_DIM).reshape(1, LANES)
        kn = jnp.tile(gqa_knorm[i], LANES // GQA_HEAD_DIM).reshape(1, LANES)
        subln = jnp.tile(diff_subln[i], DIFF_HEADS).reshape(1, DIFF_WIDTH)
        lqk = jnp.stack([diff_lq1[i], diff_lk1[i], diff_lq2[i], diff_lk2[i]]).astype(F32)
        hy_p = (hy_conv_w[i], hy_conv_b[i], hy_w1[i], hy_b1[i], hy_w2[i], hy_b2[i], hy_w3[i],
                hy_freq[i], hy_skip[i])
        moe = i % 2 == 1
        router_pad = None
        if moe:
            router_pad = jnp.zeros((d, LANES), F32).at[:, :N_EXPERTS].set(moe_router[i // 2])

        zhy, qd, kd, vd, qg, kg, vg = _inproj(h, sc_m, sh_m, g_m, w_in_bf, qn, kn, bd128, rope_tabs)
        czhy, cqd, ckd, cvd, cqg, ckg, cvg = _inproj(hc, csc_m, csh_m, g_m, w_in_bf, qn, kn, bd128, None)
        kdt, vdx, kgt, vgx = kv_layouts(kd, vd, kg, vg)
        ckdt, cvdx, ckgt, cvgx = kv_layouts(ckd, cvd, ckg, cvg)

        cat = lambda a, c_, axis: jnp.concatenate([a, c_], axis=axis)
        y_hy = _hyena(zhy, hy_p, hy_tabs)
        o_d = _diff_attention(lqk, qd, cat(kdt, ckdt, 3), cat(vdx, cvdx, 2), subln, bd256, lam_init)
        o_g = _gqa_attention(qg, cat(kgt, ckgt, 3), cat(vgx, cvgx, 2))
        mixed = _outproj(y_hy, o_d, o_g, h, gt_m, w_out_bf, g_f, sc_f, sh_f, router_pad)
        if not last:
            yc_hy = _hyena(czhy, hy_p, hy_tabs_c)
            oc_d = _diff_attention(lqk, cqd, ckdt, cvdx, subln, bd256, lam_init)
            oc_g = _gqa_attention(cqg, ckgt, cvgx)
            cmixed = _outproj(yc_hy, oc_d, oc_g, hc, cgt_m, w_out_bf, g_f, csc_f, csh_f, router_pad)

        if moe:
            wg, wu, wd = (w[i // 2].astype(BF16) for w in (moe_wg, moe_wu, moe_wd))
            flat = lambda a: a.reshape(-1, a.shape[-1])
            gt_tab = jnp.concatenate([gt_f, cgt_f[:1]], axis=0)
            if last:
                tile_vec = lambda t, tm: t // (seq // tm)
                h = _moe(flat(mixed[1]), flat(mixed[2]), wg, wu, wd, flat(mixed[0]), gt_tab,
                         tile_vec).reshape(b, seq, d)
            else:
                n_c = b * n_ctx
                tile_vec = lambda t, tm: jnp.where(t < n_c // tm, b, (t - n_c // tm) // (seq // tm))
                both = lambda c_, a: jnp.concatenate([flat(c_), flat(a)], axis=0)
                h_all = _moe(both(cmixed[1], mixed[1]), both(cmixed[2], mixed[2]), wg, wu, wd,
                             both(cmixed[0], mixed[0]), gt_tab, tile_vec)
                hc = h_all[:n_c].reshape(b, n_ctx, d)
                h = h_all[n_c:].reshape(b, seq, d)
        else:
            wg, wu, wd = (w[i // 2].astype(BF16) for w in (ffn_wg, ffn_wu, ffn_wd))
            h = _ffn(mixed[1], wg, wu, wd, mixed[0], gt_f)
            if not last:
                hc = _ffn(cmixed[1], wg, wu, wd, cmixed[0], cgt_f)
    return _final_norm(h, g_final)
```

```python
import functools
import math

import jax
import jax.numpy as jnp
from jax import lax
from jax.experimental import pallas as pl
from jax.experimental.pallas import tpu as pltpu
from jax.experimental.pallas import tpu_sc as plsc

F32 = jnp.float32
BF16 = jnp.bfloat16
HIGHEST = lax.Precision.HIGHEST

D_MODEL = 1024
GRID_W = 64
ROPE_THETA = 10000.0
NORM_EPS = 1e-6

HY_WIDTH = 256
HY_EMB = 33
HY_FAST_DECAY = 0.3
HY_SLOW_DECAY = 1.5
HY_DECAY_TARGET = 1e-2

DIFF_HEADS = 4
DIFF_QK_DIM = 32
DIFF_V_DIM = 64
DIFF_WIDTH = 256
GQA_HEAD_DIM = 64
GQA_WIDTH = 512
GQA_KV_WIDTH = 128
GQA_REP = 4

OFF_DQ = 3 * HY_WIDTH
OFF_DK = OFF_DQ + DIFF_WIDTH
OFF_DV = OFF_DK + DIFF_WIDTH
OFF_GQ = OFF_DV + DIFF_WIDTH
OFF_GK = OFF_GQ + GQA_WIDTH
OFF_GV = OFF_GK + GQA_KV_WIDTH
IN_COLS = OFF_GV + GQA_KV_WIDTH

N_EXPERTS = 8
LOG2_E = math.log2(math.e)
LANES = 128
VMEM_LIMIT = 56 * 1024 * 1024


def _params(*sem):
    return pltpu.CompilerParams(dimension_semantics=sem, vmem_limit_bytes=VMEM_LIMIT)


def _dot(a, b):
    return jnp.dot(a, b, preferred_element_type=F32)


def _dot_hi(a, b):
    return jnp.dot(a, b, preferred_element_type=F32, precision=HIGHEST)


def _rms(x, g):
    ms = jnp.mean(x * x, axis=-1, keepdims=True)
    return x * lax.rsqrt(ms + NORM_EPS) * g


def _silu(x):
    return x * jax.nn.sigmoid(x)


def _group_mean_sq(x, ones_bd, width):
    return _dot((x * x).astype(BF16), ones_bd) * (1.0 / width)


def _mods_kernel(c_ref, w_ref, b_ref, o_ref):
    o_ref[0] = _dot_hi(_silu(c_ref[...]), w_ref[0]) + b_ref[0]


def _mods(cc, w_ada, b_ada):
    depth, d, n = w_ada.shape
    tn = 1536
    return pl.pallas_call(
        _mods_kernel, name="mods",
        grid=(depth, n // tn),
        in_specs=[pl.BlockSpec(cc.shape, lambda i, j: (0, 0)),
                  pl.BlockSpec((1, d, tn), lambda i, j: (i, 0, j)),
                  pl.BlockSpec((1, 1, tn), lambda i, j: (i, 0, j))],
        out_specs=pl.BlockSpec((1, cc.shape[0], tn), lambda i, j: (i, 0, j)),
        out_shape=jax.ShapeDtypeStruct((depth, cc.shape[0], n), F32),
        compiler_params=_params("arbitrary", "arbitrary"),
    )(cc, w_ada, b_ada.reshape(depth, 1, n))


def _rope128(x, cos, sin_signed, half):
    lane = lax.broadcasted_iota(jnp.int32, x.shape, 1)
    first = (lane & (2 * half - 1)) < half
    swapped = jnp.where(first, pltpu.roll(x, LANES - half, 1), pltpu.roll(x, half, 1))
    return x * cos + swapped * sin_signed


def _inproj_kernel(*refs, rope):
    h_ref, sc_ref, sh_ref, g_ref, w_ref, qn_ref, kn_ref, bd_ref = refs[:8]
    if rope:
        cd_ref, sd_ref, cg_ref, sg_ref = refs[8:12]
    zhy_ref, qd_ref, kd_ref, vd_ref, qg_ref, kg_ref, vg_ref = refs[-7:]
    u = _rms(h_ref[0], g_ref[...]) * (1.0 + sc_ref[0]) + sh_ref[0]
    z = _dot(u.astype(BF16), w_ref[...])
    zhy_ref[0] = z[:, :OFF_DQ]

    def piece(off, j):
        return z[:, off + LANES * j: off + LANES * (j + 1)]

    for j in range(DIFF_WIDTH // LANES):
        q, k = piece(OFF_DQ, j), piece(OFF_DK, j)
        if rope:
            q = _rope128(q, cd_ref[...], sd_ref[...], DIFF_QK_DIM // 2)
            k = _rope128(k, cd_ref[...], sd_ref[...], DIFF_QK_DIM // 2)
        qd_ref[0, :, LANES * j: LANES * (j + 1)] = (q * (LOG2_E * DIFF_QK_DIM ** -0.5)).astype(BF16)
        kd_ref[0, :, LANES * j: LANES * (j + 1)] = k.astype(BF16)
    vd_ref[0] = z[:, OFF_DV:OFF_GQ].astype(BF16)

    def gqa_piece(x, gain):
        ms = _group_mean_sq(x, bd_ref[...], GQA_HEAD_DIM)
        x = x * lax.rsqrt(ms + NORM_EPS) * gain
        if rope:
            x = _rope128(x, cg_ref[...], sg_ref[...], GQA_HEAD_DIM // 2)
        return x

    for j in range(GQA_WIDTH // LANES):
        q = gqa_piece(piece(OFF_GQ, j), qn_ref[...])
        qg_ref[0, :, LANES * j: LANES * (j + 1)] = (q * (LOG2_E * GQA_HEAD_DIM ** -0.5)).astype(BF16)
    kg_ref[0] = gqa_piece(piece(OFF_GK, 0), kn_ref[...]).astype(BF16)
    vg_ref[0] = z[:, OFF_GV:].astype(BF16)


def _inproj(h, sc, sh, g, w_bf, qn, kn, bd, rope_tabs):
    b, l, d = h.shape
    tm = min(l, 512)
    row = lambda bi, i: (bi, i, 0)
    vec = lambda bi, i: (bi, 0, 0)
    const = lambda bi, i: (0, 0)
    in_specs = [pl.BlockSpec((1, tm, d), row), pl.BlockSpec((1, 1, d), vec), pl.BlockSpec((1, 1, d), vec),
                pl.BlockSpec((1, d), const), pl.BlockSpec((d, IN_COLS), const),
                pl.BlockSpec((1, LANES), const), pl.BlockSpec((1, LANES), const),
                pl.BlockSpec((LANES, LANES), const)]
    args = [h, sc, sh, g, w_bf, qn, kn, bd]
    if rope_tabs is not None:
        in_specs += [pl.BlockSpec((tm, LANES), lambda bi, i: (i, 0))] * 4
        args += list(rope_tabs)
    widths = [(OFF_DQ, F32), (DIFF_WIDTH, BF16), (DIFF_WIDTH, BF16), (DIFF_WIDTH, BF16),
              (GQA_WIDTH, BF16), (GQA_KV_WIDTH, BF16), (GQA_KV_WIDTH, BF16)]
    return pl.pallas_call(
        functools.partial(_inproj_kernel, rope=rope_tabs is not None), name="inproj",
        grid=(b, l // tm),
        in_specs=in_specs,
        out_specs=[pl.BlockSpec((1, tm, w), row) for w, _ in widths],
        out_shape=[jax.ShapeDtypeStruct((b, l, w), dt) for w, dt in widths],
        compiler_params=_params("arbitrary", "arbitrary"),
    )(*args)


def _hy_filter_kernel(f_ref, w1_ref, b1_ref, w2_ref, b2_ref, w3_ref, fr_ref, dl_ref, hs_ref, hd_ref):
    f = f_ref[...]
    fr = fr_ref[...]
    a = jnp.sin(fr * (_dot_hi(f, w1_ref[...]) + b1_ref[...]))
    a = jnp.sin(fr * (_dot_hi(a, w2_ref[...]) + b2_ref[...]))
    hf = _dot_hi(a, w3_ref[...])
    decay = jnp.exp(-f[:, 0:1] * dl_ref[...])
    h_fwd = hf[:, :HY_WIDTH] * decay
    h_bwd = hf[:, HY_WIDTH:] * decay
    row = lax.broadcasted_iota(jnp.int32, h_bwd.shape, 0) + pl.program_id(0) * f.shape[0]
    h_bwd = jnp.where(row == 0, 0.0, h_bwd)
    hs_ref[...] = (h_fwd + h_bwd).astype(BF16)
    hd_ref[...] = (h_bwd - h_fwd).astype(BF16)


def _hy_spectrum_kernel(c_ref, s_ref, hs_ref, hd_ref, gre_ref, gim_ref, *, scale):
    gre_ref[...] = _dot(c_ref[...], hs_ref[...]) * scale
    gim_ref[...] = _dot(s_ref[...], hd_ref[...]) * scale


def _hy_pre_kernel(z_ref, zp_ref, zn_ref, cw_ref, cb_ref, wb_ref, wf_ref, x0_ref, *, nt):
    ti = pl.program_id(1)
    z = z_ref[0]
    tl = z.shape[0]
    row = lax.broadcasted_iota(jnp.int32, z.shape, 0)
    prev_row = jnp.where(ti > 0, zp_ref[0, 7:8, :], 0.0)
    next_row = jnp.where(ti < nt - 1, zn_ref[0, 0:1, :], 0.0)
    z_prev = jnp.where(row == 0, prev_row, pltpu.roll(z, 1, 0))
    z_next = jnp.where(row == tl - 1, next_row, pltpu.roll(z, tl - 1, 0))
    cw = cw_ref[...]
    y = cb_ref[...] + z_prev * cw[0:1] + z * cw[1:2] + z_next * cw[2:3]
    x0, x1, v = y[:, :HY_WIDTH], y[:, HY_WIDTH:2 * HY_WIDTH], y[:, 2 * HY_WIDTH:]
    w = v * x1
    wb_ref[...] = w.astype(BF16)
    wf_ref[0] = w
    x0_ref[0] = x0


def _hy_fwd_kernel(c_ref, s_ref, x_ref, gre_ref, gim_ref, yre_ref, yim_ref, *, reps):
    x = x_ref[...]
    a = _dot(c_ref[...], x)
    b = _dot(s_ref[...], x)
    gre = jnp.concatenate([gre_ref[...]] * reps, axis=1)
    gim = jnp.concatenate([gim_ref[...]] * reps, axis=1)
    yre_ref[...] = (a * gre + b * gim).astype(BF16)
    yim_ref[...] = (a * gim - b * gre).astype(BF16)


def _hy_inv_kernel(ci_ref, si_ref, yre_ref, yim_ref, wf_ref, x0_ref, skip_ref, o_ref, *, reps):
    y = _dot(ci_ref[...], yre_ref[...]) - _dot(si_ref[...], yim_ref[...])
    for r in range(reps):
        yr = y[:, r * HY_WIDTH:(r + 1) * HY_WIDTH]
        o_ref[r] = ((yr + wf_ref[r] * skip_ref[...]) * x0_ref[r]).astype(BF16)


def _hyena(zhy, p, tabs):
    conv_w, conv_b, w1, b1, w2, b2, w3, freq, skip = p
    c_tab, s_tab, ci_tab, si_tab, feats, deltas = tabs
    b, l, _ = zhy.shape
    c = HY_WIDTH
    tl = min(l, 512)
    nt = l // tl
    hid = w2.shape[0]
    const1 = lambda i: (0, 0)
    w1p = jnp.zeros((LANES, hid), F32).at[:HY_EMB].set(w1)
    hs, hd = pl.pallas_call(
        _hy_filter_kernel, name="hy_filter",
        grid=(nt,),
        in_specs=[pl.BlockSpec((tl, LANES), lambda i: (i, 0)),
                  pl.BlockSpec((LANES, hid), const1), pl.BlockSpec((1, hid), const1),
                  pl.BlockSpec((hid, hid), const1), pl.BlockSpec((1, hid), const1),
                  pl.BlockSpec((hid, 2 * c), const1), pl.BlockSpec((1, hid), const1),
                  pl.BlockSpec((1, c), const1)],
        out_specs=[pl.BlockSpec((tl, c), lambda i: (i, 0))] * 2,
        out_shape=[jax.ShapeDtypeStruct((l, c), BF16)] * 2,
        compiler_params=_params("arbitrary"),
    )(feats, w1p, b1.reshape(1, hid), w2, b2.reshape(1, hid), w3, freq.reshape(1, hid), deltas)

    gre, gim = pl.pallas_call(
        functools.partial(_hy_spectrum_kernel, scale=1.0 / l), name="hy_spectrum",
        grid=(nt,),
        in_specs=[pl.BlockSpec((tl, l), lambda i: (i, 0)), pl.BlockSpec((tl, l), lambda i: (i, 0)),
                  pl.BlockSpec((l, c), const1), pl.BlockSpec((l, c), const1)],
        out_specs=[pl.BlockSpec((tl, c), lambda i: (i, 0))] * 2,
        out_shape=[jax.ShapeDtypeStruct((l, c), F32)] * 2,
        compiler_params=_params("arbitrary"),
    )(c_tab, s_tab, hs, hd)

    halo = 8
    wb, wf, x0 = pl.pallas_call(
        functools.partial(_hy_pre_kernel, nt=nt), name="hy_pre",
        grid=(b, nt),
        in_specs=[pl.BlockSpec((1, tl, 3 * c), lambda bi, i: (bi, i, 0)),
                  pl.BlockSpec((1, halo, 3 * c), lambda bi, i: (bi, jnp.maximum(i * (tl // halo) - 1, 0), 0)),
                  pl.BlockSpec((1, halo, 3 * c),
                               lambda bi, i: (bi, jnp.minimum((i + 1) * (tl // halo), l // halo - 1), 0)),
                  pl.BlockSpec((3, 3 * c), lambda bi, i: (0, 0)),
                  pl.BlockSpec((1, 3 * c), lambda bi, i: (0, 0))],
        out_specs=[pl.BlockSpec((tl, c), lambda bi, i: (i, bi)),
                   pl.BlockSpec((1, tl, c), lambda bi, i: (bi, i, 0)),
                   pl.BlockSpec((1, tl, c), lambda bi, i: (bi, i, 0))],
        out_shape=[jax.ShapeDtypeStruct((l, b * c), BF16),
                   jax.ShapeDtypeStruct((b, l, c), F32),
                   jax.ShapeDtypeStruct((b, l, c), F32)],
        compiler_params=_params("arbitrary", "arbitrary"),
    )(zhy, zhy, zhy, conv_w, conv_b.reshape(1, 3 * c))

    reps = 2
    tn = reps * c
    nj = b * c // tn
    yre, yim = pl.pallas_call(
        functools.partial(_hy_fwd_kernel, reps=reps), name="hy_fwd",
        grid=(nj, nt),
        in_specs=[pl.BlockSpec((tl, l), lambda j, i: (i, 0)), pl.BlockSpec((tl, l), lambda j, i: (i, 0)),
                  pl.BlockSpec((l, tn), lambda j, i: (0, j)),
                  pl.BlockSpec((tl, c), lambda j, i: (i, 0)), pl.BlockSpec((tl, c), lambda j, i: (i, 0))],
        out_specs=[pl.BlockSpec((tl, tn), lambda j, i: (i, j))] * 2,
        out_shape=[jax.ShapeDtypeStruct((l, b * c), BF16)] * 2,
        compiler_params=_params("arbitrary", "arbitrary"),
    )(c_tab, s_tab, wb, gre, gim)

    return pl.pallas_call(
        functools.partial(_hy_inv_kernel, reps=reps), name="hy_inv",
        grid=(nj, nt),
        in_specs=[pl.BlockSpec((tl, l), lambda j, i: (i, 0)), pl.BlockSpec((tl, l), lambda j, i: (i, 0)),
                  pl.BlockSpec((l, tn), lambda j, i: (0, j)), pl.BlockSpec((l, tn), lambda j, i: (0, j)),
                  pl.BlockSpec((reps, tl, c), lambda j, i: (j, i, 0)),
                  pl.BlockSpec((reps, tl, c), lambda j, i: (j, i, 0)),
                  pl.BlockSpec((1, c), lambda j, i: (0, 0))],
        out_specs=pl.BlockSpec((reps, tl, c), lambda j, i: (j, i, 0)),
        out_shape=jax.ShapeDtypeStruct((b, l, c), BF16),
        compiler_params=_params("arbitrary", "arbitrary"),
    )(ci_tab, si_tab, yre, yim, wf, x0, skip.reshape(1, c))


def _lane_mask(width, shift, idx):
    lane = lax.broadcasted_iota(jnp.int32, (1, width), 1)
    return jnp.where((lane >> shift) == idx, 1.0, 0.0).astype(BF16)


ATTN_WIDTH = 256
ATTN_CHUNK = 512


def _key_chunks(lk):
    return [(c0, min(ATTN_CHUNK, lk - c0)) for c0 in range(0, lk, ATTN_CHUNK)]


ATTN_STACK = 4
ATTN_TQ = 128
ATTN_STEP_ROWS = 512


def _qk_stage(q, shift, first, kt_ref, qs_scr, s_scr):
    tq = q.shape[0]
    for i in range(ATTN_STACK):
        qs_scr[i * tq:(i + 1) * tq, :] = q * _lane_mask(ATTN_WIDTH, shift, first + i)
    qs = qs_scr[...]
    m = jnp.full((ATTN_STACK * tq, LANES), -jnp.inf, F32)
    for c0, ck in _key_chunks(kt_ref.shape[-1]):
        s = _dot(qs, kt_ref[:, c0:c0 + ck])
        s_scr[:, c0:c0 + ck] = s
        for j in range(ck // LANES):
            m = jnp.maximum(m, s[:, j * LANES:(j + 1) * LANES])
    return jnp.max(m, axis=-1, keepdims=True)


def _exp_stage(s_scr, mx, p_scr):
    l = jnp.zeros((s_scr.shape[0], LANES), F32)
    for c0, ck in _key_chunks(s_scr.shape[-1]):
        e = jnp.exp2(s_scr[:, c0:c0 + ck] - mx)
        p_scr[:, c0:c0 + ck] = e.astype(BF16)
        for j in range(ck // LANES):
            l = l + e[:, j * LANES:(j + 1) * LANES]
    return 1.0 / jnp.sum(l, axis=-1, keepdims=True)


def _attention_passes(passes, consume, kt_ref, v_ref, scratch):
    qs, ss, ps = scratch[0:2], scratch[2:4], scratch[4:6]
    tq = passes[0][0].shape[0]

    def qk(i):
        q, shift, first, g = passes[i]
        return _qk_stage(q, shift, first, kt_ref.at[0, g], qs[i % 2], ss[i % 2])

    mx = qk(0)
    for i in range(len(passes)):
        mx_next = qk(i + 1) if i + 1 < len(passes) else None
        inv = _exp_stage(ss[i % 2], mx, ps[i % 2])
        pv = _dot(ps[i % 2][...], v_ref[0, passes[i][3]]) * inv
        consume(i, [pv[r * tq:(r + 1) * tq] for r in range(ATTN_STACK)])
        mx = mx_next


def _diff_kernel(lqk_ref, q_ref, kt_ref, v_ref, g_ref, bd_ref, o_ref, *scratch, lam_init):
    x = lqk_ref[...]
    lam = (jnp.exp(jnp.sum(x[0:1] * x[1:2], axis=-1, keepdims=True))
           - jnp.exp(jnp.sum(x[2:3] * x[3:4], axis=-1, keepdims=True)) + lam_init)
    tq = min(ATTN_TQ, q_ref.shape[1])
    group = lax.broadcasted_iota(jnp.int32, (tq, ATTN_WIDTH), 1) >> 6
    halves = 2 * DIFF_HEADS // ATTN_STACK
    passes = []
    for t in range(q_ref.shape[1] // tq):
        q = q_ref[0, t * tq:(t + 1) * tq, :]
        passes += [(q, 5, half * ATTN_STACK, 0) for half in range(halves)]
    acc = {}

    def consume(i, pvs):
        t, half = divmod(i, halves)
        o1, o2 = acc.pop(t, (jnp.zeros(group.shape, F32),) * 2)
        for r, pv in enumerate(pvs):
            head = (half * ATTN_STACK + r) >> 1
            if r & 1 == 0:
                o1 = jnp.where(group == head, pv, o1)
            else:
                o2 = jnp.where(group == head, pv, o2)
        if half + 1 < halves:
            acc[t] = (o1, o2)
            return
        o = o1 - lam * o2
        ms = _group_mean_sq(o, bd_ref[...], DIFF_V_DIM)
        o = o * lax.rsqrt(ms + NORM_EPS) * (g_ref[...] * (1.0 - lam_init))
        o_ref[0, t * tq:(t + 1) * tq, :] = o.astype(BF16)

    _attention_passes(passes, consume, kt_ref, v_ref, scratch)


def _gqa_kernel(q_ref, kt_ref, v_ref, o_ref, *scratch):
    tq = min(ATTN_TQ, q_ref.shape[1])
    group = lax.broadcasted_iota(jnp.int32, (tq, ATTN_WIDTH), 1) >> 6
    groups = q_ref.shape[2] // ATTN_WIDTH
    passes = []
    for t in range(q_ref.shape[1] // tq):
        for g in range(groups):
            passes.append((q_ref[0, t * tq:(t + 1) * tq, g * ATTN_WIDTH:(g + 1) * ATTN_WIDTH], 6, 0, g))

    def consume(i, pvs):
        t, g = divmod(i, groups)
        o = jnp.zeros(group.shape, F32)
        for r, pv in enumerate(pvs):
            o = jnp.where(group == r, pv, o)
        o_ref[0, t * tq:(t + 1) * tq, g * ATTN_WIDTH:(g + 1) * ATTN_WIDTH] = o.astype(BF16)

    _attention_passes(passes, consume, kt_ref, v_ref, scratch)


def _attention(kernel_fn, q, kt, v, extra=()):
    b, lq, width = q.shape
    groups, w, lk = kt.shape[1:]
    rows = min(lq, ATTN_STEP_ROWS)
    pass_rows = ATTN_STACK * min(ATTN_TQ, rows)
    whole = lambda a: pl.BlockSpec(a.shape, lambda bi, i: (0, 0))
    return pl.pallas_call(
        kernel_fn, name="diff_attn" if extra else "gqa_attn",
        grid=(b, lq // rows),
        in_specs=[whole(a) for a in extra[:1]]
        + [pl.BlockSpec((1, rows, width), lambda bi, i: (bi, i, 0)),
           pl.BlockSpec((1, groups, w, lk), lambda bi, i: (bi, 0, 0, 0)),
           pl.BlockSpec((1, groups, lk, w), lambda bi, i: (bi, 0, 0, 0))]
        + [whole(a) for a in extra[1:]],
        out_specs=pl.BlockSpec((1, rows, width), lambda bi, i: (bi, i, 0)),
        out_shape=jax.ShapeDtypeStruct((b, lq, width), BF16),
        scratch_shapes=[pltpu.VMEM((pass_rows, w), BF16)] * 2 + [pltpu.VMEM((pass_rows, lk), F32)] * 2
        + [pltpu.VMEM((pass_rows, lk), BF16)] * 2,
        compiler_params=_params("arbitrary", "arbitrary"),
    )(*extra[:1], q, kt, v, *extra[1:])


def _diff_attention(lqk, q, kt, v, g256, bd256, lam_init):
    return _attention(functools.partial(_diff_kernel, lam_init=lam_init), q, kt, v, (lqk, g256, bd256))


def _gqa_attention(q, kt, v):
    return _attention(_gqa_kernel, q, kt, v)


def _top2_route(logits):
    lane = lax.broadcasted_iota(jnp.int32, logits.shape, 1).astype(F32)
    neg = jnp.float32(-jnp.inf)
    lg = jnp.where(lane < N_EXPERTS, logits, neg)
    m1 = jnp.max(lg, axis=-1, keepdims=True)
    i1 = jnp.min(jnp.where(lg == m1, lane, float(LANES)), axis=-1, keepdims=True)
    lg2 = jnp.where(lane == i1, neg, lg)
    m2 = jnp.max(lg2, axis=-1, keepdims=True)
    i2 = jnp.min(jnp.where(lg2 == m2, lane, float(LANES)), axis=-1, keepdims=True)
    e2 = jnp.exp(m2 - m1)
    w1 = 1.0 / (1.0 + e2)
    return (jnp.where(lane == 0, i1, 0.0) + jnp.where(lane == 1, i2, 0.0)
            + jnp.where(lane == 2, w1, 0.0) + jnp.where(lane == 3, e2 * w1, 0.0))


def _outproj_kernel(*refs, moe):
    yhy_ref, od_ref, og_ref, h_ref, gt_ref, w_ref, gf_ref, sc_ref, sh_ref = refs[:9]
    if moe:
        rt_ref, hn_ref, u_ref, gates_ref = refs[9:]
    else:
        hn_ref, u_ref = refs[9:]
    y = (_dot(yhy_ref[0], w_ref[0:HY_WIDTH, :])
         + _dot(od_ref[0], w_ref[HY_WIDTH:HY_WIDTH + DIFF_WIDTH, :])
         + _dot(og_ref[0], w_ref[HY_WIDTH + DIFF_WIDTH:, :]))
    hn = h_ref[0] + gt_ref[0] * y
    hn_ref[0] = hn
    u = _rms(hn, gf_ref[...]) * (1.0 + sc_ref[0]) + sh_ref[0]
    u_ref[0] = u.astype(u_ref.dtype)
    if moe:
        gates_ref[0] = _top2_route(_dot_hi(u, rt_ref[...]))


def _outproj(yhy, od, og, h, gt, w_bf, gf, sc, sh, router_pad):
    b, l, d = h.shape
    tm = min(l, 512)
    row = lambda bi, i: (bi, i, 0)
    vec = lambda bi, i: (bi, 0, 0)
    const = lambda bi, i: (0, 0)
    moe = router_pad is not None
    in_specs = [pl.BlockSpec((1, tm, HY_WIDTH), row), pl.BlockSpec((1, tm, DIFF_WIDTH), row),
                pl.BlockSpec((1, tm, GQA_WIDTH), row), pl.BlockSpec((1, tm, d), row),
                pl.BlockSpec((1, 1, d), vec), pl.BlockSpec((d, d), const), pl.BlockSpec((1, d), const),
                pl.BlockSpec((1, 1, d), vec), pl.BlockSpec((1, 1, d), vec)]
    args = [yhy, od, og, h, gt, w_bf, gf, sc, sh]
    out_specs = [pl.BlockSpec((1, tm, d), row), pl.BlockSpec((1, tm, d), row)]
    out_shape = [jax.ShapeDtypeStruct((b, l, d), F32), jax.ShapeDtypeStruct((b, l, d), F32 if moe else BF16)]
    if moe:
        in_specs.append(pl.BlockSpec((d, LANES), const))
        args.append(router_pad)
        out_specs.append(pl.BlockSpec((1, tm, LANES), row))
        out_shape.append(jax.ShapeDtypeStruct((b, l, LANES), F32))
    return pl.pallas_call(
        functools.partial(_outproj_kernel, moe=moe), name="outproj",
        grid=(b, l // tm), in_specs=in_specs, out_specs=out_specs, out_shape=out_shape,
        compiler_params=_params("arbitrary", "arbitrary"),
    )(*args)


def _ffn_kernel(u_ref, wg_ref, wu_ref, wd_ref, h_ref, gt_ref, o_ref, acc_ref):
    f = pl.program_id(2)

    @pl.when(f == 0)
    def _():
        acc_ref[...] = jnp.zeros_like(acc_ref)

    u = u_ref[0]
    hid = _silu(_dot(u, wg_ref[...])) * _dot(u, wu_ref[...])
    acc_ref[...] += _dot(hid.astype(BF16), wd_ref[...])

    @pl.when(f == pl.num_programs(2) - 1)
    def _():
        o_ref[0] = h_ref[0] + gt_ref[0] * acc_ref[...]


def _ffn(u, wg, wu, wd, h, gt):
    b, l, d = h.shape
    dff = wg.shape[1]
    tm = min(l, 512)
    tf = dff // 2
    row = lambda bi, i, f: (bi, i, 0)
    return pl.pallas_call(
        _ffn_kernel, name="ffn",
        grid=(b, l // tm, dff // tf),
        in_specs=[pl.BlockSpec((1, tm, d), row),
                  pl.BlockSpec((d, tf), lambda bi, i, f: (0, f)), pl.BlockSpec((d, tf), lambda bi, i, f: (0, f)),
                  pl.BlockSpec((tf, d), lambda bi, i, f: (f, 0)),
                  pl.BlockSpec((1, tm, d), row), pl.BlockSpec((1, 1, d), lambda bi, i, f: (bi, 0, 0))],
        out_specs=pl.BlockSpec((1, tm, d), row),
        out_shape=jax.ShapeDtypeStruct((b, l, d), F32),
        scratch_shapes=[pltpu.VMEM((tm, d), F32)],
        compiler_params=_params("arbitrary", "arbitrary", "arbitrary"),
    )(u, wg, wu, wd, h, gt)


MOE_TILE = 512
SC_CHUNK = 64


def _sc_gather(table, idx):
    info = plsc.get_sparse_core_info()
    nc, ns = info.num_cores, info.num_subcores
    n, d = idx.shape[0], table.shape[1]
    per_worker = n // (nc * ns)
    assert per_worker * nc * ns == n and per_worker % SC_CHUNK == 0
    mesh = plsc.VectorSubcoreMesh(core_axis_name="c", subcore_axis_name="s")

    @functools.partial(
        pl.kernel, mesh=mesh,
        out_type=jax.ShapeDtypeStruct((n, d), table.dtype),
        scratch_types=[pltpu.VMEM((SC_CHUNK,), jnp.int32), pltpu.VMEM((SC_CHUNK, d), table.dtype),
                       pltpu.SemaphoreType.DMA],
    )
    def gather_kernel(table_hbm, idx_hbm, out_hbm, idx_v, rows_v, sem):
        base = (lax.axis_index("s") * nc + lax.axis_index("c")) * per_worker

        @pl.loop(0, per_worker // SC_CHUNK)
        def _(j):
            off = pl.multiple_of(base + j * SC_CHUNK, 8)
            pltpu.sync_copy(idx_hbm.at[pl.ds(off, SC_CHUNK)], idx_v)
            pltpu.async_copy(table_hbm.at[idx_v], rows_v, sem).wait()
            pltpu.sync_copy(rows_v, out_hbm.at[pl.ds(off, SC_CHUNK)])

    return gather_kernel(table, idx)


def _route_plan(route, n_rows):
    t = route.shape[0]
    e = route[:, :2].astype(jnp.int32).T.reshape(2 * t)
    onehot = (e[:, None] == jnp.arange(N_EXPERTS, dtype=jnp.int32)[None, :]).astype(jnp.int32)
    csum = jnp.cumsum(onehot, axis=0)
    counts = csum[-1]
    rank = jnp.take_along_axis(csum, e[:, None], axis=1)[:, 0] - 1
    padded = (counts + MOE_TILE - 1) // MOE_TILE * MOE_TILE
    ends = jnp.cumsum(padded)
    pos = (ends - padded)[e] + rank
    row_token = jnp.zeros((n_rows,), jnp.int32).at[pos].set(jnp.arange(2 * t, dtype=jnp.int32) % t)
    tile_start = jnp.arange(n_rows // MOE_TILE, dtype=jnp.int32) * MOE_TILE
    tile_expert = jnp.minimum(jnp.sum((tile_start[:, None] >= ends[None, :]).astype(jnp.int32), axis=1),
                              N_EXPERTS - 1)
    n_used = (ends[-1] // MOE_TILE).reshape(1)
    return pos, row_token, tile_expert, n_used


def _gmm_kernel(te_ref, nu_ref, x_ref, wg_ref, wu_ref, wd_ref, o_ref, acc_ref):
    t, f = pl.program_id(0), pl.program_id(1)

    @pl.when(t < nu_ref[0])
    def _():
        @pl.when(f == 0)
        def _():
            acc_ref[...] = jnp.zeros_like(acc_ref)

        x = x_ref[...].astype(BF16)
        hid = _silu(_dot(x, wg_ref[0])) * _dot(x, wu_ref[0])
        acc_ref[...] += _dot(hid.astype(BF16), wd_ref[0])

        @pl.when(f == pl.num_programs(1) - 1)
        def _():
            o_ref[...] = acc_ref[...]


def _grouped_swiglu(x_sorted, tile_expert, n_used, wg, wu, wd):
    r, d = x_sorted.shape
    dff = wg.shape[2]
    n_f = 2
    tf = dff // n_f
    fidx = lambda t, f, te, nu: jnp.where(t < nu[0], f, n_f - 1)
    return pl.pallas_call(
        _gmm_kernel, name="grouped_swiglu",
        grid_spec=pltpu.PrefetchScalarGridSpec(
            num_scalar_prefetch=2,
            grid=(r // MOE_TILE, n_f),
            in_specs=[pl.BlockSpec((MOE_TILE, d), lambda t, f, te, nu: (t, 0)),
                      pl.BlockSpec((1, d, tf), lambda t, f, te, nu: (te[t], 0, fidx(t, f, te, nu))),
                      pl.BlockSpec((1, d, tf), lambda t, f, te, nu: (te[t], 0, fidx(t, f, te, nu))),
                      pl.BlockSpec((1, tf, d), lambda t, f, te, nu: (te[t], fidx(t, f, te, nu), 0))],
            out_specs=pl.BlockSpec((MOE_TILE, d), lambda t, f, te, nu: (t, 0)),
            scratch_shapes=[pltpu.VMEM((MOE_TILE, d), F32)]),
        out_shape=jax.ShapeDtypeStruct((r, d), F32),
        compiler_params=_params("arbitrary", "arbitrary"),
    )(tile_expert, n_used, x_sorted, wg, wu, wd)


def _moe_combine_kernel(y1_ref, y2_ref, rt_ref, h_ref, gt_ref, o_ref):
    rt = rt_ref[...]
    out = rt[:, 2:3] * y1_ref[...] + rt[:, 3:4] * y2_ref[...]
    o_ref[...] = h_ref[...] + gt_ref[0] * out


def _moe_combine(y_pair, route, h, gt_tab, tile_vec):
    t, d = h.shape
    tm = 512
    return pl.pallas_call(
        _moe_combine_kernel, name="moe_combine",
        grid=(t // tm,),
        in_specs=[pl.BlockSpec((tm, d), lambda i: (i, 0)), pl.BlockSpec((tm, d), lambda i: (i + t // tm, 0)),
                  pl.BlockSpec((tm, LANES), lambda i: (i, 0)), pl.BlockSpec((tm, d), lambda i: (i, 0)),
                  pl.BlockSpec((1, 1, d), lambda i: (tile_vec(i, tm), 0, 0))],
        out_specs=pl.BlockSpec((tm, d), lambda i: (i, 0)),
        out_shape=jax.ShapeDtypeStruct((t, d), F32),
        compiler_params=_params("arbitrary"),
    )(y_pair, y_pair, route, h, gt_tab)


def _moe(u, route, wg, wu, wd, h, gt_tab, tile_vec):
    t, d = u.shape
    n_rows = 2 * t + N_EXPERTS * MOE_TILE
    pos, row_token, tile_expert, n_used = _route_plan(route, n_rows)
    x_sorted = _sc_gather(u, row_token)
    y_sorted = _grouped_swiglu(x_sorted, tile_expert, n_used, wg, wu, wd)
    return _moe_combine(_sc_gather(y_sorted, pos), route, h, gt_tab, tile_vec)


def _final_norm_kernel(h_ref, g_ref, o_ref):
    o_ref[0] = _rms(h_ref[0], g_ref[...])


def _final_norm(h, g):
    b, l, d = h.shape
    tm = min(l, 1024)
    return pl.pallas_call(
        _final_norm_kernel, name="final_norm",
        grid=(b, l // tm),
        in_specs=[pl.BlockSpec((1, tm, d), lambda bi, i: (bi, i, 0)), pl.BlockSpec((1, d), lambda bi, i: (0, 0))],
        out_specs=pl.BlockSpec((1, tm, d), lambda bi, i: (bi, i, 0)),
        out_shape=jax.ShapeDtypeStruct((b, l, d), F32),
        compiler_params=_params("arbitrary", "arbitrary"),
    )(h, g.reshape(1, d))


def _rope_tables(length, head_dim):
    t = jnp.arange(length)
    row = (t // GRID_W).astype(F32)
    col = (t % GRID_W).astype(F32)
    n = head_dim // 4
    inv = ROPE_THETA ** (-jnp.arange(n, dtype=F32) / n)
    ang = jnp.concatenate([row[:, None] * inv, col[:, None] * inv], axis=-1)
    cos = jnp.concatenate([jnp.cos(ang)] * 2, axis=-1)
    sin = jnp.concatenate([-jnp.sin(ang), jnp.sin(ang)], axis=-1)
    reps = LANES // head_dim
    return jnp.tile(cos, (1, reps)), jnp.tile(sin, (1, reps))


def _hyena_tables(length):
    k = jnp.arange(length, dtype=jnp.int32)
    period = 4 * length
    step = 2.0 * math.pi / period
    lo = jnp.arange(64, dtype=jnp.int32)
    hi = jnp.arange(length // 64, dtype=jnp.int32)

    def tabs(x, y_hi, y_lo):
        def cs(y):
            ang = ((x[:, None] * y[None, :]) % period).astype(F32) * step
            return jnp.cos(ang)[:, :, None], jnp.sin(ang)[:, :, None]
        (ca, sa), (cb, sb) = cs(y_hi), cs(y_lo)
        cb, sb = jnp.swapaxes(cb, 1, 2), jnp.swapaxes(sb, 1, 2)
        c = (ca * cb - sa * sb).reshape(length, length)
        s = (sa * cb + ca * sb).reshape(length, length)
        return c.astype(BF16), s.astype(BF16)

    c_tab, s_tab = tabs(2 * k + 1, 64 * hi, lo)
    ci_tab, si_tab = tabs(k, 128 * hi, 2 * lo + 1)
    t = jnp.linspace(0.0, 1.0, length, dtype=F32)[:, None]
    bands = (HY_EMB - 1) // 2
    ang = (2.0 * math.pi / length) * jnp.arange(length, dtype=F32)[:, None] \
        * jnp.linspace(1e-4, bands - 1, bands, dtype=F32)
    feats = jnp.concatenate([t, jnp.cos(ang), -jnp.sin(ang),
                             jnp.zeros((length, LANES - HY_EMB), F32)], axis=-1)
    max_decay = math.log(HY_DECAY_TARGET) / HY_FAST_DECAY
    min_decay = math.log(HY_DECAY_TARGET) / HY_SLOW_DECAY
    deltas = jnp.abs(jnp.linspace(min_decay, max_decay, HY_WIDTH, dtype=F32)).reshape(1, HY_WIDTH)
    return c_tab, s_tab, ci_tab, si_tab, feats, deltas


def _block_diag_ones(n, group):
    i = jnp.arange(n) // group
    return (i[:, None] == i[None, :]).astype(BF16)


def kernel(x, c, ctx, c_ctx, w_ada, b_ada, g_mix, g_ffn, w_in, w_out, hy_conv_w, hy_conv_b, hy_w1, hy_b1, hy_w2, hy_b2, hy_w3, hy_freq, hy_skip, diff_lq1, diff_lk1, diff_lq2, diff_lk2, diff_subln, gqa_qnorm, gqa_knorm, ffn_wg, ffn_wu, ffn_wd, moe_router, moe_wg, moe_wu, moe_wd, g_final):
    b, seq, d = x.shape
    n_ctx = ctx.shape[1]
    depth = w_ada.shape[0]

    rope_tabs = _rope_tables(seq, DIFF_QK_DIM) + _rope_tables(seq, GQA_HEAD_DIM)
    hy_tabs = _hyena_tables(seq)
    hy_tabs_c = _hyena_tables(n_ctx)
    bd128 = _block_diag_ones(LANES, GQA_HEAD_DIM)
    bd256 = _block_diag_ones(DIFF_WIDTH, DIFF_V_DIM)

    rows = 16
    cc = jnp.zeros((rows, d), F32).at[:b].set(c).at[b].set(c_ctx)
    mods = _mods(cc, w_ada, b_ada)

    def mod_vecs(i):
        lat = [mods[i, :b, j * d:(j + 1) * d].reshape(b, 1, d) for j in range(6)]
        cx = [jnp.broadcast_to(mods[i, b, j * d:(j + 1) * d].reshape(1, 1, d), (b, 1, d)) for j in range(6)]
        return lat, cx

    def kv_layouts(kd, vd, kg, vg):
        l = kd.shape[1]
        kdt = jnp.swapaxes(kd, 1, 2)[:, None]
        vd = vd[:, None]
        kgt = jnp.swapaxes(kg, 1, 2).reshape(b, 2, 1, GQA_HEAD_DIM, l)
        kgt = jnp.broadcast_to(kgt, (b, 2, GQA_REP, GQA_HEAD_DIM, l)).reshape(b, 2, GQA_REP * GQA_HEAD_DIM, l)
        vgx = jnp.swapaxes(vg.reshape(b, l, 2, 1, GQA_HEAD_DIM), 1, 2)
        vgx = jnp.broadcast_to(vgx, (b, 2, l, GQA_REP, GQA_HEAD_DIM)).reshape(b, 2, l, GQA_REP * GQA_HEAD_DIM)
        return kdt, vd, kgt, vgx

    h, hc = x, ctx
    for i in range(depth):
        last = i == depth - 1
        lam_init = 0.8 - 0.6 * math.exp(-0.3 * i)
        (sh_m, sc_m, gt_m, sh_f, sc_f, gt_f), (csh_m, csc_m, cgt_m, csh_f, csc_f, cgt_f) = mod_vecs(i)
        w_in_bf = w_in[i].astype(BF16)
        w_out_bf = w_out[i].astype(BF16)
        g_m = g_mix[i].reshape(1, d)
        g_f = g_ffn[i].reshape(1, d)
        qn = jnp.tile(gqa_qnorm[i], LANES // GQA_HEAD_DIM).reshape(1, LANES)
        kn = jnp.tile(gqa_knorm[i], LANES // GQA_HEAD_DIM).reshape(1, LANES)
        subln = jnp.tile(diff_subln[i], DIFF_HEADS).reshape(1, DIFF_WIDTH)
        lqk = jnp.stack([diff_lq1[i], diff_lk1[i], diff_lq2[i], diff_lk2[i]]).astype(F32)
        hy_p = (hy_conv_w[i], hy_conv_b[i], hy_w1[i], hy_b1[i], hy_w2[i], hy_b2[i], hy_w3[i],
                hy_freq[i], hy_skip[i])
        moe = i % 2 == 1
        router_pad = None
        if moe:
            router_pad = jnp.zeros((d, LANES), F32).at[:, :N_EXPERTS].set(moe_router[i // 2])

        zhy, qd, kd, vd, qg, kg, vg = _inproj(h, sc_m, sh_m, g_m, w_in_bf, qn, kn, bd128, rope_tabs)
        czhy, cqd, ckd, cvd, cqg, ckg, cvg = _inproj(hc, csc_m, csh_m, g_m, w_in_bf, qn, kn, bd128, None)
        kdt, vdx, kgt, vgx = kv_layouts(kd, vd, kg, vg)
        ckdt, cvdx, ckgt, cvgx = kv_layouts(ckd, cvd, ckg, cvg)

        cat = lambda a, c_, axis: jnp.concatenate([a, c_], axis=axis)
        y_hy = _hyena(zhy, hy_p, hy_tabs)
        o_d = _diff_attention(lqk, qd, cat(kdt, ckdt, 3), cat(vdx, cvdx, 2), subln, bd256, lam_init)
        o_g = _gqa_attention(qg, cat(kgt, ckgt, 3), cat(vgx, cvgx, 2))
        mixed = _outproj(y_hy, o_d, o_g, h, gt_m, w_out_bf, g_f, sc_f, sh_f, router_pad)
        if not last:
            yc_hy = _hyena(czhy, hy_p, hy_tabs_c)
            oc_d = _diff_attention(lqk, cqd, ckdt, cvdx, subln, bd256, lam_init)
            oc_g = _gqa_attention(cqg, ckgt, cvgx)
            cmixed = _outproj(yc_hy, oc_d, oc_g, hc, cgt_m, w_out_bf, g_f, csc_f, csh_f, router_pad)

        if moe:
            wg, wu, wd = (w[i // 2].astype(BF16) for w in (moe_wg, moe_wu, moe_wd))
            flat = lambda a: a.reshape(-1, a.shape[-1])
            gt_tab = jnp.concatenate([gt_f, cgt_f[:1]], axis=0)
            if last:
                tile_vec = lambda t, tm: t // (seq // tm)
                h = _moe(flat(mixed[1]), flat(mixed[2]), wg, wu, wd, flat(mixed[0]), gt_tab,
                         tile_vec).reshape(b, seq, d)
            else:
                n_c = b * n_ctx
                tile_vec = lambda t, tm: jnp.where(t < n_c // tm, b, (t - n_c // tm) // (seq // tm))
                both = lambda c_, a: jnp.concatenate([flat(c_), flat(a)], axis=0)
                h_all = _moe(both(cmixed[1], mixed[1]), both(cmixed[2], mixed[2]), wg, wu, wd,
                             both(cmixed[0], mixed[0]), gt_tab, tile_vec)
                hc = h_all[:n_c].reshape(b, n_ctx, d)
                h = h_all[n_c:].reshape(b, seq, d)
        else:
            wg, wu, wd = (w[i // 2].astype(BF16) for w in (ffn_wg, ffn_wu, ffn_wd))
            h = _ffn(mixed[1], wg, wu, wd, mixed[0], gt_f)
            if not last:
                hc = _ffn(cmixed[1], wg, wu, wd, cmixed[0], cgt_f)
    return _final_norm(h, g_final)
```

```python
import functools
import math

import jax
import jax.numpy as jnp
from jax import lax
from jax.experimental import pallas as pl
from jax.experimental.pallas import tpu as pltpu
from jax.experimental.pallas import tpu_sc as plsc

F32 = jnp.float32
BF16 = jnp.bfloat16
HIGHEST = lax.Precision.HIGHEST

D_MODEL = 1024
GRID_W = 64
ROPE_THETA = 10000.0
NORM_EPS = 1e-6

HY_WIDTH = 256
HY_EMB = 33
HY_FAST_DECAY = 0.3
HY_SLOW_DECAY = 1.5
HY_DECAY_TARGET = 1e-2

DIFF_HEADS = 4
DIFF_QK_DIM = 32
DIFF_V_DIM = 64
DIFF_WIDTH = 256
GQA_HEAD_DIM = 64
GQA_WIDTH = 512
GQA_KV_WIDTH = 128
GQA_REP = 4

OFF_DQ = 3 * HY_WIDTH
OFF_DK = OFF_DQ + DIFF_WIDTH
OFF_DV = OFF_DK + DIFF_WIDTH
OFF_GQ = OFF_DV + DIFF_WIDTH
OFF_GK = OFF_GQ + GQA_WIDTH
OFF_GV = OFF_GK + GQA_KV_WIDTH
IN_COLS = OFF_GV + GQA_KV_WIDTH

N_EXPERTS = 8
LOG2_E = math.log2(math.e)
LANES = 128
VMEM_LIMIT = 56 * 1024 * 1024


def _params(*sem):
    return pltpu.CompilerParams(dimension_semantics=sem, vmem_limit_bytes=VMEM_LIMIT)


def _dot(a, b):
    return jnp.dot(a, b, preferred_element_type=F32)


def _dot_hi(a, b):
    return jnp.dot(a, b, preferred_element_type=F32, precision=HIGHEST)


def _rms(x, g):
    ms = jnp.mean(x * x, axis=-1, keepdims=True)
    return x * lax.rsqrt(ms + NORM_EPS) * g


def _silu(x):
    return x * jax.nn.sigmoid(x)


def _group_mean_sq(x, ones_bd, width):
    return _dot((x * x).astype(BF16), ones_bd) * (1.0 / width)


def _mods_kernel(c_ref, w_ref, b_ref, o_ref):
    o_ref[0] = _dot_hi(_silu(c_ref[...]), w_ref[0]) + b_ref[0]


def _mods(cc, w_ada, b_ada):
    depth, d, n = w_ada.shape
    tn = 1536
    return pl.pallas_call(
        _mods_kernel, name="mods",
        grid=(depth, n // tn),
        in_specs=[pl.BlockSpec(cc.shape, lambda i, j: (0, 0)),
                  pl.BlockSpec((1, d, tn), lambda i, j: (i, 0, j)),
                  pl.BlockSpec((1, 1, tn), lambda i, j: (i, 0, j))],
        out_specs=pl.BlockSpec((1, cc.shape[0], tn), lambda i, j: (i, 0, j)),
        out_shape=jax.ShapeDtypeStruct((depth, cc.shape[0], n), F32),
        compiler_params=_params("arbitrary", "arbitrary"),
    )(cc, w_ada, b_ada.reshape(depth, 1, n))


def _rope128(x, cos, sin_signed, half):
    lane = lax.broadcasted_iota(jnp.int32, x.shape, 1)
    first = (lane & (2 * half - 1)) < half
    swapped = jnp.where(first, pltpu.roll(x, LANES - half, 1), pltpu.roll(x, half, 1))
    return x * cos + swapped * sin_signed


def _inproj_kernel(*refs, rope):
    h_ref, sc_ref, sh_ref, g_ref, w_ref, qn_ref, kn_ref, bd_ref = refs[:8]
    if rope:
        cd_ref, sd_ref, cg_ref, sg_ref = refs[8:12]
    zhy_ref, qd_ref, kd_ref, vd_ref, qg_ref, kg_ref, vg_ref = refs[-7:]
    u = _rms(h_ref[0], g_ref[...]) * (1.0 + sc_ref[0]) + sh_ref[0]
    z = _dot(u.astype(BF16), w_ref[...])
    zhy_ref[0] = z[:, :OFF_DQ]

    def piece(off, j):
        return z[:, off + LANES * j: off + LANES * (j + 1)]

    for j in range(DIFF_WIDTH // LANES):
        q, k = piece(OFF_DQ, j), piece(OFF_DK, j)
        if rope:
            q = _rope128(q, cd_ref[...], sd_ref[...], DIFF_QK_DIM // 2)
            k = _rope128(k, cd_ref[...], sd_ref[...], DIFF_QK_DIM // 2)
        qd_ref[0, :, LANES * j: LANES * (j + 1)] = (q * (LOG2_E * DIFF_QK_DIM ** -0.5)).astype(BF16)
        kd_ref[0, :, LANES * j: LANES * (j + 1)] = k.astype(BF16)
    vd_ref[0] = z[:, OFF_DV:OFF_GQ].astype(BF16)

    def gqa_piece(x, gain):
        ms = _group_mean_sq(x, bd_ref[...], GQA_HEAD_DIM)
        x = x * lax.rsqrt(ms + NORM_EPS) * gain
        if rope:
            x = _rope128(x, cg_ref[...], sg_ref[...], GQA_HEAD_DIM // 2)
        return x

    for j in range(GQA_WIDTH // LANES):
        q = gqa_piece(piece(OFF_GQ, j), qn_ref[...])
        qg_ref[0, :, LANES * j: LANES * (j + 1)] = (q * (LOG2_E * GQA_HEAD_DIM ** -0.5)).astype(BF16)
    kg_ref[0] = gqa_piece(piece(OFF_GK, 0), kn_ref[...]).astype(BF16)
    vg_ref[0] = z[:, OFF_GV:].astype(BF16)


def _inproj(h, sc, sh, g, w_bf, qn, kn, bd, rope_tabs):
    b, l, d = h.shape
    tm = min(l, 512)
    row = lambda bi, i: (bi, i, 0)
    vec = lambda bi, i: (bi, 0, 0)
    const = lambda bi, i: (0, 0)
    in_specs = [pl.BlockSpec((1, tm, d), row), pl.BlockSpec((1, 1, d), vec), pl.BlockSpec((1, 1, d), vec),
                pl.BlockSpec((1, d), const), pl.BlockSpec((d, IN_COLS), const),
                pl.BlockSpec((1, LANES), const), pl.BlockSpec((1, LANES), const),
                pl.BlockSpec((LANES, LANES), const)]
    args = [h, sc, sh, g, w_bf, qn, kn, bd]
    if rope_tabs is not None:
        in_specs += [pl.BlockSpec((tm, LANES), lambda bi, i: (i, 0))] * 4
        args += list(rope_tabs)
    widths = [(OFF_DQ, F32), (DIFF_WIDTH, BF16), (DIFF_WIDTH, BF16), (DIFF_WIDTH, BF16),
              (GQA_WIDTH, BF16), (GQA_KV_WIDTH, BF16), (GQA_KV_WIDTH, BF16)]
    return pl.pallas_call(
        functools.partial(_inproj_kernel, rope=rope_tabs is not None), name="inproj",
        grid=(b, l // tm),
        in_specs=in_specs,
        out_specs=[pl.BlockSpec((1, tm, w), row) for w, _ in widths],
        out_shape=[jax.ShapeDtypeStruct((b, l, w), dt) for w, dt in widths],
        compiler_params=_params("arbitrary", "arbitrary"),
    )(*args)


def _hy_filter_kernel(f_ref, w1_ref, b1_ref, w2_ref, b2_ref, w3_ref, fr_ref, dl_ref, hs_ref, hd_ref):
    f = f_ref[...]
    fr = fr_ref[...]
    a = jnp.sin(fr * (_dot_hi(f, w1_ref[...]) + b1_ref[...]))
    a = jnp.sin(fr * (_dot_hi(a, w2_ref[...]) + b2_ref[...]))
    hf = _dot_hi(a, w3_ref[...])
    decay = jnp.exp(-f[:, 0:1] * dl_ref[...])
    h_fwd = hf[:, :HY_WIDTH] * decay
    h_bwd = hf[:, HY_WIDTH:] * decay
    row = lax.broadcasted_iota(jnp.int32, h_bwd.shape, 0) + pl.program_id(0) * f.shape[0]
    h_bwd = jnp.where(row == 0, 0.0, h_bwd)
    hs_ref[...] = (h_fwd + h_bwd).astype(BF16)
    hd_ref[...] = (h_bwd - h_fwd).astype(BF16)


def _hy_spectrum_kernel(c_ref, s_ref, hs_ref, hd_ref, gre_ref, gim_ref, *, scale):
    gre_ref[...] = _dot(c_ref[...], hs_ref[...]) * scale
    gim_ref[...] = _dot(s_ref[...], hd_ref[...]) * scale


def _hy_pre_kernel(z_ref, zp_ref, zn_ref, cw_ref, cb_ref, wb_ref, wf_ref, x0_ref, *, nt):
    ti = pl.program_id(1)
    z = z_ref[0]
    tl = z.shape[0]
    row = lax.broadcasted_iota(jnp.int32, z.shape, 0)
    prev_row = jnp.where(ti > 0, zp_ref[0, 7:8, :], 0.0)
    next_row = jnp.where(ti < nt - 1, zn_ref[0, 0:1, :], 0.0)
    z_prev = jnp.where(row == 0, prev_row, pltpu.roll(z, 1, 0))
    z_next = jnp.where(row == tl - 1, next_row, pltpu.roll(z, tl - 1, 0))
    cw = cw_ref[...]
    y = cb_ref[...] + z_prev * cw[0:1] + z * cw[1:2] + z_next * cw[2:3]
    x0, x1, v = y[:, :HY_WIDTH], y[:, HY_WIDTH:2 * HY_WIDTH], y[:, 2 * HY_WIDTH:]
    w = v * x1
    wb_ref[...] = w.astype(BF16)
    wf_ref[0] = w
    x0_ref[0] = x0


def _hy_fwd_kernel(c_ref, s_ref, x_ref, gre_ref, gim_ref, yre_ref, yim_ref, *, reps):
    x = x_ref[...]
    a = _dot(c_ref[...], x)
    b = _dot(s_ref[...], x)
    gre = jnp.concatenate([gre_ref[...]] * reps, axis=1)
    gim = jnp.concatenate([gim_ref[...]] * reps, axis=1)
    yre_ref[...] = (a * gre + b * gim).astype(BF16)
    yim_ref[...] = (a * gim - b * gre).astype(BF16)


def _hy_inv_kernel(ci_ref, si_ref, yre_ref, yim_ref, wf_ref, x0_ref, skip_ref, o_ref, *, reps):
    y = _dot(ci_ref[...], yre_ref[...]) - _dot(si_ref[...], yim_ref[...])
    for r in range(reps):
        yr = y[:, r * HY_WIDTH:(r + 1) * HY_WIDTH]
        o_ref[r] = ((yr + wf_ref[r] * skip_ref[...]) * x0_ref[r]).astype(BF16)


def _hyena(zhy, p, tabs):
    conv_w, conv_b, w1, b1, w2, b2, w3, freq, skip = p
    c_tab, s_tab, ci_tab, si_tab, feats, deltas = tabs
    b, l, _ = zhy.shape
    c = HY_WIDTH
    tl = min(l, 512)
    nt = l // tl
    hid = w2.shape[0]
    const1 = lambda i: (0, 0)
    w1p = jnp.zeros((LANES, hid), F32).at[:HY_EMB].set(w1)
    hs, hd = pl.pallas_call(
        _hy_filter_kernel, name="hy_filter",
        grid=(nt,),
        in_specs=[pl.BlockSpec((tl, LANES), lambda i: (i, 0)),
                  pl.BlockSpec((LANES, hid), const1), pl.BlockSpec((1, hid), const1),
                  pl.BlockSpec((hid, hid), const1), pl.BlockSpec((1, hid), const1),
                  pl.BlockSpec((hid, 2 * c), const1), pl.BlockSpec((1, hid), const1),
                  pl.BlockSpec((1, c), const1)],
        out_specs=[pl.BlockSpec((tl, c), lambda i: (i, 0))] * 2,
        out_shape=[jax.ShapeDtypeStruct((l, c), BF16)] * 2,
        compiler_params=_params("arbitrary"),
    )(feats, w1p, b1.reshape(1, hid), w2, b2.reshape(1, hid), w3, freq.reshape(1, hid), deltas)

    gre, gim = pl.pallas_call(
        functools.partial(_hy_spectrum_kernel, scale=1.0 / l), name="hy_spectrum",
        grid=(nt,),
        in_specs=[pl.BlockSpec((tl, l), lambda i: (i, 0)), pl.BlockSpec((tl, l), lambda i: (i, 0)),
                  pl.BlockSpec((l, c), const1), pl.BlockSpec((l, c), const1)],
        out_specs=[pl.BlockSpec((tl, c), lambda i: (i, 0))] * 2,
        out_shape=[jax.ShapeDtypeStruct((l, c), F32)] * 2,
        compiler_params=_params("arbitrary"),
    )(c_tab, s_tab, hs, hd)

    halo = 8
    wb, wf, x0 = pl.pallas_call(
        functools.partial(_hy_pre_kernel, nt=nt), name="hy_pre",
        grid=(b, nt),
        in_specs=[pl.BlockSpec((1, tl, 3 * c), lambda bi, i: (bi, i, 0)),
                  pl.BlockSpec((1, halo, 3 * c), lambda bi, i: (bi, jnp.maximum(i * (tl // halo) - 1, 0), 0)),
                  pl.BlockSpec((1, halo, 3 * c),
                               lambda bi, i: (bi, jnp.minimum((i + 1) * (tl // halo), l // halo - 1), 0)),
                  pl.BlockSpec((3, 3 * c), lambda bi, i: (0, 0)),
                  pl.BlockSpec((1, 3 * c), lambda bi, i: (0, 0))],
        out_specs=[pl.BlockSpec((tl, c), lambda bi, i: (i, bi)),
                   pl.BlockSpec((1, tl, c), lambda bi, i: (bi, i, 0)),
                   pl.BlockSpec((1, tl, c), lambda bi, i: (bi, i, 0))],
        out_shape=[jax.ShapeDtypeStruct((l, b * c), BF16),
                   jax.ShapeDtypeStruct((b, l, c), F32),
                   jax.ShapeDtypeStruct((b, l, c), F32)],
        compiler_params=_params("arbitrary", "arbitrary"),
    )(zhy, zhy, zhy, conv_w, conv_b.reshape(1, 3 * c))

    reps = 2
    tn = reps * c
    nj = b * c // tn
    yre, yim = pl.pallas_call(
        functools.partial(_hy_fwd_kernel, reps=reps), name="hy_fwd",
        grid=(nj, nt),
        in_specs=[pl.BlockSpec((tl, l), lambda j, i: (i, 0)), pl.BlockSpec((tl, l), lambda j, i: (i, 0)),
                  pl.BlockSpec((l, tn), lambda j, i: (0, j)),
                  pl.BlockSpec((tl, c), lambda j, i: (i, 0)), pl.BlockSpec((tl, c), lambda j, i: (i, 0))],
        out_specs=[pl.BlockSpec((tl, tn), lambda j, i: (i, j))] * 2,
        out_shape=[jax.ShapeDtypeStruct((l, b * c), BF16)] * 2,
        compiler_params=_params("arbitrary", "arbitrary"),
    )(c_tab, s_tab, wb, gre, gim)

    return pl.pallas_call(
        functools.partial(_hy_inv_kernel, reps=reps), name="hy_inv",
        grid=(nj, nt),
        in_specs=[pl.BlockSpec((tl, l), lambda j, i: (i, 0)), pl.BlockSpec((tl, l), lambda j, i: (i, 0)),
                  pl.BlockSpec((l, tn), lambda j, i: (0, j)), pl.BlockSpec((l, tn), lambda j, i: (0, j)),
                  pl.BlockSpec((reps, tl, c), lambda j, i: (j, i, 0)),
                  pl.BlockSpec((reps, tl, c), lambda j, i: (j, i, 0)),
                  pl.BlockSpec((1, c), lambda j, i: (0, 0))],
        out_specs=pl.BlockSpec((reps, tl, c), lambda j, i: (j, i, 0)),
        out_shape=jax.ShapeDtypeStruct((b, l, c), BF16),
        compiler_params=_params("arbitrary", "arbitrary"),
    )(ci_tab, si_tab, yre, yim, wf, x0, skip.reshape(1, c))


def _lane_mask(width, shift, idx):
    lane = lax.broadcasted_iota(jnp.int32, (1, width), 1)
    return jnp.where((lane >> shift) == idx, 1.0, 0.0).astype(BF16)


ATTN_WIDTH = 256
ATTN_CHUNK = 512


def _key_chunks(lk):
    return [(c0, min(ATTN_CHUNK, lk - c0)) for c0 in range(0, lk, ATTN_CHUNK)]


ATTN_STACK = 4
ATTN_TQ = 128
ATTN_STEP_ROWS = 256
ATTN_ROW_BLOCK = 64


def _qk_stage(q, shift, first, kt_ref, qs_scr, s_scr):
    tq = q.shape[0]
    for i in range(ATTN_STACK):
        qs_scr[i * tq:(i + 1) * tq, :] = q * _lane_mask(ATTN_WIDTH, shift, first + i)
    qs = qs_scr[...]
    for c0, ck in _key_chunks(kt_ref.shape[-1]):
        s_scr[:, c0:c0 + ck] = _dot(qs, kt_ref[:, c0:c0 + ck])


def _exp_stage(s_scr, p_scr):
    n_tiles = s_scr.shape[-1] // LANES
    inv = []
    for r0 in range(0, s_scr.shape[0], ATTN_ROW_BLOCK):
        rows = slice(r0, r0 + ATTN_ROW_BLOCK)
        m = s_scr[rows, 0:LANES]
        for j in range(1, n_tiles):
            m = jnp.maximum(m, s_scr[rows, j * LANES:(j + 1) * LANES])
        mx = jnp.max(m, axis=-1, keepdims=True)
        l = None
        for j in range(n_tiles):
            e = jnp.exp2(s_scr[rows, j * LANES:(j + 1) * LANES] - mx)
            p_scr[rows, j * LANES:(j + 1) * LANES] = e.astype(BF16)
            l = e if l is None else l + e
        inv.append(1.0 / jnp.sum(l, axis=-1, keepdims=True))
    return jnp.concatenate(inv, axis=0)


def _attention_passes(passes, next_first, consume, kt_ref, v_ref, scratch):
    qs, ss, ps = scratch[0:2], scratch[2:4], scratch[4:6]
    n = len(passes)
    assert n % 2 == 0
    tq = passes[0][0].shape[0]

    def qk(p, i):
        q, shift, first, g = p
        _qk_stage(q, shift, first, kt_ref.at[0, g], qs[i % 2], ss[i % 2])

    @pl.when(pl.program_id(1) == 0)
    def _():
        qk(passes[0], 0)

    for i in range(n):
        qk(passes[i + 1] if i + 1 < n else next_first, i + 1)
        inv = _exp_stage(ss[i % 2], ps[i % 2])
        pv = _dot(ps[i % 2][...], v_ref[0, passes[i][3]]) * inv
        consume(i, [pv[r * tq:(r + 1) * tq] for r in range(ATTN_STACK)])


def _diff_kernel(lqk_ref, q_ref, qn_ref, kt_ref, v_ref, o_ref, *scratch, lam_init):
    x = lqk_ref[...]
    lam = (jnp.exp(jnp.sum(x[0:1] * x[1:2], axis=-1, keepdims=True))
           - jnp.exp(jnp.sum(x[2:3] * x[3:4], axis=-1, keepdims=True)) + lam_init)
    tq = min(ATTN_TQ, q_ref.shape[1])
    group = lax.broadcasted_iota(jnp.int32, (tq, ATTN_WIDTH), 1) >> 6
    halves = 2 * DIFF_HEADS // ATTN_STACK
    passes = []
    for t in range(q_ref.shape[1] // tq):
        q = q_ref[0, t * tq:(t + 1) * tq, :]
        passes += [(q, 5, half * ATTN_STACK, 0) for half in range(halves)]
    next_first = (qn_ref[0, 0:tq, :], 5, 0, 0)
    acc = {}

    def consume(i, pvs):
        t, half = divmod(i, halves)
        o1, o2 = acc.pop(t, (jnp.zeros(group.shape, F32),) * 2)
        for r, pv in enumerate(pvs):
            head = (half * ATTN_STACK + r) >> 1
            if r & 1 == 0:
                o1 = jnp.where(group == head, pv, o1)
            else:
                o2 = jnp.where(group == head, pv, o2)
        if half + 1 < halves:
            acc[t] = (o1, o2)
            return
        o_ref[0, t * tq:(t + 1) * tq, :] = (o1 - lam * o2).astype(BF16)

    _attention_passes(passes, next_first, consume, kt_ref, v_ref, scratch)


def _gqa_kernel(q_ref, qn_ref, kt_ref, v_ref, o_ref, *scratch):
    tq = min(ATTN_TQ, q_ref.shape[1])
    group = lax.broadcasted_iota(jnp.int32, (tq, ATTN_WIDTH), 1) >> 6
    groups = q_ref.shape[2] // ATTN_WIDTH
    passes = []
    for t in range(q_ref.shape[1] // tq):
        for g in range(groups):
            passes.append((q_ref[0, t * tq:(t + 1) * tq, g * ATTN_WIDTH:(g + 1) * ATTN_WIDTH], 6, 0, g))

    def consume(i, pvs):
        t, g = divmod(i, groups)
        o = jnp.zeros(group.shape, F32)
        for r, pv in enumerate(pvs):
            o = jnp.where(group == r, pv, o)
        o_ref[0, t * tq:(t + 1) * tq, g * ATTN_WIDTH:(g + 1) * ATTN_WIDTH] = o.astype(BF16)

    next_first = (qn_ref[0, 0:tq, 0:ATTN_WIDTH], 6, 0, 0)
    _attention_passes(passes, next_first, consume, kt_ref, v_ref, scratch)


def _attention(kernel_fn, q, kt, v, extra=()):
    b, lq, width = q.shape
    groups, w, lk = kt.shape[1:]
    rows = min(lq, ATTN_STEP_ROWS)
    pass_rows = ATTN_STACK * min(ATTN_TQ, rows)
    whole = lambda a: pl.BlockSpec(a.shape, lambda bi, i: (0, 0))
    steps = lq // rows
    return pl.pallas_call(
        kernel_fn, name="diff_attn" if extra else "gqa_attn",
        grid=(b, steps),
        in_specs=[whole(a) for a in extra[:1]]
        + [pl.BlockSpec((1, rows, width), lambda bi, i: (bi, i, 0)),
           pl.BlockSpec((1, rows, width), lambda bi, i: (bi, jnp.minimum(i + 1, steps - 1), 0)),
           pl.BlockSpec((1, groups, w, lk), lambda bi, i: (bi, 0, 0, 0), pipeline_mode=pl.Buffered(1)),
           pl.BlockSpec((1, groups, lk, w), lambda bi, i: (bi, 0, 0, 0), pipeline_mode=pl.Buffered(1))]
        + [whole(a) for a in extra[1:]],
        out_specs=pl.BlockSpec((1, rows, width), lambda bi, i: (bi, i, 0)),
        out_shape=jax.ShapeDtypeStruct((b, lq, width), BF16),
        scratch_shapes=[pltpu.VMEM((pass_rows, w), BF16)] * 2 + [pltpu.VMEM((pass_rows, lk), F32)] * 2
        + [pltpu.VMEM((pass_rows, lk), BF16)] * 2,
        compiler_params=_params("arbitrary", "arbitrary"),
    )(*extra[:1], q, q, kt, v, *extra[1:])


def _diff_attention(lqk, q, kt, v, lam_init):
    return _attention(functools.partial(_diff_kernel, lam_init=lam_init), q, kt, v, (lqk,))


def _gqa_attention(q, kt, v):
    return _attention(_gqa_kernel, q, kt, v)


def _top2_route(logits):
    lane = lax.broadcasted_iota(jnp.int32, logits.shape, 1).astype(F32)
    neg = jnp.float32(-jnp.inf)
    lg = jnp.where(lane < N_EXPERTS, logits, neg)
    m1 = jnp.max(lg, axis=-1, keepdims=True)
    i1 = jnp.min(jnp.where(lg == m1, lane, float(LANES)), axis=-1, keepdims=True)
    lg2 = jnp.where(lane == i1, neg, lg)
    m2 = jnp.max(lg2, axis=-1, keepdims=True)
    i2 = jnp.min(jnp.where(lg2 == m2, lane, float(LANES)), axis=-1, keepdims=True)
    e2 = jnp.exp(m2 - m1)
    w1 = 1.0 / (1.0 + e2)
    return (jnp.where(lane == 0, i1, 0.0) + jnp.where(lane == 1, i2, 0.0)
            + jnp.where(lane == 2, w1, 0.0) + jnp.where(lane == 3, e2 * w1, 0.0))


def _outproj_kernel(*refs, moe):
    yhy_ref, od_ref, og_ref, h_ref, gt_ref, w_ref, gf_ref, sc_ref, sh_ref, sg_ref, bd_ref = refs[:11]
    if moe:
        rt_ref, hn_ref, u_ref, gates_ref = refs[11:]
    else:
        hn_ref, u_ref = refs[11:]
    od = od_ref[0].astype(F32)
    od = od * lax.rsqrt(_group_mean_sq(od, bd_ref[...], DIFF_V_DIM) + NORM_EPS) * sg_ref[...]
    y = (_dot(yhy_ref[0], w_ref[0:HY_WIDTH, :])
         + _dot(od.astype(BF16), w_ref[HY_WIDTH:HY_WIDTH + DIFF_WIDTH, :])
         + _dot(og_ref[0], w_ref[HY_WIDTH + DIFF_WIDTH:, :]))
    hn = h_ref[0] + gt_ref[0] * y
    hn_ref[0] = hn
    u = _rms(hn, gf_ref[...]) * (1.0 + sc_ref[0]) + sh_ref[0]
    u_ref[0] = u.astype(u_ref.dtype)
    if moe:
        gates_ref[0] = _top2_route(_dot_hi(u, rt_ref[...]))


def _outproj(yhy, od, og, h, gt, w_bf, gf, sc, sh, sub_g, bd256, router_pad):
    b, l, d = h.shape
    tm = min(l, 512)
    row = lambda bi, i: (bi, i, 0)
    vec = lambda bi, i: (bi, 0, 0)
    const = lambda bi, i: (0, 0)
    moe = router_pad is not None
    in_specs = [pl.BlockSpec((1, tm, HY_WIDTH), row), pl.BlockSpec((1, tm, DIFF_WIDTH), row),
                pl.BlockSpec((1, tm, GQA_WIDTH), row), pl.BlockSpec((1, tm, d), row),
                pl.BlockSpec((1, 1, d), vec), pl.BlockSpec((d, d), const), pl.BlockSpec((1, d), const),
                pl.BlockSpec((1, 1, d), vec), pl.BlockSpec((1, 1, d), vec),
                pl.BlockSpec((1, DIFF_WIDTH), const), pl.BlockSpec((DIFF_WIDTH, DIFF_WIDTH), const)]
    args = [yhy, od, og, h, gt, w_bf, gf, sc, sh, sub_g, bd256]
    out_specs = [pl.BlockSpec((1, tm, d), row), pl.BlockSpec((1, tm, d), row)]
    out_shape = [jax.ShapeDtypeStruct((b, l, d), F32), jax.ShapeDtypeStruct((b, l, d), F32 if moe else BF16)]
    if moe:
        in_specs.append(pl.BlockSpec((d, LANES), const))
        args.append(router_pad)
        out_specs.append(pl.BlockSpec((1, tm, LANES), row))
        out_shape.append(jax.ShapeDtypeStruct((b, l, LANES), F32))
    return pl.pallas_call(
        functools.partial(_outproj_kernel, moe=moe), name="outproj",
        grid=(b, l // tm), in_specs=in_specs, out_specs=out_specs, out_shape=out_shape,
        compiler_params=_params("arbitrary", "arbitrary"),
    )(*args)


def _ffn_kernel(u_ref, wg_ref, wu_ref, wd_ref, h_ref, gt_ref, o_ref, acc_ref):
    f = pl.program_id(2)

    @pl.when(f == 0)
    def _():
        acc_ref[...] = jnp.zeros_like(acc_ref)

    u = u_ref[0]
    hid = _silu(_dot(u, wg_ref[...])) * _dot(u, wu_ref[...])
    acc_ref[...] += _dot(hid.astype(BF16), wd_ref[...])

    @pl.when(f == pl.num_programs(2) - 1)
    def _():
        o_ref[0] = h_ref[0] + gt_ref[0] * acc_ref[...]


def _ffn(u, wg, wu, wd, h, gt):
    b, l, d = h.shape
    dff = wg.shape[1]
    tm = min(l, 512)
    tf = dff // 2
    row = lambda bi, i, f: (bi, i, 0)
    return pl.pallas_call(
        _ffn_kernel, name="ffn",
        grid=(b, l // tm, dff // tf),
        in_specs=[pl.BlockSpec((1, tm, d), row),
                  pl.BlockSpec((d, tf), lambda bi, i, f: (0, f)), pl.BlockSpec((d, tf), lambda bi, i, f: (0, f)),
                  pl.BlockSpec((tf, d), lambda bi, i, f: (f, 0)),
                  pl.BlockSpec((1, tm, d), row), pl.BlockSpec((1, 1, d), lambda bi, i, f: (bi, 0, 0))],
        out_specs=pl.BlockSpec((1, tm, d), row),
        out_shape=jax.ShapeDtypeStruct((b, l, d), F32),
        scratch_shapes=[pltpu.VMEM((tm, d), F32)],
        compiler_params=_params("arbitrary", "arbitrary", "arbitrary"),
    )(u, wg, wu, wd, h, gt)


MOE_TILE = 512
SC_CHUNK = 64


def _sc_gather(table, idx):
    info = plsc.get_sparse_core_info()
    nc, ns = info.num_cores, info.num_subcores
    n, d = idx.shape[0], table.shape[1]
    per_worker = n // (nc * ns)
    assert per_worker * nc * ns == n and per_worker % SC_CHUNK == 0
    mesh = plsc.VectorSubcoreMesh(core_axis_name="c", subcore_axis_name="s")

    @functools.partial(
        pl.kernel, mesh=mesh,
        out_type=jax.ShapeDtypeStruct((n, d), table.dtype),
        scratch_types=[pltpu.VMEM((SC_CHUNK,), jnp.int32), pltpu.VMEM((SC_CHUNK, d), table.dtype),
                       pltpu.SemaphoreType.DMA],
    )
    def gather_kernel(table_hbm, idx_hbm, out_hbm, idx_v, rows_v, sem):
        base = (lax.axis_index("s") * nc + lax.axis_index("c")) * per_worker

        @pl.loop(0, per_worker // SC_CHUNK)
        def _(j):
            off = pl.multiple_of(base + j * SC_CHUNK, 8)
            pltpu.sync_copy(idx_hbm.at[pl.ds(off, SC_CHUNK)], idx_v)
            pltpu.async_copy(table_hbm.at[idx_v], rows_v, sem).wait()
            pltpu.sync_copy(rows_v, out_hbm.at[pl.ds(off, SC_CHUNK)])

    return gather_kernel(table, idx)


def _route_plan(route, n_rows):
    t = route.shape[0]
    e = route[:, :2].astype(jnp.int32).T.reshape(2 * t)
    onehot = (e[:, None] == jnp.arange(N_EXPERTS, dtype=jnp.int32)[None, :]).astype(jnp.int32)
    csum = jnp.cumsum(onehot, axis=0)
    counts = csum[-1]
    rank = jnp.take_along_axis(csum, e[:, None], axis=1)[:, 0] - 1
    padded = (counts + MOE_TILE - 1) // MOE_TILE * MOE_TILE
    ends = jnp.cumsum(padded)
    pos = (ends - padded)[e] + rank
    row_token = jnp.zeros((n_rows,), jnp.int32).at[pos].set(jnp.arange(2 * t, dtype=jnp.int32) % t)
    tile_start = jnp.arange(n_rows // MOE_TILE, dtype=jnp.int32) * MOE_TILE
    tile_expert = jnp.minimum(jnp.sum((tile_start[:, None] >= ends[None, :]).astype(jnp.int32), axis=1),
                              N_EXPERTS - 1)
    n_used = (ends[-1] // MOE_TILE).reshape(1)
    return pos, row_token, tile_expert, n_used


def _gmm_kernel(te_ref, nu_ref, x_ref, wg_ref, wu_ref, wd_ref, o_ref, acc_ref):
    t, f = pl.program_id(0), pl.program_id(1)

    @pl.when(t < nu_ref[0])
    def _():
        @pl.when(f == 0)
        def _():
            acc_ref[...] = jnp.zeros_like(acc_ref)

        x = x_ref[...].astype(BF16)
        hid = _silu(_dot(x, wg_ref[0])) * _dot(x, wu_ref[0])
        acc_ref[...] += _dot(hid.astype(BF16), wd_ref[0])

        @pl.when(f == pl.num_programs(1) - 1)
        def _():
            o_ref[...] = acc_ref[...]


def _grouped_swiglu(x_sorted, tile_expert, n_used, wg, wu, wd):
    r, d = x_sorted.shape
    dff = wg.shape[2]
    n_f = 2
    tf = dff // n_f
    fidx = lambda t, f, te, nu: jnp.where(t < nu[0], f, n_f - 1)
    return pl.pallas_call(
        _gmm_kernel, name="grouped_swiglu",
        grid_spec=pltpu.PrefetchScalarGridSpec(
            num_scalar_prefetch=2,
            grid=(r // MOE_TILE, n_f),
            in_specs=[pl.BlockSpec((MOE_TILE, d), lambda t, f, te, nu: (t, 0)),
                      pl.BlockSpec((1, d, tf), lambda t, f, te, nu: (te[t], 0, fidx(t, f, te, nu))),
                      pl.BlockSpec((1, d, tf), lambda t, f, te, nu: (te[t], 0, fidx(t, f, te, nu))),
                      pl.BlockSpec((1, tf, d), lambda t, f, te, nu: (te[t], fidx(t, f, te, nu), 0))],
            out_specs=pl.BlockSpec((MOE_TILE, d), lambda t, f, te, nu: (t, 0)),
            scratch_shapes=[pltpu.VMEM((MOE_TILE, d), F32)]),
        out_shape=jax.ShapeDtypeStruct((r, d), F32),
        compiler_params=_params("arbitrary", "arbitrary"),
    )(tile_expert, n_used, x_sorted, wg, wu, wd)


def _moe_combine_kernel(y1_ref, y2_ref, rt_ref, h_ref, gt_ref, o_ref):
    rt = rt_ref[...]
    out = rt[:, 2:3] * y1_ref[...] + rt[:, 3:4] * y2_ref[...]
    o_ref[...] = h_ref[...] + gt_ref[0] * out


def _moe_combine(y_pair, route, h, gt_tab, tile_vec):
    t, d = h.shape
    tm = 512
    return pl.pallas_call(
        _moe_combine_kernel, name="moe_combine",
        grid=(t // tm,),
        in_specs=[pl.BlockSpec((tm, d), lambda i: (i, 0)), pl.BlockSpec((tm, d), lambda i: (i + t // tm, 0)),
                  pl.BlockSpec((tm, LANES), lambda i: (i, 0)), pl.BlockSpec((tm, d), lambda i: (i, 0)),
                  pl.BlockSpec((1, 1, d), lambda i: (tile_vec(i, tm), 0, 0))],
        out_specs=pl.BlockSpec((tm, d), lambda i: (i, 0)),
        out_shape=jax.ShapeDtypeStruct((t, d), F32),
        compiler_params=_params("arbitrary"),
    )(y_pair, y_pair, route, h, gt_tab)


def _moe(u, route, wg, wu, wd, h, gt_tab, tile_vec):
    t, d = u.shape
    n_rows = 2 * t + N_EXPERTS * MOE_TILE
    pos, row_token, tile_expert, n_used = _route_plan(route, n_rows)
    x_sorted = _sc_gather(u, row_token)
    y_sorted = _grouped_swiglu(x_sorted, tile_expert, n_used, wg, wu, wd)
    return _moe_combine(_sc_gather(y_sorted, pos), route, h, gt_tab, tile_vec)


def _final_norm_kernel(h_ref, g_ref, o_ref):
    o_ref[0] = _rms(h_ref[0], g_ref[...])


def _final_norm(h, g):
    b, l, d = h.shape
    tm = min(l, 1024)
    return pl.pallas_call(
        _final_norm_kernel, name="final_norm",
        grid=(b, l // tm),
        in_specs=[pl.BlockSpec((1, tm, d), lambda bi, i: (bi, i, 0)), pl.BlockSpec((1, d), lambda bi, i: (0, 0))],
        out_specs=pl.BlockSpec((1, tm, d), lambda bi, i: (bi, i, 0)),
        out_shape=jax.ShapeDtypeStruct((b, l, d), F32),
        compiler_params=_params("arbitrary", "arbitrary"),
    )(h, g.reshape(1, d))


def _rope_tables(length, head_dim):
    t = jnp.arange(length)
    row = (t // GRID_W).astype(F32)
    col = (t % GRID_W).astype(F32)
    n = head_dim // 4
    inv = ROPE_THETA ** (-jnp.arange(n, dtype=F32) / n)
    ang = jnp.concatenate([row[:, None] * inv, col[:, None] * inv], axis=-1)
    cos = jnp.concatenate([jnp.cos(ang)] * 2, axis=-1)
    sin = jnp.concatenate([-jnp.sin(ang), jnp.sin(ang)], axis=-1)
    reps = LANES // head_dim
    return jnp.tile(cos, (1, reps)), jnp.tile(sin, (1, reps))


def _hyena_tables(length):
    k = jnp.arange(length, dtype=jnp.int32)
    period = 4 * length
    step = 2.0 * math.pi / period
    lo = jnp.arange(64, dtype=jnp.int32)
    hi = jnp.arange(length // 64, dtype=jnp.int32)

    def tabs(x, y_hi, y_lo):
        def cs(y):
            ang = ((x[:, None] * y[None, :]) % period).astype(F32) * step
            return jnp.cos(ang)[:, :, None], jnp.sin(ang)[:, :, None]
        (ca, sa), (cb, sb) = cs(y_hi), cs(y_lo)
        cb, sb = jnp.swapaxes(cb, 1, 2), jnp.swapaxes(sb, 1, 2)
        c = (ca * cb - sa * sb).reshape(length, length)
        s = (sa * cb + ca * sb).reshape(length, length)
        return c.astype(BF16), s.astype(BF16)

    c_tab, s_tab = tabs(2 * k + 1, 64 * hi, lo)
    ci_tab, si_tab = tabs(k, 128 * hi, 2 * lo + 1)
    t = jnp.linspace(0.0, 1.0, length, dtype=F32)[:, None]
    bands = (HY_EMB - 1) // 2
    ang = (2.0 * math.pi / length) * jnp.arange(length, dtype=F32)[:, None] \
        * jnp.linspace(1e-4, bands - 1, bands, dtype=F32)
    feats = jnp.concatenate([t, jnp.cos(ang), -jnp.sin(ang),
                             jnp.zeros((length, LANES - HY_EMB), F32)], axis=-1)
    max_decay = math.log(HY_DECAY_TARGET) / HY_FAST_DECAY
    min_decay = math.log(HY_DECAY_TARGET) / HY_SLOW_DECAY
    deltas = jnp.abs(jnp.linspace(min_decay, max_decay, HY_WIDTH, dtype=F32)).reshape(1, HY_WIDTH)
    return c_tab, s_tab, ci_tab, si_tab, feats, deltas


def _block_diag_ones(n, group):
    i = jnp.arange(n) // group
    return (i[:, None] == i[None, :]).astype(BF16)


def kernel(x, c, ctx, c_ctx, w_ada, b_ada, g_mix, g_ffn, w_in, w_out, hy_conv_w, hy_conv_b, hy_w1, hy_b1, hy_w2, hy_b2, hy_w3, hy_freq, hy_skip, diff_lq1, diff_lk1, diff_lq2, diff_lk2, diff_subln, gqa_qnorm, gqa_knorm, ffn_wg, ffn_wu, ffn_wd, moe_router, moe_wg, moe_wu, moe_wd, g_final):
    b, seq, d = x.shape
    n_ctx = ctx.shape[1]
    depth = w_ada.shape[0]

    rope_tabs = _rope_tables(seq, DIFF_QK_DIM) + _rope_tables(seq, GQA_HEAD_DIM)
    hy_tabs = _hyena_tables(seq)
    hy_tabs_c = _hyena_tables(n_ctx)
    bd128 = _block_diag_ones(LANES, GQA_HEAD_DIM)
    bd256 = _block_diag_ones(DIFF_WIDTH, DIFF_V_DIM)

    rows = 16
    cc = jnp.zeros((rows, d), F32).at[:b].set(c).at[b].set(c_ctx)
    mods = _mods(cc, w_ada, b_ada)

    def mod_vecs(i):
        lat = [mods[i, :b, j * d:(j + 1) * d].reshape(b, 1, d) for j in range(6)]
        cx = [jnp.broadcast_to(mods[i, b, j * d:(j + 1) * d].reshape(1, 1, d), (b, 1, d)) for j in range(6)]
        return lat, cx

    def kv_layouts(kd, vd, kg, vg):
        l = kd.shape[1]
        kdt = jnp.swapaxes(kd, 1, 2)[:, None]
        vd = vd[:, None]
        kgt = jnp.swapaxes(kg, 1, 2).reshape(b, 2, 1, GQA_HEAD_DIM, l)
        kgt = jnp.broadcast_to(kgt, (b, 2, GQA_REP, GQA_HEAD_DIM, l)).reshape(b, 2, GQA_REP * GQA_HEAD_DIM, l)
        vgx = jnp.swapaxes(vg.reshape(b, l, 2, 1, GQA_HEAD_DIM), 1, 2)
        vgx = jnp.broadcast_to(vgx, (b, 2, l, GQA_REP, GQA_HEAD_DIM)).reshape(b, 2, l, GQA_REP * GQA_HEAD_DIM)
        return kdt, vd, kgt, vgx

    h, hc = x, ctx
    for i in range(depth):
        last = i == depth - 1
        lam_init = 0.8 - 0.6 * math.exp(-0.3 * i)
        (sh_m, sc_m, gt_m, sh_f, sc_f, gt_f), (csh_m, csc_m, cgt_m, csh_f, csc_f, cgt_f) = mod_vecs(i)
        w_in_bf = w_in[i].astype(BF16)
        w_out_bf = w_out[i].astype(BF16)
        g_m = g_mix[i].reshape(1, d)
        g_f = g_ffn[i].reshape(1, d)
        qn = jnp.tile(gqa_qnorm[i], LANES // GQA_HEAD_DIM).reshape(1, LANES)
        kn = jnp.tile(gqa_knorm[i], LANES // GQA_HEAD_DIM).reshape(1, LANES)
        subln = jnp.tile(diff_subln[i], DIFF_HEADS).reshape(1, DIFF_WIDTH) * (1.0 - lam_init)
        lqk = jnp.stack([diff_lq1[i], diff_lk1[i], diff_lq2[i], diff_lk2[i]]).astype(F32)
        hy_p = (hy_conv_w[i], hy_conv_b[i], hy_w1[i], hy_b1[i], hy_w2[i], hy_b2[i], hy_w3[i],
                hy_freq[i], hy_skip[i])
        moe = i % 2 == 1
        router_pad = None
        if moe:
            router_pad = jnp.zeros((d, LANES), F32).at[:, :N_EXPERTS].set(moe_router[i // 2])

        zhy, qd, kd, vd, qg, kg, vg = _inproj(h, sc_m, sh_m, g_m, w_in_bf, qn, kn, bd128, rope_tabs)
        czhy, cqd, ckd, cvd, cqg, ckg, cvg = _inproj(hc, csc_m, csh_m, g_m, w_in_bf, qn, kn, bd128, None)
        kdt, vdx, kgt, vgx = kv_layouts(kd, vd, kg, vg)
        ckdt, cvdx, ckgt, cvgx = kv_layouts(ckd, cvd, ckg, cvg)

        cat = lambda a, c_, axis: jnp.concatenate([a, c_], axis=axis)
        y_hy = _hyena(zhy, hy_p, hy_tabs)
        o_d = _diff_attention(lqk, qd, cat(kdt, ckdt, 3), cat(vdx, cvdx, 2), lam_init)
        o_g = _gqa_attention(qg, cat(kgt, ckgt, 3), cat(vgx, cvgx, 2))
        mixed = _outproj(y_hy, o_d, o_g, h, gt_m, w_out_bf, g_f, sc_f, sh_f, subln, bd256, router_pad)
        if not last:
            yc_hy = _hyena(czhy, hy_p, hy_tabs_c)
            oc_d = _diff_attention(lqk, cqd, ckdt, cvdx, lam_init)
            oc_g = _gqa_attention(cqg, ckgt, cvgx)
            cmixed = _outproj(yc_hy, oc_d, oc_g, hc, cgt_m, w_out_bf, g_f, csc_f, csh_f, subln, bd256,
                              router_pad)

        if moe:
            wg, wu, wd = (w[i // 2].astype(BF16) for w in (moe_wg, moe_wu, moe_wd))
            flat = lambda a: a.reshape(-1, a.shape[-1])
            gt_tab = jnp.concatenate([gt_f, cgt_f[:1]], axis=0)
            if last:
                tile_vec = lambda t, tm: t // (seq // tm)
                h = _moe(flat(mixed[1]), flat(mixed[2]), wg, wu, wd, flat(mixed[0]), gt_tab,
                         tile_vec).reshape(b, seq, d)
            else:
                n_c = b * n_ctx
                tile_vec = lambda t, tm: jnp.where(t < n_c // tm, b, (t - n_c // tm) // (seq // tm))
                both = lambda c_, a: jnp.concatenate([flat(c_), flat(a)], axis=0)
                h_all = _moe(both(cmixed[1], mixed[1]), both(cmixed[2], mixed[2]), wg, wu, wd,
                             both(cmixed[0], mixed[0]), gt_tab, tile_vec)
                hc = h_all[:n_c].reshape(b, n_ctx, d)
                h = h_all[n_c:].reshape(b, seq, d)
        else:
            wg, wu, wd = (w[i // 2].astype(BF16) for w in (ffn_wg, ffn_wu, ffn_wd))
            h = _ffn(mixed[1], wg, wu, wd, mixed[0], gt_f)
            if not last:
                hc = _ffn(cmixed[1], wg, wu, wd, cmixed[0], cgt_f)
    return _final_norm(h, g_final)
```

```python
import functools
import math

import jax
import jax.numpy as jnp
from jax import lax
from jax.experimental import pallas as pl
from jax.experimental.pallas import tpu as pltpu
from jax.experimental.pallas import tpu_sc as plsc

F32 = jnp.float32
BF16 = jnp.bfloat16
HIGHEST = lax.Precision.HIGHEST

D_MODEL = 1024
GRID_W = 64
ROPE_THETA = 10000.0
NORM_EPS = 1e-6

HY_WIDTH = 256
HY_EMB = 33
HY_FAST_DECAY = 0.3
HY_SLOW_DECAY = 1.5
HY_DECAY_TARGET = 1e-2

DIFF_HEADS = 4
DIFF_QK_DIM = 32
DIFF_V_DIM = 64
DIFF_WIDTH = 256
GQA_HEAD_DIM = 64
GQA_WIDTH = 512
GQA_KV_WIDTH = 128
GQA_REP = 4

OFF_DQ = 3 * HY_WIDTH
OFF_DK = OFF_DQ + DIFF_WIDTH
OFF_DV = OFF_DK + DIFF_WIDTH
OFF_GQ = OFF_DV + DIFF_WIDTH
OFF_GK = OFF_GQ + GQA_WIDTH
OFF_GV = OFF_GK + GQA_KV_WIDTH
IN_COLS = OFF_GV + GQA_KV_WIDTH

N_EXPERTS = 8
LOG2_E = math.log2(math.e)
LANES = 128
VMEM_LIMIT = 56 * 1024 * 1024


def _params(*sem):
    return pltpu.CompilerParams(dimension_semantics=sem, vmem_limit_bytes=VMEM_LIMIT)


def _dot(a, b):
    return jnp.dot(a, b, preferred_element_type=F32)


def _dot_hi(a, b):
    return jnp.dot(a, b, preferred_element_type=F32, precision=HIGHEST)


def _rms(x, g):
    ms = jnp.mean(x * x, axis=-1, keepdims=True)
    return x * lax.rsqrt(ms + NORM_EPS) * g


def _silu(x):
    return x * jax.nn.sigmoid(x)


def _group_mean_sq(x, ones_bd, width):
    return _dot((x * x).astype(BF16), ones_bd) * (1.0 / width)


def _mods_kernel(c_ref, w_ref, b_ref, o_ref):
    o_ref[0] = _dot_hi(_silu(c_ref[...]), w_ref[0]) + b_ref[0]


def _mods(cc, w_ada, b_ada):
    depth, d, n = w_ada.shape
    tn = 1536
    return pl.pallas_call(
        _mods_kernel, name="mods",
        grid=(depth, n // tn),
        in_specs=[pl.BlockSpec(cc.shape, lambda i, j: (0, 0)),
                  pl.BlockSpec((1, d, tn), lambda i, j: (i, 0, j)),
                  pl.BlockSpec((1, 1, tn), lambda i, j: (i, 0, j))],
        out_specs=pl.BlockSpec((1, cc.shape[0], tn), lambda i, j: (i, 0, j)),
        out_shape=jax.ShapeDtypeStruct((depth, cc.shape[0], n), F32),
        compiler_params=_params("arbitrary", "arbitrary"),
    )(cc, w_ada, b_ada.reshape(depth, 1, n))


def _rope128(x, cos, sin_signed, half):
    lane = lax.broadcasted_iota(jnp.int32, x.shape, 1)
    first = (lane & (2 * half - 1)) < half
    swapped = jnp.where(first, pltpu.roll(x, LANES - half, 1), pltpu.roll(x, half, 1))
    return x * cos + swapped * sin_signed


def _inproj_kernel(*refs, rope):
    h_ref, sc_ref, sh_ref, g_ref, w_ref, qn_ref, kn_ref, bd_ref = refs[:8]
    if rope:
        cd_ref, sd_ref, cg_ref, sg_ref = refs[8:12]
    zhy_ref, qd_ref, kd_ref, vd_ref, qg_ref, kg_ref, vg_ref = refs[-7:]
    u = _rms(h_ref[0], g_ref[...]) * (1.0 + sc_ref[0]) + sh_ref[0]
    z = _dot(u.astype(BF16), w_ref[...])
    zhy_ref[0] = z[:, :OFF_DQ]

    def piece(off, j):
        return z[:, off + LANES * j: off + LANES * (j + 1)]

    for j in range(DIFF_WIDTH // LANES):
        q, k = piece(OFF_DQ, j), piece(OFF_DK, j)
        if rope:
            q = _rope128(q, cd_ref[...], sd_ref[...], DIFF_QK_DIM // 2)
            k = _rope128(k, cd_ref[...], sd_ref[...], DIFF_QK_DIM // 2)
        qd_ref[0, :, LANES * j: LANES * (j + 1)] = (q * (LOG2_E * DIFF_QK_DIM ** -0.5)).astype(BF16)
        kd_ref[0, :, LANES * j: LANES * (j + 1)] = k.astype(BF16)
    vd_ref[0] = z[:, OFF_DV:OFF_GQ].astype(BF16)

    def gqa_piece(x, gain):
        ms = _group_mean_sq(x, bd_ref[...], GQA_HEAD_DIM)
        x = x * lax.rsqrt(ms + NORM_EPS) * gain
        if rope:
            x = _rope128(x, cg_ref[...], sg_ref[...], GQA_HEAD_DIM // 2)
        return x

    for j in range(GQA_WIDTH // LANES):
        q = gqa_piece(piece(OFF_GQ, j), qn_ref[...])
        qg_ref[0, :, LANES * j: LANES * (j + 1)] = (q * (LOG2_E * GQA_HEAD_DIM ** -0.5)).astype(BF16)
    kg_ref[0] = gqa_piece(piece(OFF_GK, 0), kn_ref[...]).astype(BF16)
    vg_ref[0] = z[:, OFF_GV:].astype(BF16)


def _inproj(h, sc, sh, g, w_bf, qn, kn, bd, rope_tabs):
    b, l, d = h.shape
    tm = min(l, 512)
    row = lambda bi, i: (bi, i, 0)
    vec = lambda bi, i: (bi, 0, 0)
    const = lambda bi, i: (0, 0)
    in_specs = [pl.BlockSpec((1, tm, d), row), pl.BlockSpec((1, 1, d), vec), pl.BlockSpec((1, 1, d), vec),
                pl.BlockSpec((1, d), const), pl.BlockSpec((d, IN_COLS), const),
                pl.BlockSpec((1, LANES), const), pl.BlockSpec((1, LANES), const),
                pl.BlockSpec((LANES, LANES), const)]
    args = [h, sc, sh, g, w_bf, qn, kn, bd]
    if rope_tabs is not None:
        in_specs += [pl.BlockSpec((tm, LANES), lambda bi, i: (i, 0))] * 4
        args += list(rope_tabs)
    widths = [(OFF_DQ, F32), (DIFF_WIDTH, BF16), (DIFF_WIDTH, BF16), (DIFF_WIDTH, BF16),
              (GQA_WIDTH, BF16), (GQA_KV_WIDTH, BF16), (GQA_KV_WIDTH, BF16)]
    return pl.pallas_call(
        functools.partial(_inproj_kernel, rope=rope_tabs is not None), name="inproj",
        grid=(b, l // tm),
        in_specs=in_specs,
        out_specs=[pl.BlockSpec((1, tm, w), row) for w, _ in widths],
        out_shape=[jax.ShapeDtypeStruct((b, l, w), dt) for w, dt in widths],
        compiler_params=_params("arbitrary", "arbitrary"),
    )(*args)


def _hy_filter_kernel(f_ref, w1_ref, b1_ref, w2_ref, b2_ref, w3_ref, fr_ref, dl_ref, hs_ref, hd_ref):
    f = f_ref[...]
    fr = fr_ref[...]
    a = jnp.sin(fr * (_dot_hi(f, w1_ref[...]) + b1_ref[...]))
    a = jnp.sin(fr * (_dot_hi(a, w2_ref[...]) + b2_ref[...]))
    hf = _dot_hi(a, w3_ref[...])
    decay = jnp.exp(-f[:, 0:1] * dl_ref[...])
    h_fwd = hf[:, :HY_WIDTH] * decay
    h_bwd = hf[:, HY_WIDTH:] * decay
    row = lax.broadcasted_iota(jnp.int32, h_bwd.shape, 0) + pl.program_id(0) * f.shape[0]
    h_bwd = jnp.where(row == 0, 0.0, h_bwd)
    hs_ref[...] = (h_fwd + h_bwd).astype(BF16)
    hd_ref[...] = (h_bwd - h_fwd).astype(BF16)


def _hy_spectrum_kernel(c_ref, s_ref, hs_ref, hd_ref, gre_ref, gim_ref, *, scale):
    gre_ref[...] = _dot(c_ref[...], hs_ref[...]) * scale
    gim_ref[...] = _dot(s_ref[...], hd_ref[...]) * scale


def _hy_pre_kernel(z_ref, zp_ref, zn_ref, cw_ref, cb_ref, wb_ref, wf_ref, x0_ref, *, nt):
    ti = pl.program_id(1)
    z = z_ref[0]
    tl = z.shape[0]
    row = lax.broadcasted_iota(jnp.int32, z.shape, 0)
    prev_row = jnp.where(ti > 0, zp_ref[0, 7:8, :], 0.0)
    next_row = jnp.where(ti < nt - 1, zn_ref[0, 0:1, :], 0.0)
    z_prev = jnp.where(row == 0, prev_row, pltpu.roll(z, 1, 0))
    z_next = jnp.where(row == tl - 1, next_row, pltpu.roll(z, tl - 1, 0))
    cw = cw_ref[...]
    y = cb_ref[...] + z_prev * cw[0:1] + z * cw[1:2] + z_next * cw[2:3]
    x0, x1, v = y[:, :HY_WIDTH], y[:, HY_WIDTH:2 * HY_WIDTH], y[:, 2 * HY_WIDTH:]
    w = v * x1
    wb_ref[...] = w.astype(BF16)
    wf_ref[0] = w
    x0_ref[0] = x0


def _hy_fwd_kernel(c_ref, s_ref, x_ref, gre_ref, gim_ref, yre_ref, yim_ref, *, reps):
    x = x_ref[...]
    a = _dot(c_ref[...], x)
    b = _dot(s_ref[...], x)
    gre = jnp.concatenate([gre_ref[...]] * reps, axis=1)
    gim = jnp.concatenate([gim_ref[...]] * reps, axis=1)
    yre_ref[...] = (a * gre + b * gim).astype(BF16)
    yim_ref[...] = (a * gim - b * gre).astype(BF16)


def _hy_inv_kernel(ci_ref, si_ref, yre_ref, yim_ref, wf_ref, x0_ref, skip_ref, o_ref, *, reps):
    y = _dot(ci_ref[...], yre_ref[...]) - _dot(si_ref[...], yim_ref[...])
    for r in range(reps):
        yr = y[:, r * HY_WIDTH:(r + 1) * HY_WIDTH]
        o_ref[r] = ((yr + wf_ref[r] * skip_ref[...]) * x0_ref[r]).astype(BF16)


def _hyena(zhy, p, tabs):
    conv_w, conv_b, w1, b1, w2, b2, w3, freq, skip = p
    c_tab, s_tab, ci_tab, si_tab, feats, deltas = tabs
    b, l, _ = zhy.shape
    c = HY_WIDTH
    tl = min(l, 512)
    nt = l // tl
    hid = w2.shape[0]
    const1 = lambda i: (0, 0)
    w1p = jnp.zeros((LANES, hid), F32).at[:HY_EMB].set(w1)
    hs, hd = pl.pallas_call(
        _hy_filter_kernel, name="hy_filter",
        grid=(nt,),
        in_specs=[pl.BlockSpec((tl, LANES), lambda i: (i, 0)),
                  pl.BlockSpec((LANES, hid), const1), pl.BlockSpec((1, hid), const1),
                  pl.BlockSpec((hid, hid), const1), pl.BlockSpec((1, hid), const1),
                  pl.BlockSpec((hid, 2 * c), const1), pl.BlockSpec((1, hid), const1),
                  pl.BlockSpec((1, c), const1)],
        out_specs=[pl.BlockSpec((tl, c), lambda i: (i, 0))] * 2,
        out_shape=[jax.ShapeDtypeStruct((l, c), BF16)] * 2,
        compiler_params=_params("arbitrary"),
    )(feats, w1p, b1.reshape(1, hid), w2, b2.reshape(1, hid), w3, freq.reshape(1, hid), deltas)

    gre, gim = pl.pallas_call(
        functools.partial(_hy_spectrum_kernel, scale=1.0 / l), name="hy_spectrum",
        grid=(nt,),
        in_specs=[pl.BlockSpec((tl, l), lambda i: (i, 0)), pl.BlockSpec((tl, l), lambda i: (i, 0)),
                  pl.BlockSpec((l, c), const1), pl.BlockSpec((l, c), const1)],
        out_specs=[pl.BlockSpec((tl, c), lambda i: (i, 0))] * 2,
        out_shape=[jax.ShapeDtypeStruct((l, c), F32)] * 2,
        compiler_params=_params("arbitrary"),
    )(c_tab, s_tab, hs, hd)

    halo = 8
    wb, wf, x0 = pl.pallas_call(
        functools.partial(_hy_pre_kernel, nt=nt), name="hy_pre",
        grid=(b, nt),
        in_specs=[pl.BlockSpec((1, tl, 3 * c), lambda bi, i: (bi, i, 0)),
                  pl.BlockSpec((1, halo, 3 * c), lambda bi, i: (bi, jnp.maximum(i * (tl // halo) - 1, 0), 0)),
                  pl.BlockSpec((1, halo, 3 * c),
                               lambda bi, i: (bi, jnp.minimum((i + 1) * (tl // halo), l // halo - 1), 0)),
                  pl.BlockSpec((3, 3 * c), lambda bi, i: (0, 0)),
                  pl.BlockSpec((1, 3 * c), lambda bi, i: (0, 0))],
        out_specs=[pl.BlockSpec((tl, c), lambda bi, i: (i, bi)),
                   pl.BlockSpec((1, tl, c), lambda bi, i: (bi, i, 0)),
                   pl.BlockSpec((1, tl, c), lambda bi, i: (bi, i, 0))],
        out_shape=[jax.ShapeDtypeStruct((l, b * c), BF16),
                   jax.ShapeDtypeStruct((b, l, c), F32),
                   jax.ShapeDtypeStruct((b, l, c), F32)],
        compiler_params=_params("arbitrary", "arbitrary"),
    )(zhy, zhy, zhy, conv_w, conv_b.reshape(1, 3 * c))

    reps = 2
    tn = reps * c
    nj = b * c // tn
    yre, yim = pl.pallas_call(
        functools.partial(_hy_fwd_kernel, reps=reps), name="hy_fwd",
        grid=(nj, nt),
        in_specs=[pl.BlockSpec((tl, l), lambda j, i: (i, 0)), pl.BlockSpec((tl, l), lambda j, i: (i, 0)),
                  pl.BlockSpec((l, tn), lambda j, i: (0, j)),
                  pl.BlockSpec((tl, c), lambda j, i: (i, 0)), pl.BlockSpec((tl, c), lambda j, i: (i, 0))],
        out_specs=[pl.BlockSpec((tl, tn), lambda j, i: (i, j))] * 2,
        out_shape=[jax.ShapeDtypeStruct((l, b * c), BF16)] * 2,
        compiler_params=_params("arbitrary", "arbitrary"),
    )(c_tab, s_tab, wb, gre, gim)

    return pl.pallas_call(
        functools.partial(_hy_inv_kernel, reps=reps), name="hy_inv",
        grid=(nj, nt),
        in_specs=[pl.BlockSpec((tl, l), lambda j, i: (i, 0)), pl.BlockSpec((tl, l), lambda j, i: (i, 0)),
                  pl.BlockSpec((l, tn), lambda j, i: (0, j)), pl.BlockSpec((l, tn), lambda j, i: (0, j)),
                  pl.BlockSpec((reps, tl, c), lambda j, i: (j, i, 0)),
                  pl.BlockSpec((reps, tl, c), lambda j, i: (j, i, 0)),
                  pl.BlockSpec((1, c), lambda j, i: (0, 0))],
        out_specs=pl.BlockSpec((reps, tl, c), lambda j, i: (j, i, 0)),
        out_shape=jax.ShapeDtypeStruct((b, l, c), BF16),
        compiler_params=_params("arbitrary", "arbitrary"),
    )(ci_tab, si_tab, yre, yim, wf, x0, skip.reshape(1, c))


def _lane_mask(width, shift, idx):
    lane = lax.broadcasted_iota(jnp.int32, (1, width), 1)
    return jnp.where((lane >> shift) == idx, 1.0, 0.0).astype(BF16)


ATTN_WIDTH = 256
ATTN_CHUNK = 512


def _key_chunks(lk):
    return [(c0, min(ATTN_CHUNK, lk - c0)) for c0 in range(0, lk, ATTN_CHUNK)]


ATTN_STACK = 4
ATTN_TQ = 128
ATTN_STEP_ROWS = 256
ATTN_ROW_BLOCK = 64


def _qk_stage(q, shift, first, kt_refs, qs_scr, s_scr):
    tq = q.shape[0]
    for i in range(ATTN_STACK):
        qs_scr[i * tq:(i + 1) * tq, :] = q * _lane_mask(ATTN_WIDTH, shift, first + i)
    qs = qs_scr[...]
    base = 0
    for kt_ref in kt_refs:
        for c0, ck in _key_chunks(kt_ref.shape[-1]):
            s_scr[:, base + c0:base + c0 + ck] = _dot(qs, kt_ref[:, c0:c0 + ck])
        base += kt_ref.shape[-1]


def _exp_stage(s_scr, p_scr):
    n_tiles = s_scr.shape[-1] // LANES
    inv = []
    for r0 in range(0, s_scr.shape[0], ATTN_ROW_BLOCK):
        rows = slice(r0, r0 + ATTN_ROW_BLOCK)
        m = s_scr[rows, 0:LANES]
        for j in range(1, n_tiles):
            m = jnp.maximum(m, s_scr[rows, j * LANES:(j + 1) * LANES])
        mx = jnp.max(m, axis=-1, keepdims=True)
        l = None
        for j in range(n_tiles):
            e = jnp.exp2(s_scr[rows, j * LANES:(j + 1) * LANES] - mx)
            p_scr[rows, j * LANES:(j + 1) * LANES] = e.astype(BF16)
            l = e if l is None else l + e
        inv.append(1.0 / jnp.sum(l, axis=-1, keepdims=True))
    return jnp.concatenate(inv, axis=0)


def _attention_passes(passes, next_first, consume, kt_refs, v_refs, scratch):
    qs, ss, ps = scratch[0:2], scratch[2:4], scratch[4:6]
    n = len(passes)
    assert n % 2 == 0
    tq = passes[0][0].shape[0]

    def qk(p, i):
        q, shift, first, g = p
        _qk_stage(q, shift, first, [kt.at[0, g] for kt in kt_refs], qs[i % 2], ss[i % 2])

    @pl.when(pl.program_id(1) == 0)
    def _():
        qk(passes[0], 0)

    for i in range(n):
        qk(passes[i + 1] if i + 1 < n else next_first, i + 1)
        inv = _exp_stage(ss[i % 2], ps[i % 2])
        pv, base = None, 0
        for v_ref in v_refs:
            lk = v_ref.shape[2]
            part = _dot(ps[i % 2][:, base:base + lk], v_ref[0, passes[i][3]])
            pv = part if pv is None else pv + part
            base += lk
        pv = pv * inv
        consume(i, [pv[r * tq:(r + 1) * tq] for r in range(ATTN_STACK)])


def _diff_kernel(lqk_ref, q_ref, qn_ref, *refs, n_kv, lam_init):
    kt_refs, v_refs, o_ref, scratch = refs[:n_kv], refs[n_kv:2 * n_kv], refs[2 * n_kv], refs[2 * n_kv + 1:]
    x = lqk_ref[...]
    lam = (jnp.exp(jnp.sum(x[0:1] * x[1:2], axis=-1, keepdims=True))
           - jnp.exp(jnp.sum(x[2:3] * x[3:4], axis=-1, keepdims=True)) + lam_init)
    tq = min(ATTN_TQ, q_ref.shape[1])
    group = lax.broadcasted_iota(jnp.int32, (tq, ATTN_WIDTH), 1) >> 6
    halves = 2 * DIFF_HEADS // ATTN_STACK
    passes = []
    for t in range(q_ref.shape[1] // tq):
        q = q_ref[0, t * tq:(t + 1) * tq, :]
        passes += [(q, 5, half * ATTN_STACK, 0) for half in range(halves)]
    next_first = (qn_ref[0, 0:tq, :], 5, 0, 0)
    acc = {}

    def consume(i, pvs):
        t, half = divmod(i, halves)
        o1, o2 = acc.pop(t, (jnp.zeros(group.shape, F32),) * 2)
        for r, pv in enumerate(pvs):
            head = (half * ATTN_STACK + r) >> 1
            if r & 1 == 0:
                o1 = jnp.where(group == head, pv, o1)
            else:
                o2 = jnp.where(group == head, pv, o2)
        if half + 1 < halves:
            acc[t] = (o1, o2)
            return
        o_ref[0, t * tq:(t + 1) * tq, :] = (o1 - lam * o2).astype(BF16)

    _attention_passes(passes, next_first, consume, kt_refs, v_refs, scratch)


def _gqa_kernel(q_ref, qn_ref, *refs, n_kv):
    kt_refs, v_refs, o_ref, scratch = refs[:n_kv], refs[n_kv:2 * n_kv], refs[2 * n_kv], refs[2 * n_kv + 1:]
    tq = min(ATTN_TQ, q_ref.shape[1])
    group = lax.broadcasted_iota(jnp.int32, (tq, ATTN_WIDTH), 1) >> 6
    groups = q_ref.shape[2] // ATTN_WIDTH
    passes = []
    for t in range(q_ref.shape[1] // tq):
        for g in range(groups):
            passes.append((q_ref[0, t * tq:(t + 1) * tq, g * ATTN_WIDTH:(g + 1) * ATTN_WIDTH], 6, 0, g))

    def consume(i, pvs):
        t, g = divmod(i, groups)
        o = jnp.zeros(group.shape, F32)
        for r, pv in enumerate(pvs):
            o = jnp.where(group == r, pv, o)
        o_ref[0, t * tq:(t + 1) * tq, g * ATTN_WIDTH:(g + 1) * ATTN_WIDTH] = o.astype(BF16)

    next_first = (qn_ref[0, 0:tq, 0:ATTN_WIDTH], 6, 0, 0)
    _attention_passes(passes, next_first, consume, kt_refs, v_refs, scratch)


def _attention(kernel_fn, name, q, kts, vs, extra=()):
    b, lq, width = q.shape
    groups, w = kts[0].shape[1:3]
    lk = sum(kt.shape[3] for kt in kts)
    rows = min(lq, ATTN_STEP_ROWS)
    pass_rows = ATTN_STACK * min(ATTN_TQ, rows)
    steps = lq // rows
    kv_spec = lambda a: pl.BlockSpec((1,) + a.shape[1:], lambda bi, i: (bi, 0, 0, 0), pipeline_mode=pl.Buffered(1))
    return pl.pallas_call(
        functools.partial(kernel_fn, n_kv=len(kts)), name=name,
        grid=(b, steps),
        in_specs=[pl.BlockSpec(a.shape, lambda bi, i: (0, 0)) for a in extra]
        + [pl.BlockSpec((1, rows, width), lambda bi, i: (bi, i, 0)),
           pl.BlockSpec((1, rows, width), lambda bi, i: (bi, jnp.minimum(i + 1, steps - 1), 0))]
        + [kv_spec(a) for a in kts] + [kv_spec(a) for a in vs],
        out_specs=pl.BlockSpec((1, rows, width), lambda bi, i: (bi, i, 0)),
        out_shape=jax.ShapeDtypeStruct((b, lq, width), BF16),
        scratch_shapes=[pltpu.VMEM((pass_rows, w), BF16)] * 2 + [pltpu.VMEM((pass_rows, lk), F32)] * 2
        + [pltpu.VMEM((pass_rows, lk), BF16)] * 2,
        compiler_params=_params("arbitrary", "arbitrary"),
    )(*extra, q, q, *kts, *vs)


def _diff_attention(lqk, q, kts, vs, lam_init):
    return _attention(functools.partial(_diff_kernel, lam_init=lam_init), "diff_attn", q, kts, vs, (lqk,))


def _gqa_attention(q, kts, vs):
    return _attention(_gqa_kernel, "gqa_attn", q, kts, vs)


def _top2_route(logits):
    lane = lax.broadcasted_iota(jnp.int32, logits.shape, 1).astype(F32)
    neg = jnp.float32(-jnp.inf)
    lg = jnp.where(lane < N_EXPERTS, logits, neg)
    m1 = jnp.max(lg, axis=-1, keepdims=True)
    i1 = jnp.min(jnp.where(lg == m1, lane, float(LANES)), axis=-1, keepdims=True)
    lg2 = jnp.where(lane == i1, neg, lg)
    m2 = jnp.max(lg2, axis=-1, keepdims=True)
    i2 = jnp.min(jnp.where(lg2 == m2, lane, float(LANES)), axis=-1, keepdims=True)
    e2 = jnp.exp(m2 - m1)
    w1 = 1.0 / (1.0 + e2)
    return (jnp.where(lane == 0, i1, 0.0) + jnp.where(lane == 1, i2, 0.0)
            + jnp.where(lane == 2, w1, 0.0) + jnp.where(lane == 3, e2 * w1, 0.0))


def _outproj_kernel(*refs, moe):
    yhy_ref, od_ref, og_ref, h_ref, gt_ref, w_ref, gf_ref, sc_ref, sh_ref, sg_ref, bd_ref = refs[:11]
    if moe:
        rt_ref, hn_ref, u_ref, gates_ref = refs[11:]
    else:
        hn_ref, u_ref = refs[11:]
    od = od_ref[0].astype(F32)
    od = od * lax.rsqrt(_group_mean_sq(od, bd_ref[...], DIFF_V_DIM) + NORM_EPS) * sg_ref[...]
    y = (_dot(yhy_ref[0], w_ref[0:HY_WIDTH, :])
         + _dot(od.astype(BF16), w_ref[HY_WIDTH:HY_WIDTH + DIFF_WIDTH, :])
         + _dot(og_ref[0], w_ref[HY_WIDTH + DIFF_WIDTH:, :]))
    hn = h_ref[0] + gt_ref[0] * y
    hn_ref[0] = hn
    u = _rms(hn, gf_ref[...]) * (1.0 + sc_ref[0]) + sh_ref[0]
    u_ref[0] = u.astype(u_ref.dtype)
    if moe:
        u_hi = u.astype(BF16)
        u_lo = (u - u_hi.astype(F32)).astype(BF16)
        logits = _dot(u_hi, rt_ref[0]) + (_dot(u_lo, rt_ref[0]) + _dot(u_hi, rt_ref[1]))
        gates_ref[0] = _top2_route(logits)


def _outproj(yhy, od, og, h, gt, w_bf, gf, sc, sh, sub_g, bd256, router_pad):
    b, l, d = h.shape
    tm = min(l, 512)
    row = lambda bi, i: (bi, i, 0)
    vec = lambda bi, i: (bi, 0, 0)
    const = lambda bi, i: (0, 0)
    moe = router_pad is not None
    in_specs = [pl.BlockSpec((1, tm, HY_WIDTH), row), pl.BlockSpec((1, tm, DIFF_WIDTH), row),
                pl.BlockSpec((1, tm, GQA_WIDTH), row), pl.BlockSpec((1, tm, d), row),
                pl.BlockSpec((1, 1, d), vec), pl.BlockSpec((d, d), const), pl.BlockSpec((1, d), const),
                pl.BlockSpec((1, 1, d), vec), pl.BlockSpec((1, 1, d), vec),
                pl.BlockSpec((1, DIFF_WIDTH), const), pl.BlockSpec((DIFF_WIDTH, DIFF_WIDTH), const)]
    args = [yhy, od, og, h, gt, w_bf, gf, sc, sh, sub_g, bd256]
    out_specs = [pl.BlockSpec((1, tm, d), row), pl.BlockSpec((1, tm, d), row)]
    out_shape = [jax.ShapeDtypeStruct((b, l, d), F32), jax.ShapeDtypeStruct((b, l, d), F32 if moe else BF16)]
    if moe:
        in_specs.append(pl.BlockSpec((2, d, LANES), lambda bi, i: (0, 0, 0)))
        args.append(router_pad)
        out_specs.append(pl.BlockSpec((1, tm, LANES), row))
        out_shape.append(jax.ShapeDtypeStruct((b, l, LANES), F32))
    return pl.pallas_call(
        functools.partial(_outproj_kernel, moe=moe), name="outproj",
        grid=(b, l // tm), in_specs=in_specs, out_specs=out_specs, out_shape=out_shape,
        compiler_params=_params("arbitrary", "arbitrary"),
    )(*args)


def _ffn_kernel(u_ref, wg_ref, wu_ref, wd_ref, h_ref, gt_ref, o_ref, acc_ref):
    f = pl.program_id(2)

    @pl.when(f == 0)
    def _():
        acc_ref[...] = jnp.zeros_like(acc_ref)

    u = u_ref[0]
    hid = _silu(_dot(u, wg_ref[...])) * _dot(u, wu_ref[...])
    acc_ref[...] += _dot(hid.astype(BF16), wd_ref[...])

    @pl.when(f == pl.num_programs(2) - 1)
    def _():
        o_ref[0] = h_ref[0] + gt_ref[0] * acc_ref[...]


def _ffn(u, wg, wu, wd, h, gt):
    b, l, d = h.shape
    dff = wg.shape[1]
    tm = min(l, 512)
    tf = dff // 2
    row = lambda bi, i, f: (bi, i, 0)
    return pl.pallas_call(
        _ffn_kernel, name="ffn",
        grid=(b, l // tm, dff // tf),
        in_specs=[pl.BlockSpec((1, tm, d), row),
                  pl.BlockSpec((d, tf), lambda bi, i, f: (0, f)), pl.BlockSpec((d, tf), lambda bi, i, f: (0, f)),
                  pl.BlockSpec((tf, d), lambda bi, i, f: (f, 0)),
                  pl.BlockSpec((1, tm, d), row), pl.BlockSpec((1, 1, d), lambda bi, i, f: (bi, 0, 0))],
        out_specs=pl.BlockSpec((1, tm, d), row),
        out_shape=jax.ShapeDtypeStruct((b, l, d), F32),
        scratch_shapes=[pltpu.VMEM((tm, d), F32)],
        compiler_params=_params("arbitrary", "arbitrary", "arbitrary"),
    )(u, wg, wu, wd, h, gt)


MOE_TILE = 512
SC_CHUNK = 64


def _sc_gather(table, idx):
    info = plsc.get_sparse_core_info()
    nc, ns = info.num_cores, info.num_subcores
    n, d = idx.shape[0], table.shape[1]
    per_worker = n // (nc * ns)
    assert per_worker * nc * ns == n and per_worker % SC_CHUNK == 0
    mesh = plsc.VectorSubcoreMesh(core_axis_name="c", subcore_axis_name="s")

    @functools.partial(
        pl.kernel, mesh=mesh,
        out_type=jax.ShapeDtypeStruct((n, d), table.dtype),
        scratch_types=[pltpu.VMEM((SC_CHUNK,), jnp.int32), pltpu.VMEM((SC_CHUNK, d), table.dtype),
                       pltpu.SemaphoreType.DMA],
    )
    def gather_kernel(table_hbm, idx_hbm, out_hbm, idx_v, rows_v, sem):
        base = (lax.axis_index("s") * nc + lax.axis_index("c")) * per_worker

        @pl.loop(0, per_worker // SC_CHUNK)
        def _(j):
            off = pl.multiple_of(base + j * SC_CHUNK, 8)
            pltpu.sync_copy(idx_hbm.at[pl.ds(off, SC_CHUNK)], idx_v)
            pltpu.async_copy(table_hbm.at[idx_v], rows_v, sem).wait()
            pltpu.sync_copy(rows_v, out_hbm.at[pl.ds(off, SC_CHUNK)])

    return gather_kernel(table, idx)


def _route_plan(route, n_rows):
    t = route.shape[0]
    e = route[:, :2].astype(jnp.int32).T.reshape(2 * t)
    onehot = (e[:, None] == jnp.arange(N_EXPERTS, dtype=jnp.int32)[None, :]).astype(jnp.int32)
    csum = jnp.cumsum(onehot, axis=0)
    counts = csum[-1]
    rank = jnp.take_along_axis(csum, e[:, None], axis=1)[:, 0] - 1
    padded = (counts + MOE_TILE - 1) // MOE_TILE * MOE_TILE
    ends = jnp.cumsum(padded)
    pos = (ends - padded)[e] + rank
    row_token = jnp.zeros((n_rows,), jnp.int32).at[pos].set(jnp.arange(2 * t, dtype=jnp.int32) % t)
    tile_start = jnp.arange(n_rows // MOE_TILE, dtype=jnp.int32) * MOE_TILE
    tile_expert = jnp.minimum(jnp.sum((tile_start[:, None] >= ends[None, :]).astype(jnp.int32), axis=1),
                              N_EXPERTS - 1)
    n_used = (ends[-1] // MOE_TILE).reshape(1)
    return pos, row_token, tile_expert, n_used


def _gmm_kernel(te_ref, nu_ref, x_ref, wg_ref, wu_ref, wd_ref, o_ref, acc_ref):
    t, f = pl.program_id(0), pl.program_id(1)

    @pl.when(t < nu_ref[0])
    def _():
        @pl.when(f == 0)
        def _():
            acc_ref[...] = jnp.zeros_like(acc_ref)

        x = x_ref[...].astype(BF16)
        hid = _silu(_dot(x, wg_ref[0])) * _dot(x, wu_ref[0])
        acc_ref[...] += _dot(hid.astype(BF16), wd_ref[0])

        @pl.when(f == pl.num_programs(1) - 1)
        def _():
            o_ref[...] = acc_ref[...]


def _grouped_swiglu(x_sorted, tile_expert, n_used, wg, wu, wd):
    r, d = x_sorted.shape
    dff = wg.shape[2]
    n_f = 2
    tf = dff // n_f
    fidx = lambda t, f, te, nu: jnp.where(t < nu[0], f, n_f - 1)
    return pl.pallas_call(
        _gmm_kernel, name="grouped_swiglu",
        grid_spec=pltpu.PrefetchScalarGridSpec(
            num_scalar_prefetch=2,
            grid=(r // MOE_TILE, n_f),
            in_specs=[pl.BlockSpec((MOE_TILE, d), lambda t, f, te, nu: (t, 0)),
                      pl.BlockSpec((1, d, tf), lambda t, f, te, nu: (te[t], 0, fidx(t, f, te, nu))),
                      pl.BlockSpec((1, d, tf), lambda t, f, te, nu: (te[t], 0, fidx(t, f, te, nu))),
                      pl.BlockSpec((1, tf, d), lambda t, f, te, nu: (te[t], fidx(t, f, te, nu), 0))],
            out_specs=pl.BlockSpec((MOE_TILE, d), lambda t, f, te, nu: (t, 0)),
            scratch_shapes=[pltpu.VMEM((MOE_TILE, d), F32)]),
        out_shape=jax.ShapeDtypeStruct((r, d), F32),
        compiler_params=_params("arbitrary", "arbitrary"),
    )(tile_expert, n_used, x_sorted, wg, wu, wd)


def _moe_combine_kernel(y1_ref, y2_ref, rt_ref, h_ref, gt_ref, o_ref):
    rt = rt_ref[...]
    out = rt[:, 2:3] * y1_ref[...] + rt[:, 3:4] * y2_ref[...]
    o_ref[...] = h_ref[...] + gt_ref[0] * out


def _moe_combine(y_pair, route, h, gt_tab, tile_vec):
    t, d = h.shape
    tm = 512
    return pl.pallas_call(
        _moe_combine_kernel, name="moe_combine",
        grid=(t // tm,),
        in_specs=[pl.BlockSpec((tm, d), lambda i: (i, 0)), pl.BlockSpec((tm, d), lambda i: (i + t // tm, 0)),
                  pl.BlockSpec((tm, LANES), lambda i: (i, 0)), pl.BlockSpec((tm, d), lambda i: (i, 0)),
                  pl.BlockSpec((1, 1, d), lambda i: (tile_vec(i, tm), 0, 0))],
        out_specs=pl.BlockSpec((tm, d), lambda i: (i, 0)),
        out_shape=jax.ShapeDtypeStruct((t, d), F32),
        compiler_params=_params("arbitrary"),
    )(y_pair, y_pair, route, h, gt_tab)


def _moe(u, route, wg, wu, wd, h, gt_tab, tile_vec):
    t, d = u.shape
    n_rows = 2 * t + N_EXPERTS * MOE_TILE
    pos, row_token, tile_expert, n_used = _route_plan(route, n_rows)
    x_sorted = _sc_gather(u, row_token)
    y_sorted = _grouped_swiglu(x_sorted, tile_expert, n_used, wg, wu, wd)
    return _moe_combine(_sc_gather(y_sorted, pos), route, h, gt_tab, tile_vec)


def _final_norm_kernel(h_ref, g_ref, o_ref):
    o_ref[0] = _rms(h_ref[0], g_ref[...])


def _final_norm(h, g):
    b, l, d = h.shape
    tm = min(l, 1024)
    return pl.pallas_call(
        _final_norm_kernel, name="final_norm",
        grid=(b, l // tm),
        in_specs=[pl.BlockSpec((1, tm, d), lambda bi, i: (bi, i, 0)), pl.BlockSpec((1, d), lambda bi, i: (0, 0))],
        out_specs=pl.BlockSpec((1, tm, d), lambda bi, i: (bi, i, 0)),
        out_shape=jax.ShapeDtypeStruct((b, l, d), F32),
        compiler_params=_params("arbitrary", "arbitrary"),
    )(h, g.reshape(1, d))


def _rope_tables(length, head_dim):
    t = jnp.arange(length)
    row = (t // GRID_W).astype(F32)
    col = (t % GRID_W).astype(F32)
    n = head_dim // 4
    inv = ROPE_THETA ** (-jnp.arange(n, dtype=F32) / n)
    ang = jnp.concatenate([row[:, None] * inv, col[:, None] * inv], axis=-1)
    cos = jnp.concatenate([jnp.cos(ang)] * 2, axis=-1)
    sin = jnp.concatenate([-jnp.sin(ang), jnp.sin(ang)], axis=-1)
    reps = LANES // head_dim
    return jnp.tile(cos, (1, reps)), jnp.tile(sin, (1, reps))


def _hyena_tables(length):
    k = jnp.arange(length, dtype=jnp.int32)
    period = 4 * length
    step = 2.0 * math.pi / period
    lo = jnp.arange(64, dtype=jnp.int32)
    hi = jnp.arange(length // 64, dtype=jnp.int32)

    def tabs(x, y_hi, y_lo):
        def cs(y):
            ang = ((x[:, None] * y[None, :]) % period).astype(F32) * step
            return jnp.cos(ang)[:, :, None], jnp.sin(ang)[:, :, None]
        (ca, sa), (cb, sb) = cs(y_hi), cs(y_lo)
        cb, sb = jnp.swapaxes(cb, 1, 2), jnp.swapaxes(sb, 1, 2)
        c = (ca * cb - sa * sb).reshape(length, length)
        s = (sa * cb + ca * sb).reshape(length, length)
        return c.astype(BF16), s.astype(BF16)

    c_tab, s_tab = tabs(2 * k + 1, 64 * hi, lo)
    ci_tab, si_tab = tabs(k, 128 * hi, 2 * lo + 1)
    t = jnp.linspace(0.0, 1.0, length, dtype=F32)[:, None]
    bands = (HY_EMB - 1) // 2
    ang = (2.0 * math.pi / length) * jnp.arange(length, dtype=F32)[:, None] \
        * jnp.linspace(1e-4, bands - 1, bands, dtype=F32)
    feats = jnp.concatenate([t, jnp.cos(ang), -jnp.sin(ang),
                             jnp.zeros((length, LANES - HY_EMB), F32)], axis=-1)
    max_decay = math.log(HY_DECAY_TARGET) / HY_FAST_DECAY
    min_decay = math.log(HY_DECAY_TARGET) / HY_SLOW_DECAY
    deltas = jnp.abs(jnp.linspace(min_decay, max_decay, HY_WIDTH, dtype=F32)).reshape(1, HY_WIDTH)
    return c_tab, s_tab, ci_tab, si_tab, feats, deltas


def _block_diag_ones(n, group):
    i = jnp.arange(n) // group
    return (i[:, None] == i[None, :]).astype(BF16)


def kernel(x, c, ctx, c_ctx, w_ada, b_ada, g_mix, g_ffn, w_in, w_out, hy_conv_w, hy_conv_b, hy_w1, hy_b1, hy_w2, hy_b2, hy_w3, hy_freq, hy_skip, diff_lq1, diff_lk1, diff_lq2, diff_lk2, diff_subln, gqa_qnorm, gqa_knorm, ffn_wg, ffn_wu, ffn_wd, moe_router, moe_wg, moe_wu, moe_wd, g_final):
    b, seq, d = x.shape
    n_ctx = ctx.shape[1]
    depth = w_ada.shape[0]

    rope_tabs = _rope_tables(seq, DIFF_QK_DIM) + _rope_tables(seq, GQA_HEAD_DIM)
    hy_tabs = _hyena_tables(seq)
    hy_tabs_c = _hyena_tables(n_ctx)
    bd128 = _block_diag_ones(LANES, GQA_HEAD_DIM)
    bd256 = _block_diag_ones(DIFF_WIDTH, DIFF_V_DIM)

    rows = 16
    cc = jnp.zeros((rows, d), F32).at[:b].set(c).at[b].set(c_ctx)
    mods = _mods(cc, w_ada, b_ada)

    def mod_vecs(i):
        lat = [mods[i, :b, j * d:(j + 1) * d].reshape(b, 1, d) for j in range(6)]
        cx = [jnp.broadcast_to(mods[i, b, j * d:(j + 1) * d].reshape(1, 1, d), (b, 1, d)) for j in range(6)]
        return lat, cx

    def kv_layouts(kd, vd, kg, vg):
        l = kd.shape[1]
        kdt = jnp.swapaxes(kd, 1, 2)[:, None]
        vd = vd[:, None]
        kgt = jnp.swapaxes(kg, 1, 2).reshape(b, 2, 1, GQA_HEAD_DIM, l)
        kgt = jnp.broadcast_to(kgt, (b, 2, GQA_REP, GQA_HEAD_DIM, l)).reshape(b, 2, GQA_REP * GQA_HEAD_DIM, l)
        vgx = jnp.swapaxes(vg.reshape(b, l, 2, 1, GQA_HEAD_DIM), 1, 2)
        vgx = jnp.broadcast_to(vgx, (b, 2, l, GQA_REP, GQA_HEAD_DIM)).reshape(b, 2, l, GQA_REP * GQA_HEAD_DIM)
        return kdt, vd, kgt, vgx

    h, hc = x, ctx
    for i in range(depth):
        last = i == depth - 1
        lam_init = 0.8 - 0.6 * math.exp(-0.3 * i)
        (sh_m, sc_m, gt_m, sh_f, sc_f, gt_f), (csh_m, csc_m, cgt_m, csh_f, csc_f, cgt_f) = mod_vecs(i)
        w_in_bf = w_in[i].astype(BF16)
        w_out_bf = w_out[i].astype(BF16)
        g_m = g_mix[i].reshape(1, d)
        g_f = g_ffn[i].reshape(1, d)
        qn = jnp.tile(gqa_qnorm[i], LANES // GQA_HEAD_DIM).reshape(1, LANES)
        kn = jnp.tile(gqa_knorm[i], LANES // GQA_HEAD_DIM).reshape(1, LANES)
        subln = jnp.tile(diff_subln[i], DIFF_HEADS).reshape(1, DIFF_WIDTH) * (1.0 - lam_init)
        lqk = jnp.stack([diff_lq1[i], diff_lk1[i], diff_lq2[i], diff_lk2[i]]).astype(F32)
        hy_p = (hy_conv_w[i], hy_conv_b[i], hy_w1[i], hy_b1[i], hy_w2[i], hy_b2[i], hy_w3[i],
                hy_freq[i], hy_skip[i])
        moe = i % 2 == 1
        router_pad = None
        if moe:
            router = jnp.zeros((d, LANES), F32).at[:, :N_EXPERTS].set(moe_router[i // 2])
            router_hi = router.astype(BF16)
            router_pad = jnp.stack([router_hi, (router - router_hi.astype(F32)).astype(BF16)])

        zhy, qd, kd, vd, qg, kg, vg = _inproj(h, sc_m, sh_m, g_m, w_in_bf, qn, kn, bd128, rope_tabs)
        czhy, cqd, ckd, cvd, cqg, ckg, cvg = _inproj(hc, csc_m, csh_m, g_m, w_in_bf, qn, kn, bd128, None)
        kdt, vdx, kgt, vgx = kv_layouts(kd, vd, kg, vg)
        ckdt, cvdx, ckgt, cvgx = kv_layouts(ckd, cvd, ckg, cvg)

        y_hy = _hyena(zhy, hy_p, hy_tabs)
        o_d = _diff_attention(lqk, qd, [kdt, ckdt], [vdx, cvdx], lam_init)
        o_g = _gqa_attention(qg, [kgt, ckgt], [vgx, cvgx])
        mixed = _outproj(y_hy, o_d, o_g, h, gt_m, w_out_bf, g_f, sc_f, sh_f, subln, bd256, router_pad)
        if not last:
            yc_hy = _hyena(czhy, hy_p, hy_tabs_c)
            oc_d = _diff_attention(lqk, cqd, [ckdt], [cvdx], lam_init)
            oc_g = _gqa_attention(cqg, [ckgt], [cvgx])
            cmixed = _outproj(yc_hy, oc_d, oc_g, hc, cgt_m, w_out_bf, g_f, csc_f, csh_f, subln, bd256,
                              router_pad)

        if moe:
            wg, wu, wd = (w[i // 2].astype(BF16) for w in (moe_wg, moe_wu, moe_wd))
            flat = lambda a: a.reshape(-1, a.shape[-1])
            gt_tab = jnp.concatenate([gt_f, cgt_f[:1]], axis=0)
            if last:
                tile_vec = lambda t, tm: t // (seq // tm)
                h = _moe(flat(mixed[1]), flat(mixed[2]), wg, wu, wd, flat(mixed[0]), gt_tab,
                         tile_vec).reshape(b, seq, d)
            else:
                n_c = b * n_ctx
                tile_vec = lambda t, tm: jnp.where(t < n_c // tm, b, (t - n_c // tm) // (seq // tm))
                both = lambda c_, a: jnp.concatenate([flat(c_), flat(a)], axis=0)
                h_all = _moe(both(cmixed[1], mixed[1]), both(cmixed[2], mixed[2]), wg, wu, wd,
                             both(cmixed[0], mixed[0]), gt_tab, tile_vec)
                hc = h_all[:n_c].reshape(b, n_ctx, d)
                h = h_all[n_c:].reshape(b, seq, d)
        else:
            wg, wu, wd = (w[i // 2].astype(BF16) for w in (ffn_wg, ffn_wu, ffn_wd))
            h = _ffn(mixed[1], wg, wu, wd, mixed[0], gt_f)
            if not last:
                hc = _ffn(cmixed[1], wg, wu, wd, cmixed[0], cgt_f)
    return _final_norm(h, g_final)
```

```python
import functools
import math

import jax
import jax.numpy as jnp
from jax import lax
from jax.experimental import pallas as pl
from jax.experimental.pallas import tpu as pltpu
from jax.experimental.pallas import tpu_sc as plsc

F32 = jnp.float32
BF16 = jnp.bfloat16
HIGHEST = lax.Precision.HIGHEST

D_MODEL = 1024
GRID_W = 64
ROPE_THETA = 10000.0
NORM_EPS = 1e-6

HY_WIDTH = 256
HY_EMB = 33
HY_FAST_DECAY = 0.3
HY_SLOW_DECAY = 1.5
HY_DECAY_TARGET = 1e-2

DIFF_HEADS = 4
DIFF_QK_DIM = 32
DIFF_V_DIM = 64
DIFF_WIDTH = 256
GQA_HEAD_DIM = 64
GQA_WIDTH = 512
GQA_KV_WIDTH = 128
GQA_REP = 4

OFF_DQ = 3 * HY_WIDTH
OFF_DK = OFF_DQ + DIFF_WIDTH
OFF_DV = OFF_DK + DIFF_WIDTH
OFF_GQ = OFF_DV + DIFF_WIDTH
OFF_GK = OFF_GQ + GQA_WIDTH
OFF_GV = OFF_GK + GQA_KV_WIDTH
IN_COLS = OFF_GV + GQA_KV_WIDTH

N_EXPERTS = 8
LOG2_E = math.log2(math.e)
LANES = 128
VMEM_LIMIT = 56 * 1024 * 1024


def _params(*sem):
    return pltpu.CompilerParams(dimension_semantics=sem, vmem_limit_bytes=VMEM_LIMIT)


def _dot(a, b):
    return jnp.dot(a, b, preferred_element_type=F32)


def _dot_hi(a, b):
    return jnp.dot(a, b, preferred_element_type=F32, precision=HIGHEST)


def _rms(x, g):
    ms = jnp.mean(x * x, axis=-1, keepdims=True)
    return x * lax.rsqrt(ms + NORM_EPS) * g


def _silu(x):
    return x * jax.nn.sigmoid(x)


def _group_mean_sq(x, ones_bd, width):
    return _dot((x * x).astype(BF16), ones_bd) * (1.0 / width)


def _cast_kernel(x_ref, o_ref):
    o_ref[...] = x_ref[...].astype(o_ref.dtype)


def _to_bf16(w):
    r, c = w.shape[-2:]
    w3 = w.reshape(-1, r, c)
    tr = next(t for t in (512, 256, 128) if r % t == 0)
    return pl.pallas_call(
        _cast_kernel, name="to_bf16",
        grid=(w3.shape[0], r // tr),
        in_specs=[pl.BlockSpec((1, tr, c), lambda e, i: (e, i, 0))],
        out_specs=pl.BlockSpec((1, tr, c), lambda e, i: (e, i, 0)),
        out_shape=jax.ShapeDtypeStruct(w3.shape, BF16),
        compiler_params=_params("arbitrary", "arbitrary"),
    )(w3).reshape(w.shape)


def _mods_kernel(c_ref, w_ref, b_ref, o_ref):
    o_ref[0] = _dot_hi(_silu(c_ref[...]), w_ref[0]) + b_ref[0]


def _mods(cc, w_ada, b_ada):
    depth, d, n = w_ada.shape
    tn = 1536
    return pl.pallas_call(
        _mods_kernel, name="mods",
        grid=(depth, n // tn),
        in_specs=[pl.BlockSpec(cc.shape, lambda i, j: (0, 0)),
                  pl.BlockSpec((1, d, tn), lambda i, j: (i, 0, j)),
                  pl.BlockSpec((1, 1, tn), lambda i, j: (i, 0, j))],
        out_specs=pl.BlockSpec((1, cc.shape[0], tn), lambda i, j: (i, 0, j)),
        out_shape=jax.ShapeDtypeStruct((depth, cc.shape[0], n), F32),
        compiler_params=_params("arbitrary", "arbitrary"),
    )(cc, w_ada, b_ada.reshape(depth, 1, n))


def _rope128(x, cos, sin_signed, half):
    lane = lax.broadcasted_iota(jnp.int32, x.shape, 1)
    first = (lane & (2 * half - 1)) < half
    swapped = jnp.where(first, pltpu.roll(x, LANES - half, 1), pltpu.roll(x, half, 1))
    return x * cos + swapped * sin_signed


def _inproj_kernel(*refs, rope):
    h_ref, sc_ref, sh_ref, g_ref, w_ref, qn_ref, kn_ref, bd_ref = refs[:8]
    if rope:
        cd_ref, sd_ref, cg_ref, sg_ref = refs[8:12]
    zhy_ref, qd_ref, kd_ref, vd_ref, qg_ref, kg_ref, vg_ref = refs[-7:]
    u = _rms(h_ref[0], g_ref[...]) * (1.0 + sc_ref[0]) + sh_ref[0]
    z = _dot(u.astype(BF16), w_ref[...])
    zhy_ref[0] = z[:, :OFF_DQ]

    def piece(off, j):
        return z[:, off + LANES * j: off + LANES * (j + 1)]

    for j in range(DIFF_WIDTH // LANES):
        q, k = piece(OFF_DQ, j), piece(OFF_DK, j)
        if rope:
            q = _rope128(q, cd_ref[...], sd_ref[...], DIFF_QK_DIM // 2)
            k = _rope128(k, cd_ref[...], sd_ref[...], DIFF_QK_DIM // 2)
        qd_ref[0, :, LANES * j: LANES * (j + 1)] = (q * (LOG2_E * DIFF_QK_DIM ** -0.5)).astype(BF16)
        kd_ref[0, 0, LANES * j: LANES * (j + 1), :] = k.T.astype(BF16)
    vd_ref[0, 0] = z[:, OFF_DV:OFF_GQ].astype(BF16)

    def gqa_piece(x, gain):
        ms = _group_mean_sq(x, bd_ref[...], GQA_HEAD_DIM)
        x = x * lax.rsqrt(ms + NORM_EPS) * gain
        if rope:
            x = _rope128(x, cg_ref[...], sg_ref[...], GQA_HEAD_DIM // 2)
        return x

    for j in range(GQA_WIDTH // LANES):
        q = gqa_piece(piece(OFF_GQ, j), qn_ref[...])
        qg_ref[0, :, LANES * j: LANES * (j + 1)] = (q * (LOG2_E * GQA_HEAD_DIM ** -0.5)).astype(BF16)
    kt = gqa_piece(piece(OFF_GK, 0), kn_ref[...]).T.astype(BF16)
    v = z[:, OFF_GV:]
    v_swapped = pltpu.roll(v, GQA_HEAD_DIM, 1)
    low = lax.broadcasted_iota(jnp.int32, v.shape, 1) < GQA_HEAD_DIM
    v_rep = (jnp.where(low, v, v_swapped), jnp.where(low, v_swapped, v))
    for g in range(GQA_KV_WIDTH // GQA_HEAD_DIM):
        for r in range(GQA_REP):
            kg_ref[0, g, GQA_HEAD_DIM * r: GQA_HEAD_DIM * (r + 1), :] = kt[GQA_HEAD_DIM * g: GQA_HEAD_DIM * (g + 1), :]
        for half in range(GQA_REP * GQA_HEAD_DIM // LANES):
            vg_ref[0, g, :, LANES * half: LANES * (half + 1)] = v_rep[g].astype(BF16)


def _inproj(h, sc, sh, g, w_bf, qn, kn, bd, rope_tabs):
    b, l, d = h.shape
    tm = min(l, 512)
    row = lambda bi, i: (bi, i, 0)
    vec = lambda bi, i: (bi, 0, 0)
    const = lambda bi, i: (0, 0)
    in_specs = [pl.BlockSpec((1, tm, d), row), pl.BlockSpec((1, 1, d), vec), pl.BlockSpec((1, 1, d), vec),
                pl.BlockSpec((1, d), const), pl.BlockSpec((d, IN_COLS), const),
                pl.BlockSpec((1, LANES), const), pl.BlockSpec((1, LANES), const),
                pl.BlockSpec((LANES, LANES), const)]
    args = [h, sc, sh, g, w_bf, qn, kn, bd]
    if rope_tabs is not None:
        in_specs += [pl.BlockSpec((tm, LANES), lambda bi, i: (i, 0))] * 4
        args += list(rope_tabs)
    aw = ATTN_WIDTH
    kv_groups = GQA_KV_WIDTH // GQA_HEAD_DIM
    rows_spec = lambda w: pl.BlockSpec((1, tm, w), row)
    kt_spec = lambda g: pl.BlockSpec((1, g, aw, tm), lambda bi, i: (bi, 0, 0, i))
    v_spec = lambda g: pl.BlockSpec((1, g, tm, aw), lambda bi, i: (bi, 0, i, 0))
    sds = jax.ShapeDtypeStruct
    return pl.pallas_call(
        functools.partial(_inproj_kernel, rope=rope_tabs is not None), name="inproj",
        grid=(b, l // tm),
        in_specs=in_specs,
        out_specs=[rows_spec(OFF_DQ), rows_spec(DIFF_WIDTH), kt_spec(1), v_spec(1),
                   rows_spec(GQA_WIDTH), kt_spec(kv_groups), v_spec(kv_groups)],
        out_shape=[sds((b, l, OFF_DQ), F32), sds((b, l, DIFF_WIDTH), BF16), sds((b, 1, aw, l), BF16),
                   sds((b, 1, l, aw), BF16), sds((b, l, GQA_WIDTH), BF16), sds((b, kv_groups, aw, l), BF16),
                   sds((b, kv_groups, l, aw), BF16)],
        compiler_params=_params("arbitrary", "arbitrary"),
    )(*args)


def _hy_filter_kernel(f_ref, w1_ref, b1_ref, w2_ref, b2_ref, w3_ref, fr_ref, dl_ref, hs_ref, hd_ref):
    f = f_ref[...]
    fr = fr_ref[...]
    a = jnp.sin(fr * (_dot_hi(f, w1_ref[...]) + b1_ref[...]))
    a = jnp.sin(fr * (_dot_hi(a, w2_ref[...]) + b2_ref[...]))
    hf = _dot_hi(a, w3_ref[...])
    decay = jnp.exp(-f[:, 0:1] * dl_ref[...])
    h_fwd = hf[:, :HY_WIDTH] * decay
    h_bwd = hf[:, HY_WIDTH:] * decay
    row = lax.broadcasted_iota(jnp.int32, h_bwd.shape, 0) + pl.program_id(0) * f.shape[0]
    h_bwd = jnp.where(row == 0, 0.0, h_bwd)
    hs_ref[...] = (h_fwd + h_bwd).astype(BF16)
    hd_ref[...] = (h_bwd - h_fwd).astype(BF16)


def _hy_spectrum_kernel(c_ref, s_ref, hs_ref, hd_ref, gre_ref, gim_ref, *, scale):
    gre_ref[...] = _dot(c_ref[...], hs_ref[...]) * scale
    gim_ref[...] = _dot(s_ref[...], hd_ref[...]) * scale


def _hy_pre_kernel(z_ref, zp_ref, zn_ref, cw_ref, cb_ref, wb_ref, wf_ref, x0_ref, *, nt):
    ti = pl.program_id(1)
    z = z_ref[0]
    tl = z.shape[0]
    row = lax.broadcasted_iota(jnp.int32, z.shape, 0)
    prev_row = jnp.where(ti > 0, zp_ref[0, 7:8, :], 0.0)
    next_row = jnp.where(ti < nt - 1, zn_ref[0, 0:1, :], 0.0)
    z_prev = jnp.where(row == 0, prev_row, pltpu.roll(z, 1, 0))
    z_next = jnp.where(row == tl - 1, next_row, pltpu.roll(z, tl - 1, 0))
    cw = cw_ref[...]
    y = cb_ref[...] + z_prev * cw[0:1] + z * cw[1:2] + z_next * cw[2:3]
    x0, x1, v = y[:, :HY_WIDTH], y[:, HY_WIDTH:2 * HY_WIDTH], y[:, 2 * HY_WIDTH:]
    w = v * x1
    wb_ref[...] = w.astype(BF16)
    wf_ref[0] = w
    x0_ref[0] = x0


def _hy_fwd_kernel(c_ref, s_ref, x_ref, gre_ref, gim_ref, yre_ref, yim_ref, *, reps):
    x = x_ref[...]
    a = _dot(c_ref[...], x)
    b = _dot(s_ref[...], x)
    gre = jnp.concatenate([gre_ref[...]] * reps, axis=1)
    gim = jnp.concatenate([gim_ref[...]] * reps, axis=1)
    yre_ref[...] = (a * gre + b * gim).astype(BF16)
    yim_ref[...] = (a * gim - b * gre).astype(BF16)


def _hy_inv_kernel(ci_ref, si_ref, yre_ref, yim_ref, wf_ref, x0_ref, skip_ref, o_ref, *, reps):
    y = _dot(ci_ref[...], yre_ref[...]) - _dot(si_ref[...], yim_ref[...])
    for r in range(reps):
        yr = y[:, r * HY_WIDTH:(r + 1) * HY_WIDTH]
        o_ref[r] = ((yr + wf_ref[r] * skip_ref[...]) * x0_ref[r]).astype(BF16)


def _hyena(zhy, p, tabs):
    conv_w, conv_b, w1, b1, w2, b2, w3, freq, skip = p
    c_tab, s_tab, ci_tab, si_tab, feats, deltas = tabs
    b, l, _ = zhy.shape
    c = HY_WIDTH
    tl = min(l, 512)
    nt = l // tl
    hid = w2.shape[0]
    const1 = lambda i: (0, 0)
    w1p = jnp.zeros((LANES, hid), F32).at[:HY_EMB].set(w1)
    hs, hd = pl.pallas_call(
        _hy_filter_kernel, name="hy_filter",
        grid=(nt,),
        in_specs=[pl.BlockSpec((tl, LANES), lambda i: (i, 0)),
                  pl.BlockSpec((LANES, hid), const1), pl.BlockSpec((1, hid), const1),
                  pl.BlockSpec((hid, hid), const1), pl.BlockSpec((1, hid), const1),
                  pl.BlockSpec((hid, 2 * c), const1), pl.BlockSpec((1, hid), const1),
                  pl.BlockSpec((1, c), const1)],
        out_specs=[pl.BlockSpec((tl, c), lambda i: (i, 0))] * 2,
        out_shape=[jax.ShapeDtypeStruct((l, c), BF16)] * 2,
        compiler_params=_params("arbitrary"),
    )(feats, w1p, b1.reshape(1, hid), w2, b2.reshape(1, hid), w3, freq.reshape(1, hid), deltas)

    gre, gim = pl.pallas_call(
        functools.partial(_hy_spectrum_kernel, scale=1.0 / l), name="hy_spectrum",
        grid=(nt,),
        in_specs=[pl.BlockSpec((tl, l), lambda i: (i, 0)), pl.BlockSpec((tl, l), lambda i: (i, 0)),
                  pl.BlockSpec((l, c), const1), pl.BlockSpec((l, c), const1)],
        out_specs=[pl.BlockSpec((tl, c), lambda i: (i, 0))] * 2,
        out_shape=[jax.ShapeDtypeStruct((l, c), F32)] * 2,
        compiler_params=_params("arbitrary"),
    )(c_tab, s_tab, hs, hd)

    halo = 8
    wb, wf, x0 = pl.pallas_call(
        functools.partial(_hy_pre_kernel, nt=nt), name="hy_pre",
        grid=(b, nt),
        in_specs=[pl.BlockSpec((1, tl, 3 * c), lambda bi, i: (bi, i, 0)),
                  pl.BlockSpec((1, halo, 3 * c), lambda bi, i: (bi, jnp.maximum(i * (tl // halo) - 1, 0), 0)),
                  pl.BlockSpec((1, halo, 3 * c),
                               lambda bi, i: (bi, jnp.minimum((i + 1) * (tl // halo), l // halo - 1), 0)),
                  pl.BlockSpec((3, 3 * c), lambda bi, i: (0, 0)),
                  pl.BlockSpec((1, 3 * c), lambda bi, i: (0, 0))],
        out_specs=[pl.BlockSpec((tl, c), lambda bi, i: (i, bi)),
                   pl.BlockSpec((1, tl, c), lambda bi, i: (bi, i, 0)),
                   pl.BlockSpec((1, tl, c), lambda bi, i: (bi, i, 0))],
        out_shape=[jax.ShapeDtypeStruct((l, b * c), BF16),
                   jax.ShapeDtypeStruct((b, l, c), F32),
                   jax.ShapeDtypeStruct((b, l, c), F32)],
        compiler_params=_params("arbitrary", "arbitrary"),
    )(zhy, zhy, zhy, conv_w, conv_b.reshape(1, 3 * c))

    reps = 2
    tn = reps * c
    nj = b * c // tn
    yre, yim = pl.pallas_call(
        functools.partial(_hy_fwd_kernel, reps=reps), name="hy_fwd",
        grid=(nj, nt),
        in_specs=[pl.BlockSpec((tl, l), lambda j, i: (i, 0)), pl.BlockSpec((tl, l), lambda j, i: (i, 0)),
                  pl.BlockSpec((l, tn), lambda j, i: (0, j)),
                  pl.BlockSpec((tl, c), lambda j, i: (i, 0)), pl.BlockSpec((tl, c), lambda j, i: (i, 0))],
        out_specs=[pl.BlockSpec((tl, tn), lambda j, i: (i, j))] * 2,
        out_shape=[jax.ShapeDtypeStruct((l, b * c), BF16)] * 2,
        compiler_params=_params("arbitrary", "arbitrary"),
    )(c_tab, s_tab, wb, gre, gim)

    return pl.pallas_call(
        functools.partial(_hy_inv_kernel, reps=reps), name="hy_inv",
        grid=(nj, nt),
        in_specs=[pl.BlockSpec((tl, l), lambda j, i: (i, 0)), pl.BlockSpec((tl, l), lambda j, i: (i, 0)),
                  pl.BlockSpec((l, tn), lambda j, i: (0, j)), pl.BlockSpec((l, tn), lambda j, i: (0, j)),
                  pl.BlockSpec((reps, tl, c), lambda j, i: (j, i, 0)),
                  pl.BlockSpec((reps, tl, c), lambda j, i: (j, i, 0)),
                  pl.BlockSpec((1, c), lambda j, i: (0, 0))],
        out_specs=pl.BlockSpec((reps, tl, c), lambda j, i: (j, i, 0)),
        out_shape=jax.ShapeDtypeStruct((b, l, c), BF16),
        compiler_params=_params("arbitrary", "arbitrary"),
    )(ci_tab, si_tab, yre, yim, wf, x0, skip.reshape(1, c))


def _lane_mask(width, shift, idx):
    lane = lax.broadcasted_iota(jnp.int32, (1, width), 1)
    return jnp.where((lane >> shift) == idx, 1.0, 0.0).astype(BF16)


ATTN_WIDTH = 256
ATTN_CHUNK = 512


def _key_chunks(lk):
    return [(c0, min(ATTN_CHUNK, lk - c0)) for c0 in range(0, lk, ATTN_CHUNK)]


ATTN_STACK = 4
ATTN_TQ = 128
ATTN_STEP_ROWS = 256
ATTN_ROW_BLOCK = 64


def _qk_stage(q, shift, first, kt_refs, qs_scr, s_scr):
    tq = q.shape[0]
    for i in range(ATTN_STACK):
        qs_scr[i * tq:(i + 1) * tq, :] = q * _lane_mask(ATTN_WIDTH, shift, first + i)
    qs = qs_scr[...]
    base = 0
    for kt_ref in kt_refs:
        for c0, ck in _key_chunks(kt_ref.shape[-1]):
            s_scr[:, base + c0:base + c0 + ck] = _dot(qs, kt_ref[:, c0:c0 + ck])
        base += kt_ref.shape[-1]


def _exp_stage(s_scr, p_scr):
    n_tiles = s_scr.shape[-1] // LANES
    inv = []
    for r0 in range(0, s_scr.shape[0], ATTN_ROW_BLOCK):
        rows = slice(r0, r0 + ATTN_ROW_BLOCK)
        m = s_scr[rows, 0:LANES]
        for j in range(1, n_tiles):
            m = jnp.maximum(m, s_scr[rows, j * LANES:(j + 1) * LANES])
        mx = jnp.max(m, axis=-1, keepdims=True)
        l = None
        for j in range(n_tiles):
            e = jnp.exp2(s_scr[rows, j * LANES:(j + 1) * LANES] - mx)
            p_scr[rows, j * LANES:(j + 1) * LANES] = e.astype(BF16)
            l = e if l is None else l + e
        inv.append(1.0 / jnp.sum(l, axis=-1, keepdims=True))
    return jnp.concatenate(inv, axis=0)


def _attention_passes(passes, next_first, consume, kt_refs, v_refs, scratch):
    qs, ss, ps = scratch[0:2], scratch[2:4], scratch[4:6]
    n = len(passes)
    assert n % 2 == 0
    tq = passes[0][0].shape[0]

    def qk(p, i):
        q, shift, first, g = p
        _qk_stage(q, shift, first, [kt.at[0, g] for kt in kt_refs], qs[i % 2], ss[i % 2])

    @pl.when(pl.program_id(1) == 0)
    def _():
        qk(passes[0], 0)

    for i in range(n):
        qk(passes[i + 1] if i + 1 < n else next_first, i + 1)
        inv = _exp_stage(ss[i % 2], ps[i % 2])
        pv, base = None, 0
        for v_ref in v_refs:
            lk = v_ref.shape[2]
            part = _dot(ps[i % 2][:, base:base + lk], v_ref[0, passes[i][3]])
            pv = part if pv is None else pv + part
            base += lk
        pv = pv * inv
        consume(i, [pv[r * tq:(r + 1) * tq] for r in range(ATTN_STACK)])


def _diff_kernel(lqk_ref, q_ref, qn_ref, *refs, n_kv, lam_init):
    kt_refs, v_refs, o_ref, scratch = refs[:n_kv], refs[n_kv:2 * n_kv], refs[2 * n_kv], refs[2 * n_kv + 1:]
    x = lqk_ref[...]
    lam = (jnp.exp(jnp.sum(x[0:1] * x[1:2], axis=-1, keepdims=True))
           - jnp.exp(jnp.sum(x[2:3] * x[3:4], axis=-1, keepdims=True)) + lam_init)
    tq = min(ATTN_TQ, q_ref.shape[1])
    group = lax.broadcasted_iota(jnp.int32, (tq, ATTN_WIDTH), 1) >> 6
    halves = 2 * DIFF_HEADS // ATTN_STACK
    passes = []
    for t in range(q_ref.shape[1] // tq):
        q = q_ref[0, t * tq:(t + 1) * tq, :]
        passes += [(q, 5, half * ATTN_STACK, 0) for half in range(halves)]
    next_first = (qn_ref[0, 0:tq, :], 5, 0, 0)
    acc = {}

    def consume(i, pvs):
        t, half = divmod(i, halves)
        o1, o2 = acc.pop(t, (jnp.zeros(group.shape, F32),) * 2)
        for r, pv in enumerate(pvs):
            head = (half * ATTN_STACK + r) >> 1
            if r & 1 == 0:
                o1 = jnp.where(group == head, pv, o1)
            else:
                o2 = jnp.where(group == head, pv, o2)
        if half + 1 < halves:
            acc[t] = (o1, o2)
            return
        o_ref[0, t * tq:(t + 1) * tq, :] = (o1 - lam * o2).astype(BF16)

    _attention_passes(passes, next_first, consume, kt_refs, v_refs, scratch)


def _gqa_kernel(q_ref, qn_ref, *refs, n_kv):
    kt_refs, v_refs, o_ref, scratch = refs[:n_kv], refs[n_kv:2 * n_kv], refs[2 * n_kv], refs[2 * n_kv + 1:]
    tq = min(ATTN_TQ, q_ref.shape[1])
    group = lax.broadcasted_iota(jnp.int32, (tq, ATTN_WIDTH), 1) >> 6
    groups = q_ref.shape[2] // ATTN_WIDTH
    passes = []
    for t in range(q_ref.shape[1] // tq):
        for g in range(groups):
            passes.append((q_ref[0, t * tq:(t + 1) * tq, g * ATTN_WIDTH:(g + 1) * ATTN_WIDTH], 6, 0, g))

    def consume(i, pvs):
        t, g = divmod(i, groups)
        o = jnp.zeros(group.shape, F32)
        for r, pv in enumerate(pvs):
            o = jnp.where(group == r, pv, o)
        o_ref[0, t * tq:(t + 1) * tq, g * ATTN_WIDTH:(g + 1) * ATTN_WIDTH] = o.astype(BF16)

    next_first = (qn_ref[0, 0:tq, 0:ATTN_WIDTH], 6, 0, 0)
    _attention_passes(passes, next_first, consume, kt_refs, v_refs, scratch)


def _attention(kernel_fn, name, q, kts, vs, extra=()):
    b, lq, width = q.shape
    groups, w = kts[0].shape[1:3]
    lk = sum(kt.shape[3] for kt in kts)
    rows = min(lq, ATTN_STEP_ROWS)
    pass_rows = ATTN_STACK * min(ATTN_TQ, rows)
    steps = lq // rows
    kv_spec = lambda a: pl.BlockSpec((1,) + a.shape[1:], lambda bi, i: (bi, 0, 0, 0), pipeline_mode=pl.Buffered(1))
    return pl.pallas_call(
        functools.partial(kernel_fn, n_kv=len(kts)), name=name,
        grid=(b, steps),
        in_specs=[pl.BlockSpec(a.shape, lambda bi, i: (0, 0)) for a in extra]
        + [pl.BlockSpec((1, rows, width), lambda bi, i: (bi, i, 0)),
           pl.BlockSpec((1, rows, width), lambda bi, i: (bi, jnp.minimum(i + 1, steps - 1), 0))]
        + [kv_spec(a) for a in kts] + [kv_spec(a) for a in vs],
        out_specs=pl.BlockSpec((1, rows, width), lambda bi, i: (bi, i, 0)),
        out_shape=jax.ShapeDtypeStruct((b, lq, width), BF16),
        scratch_shapes=[pltpu.VMEM((pass_rows, w), BF16)] * 2 + [pltpu.VMEM((pass_rows, lk), F32)] * 2
        + [pltpu.VMEM((pass_rows, lk), BF16)] * 2,
        compiler_params=_params("arbitrary", "arbitrary"),
    )(*extra, q, q, *kts, *vs)


def _diff_attention(lqk, q, kts, vs, lam_init):
    return _attention(functools.partial(_diff_kernel, lam_init=lam_init), "diff_attn", q, kts, vs, (lqk,))


def _gqa_attention(q, kts, vs):
    return _attention(_gqa_kernel, "gqa_attn", q, kts, vs)


def _top2_route(logits):
    lane = lax.broadcasted_iota(jnp.int32, logits.shape, 1).astype(F32)
    neg = jnp.float32(-jnp.inf)
    lg = jnp.where(lane < N_EXPERTS, logits, neg)
    m1 = jnp.max(lg, axis=-1, keepdims=True)
    i1 = jnp.min(jnp.where(lg == m1, lane, float(LANES)), axis=-1, keepdims=True)
    lg2 = jnp.where(lane == i1, neg, lg)
    m2 = jnp.max(lg2, axis=-1, keepdims=True)
    i2 = jnp.min(jnp.where(lg2 == m2, lane, float(LANES)), axis=-1, keepdims=True)
    e2 = jnp.exp(m2 - m1)
    w1 = 1.0 / (1.0 + e2)
    return (jnp.where(lane == 0, i1, 0.0) + jnp.where(lane == 1, i2, 0.0)
            + jnp.where(lane == 2, w1, 0.0) + jnp.where(lane == 3, e2 * w1, 0.0))


def _outproj_kernel(*refs, moe):
    yhy_ref, od_ref, og_ref, h_ref, gt_ref, w_ref, gf_ref, sc_ref, sh_ref, sg_ref, bd_ref = refs[:11]
    if moe:
        rt_ref, hn_ref, u_ref, gates_ref = refs[11:]
    else:
        hn_ref, u_ref = refs[11:]
    od = od_ref[0].astype(F32)
    od = od * lax.rsqrt(_group_mean_sq(od, bd_ref[...], DIFF_V_DIM) + NORM_EPS) * sg_ref[...]
    y = (_dot(yhy_ref[0], w_ref[0:HY_WIDTH, :])
         + _dot(od.astype(BF16), w_ref[HY_WIDTH:HY_WIDTH + DIFF_WIDTH, :])
         + _dot(og_ref[0], w_ref[HY_WIDTH + DIFF_WIDTH:, :]))
    hn = h_ref[0] + gt_ref[0] * y
    hn_ref[0] = hn
    u = _rms(hn, gf_ref[...]) * (1.0 + sc_ref[0]) + sh_ref[0]
    u_ref[0] = u.astype(u_ref.dtype)
    if moe:
        u_hi = u.astype(BF16)
        u_lo = (u - u_hi.astype(F32)).astype(BF16)
        logits = _dot(u_hi, rt_ref[0]) + (_dot(u_lo, rt_ref[0]) + _dot(u_hi, rt_ref[1]))
        gates_ref[0] = _top2_route(logits)


def _outproj(yhy, od, og, h, gt, w_bf, gf, sc, sh, sub_g, bd256, router_pad):
    b, l, d = h.shape
    tm = min(l, 512)
    row = lambda bi, i: (bi, i, 0)
    vec = lambda bi, i: (bi, 0, 0)
    const = lambda bi, i: (0, 0)
    moe = router_pad is not None
    in_specs = [pl.BlockSpec((1, tm, HY_WIDTH), row), pl.BlockSpec((1, tm, DIFF_WIDTH), row),
                pl.BlockSpec((1, tm, GQA_WIDTH), row), pl.BlockSpec((1, tm, d), row),
                pl.BlockSpec((1, 1, d), vec), pl.BlockSpec((d, d), const), pl.BlockSpec((1, d), const),
                pl.BlockSpec((1, 1, d), vec), pl.BlockSpec((1, 1, d), vec),
                pl.BlockSpec((1, DIFF_WIDTH), const), pl.BlockSpec((DIFF_WIDTH, DIFF_WIDTH), const)]
    args = [yhy, od, og, h, gt, w_bf, gf, sc, sh, sub_g, bd256]
    out_specs = [pl.BlockSpec((1, tm, d), row), pl.BlockSpec((1, tm, d), row)]
    out_shape = [jax.ShapeDtypeStruct((b, l, d), F32), jax.ShapeDtypeStruct((b, l, d), F32 if moe else BF16)]
    if moe:
        in_specs.append(pl.BlockSpec((2, d, LANES), lambda bi, i: (0, 0, 0)))
        args.append(router_pad)
        out_specs.append(pl.BlockSpec((1, tm, LANES), row))
        out_shape.append(jax.ShapeDtypeStruct((b, l, LANES), F32))
    return pl.pallas_call(
        functools.partial(_outproj_kernel, moe=moe), name="outproj",
        grid=(b, l // tm), in_specs=in_specs, out_specs=out_specs, out_shape=out_shape,
        compiler_params=_params("arbitrary", "arbitrary"),
    )(*args)


def _ffn_kernel(u_ref, wg_ref, wu_ref, wd_ref, h_ref, gt_ref, o_ref, acc_ref):
    f = pl.program_id(2)

    @pl.when(f == 0)
    def _():
        acc_ref[...] = jnp.zeros_like(acc_ref)

    u = u_ref[0]
    hid = _silu(_dot(u, wg_ref[...])) * _dot(u, wu_ref[...])
    acc_ref[...] += _dot(hid.astype(BF16), wd_ref[...])

    @pl.when(f == pl.num_programs(2) - 1)
    def _():
        o_ref[0] = h_ref[0] + gt_ref[0] * acc_ref[...]


def _ffn(u, wg, wu, wd, h, gt):
    b, l, d = h.shape
    dff = wg.shape[1]
    tm = min(l, 512)
    tf = dff // 2
    row = lambda bi, i, f: (bi, i, 0)
    return pl.pallas_call(
        _ffn_kernel, name="ffn",
        grid=(b, l // tm, dff // tf),
        in_specs=[pl.BlockSpec((1, tm, d), row),
                  pl.BlockSpec((d, tf), lambda bi, i, f: (0, f)), pl.BlockSpec((d, tf), lambda bi, i, f: (0, f)),
                  pl.BlockSpec((tf, d), lambda bi, i, f: (f, 0)),
                  pl.BlockSpec((1, tm, d), row), pl.BlockSpec((1, 1, d), lambda bi, i, f: (bi, 0, 0))],
        out_specs=pl.BlockSpec((1, tm, d), row),
        out_shape=jax.ShapeDtypeStruct((b, l, d), F32),
        scratch_shapes=[pltpu.VMEM((tm, d), F32)],
        compiler_params=_params("arbitrary", "arbitrary", "arbitrary"),
    )(u, wg, wu, wd, h, gt)


MOE_TILE = 512
SC_CHUNK = 64


def _sc_gather(table, idx):
    info = plsc.get_sparse_core_info()
    nc, ns = info.num_cores, info.num_subcores
    n, d = idx.shape[0], table.shape[1]
    per_worker = n // (nc * ns)
    assert per_worker * nc * ns == n and per_worker % SC_CHUNK == 0
    mesh = plsc.VectorSubcoreMesh(core_axis_name="c", subcore_axis_name="s")

    @functools.partial(
        pl.kernel, mesh=mesh,
        out_type=jax.ShapeDtypeStruct((n, d), table.dtype),
        scratch_types=[pltpu.VMEM((SC_CHUNK,), jnp.int32), pltpu.VMEM((SC_CHUNK, d), table.dtype),
                       pltpu.SemaphoreType.DMA],
    )
    def gather_kernel(table_hbm, idx_hbm, out_hbm, idx_v, rows_v, sem):
        base = (lax.axis_index("s") * nc + lax.axis_index("c")) * per_worker

        @pl.loop(0, per_worker // SC_CHUNK)
        def _(j):
            off = pl.multiple_of(base + j * SC_CHUNK, 8)
            pltpu.sync_copy(idx_hbm.at[pl.ds(off, SC_CHUNK)], idx_v)
            pltpu.async_copy(table_hbm.at[idx_v], rows_v, sem).wait()
            pltpu.sync_copy(rows_v, out_hbm.at[pl.ds(off, SC_CHUNK)])

    return gather_kernel(table, idx)


def _route_plan(route, n_rows):
    t = route.shape[0]
    e = route[:, :2].astype(jnp.int32).T.reshape(2 * t)
    onehot = (e[:, None] == jnp.arange(N_EXPERTS, dtype=jnp.int32)[None, :]).astype(jnp.int32)
    csum = jnp.cumsum(onehot, axis=0)
    counts = csum[-1]
    rank = jnp.take_along_axis(csum, e[:, None], axis=1)[:, 0] - 1
    padded = (counts + MOE_TILE - 1) // MOE_TILE * MOE_TILE
    ends = jnp.cumsum(padded)
    pos = (ends - padded)[e] + rank
    row_token = jnp.zeros((n_rows,), jnp.int32).at[pos].set(jnp.arange(2 * t, dtype=jnp.int32) % t)
    tile_start = jnp.arange(n_rows // MOE_TILE, dtype=jnp.int32) * MOE_TILE
    tile_expert = jnp.minimum(jnp.sum((tile_start[:, None] >= ends[None, :]).astype(jnp.int32), axis=1),
                              N_EXPERTS - 1)
    n_used = (ends[-1] // MOE_TILE).reshape(1)
    return pos, row_token, tile_expert, n_used


def _gmm_kernel(te_ref, nu_ref, x_ref, wg_ref, wu_ref, wd_ref, o_ref, acc_ref):
    t, f = pl.program_id(0), pl.program_id(1)

    @pl.when(t < nu_ref[0])
    def _():
        @pl.when(f == 0)
        def _():
            acc_ref[...] = jnp.zeros_like(acc_ref)

        x = x_ref[...].astype(BF16)
        hid = _silu(_dot(x, wg_ref[0])) * _dot(x, wu_ref[0])
        acc_ref[...] += _dot(hid.astype(BF16), wd_ref[0])

        @pl.when(f == pl.num_programs(1) - 1)
        def _():
            o_ref[...] = acc_ref[...]


def _grouped_swiglu(x_sorted, tile_expert, n_used, wg, wu, wd):
    r, d = x_sorted.shape
    dff = wg.shape[2]
    n_f = 2
    tf = dff // n_f
    fidx = lambda t, f, te, nu: jnp.where(t < nu[0], f, n_f - 1)
    return pl.pallas_call(
        _gmm_kernel, name="grouped_swiglu",
        grid_spec=pltpu.PrefetchScalarGridSpec(
            num_scalar_prefetch=2,
            grid=(r // MOE_TILE, n_f),
            in_specs=[pl.BlockSpec((MOE_TILE, d), lambda t, f, te, nu: (t, 0)),
                      pl.BlockSpec((1, d, tf), lambda t, f, te, nu: (te[t], 0, fidx(t, f, te, nu))),
                      pl.BlockSpec((1, d, tf), lambda t, f, te, nu: (te[t], 0, fidx(t, f, te, nu))),
                      pl.BlockSpec((1, tf, d), lambda t, f, te, nu: (te[t], fidx(t, f, te, nu), 0))],
            out_specs=pl.BlockSpec((MOE_TILE, d), lambda t, f, te, nu: (t, 0)),
            scratch_shapes=[pltpu.VMEM((MOE_TILE, d), F32)]),
        out_shape=jax.ShapeDtypeStruct((r, d), F32),
        compiler_params=_params("arbitrary", "arbitrary"),
    )(tile_expert, n_used, x_sorted, wg, wu, wd)


def _moe_combine_kernel(y1_ref, y2_ref, rt_ref, h_ref, gt_ref, o_ref):
    rt = rt_ref[...]
    out = rt[:, 2:3] * y1_ref[...] + rt[:, 3:4] * y2_ref[...]
    o_ref[...] = h_ref[...] + gt_ref[0] * out


def _moe_combine(y_pair, route, h, gt_tab, tile_vec):
    t, d = h.shape
    tm = 512
    return pl.pallas_call(
        _moe_combine_kernel, name="moe_combine",
        grid=(t // tm,),
        in_specs=[pl.BlockSpec((tm, d), lambda i: (i, 0)), pl.BlockSpec((tm, d), lambda i: (i + t // tm, 0)),
                  pl.BlockSpec((tm, LANES), lambda i: (i, 0)), pl.BlockSpec((tm, d), lambda i: (i, 0)),
                  pl.BlockSpec((1, 1, d), lambda i: (tile_vec(i, tm), 0, 0))],
        out_specs=pl.BlockSpec((tm, d), lambda i: (i, 0)),
        out_shape=jax.ShapeDtypeStruct((t, d), F32),
        compiler_params=_params("arbitrary"),
    )(y_pair, y_pair, route, h, gt_tab)


def _moe(u, route, wg, wu, wd, h, gt_tab, tile_vec):
    t, d = u.shape
    n_rows = 2 * t + N_EXPERTS * MOE_TILE
    pos, row_token, tile_expert, n_used = _route_plan(route, n_rows)
    x_sorted = _sc_gather(u, row_token)
    y_sorted = _grouped_swiglu(x_sorted, tile_expert, n_used, wg, wu, wd)
    return _moe_combine(_sc_gather(y_sorted, pos), route, h, gt_tab, tile_vec)


def _final_norm_kernel(h_ref, g_ref, o_ref):
    o_ref[0] = _rms(h_ref[0], g_ref[...])


def _final_norm(h, g):
    b, l, d = h.shape
    tm = min(l, 1024)
    return pl.pallas_call(
        _final_norm_kernel, name="final_norm",
        grid=(b, l // tm),
        in_specs=[pl.BlockSpec((1, tm, d), lambda bi, i: (bi, i, 0)), pl.BlockSpec((1, d), lambda bi, i: (0, 0))],
        out_specs=pl.BlockSpec((1, tm, d), lambda bi, i: (bi, i, 0)),
        out_shape=jax.ShapeDtypeStruct((b, l, d), F32),
        compiler_params=_params("arbitrary", "arbitrary"),
    )(h, g.reshape(1, d))


def _rope_tables(length, head_dim):
    t = jnp.arange(length)
    row = (t // GRID_W).astype(F32)
    col = (t % GRID_W).astype(F32)
    n = head_dim // 4
    inv = ROPE_THETA ** (-jnp.arange(n, dtype=F32) / n)
    ang = jnp.concatenate([row[:, None] * inv, col[:, None] * inv], axis=-1)
    cos = jnp.concatenate([jnp.cos(ang)] * 2, axis=-1)
    sin = jnp.concatenate([-jnp.sin(ang), jnp.sin(ang)], axis=-1)
    reps = LANES // head_dim
    return jnp.tile(cos, (1, reps)), jnp.tile(sin, (1, reps))


def _hyena_tables(length):
    k = jnp.arange(length, dtype=jnp.int32)
    period = 4 * length
    step = 2.0 * math.pi / period
    lo = jnp.arange(64, dtype=jnp.int32)
    hi = jnp.arange(length // 64, dtype=jnp.int32)

    def tabs(x, y_hi, y_lo):
        def cs(y):
            ang = ((x[:, None] * y[None, :]) % period).astype(F32) * step
            return jnp.cos(ang)[:, :, None], jnp.sin(ang)[:, :, None]
        (ca, sa), (cb, sb) = cs(y_hi), cs(y_lo)
        cb, sb = jnp.swapaxes(cb, 1, 2), jnp.swapaxes(sb, 1, 2)
        c = (ca * cb - sa * sb).reshape(length, length)
        s = (sa * cb + ca * sb).reshape(length, length)
        return c.astype(BF16), s.astype(BF16)

    c_tab, s_tab = tabs(2 * k + 1, 64 * hi, lo)
    ci_tab, si_tab = tabs(k, 128 * hi, 2 * lo + 1)
    t = jnp.linspace(0.0, 1.0, length, dtype=F32)[:, None]
    bands = (HY_EMB - 1) // 2
    ang = (2.0 * math.pi / length) * jnp.arange(length, dtype=F32)[:, None] \
        * jnp.linspace(1e-4, bands - 1, bands, dtype=F32)
    feats = jnp.concatenate([t, jnp.cos(ang), -jnp.sin(ang),
                             jnp.zeros((length, LANES - HY_EMB), F32)], axis=-1)
    max_decay = math.log(HY_DECAY_TARGET) / HY_FAST_DECAY
    min_decay = math.log(HY_DECAY_TARGET) / HY_SLOW_DECAY
    deltas = jnp.abs(jnp.linspace(min_decay, max_decay, HY_WIDTH, dtype=F32)).reshape(1, HY_WIDTH)
    return c_tab, s_tab, ci_tab, si_tab, feats, deltas


def _block_diag_ones(n, group):
    i = jnp.arange(n) // group
    return (i[:, None] == i[None, :]).astype(BF16)


def kernel(x, c, ctx, c_ctx, w_ada, b_ada, g_mix, g_ffn, w_in, w_out, hy_conv_w, hy_conv_b, hy_w1, hy_b1, hy_w2, hy_b2, hy_w3, hy_freq, hy_skip, diff_lq1, diff_lk1, diff_lq2, diff_lk2, diff_subln, gqa_qnorm, gqa_knorm, ffn_wg, ffn_wu, ffn_wd, moe_router, moe_wg, moe_wu, moe_wd, g_final):
    b, seq, d = x.shape
    n_ctx = ctx.shape[1]
    depth = w_ada.shape[0]

    rope_tabs = _rope_tables(seq, DIFF_QK_DIM) + _rope_tables(seq, GQA_HEAD_DIM)
    hy_tabs = _hyena_tables(seq)
    hy_tabs_c = _hyena_tables(n_ctx)
    bd128 = _block_diag_ones(LANES, GQA_HEAD_DIM)
    bd256 = _block_diag_ones(DIFF_WIDTH, DIFF_V_DIM)

    rows = 16
    cc = jnp.zeros((rows, d), F32).at[:b].set(c).at[b].set(c_ctx)
    mods = _mods(cc, w_ada, b_ada)

    def mod_vecs(i):
        lat = [mods[i, :b, j * d:(j + 1) * d].reshape(b, 1, d) for j in range(6)]
        cx = [jnp.broadcast_to(mods[i, b, j * d:(j + 1) * d].reshape(1, 1, d), (b, 1, d)) for j in range(6)]
        return lat, cx

    h, hc = x, ctx
    for i in range(depth):
        last = i == depth - 1
        lam_init = 0.8 - 0.6 * math.exp(-0.3 * i)
        (sh_m, sc_m, gt_m, sh_f, sc_f, gt_f), (csh_m, csc_m, cgt_m, csh_f, csc_f, cgt_f) = mod_vecs(i)
        w_in_bf = w_in[i].astype(BF16)
        w_out_bf = w_out[i].astype(BF16)
        g_m = g_mix[i].reshape(1, d)
        g_f = g_ffn[i].reshape(1, d)
        qn = jnp.tile(gqa_qnorm[i], LANES // GQA_HEAD_DIM).reshape(1, LANES)
        kn = jnp.tile(gqa_knorm[i], LANES // GQA_HEAD_DIM).reshape(1, LANES)
        subln = jnp.tile(diff_subln[i], DIFF_HEADS).reshape(1, DIFF_WIDTH) * (1.0 - lam_init)
        lqk = jnp.stack([diff_lq1[i], diff_lk1[i], diff_lq2[i], diff_lk2[i]]).astype(F32)
        hy_p = (hy_conv_w[i], hy_conv_b[i], hy_w1[i], hy_b1[i], hy_w2[i], hy_b2[i], hy_w3[i],
                hy_freq[i], hy_skip[i])
        moe = i % 2 == 1
        router_pad = None
        if moe:
            router = jnp.zeros((d, LANES), F32).at[:, :N_EXPERTS].set(moe_router[i // 2])
            router_hi = router.astype(BF16)
            router_pad = jnp.stack([router_hi, (router - router_hi.astype(F32)).astype(BF16)])

        zhy, qd, kdt, vdx, qg, kgt, vgx = _inproj(h, sc_m, sh_m, g_m, w_in_bf, qn, kn, bd128, rope_tabs)
        czhy, cqd, ckdt, cvdx, cqg, ckgt, cvgx = _inproj(hc, csc_m, csh_m, g_m, w_in_bf, qn, kn, bd128, None)

        y_hy = _hyena(zhy, hy_p, hy_tabs)
        o_d = _diff_attention(lqk, qd, [kdt, ckdt], [vdx, cvdx], lam_init)
        o_g = _gqa_attention(qg, [kgt, ckgt], [vgx, cvgx])
        mixed = _outproj(y_hy, o_d, o_g, h, gt_m, w_out_bf, g_f, sc_f, sh_f, subln, bd256, router_pad)
        if not last:
            yc_hy = _hyena(czhy, hy_p, hy_tabs_c)
            oc_d = _diff_attention(lqk, cqd, [ckdt], [cvdx], lam_init)
            oc_g = _gqa_attention(cqg, [ckgt], [cvgx])
            cmixed = _outproj(yc_hy, oc_d, oc_g, hc, cgt_m, w_out_bf, g_f, csc_f, csh_f, subln, bd256,
                              router_pad)

        if moe:
            wg, wu, wd = (_to_bf16(w[i // 2]) for w in (moe_wg, moe_wu, moe_wd))
            flat = lambda a: a.reshape(-1, a.shape[-1])
            gt_tab = jnp.concatenate([gt_f, cgt_f[:1]], axis=0)
            if last:
                tile_vec = lambda t, tm: t // (seq // tm)
                h = _moe(flat(mixed[1]), flat(mixed[2]), wg, wu, wd, flat(mixed[0]), gt_tab,
                         tile_vec).reshape(b, seq, d)
            else:
                n_c = b * n_ctx
                tile_vec = lambda t, tm: jnp.where(t < n_c // tm, b, (t - n_c // tm) // (seq // tm))
                both = lambda c_, a: jnp.concatenate([flat(c_), flat(a)], axis=0)
                h_all = _moe(both(cmixed[1], mixed[1]), both(cmixed[2], mixed[2]), wg, wu, wd,
                             both(cmixed[0], mixed[0]), gt_tab, tile_vec)
                hc = h_all[:n_c].reshape(b, n_ctx, d)
                h = h_all[n_c:].reshape(b, seq, d)
        else:
            wg, wu, wd = (_to_bf16(w[i // 2]) for w in (ffn_wg, ffn_wu, ffn_wd))
            h = _ffn(mixed[1], wg, wu, wd, mixed[0], gt_f)
            if not last:
                hc = _ffn(cmixed[1], wg, wu, wd, cmixed[0], cgt_f)
    return _final_norm(h, g_final)
```

```python
import functools
import math

import jax
import jax.numpy as jnp
from jax import lax
from jax.experimental import pallas as pl
from jax.experimental.pallas import tpu as pltpu
from jax.experimental.pallas import tpu_sc as plsc

F32 = jnp.float32
BF16 = jnp.bfloat16
HIGHEST = lax.Precision.HIGHEST

D_MODEL = 1024
GRID_W = 64
ROPE_THETA = 10000.0
NORM_EPS = 1e-6

HY_WIDTH = 256
HY_EMB = 33
HY_FAST_DECAY = 0.3
HY_SLOW_DECAY = 1.5
HY_DECAY_TARGET = 1e-2

DIFF_HEADS = 4
DIFF_QK_DIM = 32
DIFF_V_DIM = 64
DIFF_WIDTH = 256
GQA_HEAD_DIM = 64
GQA_WIDTH = 512
GQA_KV_WIDTH = 128
GQA_REP = 4

OFF_DQ = 3 * HY_WIDTH
OFF_DK = OFF_DQ + DIFF_WIDTH
OFF_DV = OFF_DK + DIFF_WIDTH
OFF_GQ = OFF_DV + DIFF_WIDTH
OFF_GK = OFF_GQ + GQA_WIDTH
OFF_GV = OFF_GK + GQA_KV_WIDTH
IN_COLS = OFF_GV + GQA_KV_WIDTH

N_EXPERTS = 8
LOG2_E = math.log2(math.e)
LANES = 128
VMEM_LIMIT = 56 * 1024 * 1024


def _params(*sem):
    return pltpu.CompilerParams(dimension_semantics=sem, vmem_limit_bytes=VMEM_LIMIT)


def _dot(a, b):
    return jnp.dot(a, b, preferred_element_type=F32)


def _dot_hi(a, b):
    return jnp.dot(a, b, preferred_element_type=F32, precision=HIGHEST)


def _rms(x, g):
    ms = jnp.mean(x * x, axis=-1, keepdims=True)
    return x * lax.rsqrt(ms + NORM_EPS) * g


def _silu(x):
    return x * jax.nn.sigmoid(x)


def _group_mean_sq(x, ones_bd, width):
    return _dot((x * x).astype(BF16), ones_bd) * (1.0 / width)


def _mods_kernel(c_ref, w_ref, b_ref, o_ref):
    o_ref[0] = _dot_hi(_silu(c_ref[...]), w_ref[0]) + b_ref[0]


def _mods(cc, w_ada, b_ada):
    depth, d, n = w_ada.shape
    tn = 1536
    return pl.pallas_call(
        _mods_kernel, name="mods",
        grid=(depth, n // tn),
        in_specs=[pl.BlockSpec(cc.shape, lambda i, j: (0, 0)),
                  pl.BlockSpec((1, d, tn), lambda i, j: (i, 0, j)),
                  pl.BlockSpec((1, 1, tn), lambda i, j: (i, 0, j))],
        out_specs=pl.BlockSpec((1, cc.shape[0], tn), lambda i, j: (i, 0, j)),
        out_shape=jax.ShapeDtypeStruct((depth, cc.shape[0], n), F32),
        compiler_params=_params("arbitrary", "arbitrary"),
    )(cc, w_ada, b_ada.reshape(depth, 1, n))


def _rope128(x, cos, sin_signed, half):
    lane = lax.broadcasted_iota(jnp.int32, x.shape, 1)
    first = (lane & (2 * half - 1)) < half
    swapped = jnp.where(first, pltpu.roll(x, LANES - half, 1), pltpu.roll(x, half, 1))
    return x * cos + swapped * sin_signed


def _inproj_kernel(*refs, rope):
    h_ref, sc_ref, sh_ref, g_ref, w_ref, qn_ref, kn_ref, bd_ref = refs[:8]
    if rope:
        cd_ref, sd_ref, cg_ref, sg_ref = refs[8:12]
    zhy_ref, qd_ref, kd_ref, vd_ref, qg_ref, kg_ref, vg_ref = refs[-7:]
    u = _rms(h_ref[0], g_ref[...]) * (1.0 + sc_ref[0]) + sh_ref[0]
    z = _dot(u.astype(BF16), w_ref[...])
    zhy_ref[0] = z[:, :OFF_DQ]

    def piece(off, j):
        return z[:, off + LANES * j: off + LANES * (j + 1)]

    for j in range(DIFF_WIDTH // LANES):
        q, k = piece(OFF_DQ, j), piece(OFF_DK, j)
        if rope:
            q = _rope128(q, cd_ref[...], sd_ref[...], DIFF_QK_DIM // 2)
            k = _rope128(k, cd_ref[...], sd_ref[...], DIFF_QK_DIM // 2)
        qd_ref[0, :, LANES * j: LANES * (j + 1)] = (q * (LOG2_E * DIFF_QK_DIM ** -0.5)).astype(BF16)
        kd_ref[0, 0, LANES * j: LANES * (j + 1), :] = k.T.astype(BF16)
    vd_ref[0, 0] = z[:, OFF_DV:OFF_GQ].astype(BF16)

    def gqa_piece(x, gain):
        ms = _group_mean_sq(x, bd_ref[...], GQA_HEAD_DIM)
        x = x * lax.rsqrt(ms + NORM_EPS) * gain
        if rope:
            x = _rope128(x, cg_ref[...], sg_ref[...], GQA_HEAD_DIM // 2)
        return x

    for j in range(GQA_WIDTH // LANES):
        q = gqa_piece(piece(OFF_GQ, j), qn_ref[...])
        qg_ref[0, :, LANES * j: LANES * (j + 1)] = (q * (LOG2_E * GQA_HEAD_DIM ** -0.5)).astype(BF16)
    kt = gqa_piece(piece(OFF_GK, 0), kn_ref[...]).T.astype(BF16)
    v = z[:, OFF_GV:]
    v_swapped = pltpu.roll(v, GQA_HEAD_DIM, 1)
    low = lax.broadcasted_iota(jnp.int32, v.shape, 1) < GQA_HEAD_DIM
    v_rep = (jnp.where(low, v, v_swapped), jnp.where(low, v_swapped, v))
    for g in range(GQA_KV_WIDTH // GQA_HEAD_DIM):
        for r in range(GQA_REP):
            kg_ref[0, g, GQA_HEAD_DIM * r: GQA_HEAD_DIM * (r + 1), :] = kt[GQA_HEAD_DIM * g: GQA_HEAD_DIM * (g + 1), :]
        for half in range(GQA_REP * GQA_HEAD_DIM // LANES):
            vg_ref[0, g, :, LANES * half: LANES * (half + 1)] = v_rep[g].astype(BF16)


def _inproj(h, sc, sh, g, w_bf, qn, kn, bd, rope_tabs):
    b, l, d = h.shape
    tm = min(l, 512)
    row = lambda bi, i: (bi, i, 0)
    vec = lambda bi, i: (bi, 0, 0)
    const = lambda bi, i: (0, 0)
    in_specs = [pl.BlockSpec((1, tm, d), row), pl.BlockSpec((1, 1, d), vec), pl.BlockSpec((1, 1, d), vec),
                pl.BlockSpec((1, d), const), pl.BlockSpec((d, IN_COLS), const),
                pl.BlockSpec((1, LANES), const), pl.BlockSpec((1, LANES), const),
                pl.BlockSpec((LANES, LANES), const)]
    args = [h, sc, sh, g, w_bf, qn, kn, bd]
    if rope_tabs is not None:
        in_specs += [pl.BlockSpec((tm, LANES), lambda bi, i: (i, 0))] * 4
        args += list(rope_tabs)
    aw = ATTN_WIDTH
    kv_groups = GQA_KV_WIDTH // GQA_HEAD_DIM
    rows_spec = lambda w: pl.BlockSpec((1, tm, w), row)
    kt_spec = lambda g: pl.BlockSpec((1, g, aw, tm), lambda bi, i: (bi, 0, 0, i))
    v_spec = lambda g: pl.BlockSpec((1, g, tm, aw), lambda bi, i: (bi, 0, i, 0))
    sds = jax.ShapeDtypeStruct
    return pl.pallas_call(
        functools.partial(_inproj_kernel, rope=rope_tabs is not None), name="inproj",
        grid=(b, l // tm),
        in_specs=in_specs,
        out_specs=[rows_spec(OFF_DQ), rows_spec(DIFF_WIDTH), kt_spec(1), v_spec(1),
                   rows_spec(GQA_WIDTH), kt_spec(kv_groups), v_spec(kv_groups)],
        out_shape=[sds((b, l, OFF_DQ), F32), sds((b, l, DIFF_WIDTH), BF16), sds((b, 1, aw, l), BF16),
                   sds((b, 1, l, aw), BF16), sds((b, l, GQA_WIDTH), BF16), sds((b, kv_groups, aw, l), BF16),
                   sds((b, kv_groups, l, aw), BF16)],
        compiler_params=_params("arbitrary", "arbitrary"),
    )(*args)


def _hy_filter_kernel(f_ref, w1_ref, b1_ref, w2_ref, b2_ref, w3_ref, fr_ref, dl_ref, hs_ref, hd_ref):
    f = f_ref[...]
    fr = fr_ref[...]
    a = jnp.sin(fr * (_dot_hi(f, w1_ref[...]) + b1_ref[...]))
    a = jnp.sin(fr * (_dot_hi(a, w2_ref[...]) + b2_ref[...]))
    hf = _dot_hi(a, w3_ref[...])
    decay = jnp.exp(-f[:, 0:1] * dl_ref[...])
    h_fwd = hf[:, :HY_WIDTH] * decay
    h_bwd = hf[:, HY_WIDTH:] * decay
    row = lax.broadcasted_iota(jnp.int32, h_bwd.shape, 0) + pl.program_id(0) * f.shape[0]
    h_bwd = jnp.where(row == 0, 0.0, h_bwd)
    hs_ref[...] = (h_fwd + h_bwd).astype(BF16)
    hd_ref[...] = (h_bwd - h_fwd).astype(BF16)


def _hy_spectrum_kernel(c_ref, s_ref, hs_ref, hd_ref, gre_ref, gim_ref, *, scale):
    gre_ref[...] = _dot(c_ref[...], hs_ref[...]) * scale
    gim_ref[...] = _dot(s_ref[...], hd_ref[...]) * scale


def _hy_pre_kernel(z_ref, zp_ref, zn_ref, cw_ref, cb_ref, wb_ref, wf_ref, x0_ref, *, nt):
    ti = pl.program_id(1)
    z = z_ref[0]
    tl = z.shape[0]
    row = lax.broadcasted_iota(jnp.int32, z.shape, 0)
    prev_row = jnp.where(ti > 0, zp_ref[0, 7:8, :], 0.0)
    next_row = jnp.where(ti < nt - 1, zn_ref[0, 0:1, :], 0.0)
    z_prev = jnp.where(row == 0, prev_row, pltpu.roll(z, 1, 0))
    z_next = jnp.where(row == tl - 1, next_row, pltpu.roll(z, tl - 1, 0))
    cw = cw_ref[...]
    y = cb_ref[...] + z_prev * cw[0:1] + z * cw[1:2] + z_next * cw[2:3]
    x0, x1, v = y[:, :HY_WIDTH], y[:, HY_WIDTH:2 * HY_WIDTH], y[:, 2 * HY_WIDTH:]
    w = v * x1
    wb_ref[...] = w.astype(BF16)
    wf_ref[0] = w
    x0_ref[0] = x0


def _hy_fwd_kernel(c_ref, s_ref, x_ref, gre_ref, gim_ref, yre_ref, yim_ref, *, reps):
    x = x_ref[...]
    a = _dot(c_ref[...], x)
    b = _dot(s_ref[...], x)
    gre = jnp.concatenate([gre_ref[...]] * reps, axis=1)
    gim = jnp.concatenate([gim_ref[...]] * reps, axis=1)
    yre_ref[...] = (a * gre + b * gim).astype(BF16)
    yim_ref[...] = (a * gim - b * gre).astype(BF16)


def _hy_inv_kernel(ci_ref, si_ref, yre_ref, yim_ref, wf_ref, x0_ref, skip_ref, o_ref, *, reps):
    y = _dot(ci_ref[...], yre_ref[...]) - _dot(si_ref[...], yim_ref[...])
    for r in range(reps):
        yr = y[:, r * HY_WIDTH:(r + 1) * HY_WIDTH]
        o_ref[r] = ((yr + wf_ref[r] * skip_ref[...]) * x0_ref[r]).astype(BF16)


def _hyena(zhy, p, tabs):
    conv_w, conv_b, w1, b1, w2, b2, w3, freq, skip = p
    c_tab, s_tab, ci_tab, si_tab, feats, deltas = tabs
    b, l, _ = zhy.shape
    c = HY_WIDTH
    tl = min(l, 512)
    nt = l // tl
    hid = w2.shape[0]
    const1 = lambda i: (0, 0)
    w1p = jnp.zeros((LANES, hid), F32).at[:HY_EMB].set(w1)
    hs, hd = pl.pallas_call(
        _hy_filter_kernel, name="hy_filter",
        grid=(nt,),
        in_specs=[pl.BlockSpec((tl, LANES), lambda i: (i, 0)),
                  pl.BlockSpec((LANES, hid), const1), pl.BlockSpec((1, hid), const1),
                  pl.BlockSpec((hid, hid), const1), pl.BlockSpec((1, hid), const1),
                  pl.BlockSpec((hid, 2 * c), const1), pl.BlockSpec((1, hid), const1),
                  pl.BlockSpec((1, c), const1)],
        out_specs=[pl.BlockSpec((tl, c), lambda i: (i, 0))] * 2,
        out_shape=[jax.ShapeDtypeStruct((l, c), BF16)] * 2,
        compiler_params=_params("arbitrary"),
    )(feats, w1p, b1.reshape(1, hid), w2, b2.reshape(1, hid), w3, freq.reshape(1, hid), deltas)

    gre, gim = pl.pallas_call(
        functools.partial(_hy_spectrum_kernel, scale=1.0 / l), name="hy_spectrum",
        grid=(nt,),
        in_specs=[pl.BlockSpec((tl, l), lambda i: (i, 0)), pl.BlockSpec((tl, l), lambda i: (i, 0)),
                  pl.BlockSpec((l, c), const1), pl.BlockSpec((l, c), const1)],
        out_specs=[pl.BlockSpec((tl, c), lambda i: (i, 0))] * 2,
        out_shape=[jax.ShapeDtypeStruct((l, c), F32)] * 2,
        compiler_params=_params("arbitrary"),
    )(c_tab, s_tab, hs, hd)

    halo = 8
    wb, wf, x0 = pl.pallas_call(
        functools.partial(_hy_pre_kernel, nt=nt), name="hy_pre",
        grid=(b, nt),
        in_specs=[pl.BlockSpec((1, tl, 3 * c), lambda bi, i: (bi, i, 0)),
                  pl.BlockSpec((1, halo, 3 * c), lambda bi, i: (bi, jnp.maximum(i * (tl // halo) - 1, 0), 0)),
                  pl.BlockSpec((1, halo, 3 * c),
                               lambda bi, i: (bi, jnp.minimum((i + 1) * (tl // halo), l // halo - 1), 0)),
                  pl.BlockSpec((3, 3 * c), lambda bi, i: (0, 0)),
                  pl.BlockSpec((1, 3 * c), lambda bi, i: (0, 0))],
        out_specs=[pl.BlockSpec((tl, c), lambda bi, i: (i, bi)),
                   pl.BlockSpec((1, tl, c), lambda bi, i: (bi, i, 0)),
                   pl.BlockSpec((1, tl, c), lambda bi, i: (bi, i, 0))],
        out_shape=[jax.ShapeDtypeStruct((l, b * c), BF16),
                   jax.ShapeDtypeStruct((b, l, c), F32),
                   jax.ShapeDtypeStruct((b, l, c), F32)],
        compiler_params=_params("arbitrary", "arbitrary"),
    )(zhy, zhy, zhy, conv_w, conv_b.reshape(1, 3 * c))

    reps = 2
    tn = reps * c
    nj = b * c // tn
    yre, yim = pl.pallas_call(
        functools.partial(_hy_fwd_kernel, reps=reps), name="hy_fwd",
        grid=(nj, nt),
        in_specs=[pl.BlockSpec((tl, l), lambda j, i: (i, 0)), pl.BlockSpec((tl, l), lambda j, i: (i, 0)),
                  pl.BlockSpec((l, tn), lambda j, i: (0, j)),
                  pl.BlockSpec((tl, c), lambda j, i: (i, 0)), pl.BlockSpec((tl, c), lambda j, i: (i, 0))],
        out_specs=[pl.BlockSpec((tl, tn), lambda j, i: (i, j))] * 2,
        out_shape=[jax.ShapeDtypeStruct((l, b * c), BF16)] * 2,
        compiler_params=_params("arbitrary", "arbitrary"),
    )(c_tab, s_tab, wb, gre, gim)

    return pl.pallas_call(
        functools.partial(_hy_inv_kernel, reps=reps), name="hy_inv",
        grid=(nj, nt),
        in_specs=[pl.BlockSpec((tl, l), lambda j, i: (i, 0)), pl.BlockSpec((tl, l), lambda j, i: (i, 0)),
                  pl.BlockSpec((l, tn), lambda j, i: (0, j)), pl.BlockSpec((l, tn), lambda j, i: (0, j)),
                  pl.BlockSpec((reps, tl, c), lambda j, i: (j, i, 0)),
                  pl.BlockSpec((reps, tl, c), lambda j, i: (j, i, 0)),
                  pl.BlockSpec((1, c), lambda j, i: (0, 0))],
        out_specs=pl.BlockSpec((reps, tl, c), lambda j, i: (j, i, 0)),
        out_shape=jax.ShapeDtypeStruct((b, l, c), BF16),
        compiler_params=_params("arbitrary", "arbitrary"),
    )(ci_tab, si_tab, yre, yim, wf, x0, skip.reshape(1, c))


def _lane_mask(width, shift, idx):
    lane = lax.broadcasted_iota(jnp.int32, (1, width), 1)
    return jnp.where((lane >> shift) == idx, 1.0, 0.0).astype(BF16)


ATTN_WIDTH = 256
ATTN_CHUNK = 512


def _key_chunks(lk):
    return [(c0, min(ATTN_CHUNK, lk - c0)) for c0 in range(0, lk, ATTN_CHUNK)]


ATTN_STACK = 4
ATTN_TQ = 128
ATTN_STEP_ROWS = 256
ATTN_ROW_BLOCK = 64


def _qk_stage(q, shift, first, kt_refs, qs_scr, s_scr):
    tq = q.shape[0]
    for i in range(ATTN_STACK):
        qs_scr[i * tq:(i + 1) * tq, :] = q * _lane_mask(ATTN_WIDTH, shift, first + i)
    qs = qs_scr[...]
    base = 0
    for kt_ref in kt_refs:
        for c0, ck in _key_chunks(kt_ref.shape[-1]):
            s_scr[:, base + c0:base + c0 + ck] = _dot(qs, kt_ref[:, c0:c0 + ck])
        base += kt_ref.shape[-1]


def _exp_stage(s_scr, p_scr):
    n_tiles = s_scr.shape[-1] // LANES
    inv = []
    for r0 in range(0, s_scr.shape[0], ATTN_ROW_BLOCK):
        rows = slice(r0, r0 + ATTN_ROW_BLOCK)
        m = s_scr[rows, 0:LANES]
        for j in range(1, n_tiles):
            m = jnp.maximum(m, s_scr[rows, j * LANES:(j + 1) * LANES])
        mx = jnp.max(m, axis=-1, keepdims=True)
        l = None
        for j in range(n_tiles):
            e = jnp.exp2(s_scr[rows, j * LANES:(j + 1) * LANES] - mx)
            p_scr[rows, j * LANES:(j + 1) * LANES] = e.astype(BF16)
            l = e if l is None else l + e
        inv.append(1.0 / jnp.sum(l, axis=-1, keepdims=True))
    return jnp.concatenate(inv, axis=0)


def _attention_passes(passes, next_first, consume, kt_refs, v_refs, scratch):
    qs, ss, ps = scratch[0:2], scratch[2:4], scratch[4:6]
    n = len(passes)
    assert n % 2 == 0
    tq = passes[0][0].shape[0]

    def qk(p, i):
        q, shift, first, g = p
        _qk_stage(q, shift, first, [kt.at[0, g] for kt in kt_refs], qs[i % 2], ss[i % 2])

    @pl.when(pl.program_id(1) == 0)
    def _():
        qk(passes[0], 0)

    for i in range(n):
        qk(passes[i + 1] if i + 1 < n else next_first, i + 1)
        inv = _exp_stage(ss[i % 2], ps[i % 2])
        pv, base = None, 0
        for v_ref in v_refs:
            lk = v_ref.shape[2]
            part = _dot(ps[i % 2][:, base:base + lk], v_ref[0, passes[i][3]])
            pv = part if pv is None else pv + part
            base += lk
        pv = pv * inv
        consume(i, [pv[r * tq:(r + 1) * tq] for r in range(ATTN_STACK)])


def _diff_kernel(lqk_ref, q_ref, qn_ref, *refs, n_kv, lam_init):
    kt_refs, v_refs, o_ref, scratch = refs[:n_kv], refs[n_kv:2 * n_kv], refs[2 * n_kv], refs[2 * n_kv + 1:]
    x = lqk_ref[...]
    lam = (jnp.exp(jnp.sum(x[0:1] * x[1:2], axis=-1, keepdims=True))
           - jnp.exp(jnp.sum(x[2:3] * x[3:4], axis=-1, keepdims=True)) + lam_init)
    tq = min(ATTN_TQ, q_ref.shape[1])
    group = lax.broadcasted_iota(jnp.int32, (tq, ATTN_WIDTH), 1) >> 6
    halves = 2 * DIFF_HEADS // ATTN_STACK
    passes = []
    for t in range(q_ref.shape[1] // tq):
        q = q_ref[0, t * tq:(t + 1) * tq, :]
        passes += [(q, 5, half * ATTN_STACK, 0) for half in range(halves)]
    next_first = (qn_ref[0, 0:tq, :], 5, 0, 0)
    acc = {}

    def consume(i, pvs):
        t, half = divmod(i, halves)
        o1, o2 = acc.pop(t, (jnp.zeros(group.shape, F32),) * 2)
        for r, pv in enumerate(pvs):
            head = (half * ATTN_STACK + r) >> 1
            if r & 1 == 0:
                o1 = jnp.where(group == head, pv, o1)
            else:
                o2 = jnp.where(group == head, pv, o2)
        if half + 1 < halves:
            acc[t] = (o1, o2)
            return
        o_ref[0, t * tq:(t + 1) * tq, :] = (o1 - lam * o2).astype(BF16)

    _attention_passes(passes, next_first, consume, kt_refs, v_refs, scratch)


def _gqa_kernel(q_ref, qn_ref, *refs, n_kv):
    kt_refs, v_refs, o_ref, scratch = refs[:n_kv], refs[n_kv:2 * n_kv], refs[2 * n_kv], refs[2 * n_kv + 1:]
    tq = min(ATTN_TQ, q_ref.shape[1])
    group = lax.broadcasted_iota(jnp.int32, (tq, ATTN_WIDTH), 1) >> 6
    groups = q_ref.shape[2] // ATTN_WIDTH
    passes = []
    for t in range(q_ref.shape[1] // tq):
        for g in range(groups):
            passes.append((q_ref[0, t * tq:(t + 1) * tq, g * ATTN_WIDTH:(g + 1) * ATTN_WIDTH], 6, 0, g))

    def consume(i, pvs):
        t, g = divmod(i, groups)
        o = jnp.zeros(group.shape, F32)
        for r, pv in enumerate(pvs):
            o = jnp.where(group == r, pv, o)
        o_ref[0, t * tq:(t + 1) * tq, g * ATTN_WIDTH:(g + 1) * ATTN_WIDTH] = o.astype(BF16)

    next_first = (qn_ref[0, 0:tq, 0:ATTN_WIDTH], 6, 0, 0)
    _attention_passes(passes, next_first, consume, kt_refs, v_refs, scratch)


def _attention(kernel_fn, name, q, kts, vs, extra=()):
    b, lq, width = q.shape
    groups, w = kts[0].shape[1:3]
    lk = sum(kt.shape[3] for kt in kts)
    rows = min(lq, ATTN_STEP_ROWS)
    pass_rows = ATTN_STACK * min(ATTN_TQ, rows)
    steps = lq // rows
    kv_spec = lambda a: pl.BlockSpec((1,) + a.shape[1:], lambda bi, i: (bi, 0, 0, 0), pipeline_mode=pl.Buffered(1))
    return pl.pallas_call(
        functools.partial(kernel_fn, n_kv=len(kts)), name=name,
        grid=(b, steps),
        in_specs=[pl.BlockSpec(a.shape, lambda bi, i: (0, 0)) for a in extra]
        + [pl.BlockSpec((1, rows, width), lambda bi, i: (bi, i, 0)),
           pl.BlockSpec((1, rows, width), lambda bi, i: (bi, jnp.minimum(i + 1, steps - 1), 0))]
        + [kv_spec(a) for a in kts] + [kv_spec(a) for a in vs],
        out_specs=pl.BlockSpec((1, rows, width), lambda bi, i: (bi, i, 0)),
        out_shape=jax.ShapeDtypeStruct((b, lq, width), BF16),
        scratch_shapes=[pltpu.VMEM((pass_rows, w), BF16)] * 2 + [pltpu.VMEM((pass_rows, lk), F32)] * 2
        + [pltpu.VMEM((pass_rows, lk), BF16)] * 2,
        compiler_params=_params("arbitrary", "arbitrary"),
    )(*extra, q, q, *kts, *vs)


def _diff_attention(lqk, q, kts, vs, lam_init):
    return _attention(functools.partial(_diff_kernel, lam_init=lam_init), "diff_attn", q, kts, vs, (lqk,))


def _gqa_attention(q, kts, vs):
    return _attention(_gqa_kernel, "gqa_attn", q, kts, vs)


def _top2_route(logits):
    lane = lax.broadcasted_iota(jnp.int32, logits.shape, 1).astype(F32)
    neg = jnp.float32(-jnp.inf)
    lg = jnp.where(lane < N_EXPERTS, logits, neg)
    m1 = jnp.max(lg, axis=-1, keepdims=True)
    i1 = jnp.min(jnp.where(lg == m1, lane, float(LANES)), axis=-1, keepdims=True)
    lg2 = jnp.where(lane == i1, neg, lg)
    m2 = jnp.max(lg2, axis=-1, keepdims=True)
    i2 = jnp.min(jnp.where(lg2 == m2, lane, float(LANES)), axis=-1, keepdims=True)
    e2 = jnp.exp(m2 - m1)
    w1 = 1.0 / (1.0 + e2)
    return (jnp.where(lane == 0, i1, 0.0) + jnp.where(lane == 1, i2, 0.0)
            + jnp.where(lane == 2, w1, 0.0) + jnp.where(lane == 3, e2 * w1, 0.0))


def _outproj_kernel(*refs, moe):
    yhy_ref, od_ref, og_ref, h_ref, gt_ref, w_ref, gf_ref, sc_ref, sh_ref, sg_ref, bd_ref = refs[:11]
    if moe:
        rt_ref = refs[11]
        hn_ref, u_ref, gates_ref = refs[-3:]
    else:
        hn_ref, u_ref = refs[11:]
    od = od_ref[0].astype(F32)
    od = od * lax.rsqrt(_group_mean_sq(od, bd_ref[...], DIFF_V_DIM) + NORM_EPS) * sg_ref[...]
    y = (_dot(yhy_ref[0], w_ref[0:HY_WIDTH, :])
         + _dot(od.astype(BF16), w_ref[HY_WIDTH:HY_WIDTH + DIFF_WIDTH, :])
         + _dot(og_ref[0], w_ref[HY_WIDTH + DIFF_WIDTH:, :]))
    hn = h_ref[0] + gt_ref[0] * y
    hn_ref[...] = hn.reshape(hn_ref.shape)
    u = _rms(hn, gf_ref[...]) * (1.0 + sc_ref[0]) + sh_ref[0]
    u_ref[...] = u.astype(u_ref.dtype).reshape(u_ref.shape)
    if moe:
        u_hi = u.astype(BF16)
        u_lo = (u - u_hi.astype(F32)).astype(BF16)
        logits = _dot(u_hi, rt_ref[0]) + (_dot(u_lo, rt_ref[0]) + _dot(u_hi, rt_ref[1]))
        gates_ref[...] = _top2_route(logits)


def _outproj(yhy, od, og, h, gt, w_bf, gf, sc, sh, sub_g, bd256, router_pad=None, into=None):
    b, l, d = h.shape
    tm = min(l, 512)
    row = lambda bi, i: (bi, i, 0)
    vec = lambda bi, i: (bi, 0, 0)
    const = lambda bi, i: (0, 0)
    moe = router_pad is not None
    in_specs = [pl.BlockSpec((1, tm, HY_WIDTH), row), pl.BlockSpec((1, tm, DIFF_WIDTH), row),
                pl.BlockSpec((1, tm, GQA_WIDTH), row), pl.BlockSpec((1, tm, d), row),
                pl.BlockSpec((1, 1, d), vec), pl.BlockSpec((d, d), const), pl.BlockSpec((1, d), const),
                pl.BlockSpec((1, 1, d), vec), pl.BlockSpec((1, 1, d), vec),
                pl.BlockSpec((1, DIFF_WIDTH), const), pl.BlockSpec((DIFF_WIDTH, DIFF_WIDTH), const)]
    args = [yhy, od, og, h, gt, w_bf, gf, sc, sh, sub_g, bd256]
    out_specs = [pl.BlockSpec((1, tm, d), row), pl.BlockSpec((1, tm, d), row)]
    out_shape = [jax.ShapeDtypeStruct((b, l, d), F32), jax.ShapeDtypeStruct((b, l, d), F32 if moe else BF16)]
    aliases = {}
    if moe:
        total_rows, row_offset, filled = into
        in_specs.append(pl.BlockSpec((2, d, LANES), lambda bi, i: (0, 0, 0)))
        args.append(router_pad)
        flat_row = lambda bi, i: (row_offset // tm + bi * (l // tm) + i, 0)
        widths = (d, d, LANES)
        out_specs = [pl.BlockSpec((tm, w), flat_row) for w in widths]
        out_shape = [jax.ShapeDtypeStruct((total_rows, w), F32) for w in widths]
        if filled is not None:
            aliases = {len(args) + j: j for j in range(len(filled))}
            in_specs += [pl.BlockSpec(memory_space=pl.ANY)] * len(filled)
            args += list(filled)
    return pl.pallas_call(
        functools.partial(_outproj_kernel, moe=moe), name="outproj",
        grid=(b, l // tm), in_specs=in_specs, out_specs=out_specs, out_shape=out_shape,
        input_output_aliases=aliases,
        compiler_params=_params("arbitrary", "arbitrary"),
    )(*args)


def _ffn_kernel(u_ref, wg_ref, wu_ref, wd_ref, h_ref, gt_ref, o_ref, acc_ref):
    f = pl.program_id(2)

    @pl.when(f == 0)
    def _():
        acc_ref[...] = jnp.zeros_like(acc_ref)

    u = u_ref[0]
    hid = _silu(_dot(u, wg_ref[...])) * _dot(u, wu_ref[...])
    acc_ref[...] += _dot(hid.astype(BF16), wd_ref[...])

    @pl.when(f == pl.num_programs(2) - 1)
    def _():
        o_ref[0] = h_ref[0] + gt_ref[0] * acc_ref[...]


def _ffn(u, wg, wu, wd, h, gt):
    b, l, d = h.shape
    dff = wg.shape[1]
    tm = min(l, 512)
    tf = dff // 2
    row = lambda bi, i, f: (bi, i, 0)
    return pl.pallas_call(
        _ffn_kernel, name="ffn",
        grid=(b, l // tm, dff // tf),
        in_specs=[pl.BlockSpec((1, tm, d), row),
                  pl.BlockSpec((d, tf), lambda bi, i, f: (0, f)), pl.BlockSpec((d, tf), lambda bi, i, f: (0, f)),
                  pl.BlockSpec((tf, d), lambda bi, i, f: (f, 0)),
                  pl.BlockSpec((1, tm, d), row), pl.BlockSpec((1, 1, d), lambda bi, i, f: (bi, 0, 0))],
        out_specs=pl.BlockSpec((1, tm, d), row),
        out_shape=jax.ShapeDtypeStruct((b, l, d), F32),
        scratch_shapes=[pltpu.VMEM((tm, d), F32)],
        compiler_params=_params("arbitrary", "arbitrary", "arbitrary"),
    )(u, wg, wu, wd, h, gt)


MOE_TILE = 512
SC_CHUNK = 64


def _sc_gather(table, idx):
    info = plsc.get_sparse_core_info()
    nc, ns = info.num_cores, info.num_subcores
    n, d = idx.shape[0], table.shape[1]
    per_worker = n // (nc * ns)
    assert per_worker * nc * ns == n and per_worker % SC_CHUNK == 0
    mesh = plsc.VectorSubcoreMesh(core_axis_name="c", subcore_axis_name="s")

    @functools.partial(
        pl.kernel, mesh=mesh,
        out_type=jax.ShapeDtypeStruct((n, d), table.dtype),
        scratch_types=[pltpu.VMEM((SC_CHUNK,), jnp.int32), pltpu.VMEM((SC_CHUNK, d), table.dtype),
                       pltpu.SemaphoreType.DMA],
    )
    def gather_kernel(table_hbm, idx_hbm, out_hbm, idx_v, rows_v, sem):
        base = (lax.axis_index("s") * nc + lax.axis_index("c")) * per_worker

        @pl.loop(0, per_worker // SC_CHUNK)
        def _(j):
            off = pl.multiple_of(base + j * SC_CHUNK, 8)
            pltpu.sync_copy(idx_hbm.at[pl.ds(off, SC_CHUNK)], idx_v)
            pltpu.async_copy(table_hbm.at[idx_v], rows_v, sem).wait()
            pltpu.sync_copy(rows_v, out_hbm.at[pl.ds(off, SC_CHUNK)])

    return gather_kernel(table, idx)


def _route_plan(route, n_rows):
    t = route.shape[0]
    e = route[:, :2].astype(jnp.int32).T.reshape(2 * t)
    onehot = (e[:, None] == jnp.arange(N_EXPERTS, dtype=jnp.int32)[None, :]).astype(jnp.int32)
    csum = jnp.cumsum(onehot, axis=0)
    counts = csum[-1]
    rank = jnp.take_along_axis(csum, e[:, None], axis=1)[:, 0] - 1
    padded = (counts + MOE_TILE - 1) // MOE_TILE * MOE_TILE
    ends = jnp.cumsum(padded)
    pos = (ends - padded)[e] + rank
    row_token = jnp.zeros((n_rows,), jnp.int32).at[pos].set(jnp.arange(2 * t, dtype=jnp.int32) % t)
    tile_start = jnp.arange(n_rows // MOE_TILE, dtype=jnp.int32) * MOE_TILE
    tile_expert = jnp.minimum(jnp.sum((tile_start[:, None] >= ends[None, :]).astype(jnp.int32), axis=1),
                              N_EXPERTS - 1)
    n_used = (ends[-1] // MOE_TILE).reshape(1)
    return pos, row_token, tile_expert, n_used


def _gmm_kernel(te_ref, nu_ref, x_ref, wg_ref, wu_ref, wd_ref, o_ref, acc_ref):
    t, f = pl.program_id(0), pl.program_id(1)

    @pl.when(t < nu_ref[0])
    def _():
        @pl.when(f == 0)
        def _():
            acc_ref[...] = jnp.zeros_like(acc_ref)

        x = x_ref[...].astype(BF16)
        hid = _silu(_dot(x, wg_ref[0])) * _dot(x, wu_ref[0])
        acc_ref[...] += _dot(hid.astype(BF16), wd_ref[0])

        @pl.when(f == pl.num_programs(1) - 1)
        def _():
            o_ref[...] = acc_ref[...]


def _grouped_swiglu(x_sorted, tile_expert, n_used, wg, wu, wd):
    r, d = x_sorted.shape
    dff = wg.shape[2]
    n_f = 2
    tf = dff // n_f
    fidx = lambda t, f, te, nu: jnp.where(t < nu[0], f, n_f - 1)
    return pl.pallas_call(
        _gmm_kernel, name="grouped_swiglu",
        grid_spec=pltpu.PrefetchScalarGridSpec(
            num_scalar_prefetch=2,
            grid=(r // MOE_TILE, n_f),
            in_specs=[pl.BlockSpec((MOE_TILE, d), lambda t, f, te, nu: (t, 0)),
                      pl.BlockSpec((1, d, tf), lambda t, f, te, nu: (te[t], 0, fidx(t, f, te, nu))),
                      pl.BlockSpec((1, d, tf), lambda t, f, te, nu: (te[t], 0, fidx(t, f, te, nu))),
                      pl.BlockSpec((1, tf, d), lambda t, f, te, nu: (te[t], fidx(t, f, te, nu), 0))],
            out_specs=pl.BlockSpec((MOE_TILE, d), lambda t, f, te, nu: (t, 0)),
            scratch_shapes=[pltpu.VMEM((MOE_TILE, d), F32)]),
        out_shape=jax.ShapeDtypeStruct((r, d), F32),
        compiler_params=_params("arbitrary", "arbitrary"),
    )(tile_expert, n_used, x_sorted, wg, wu, wd)


def _moe_combine_kernel(y1_ref, y2_ref, rt_ref, h_ref, gt_ref, o_ref):
    rt = rt_ref[...]
    out = rt[:, 2:3] * y1_ref[...] + rt[:, 3:4] * y2_ref[...]
    o_ref[...] = h_ref[...] + gt_ref[0] * out


def _moe_combine(y_pair, route, h, gt_tab, tile_vec, row0, n):
    t, d = h.shape
    tm = 512
    first = row0 // tm
    return pl.pallas_call(
        _moe_combine_kernel, name="moe_combine",
        grid=(n // tm,),
        in_specs=[pl.BlockSpec((tm, d), lambda i: (first + i, 0)),
                  pl.BlockSpec((tm, d), lambda i: (first + i + t // tm, 0)),
                  pl.BlockSpec((tm, LANES), lambda i: (first + i, 0)),
                  pl.BlockSpec((tm, d), lambda i: (first + i, 0)),
                  pl.BlockSpec((1, 1, d), lambda i: (tile_vec(first + i, tm), 0, 0))],
        out_specs=pl.BlockSpec((tm, d), lambda i: (i, 0)),
        out_shape=jax.ShapeDtypeStruct((n, d), F32),
        compiler_params=_params("arbitrary"),
    )(y_pair, y_pair, route, h, gt_tab)


def _moe(u, route, wg, wu, wd, h, gt_tab, tile_vec, splits):
    t, d = u.shape
    n_rows = 2 * t + N_EXPERTS * MOE_TILE
    pos, row_token, tile_expert, n_used = _route_plan(route, n_rows)
    x_sorted = _sc_gather(u, row_token)
    y_sorted = _grouped_swiglu(x_sorted, tile_expert, n_used, wg, wu, wd)
    y_pair = _sc_gather(y_sorted, pos)
    return [_moe_combine(y_pair, route, h, gt_tab, tile_vec, row0, n) for row0, n in splits]


def _final_norm_kernel(h_ref, g_ref, o_ref):
    o_ref[0] = _rms(h_ref[0], g_ref[...])


def _final_norm(h, g):
    b, l, d = h.shape
    tm = min(l, 1024)
    return pl.pallas_call(
        _final_norm_kernel, name="final_norm",
        grid=(b, l // tm),
        in_specs=[pl.BlockSpec((1, tm, d), lambda bi, i: (bi, i, 0)), pl.BlockSpec((1, d), lambda bi, i: (0, 0))],
        out_specs=pl.BlockSpec((1, tm, d), lambda bi, i: (bi, i, 0)),
        out_shape=jax.ShapeDtypeStruct((b, l, d), F32),
        compiler_params=_params("arbitrary", "arbitrary"),
    )(h, g.reshape(1, d))


def _rope_tables(length, head_dim):
    t = jnp.arange(length)
    row = (t // GRID_W).astype(F32)
    col = (t % GRID_W).astype(F32)
    n = head_dim // 4
    inv = ROPE_THETA ** (-jnp.arange(n, dtype=F32) / n)
    ang = jnp.concatenate([row[:, None] * inv, col[:, None] * inv], axis=-1)
    cos = jnp.concatenate([jnp.cos(ang)] * 2, axis=-1)
    sin = jnp.concatenate([-jnp.sin(ang), jnp.sin(ang)], axis=-1)
    reps = LANES // head_dim
    return jnp.tile(cos, (1, reps)), jnp.tile(sin, (1, reps))


def _hyena_tables(length):
    k = jnp.arange(length, dtype=jnp.int32)
    period = 4 * length
    step = 2.0 * math.pi / period
    lo = jnp.arange(64, dtype=jnp.int32)
    hi = jnp.arange(length // 64, dtype=jnp.int32)

    def tabs(x, y_hi, y_lo):
        def cs(y):
            ang = ((x[:, None] * y[None, :]) % period).astype(F32) * step
            return jnp.cos(ang)[:, :, None], jnp.sin(ang)[:, :, None]
        (ca, sa), (cb, sb) = cs(y_hi), cs(y_lo)
        cb, sb = jnp.swapaxes(cb, 1, 2), jnp.swapaxes(sb, 1, 2)
        c = (ca * cb - sa * sb).reshape(length, length)
        s = (sa * cb + ca * sb).reshape(length, length)
        return c.astype(BF16), s.astype(BF16)

    c_tab, s_tab = tabs(2 * k + 1, 64 * hi, lo)
    ci_tab, si_tab = tabs(k, 128 * hi, 2 * lo + 1)
    t = jnp.linspace(0.0, 1.0, length, dtype=F32)[:, None]
    bands = (HY_EMB - 1) // 2
    ang = (2.0 * math.pi / length) * jnp.arange(length, dtype=F32)[:, None] \
        * jnp.linspace(1e-4, bands - 1, bands, dtype=F32)
    feats = jnp.concatenate([t, jnp.cos(ang), -jnp.sin(ang),
                             jnp.zeros((length, LANES - HY_EMB), F32)], axis=-1)
    max_decay = math.log(HY_DECAY_TARGET) / HY_FAST_DECAY
    min_decay = math.log(HY_DECAY_TARGET) / HY_SLOW_DECAY
    deltas = jnp.abs(jnp.linspace(min_decay, max_decay, HY_WIDTH, dtype=F32)).reshape(1, HY_WIDTH)
    return c_tab, s_tab, ci_tab, si_tab, feats, deltas


def _block_diag_ones(n, group):
    i = jnp.arange(n) // group
    return (i[:, None] == i[None, :]).astype(BF16)


def kernel(x, c, ctx, c_ctx, w_ada, b_ada, g_mix, g_ffn, w_in, w_out, hy_conv_w, hy_conv_b, hy_w1, hy_b1, hy_w2, hy_b2, hy_w3, hy_freq, hy_skip, diff_lq1, diff_lk1, diff_lq2, diff_lk2, diff_subln, gqa_qnorm, gqa_knorm, ffn_wg, ffn_wu, ffn_wd, moe_router, moe_wg, moe_wu, moe_wd, g_final):
    b, seq, d = x.shape
    n_ctx = ctx.shape[1]
    depth = w_ada.shape[0]

    rope_tabs = _rope_tables(seq, DIFF_QK_DIM) + _rope_tables(seq, GQA_HEAD_DIM)
    hy_tabs = _hyena_tables(seq)
    hy_tabs_c = _hyena_tables(n_ctx)
    bd128 = _block_diag_ones(LANES, GQA_HEAD_DIM)
    bd256 = _block_diag_ones(DIFF_WIDTH, DIFF_V_DIM)

    rows = 16
    cc = jnp.zeros((rows, d), F32).at[:b].set(c).at[b].set(c_ctx)
    mods = _mods(cc, w_ada, b_ada)

    def mod_vecs(i):
        lat = [mods[i, :b, j * d:(j + 1) * d].reshape(b, 1, d) for j in range(6)]
        cx = [jnp.broadcast_to(mods[i, b, j * d:(j + 1) * d].reshape(1, 1, d), (b, 1, d)) for j in range(6)]
        return lat, cx

    h, hc = x, ctx
    for i in range(depth):
        last = i == depth - 1
        lam_init = 0.8 - 0.6 * math.exp(-0.3 * i)
        (sh_m, sc_m, gt_m, sh_f, sc_f, gt_f), (csh_m, csc_m, cgt_m, csh_f, csc_f, cgt_f) = mod_vecs(i)
        w_in_bf = w_in[i].astype(BF16)
        w_out_bf = w_out[i].astype(BF16)
        g_m = g_mix[i].reshape(1, d)
        g_f = g_ffn[i].reshape(1, d)
        qn = jnp.tile(gqa_qnorm[i], LANES // GQA_HEAD_DIM).reshape(1, LANES)
        kn = jnp.tile(gqa_knorm[i], LANES // GQA_HEAD_DIM).reshape(1, LANES)
        subln = jnp.tile(diff_subln[i], DIFF_HEADS).reshape(1, DIFF_WIDTH) * (1.0 - lam_init)
        lqk = jnp.stack([diff_lq1[i], diff_lk1[i], diff_lq2[i], diff_lk2[i]]).astype(F32)
        hy_p = (hy_conv_w[i], hy_conv_b[i], hy_w1[i], hy_b1[i], hy_w2[i], hy_b2[i], hy_w3[i],
                hy_freq[i], hy_skip[i])
        moe = i % 2 == 1
        router_pad = None
        if moe:
            router = jnp.zeros((d, LANES), F32).at[:, :N_EXPERTS].set(moe_router[i // 2])
            router_hi = router.astype(BF16)
            router_pad = jnp.stack([router_hi, (router - router_hi.astype(F32)).astype(BF16)])

        zhy, qd, kdt, vdx, qg, kgt, vgx = _inproj(h, sc_m, sh_m, g_m, w_in_bf, qn, kn, bd128, rope_tabs)
        czhy, cqd, ckdt, cvdx, cqg, ckgt, cvgx = _inproj(hc, csc_m, csh_m, g_m, w_in_bf, qn, kn, bd128, None)

        y_hy = _hyena(zhy, hy_p, hy_tabs)
        o_d = _diff_attention(lqk, qd, [kdt, ckdt], [vdx, cvdx], lam_init)
        o_g = _gqa_attention(qg, [kgt, ckgt], [vgx, cvgx])
        if not last:
            yc_hy = _hyena(czhy, hy_p, hy_tabs_c)
            oc_d = _diff_attention(lqk, cqd, [ckdt], [cvdx], lam_init)
            oc_g = _gqa_attention(cqg, [ckgt], [cvgx])

        if moe:
            n_c = 0 if last else b * n_ctx
            total = n_c + b * seq
            filled = None
            if not last:
                filled = _outproj(yc_hy, oc_d, oc_g, hc, cgt_m, w_out_bf, g_f, csc_f, csh_f, subln, bd256,
                                  router_pad, (total, 0, None))
            h_all, u_all, route = _outproj(y_hy, o_d, o_g, h, gt_m, w_out_bf, g_f, sc_f, sh_f, subln, bd256,
                                           router_pad, (total, n_c, filled))
            wg, wu, wd = (w[i // 2].astype(BF16) for w in (moe_wg, moe_wu, moe_wd))
            gt_tab = jnp.concatenate([gt_f, cgt_f[:1]], axis=0)
            tile_vec = lambda t, tm: jnp.where(t < n_c // tm, b, (t - n_c // tm) // (seq // tm))
            parts = _moe(u_all, route, wg, wu, wd, h_all, gt_tab, tile_vec,
                         [(n_c, b * seq)] + ([] if last else [(0, n_c)]))
            h = parts[0].reshape(b, seq, d)
            if not last:
                hc = parts[1].reshape(b, n_ctx, d)
        else:
            wg, wu, wd = (w[i // 2].astype(BF16) for w in (ffn_wg, ffn_wu, ffn_wd))
            mixed = _outproj(y_hy, o_d, o_g, h, gt_m, w_out_bf, g_f, sc_f, sh_f, subln, bd256)
            h = _ffn(mixed[1], wg, wu, wd, mixed[0], gt_f)
            if not last:
                cmixed = _outproj(yc_hy, oc_d, oc_g, hc, cgt_m, w_out_bf, g_f, csc_f, csh_f, subln, bd256)
                hc = _ffn(cmixed[1], wg, wu, wd, cmixed[0], cgt_f)
    return _final_norm(h, g_final)
```

```python
import functools
import math

import jax
import jax.numpy as jnp
from jax import lax
from jax.experimental import pallas as pl
from jax.experimental.pallas import tpu as pltpu
from jax.experimental.pallas import tpu_sc as plsc

F32 = jnp.float32
BF16 = jnp.bfloat16
HIGHEST = lax.Precision.HIGHEST

D_MODEL = 1024
GRID_W = 64
ROPE_THETA = 10000.0
NORM_EPS = 1e-6

HY_WIDTH = 256
HY_EMB = 33
HY_FAST_DECAY = 0.3
HY_SLOW_DECAY = 1.5
HY_DECAY_TARGET = 1e-2

DIFF_HEADS = 4
DIFF_QK_DIM = 32
DIFF_V_DIM = 64
DIFF_WIDTH = 256
GQA_HEAD_DIM = 64
GQA_WIDTH = 512
GQA_KV_WIDTH = 128
GQA_REP = 4

OFF_DQ = 3 * HY_WIDTH
OFF_DK = OFF_DQ + DIFF_WIDTH
OFF_DV = OFF_DK + DIFF_WIDTH
OFF_GQ = OFF_DV + DIFF_WIDTH
OFF_GK = OFF_GQ + GQA_WIDTH
OFF_GV = OFF_GK + GQA_KV_WIDTH
IN_COLS = OFF_GV + GQA_KV_WIDTH

N_EXPERTS = 8
LOG2_E = math.log2(math.e)
LANES = 128
VMEM_LIMIT = 56 * 1024 * 1024


def _params(*sem):
    return pltpu.CompilerParams(dimension_semantics=sem, vmem_limit_bytes=VMEM_LIMIT)


def _dot(a, b):
    return jnp.dot(a, b, preferred_element_type=F32)


def _dot_hi(a, b):
    return jnp.dot(a, b, preferred_element_type=F32, precision=HIGHEST)


def _rms(x, g):
    ms = jnp.mean(x * x, axis=-1, keepdims=True)
    return x * lax.rsqrt(ms + NORM_EPS) * g


def _silu(x):
    return x * jax.nn.sigmoid(x)


def _group_mean_sq(x, ones_bd, width):
    return _dot((x * x).astype(BF16), ones_bd) * (1.0 / width)


def _mods_kernel(c_ref, w_ref, b_ref, o_ref):
    o_ref[0] = _dot_hi(_silu(c_ref[...]), w_ref[0]) + b_ref[0]


def _mods(cc, w_ada, b_ada):
    depth, d, n = w_ada.shape
    tn = 1536
    return pl.pallas_call(
        _mods_kernel, name="mods",
        grid=(depth, n // tn),
        in_specs=[pl.BlockSpec(cc.shape, lambda i, j: (0, 0)),
                  pl.BlockSpec((1, d, tn), lambda i, j: (i, 0, j)),
                  pl.BlockSpec((1, 1, tn), lambda i, j: (i, 0, j))],
        out_specs=pl.BlockSpec((1, cc.shape[0], tn), lambda i, j: (i, 0, j)),
        out_shape=jax.ShapeDtypeStruct((depth, cc.shape[0], n), F32),
        compiler_params=_params("arbitrary", "arbitrary"),
    )(cc, w_ada, b_ada.reshape(depth, 1, n))


def _rope128(x, cos, sin_signed, half):
    lane = lax.broadcasted_iota(jnp.int32, x.shape, 1)
    first = (lane & (2 * half - 1)) < half
    swapped = jnp.where(first, pltpu.roll(x, LANES - half, 1), pltpu.roll(x, half, 1))
    return x * cos + swapped * sin_signed


def _inproj_kernel(*refs, rope):
    h_ref, sc_ref, sh_ref, g_ref, w_ref, qn_ref, kn_ref, bd_ref = refs[:8]
    if rope:
        cd_ref, sd_ref, cg_ref, sg_ref = refs[8:12]
    zhy_ref, qd_ref, kd_ref, vd_ref, qg_ref, kg_ref, vg_ref = refs[-7:]
    u = _rms(h_ref[0], g_ref[...]) * (1.0 + sc_ref[0]) + sh_ref[0]
    z = _dot(u.astype(BF16), w_ref[...])
    zhy_ref[0] = z[:, :OFF_DQ]

    def piece(off, j):
        return z[:, off + LANES * j: off + LANES * (j + 1)]

    for j in range(DIFF_WIDTH // LANES):
        q, k = piece(OFF_DQ, j), piece(OFF_DK, j)
        if rope:
            q = _rope128(q, cd_ref[...], sd_ref[...], DIFF_QK_DIM // 2)
            k = _rope128(k, cd_ref[...], sd_ref[...], DIFF_QK_DIM // 2)
        qd_ref[0, :, LANES * j: LANES * (j + 1)] = (q * (LOG2_E * DIFF_QK_DIM ** -0.5)).astype(BF16)
        kd_ref[0, 0, LANES * j: LANES * (j + 1), :] = k.T.astype(BF16)
    vd_ref[0, 0] = z[:, OFF_DV:OFF_GQ].astype(BF16)

    def gqa_piece(x, gain):
        ms = _group_mean_sq(x, bd_ref[...], GQA_HEAD_DIM)
        x = x * lax.rsqrt(ms + NORM_EPS) * gain
        if rope:
            x = _rope128(x, cg_ref[...], sg_ref[...], GQA_HEAD_DIM // 2)
        return x

    for j in range(GQA_WIDTH // LANES):
        q = gqa_piece(piece(OFF_GQ, j), qn_ref[...])
        qg_ref[0, :, LANES * j: LANES * (j + 1)] = (q * (LOG2_E * GQA_HEAD_DIM ** -0.5)).astype(BF16)
    kt = gqa_piece(piece(OFF_GK, 0), kn_ref[...]).T.astype(BF16)
    v = z[:, OFF_GV:]
    v_swapped = pltpu.roll(v, GQA_HEAD_DIM, 1)
    low = lax.broadcasted_iota(jnp.int32, v.shape, 1) < GQA_HEAD_DIM
    v_rep = (jnp.where(low, v, v_swapped), jnp.where(low, v_swapped, v))
    for g in range(GQA_KV_WIDTH // GQA_HEAD_DIM):
        for r in range(GQA_REP):
            kg_ref[0, g, GQA_HEAD_DIM * r: GQA_HEAD_DIM * (r + 1), :] = kt[GQA_HEAD_DIM * g: GQA_HEAD_DIM * (g + 1), :]
        for half in range(GQA_REP * GQA_HEAD_DIM // LANES):
            vg_ref[0, g, :, LANES * half: LANES * (half + 1)] = v_rep[g].astype(BF16)


def _inproj(h, sc, sh, g, w_bf, qn, kn, bd, rope_tabs):
    b, l, d = h.shape
    tm = min(l, 512)
    row = lambda bi, i: (bi, i, 0)
    vec = lambda bi, i: (bi, 0, 0)
    const = lambda bi, i: (0, 0)
    in_specs = [pl.BlockSpec((1, tm, d), row), pl.BlockSpec((1, 1, d), vec), pl.BlockSpec((1, 1, d), vec),
                pl.BlockSpec((1, d), const), pl.BlockSpec((d, IN_COLS), const),
                pl.BlockSpec((1, LANES), const), pl.BlockSpec((1, LANES), const),
                pl.BlockSpec((LANES, LANES), const)]
    args = [h, sc, sh, g, w_bf, qn, kn, bd]
    if rope_tabs is not None:
        in_specs += [pl.BlockSpec((tm, LANES), lambda bi, i: (i, 0))] * 4
        args += list(rope_tabs)
    aw = ATTN_WIDTH
    kv_groups = GQA_KV_WIDTH // GQA_HEAD_DIM
    rows_spec = lambda w: pl.BlockSpec((1, tm, w), row)
    kt_spec = lambda g: pl.BlockSpec((1, g, aw, tm), lambda bi, i: (bi, 0, 0, i))
    v_spec = lambda g: pl.BlockSpec((1, g, tm, aw), lambda bi, i: (bi, 0, i, 0))
    sds = jax.ShapeDtypeStruct
    return pl.pallas_call(
        functools.partial(_inproj_kernel, rope=rope_tabs is not None), name="inproj",
        grid=(b, l // tm),
        in_specs=in_specs,
        out_specs=[rows_spec(OFF_DQ), rows_spec(DIFF_WIDTH), kt_spec(1), v_spec(1),
                   rows_spec(GQA_WIDTH), kt_spec(kv_groups), v_spec(kv_groups)],
        out_shape=[sds((b, l, OFF_DQ), F32), sds((b, l, DIFF_WIDTH), BF16), sds((b, 1, aw, l), BF16),
                   sds((b, 1, l, aw), BF16), sds((b, l, GQA_WIDTH), BF16), sds((b, kv_groups, aw, l), BF16),
                   sds((b, kv_groups, l, aw), BF16)],
        compiler_params=_params("arbitrary", "arbitrary"),
    )(*args)


def _hy_filter_kernel(f_ref, w1_ref, b1_ref, w2_ref, b2_ref, w3_ref, fr_ref, dl_ref, hs_ref, hd_ref):
    f = f_ref[...]
    fr = fr_ref[...]
    a = jnp.sin(fr * (_dot_hi(f, w1_ref[...]) + b1_ref[...]))
    a = jnp.sin(fr * (_dot_hi(a, w2_ref[...]) + b2_ref[...]))
    hf = _dot_hi(a, w3_ref[...])
    decay = jnp.exp(-f[:, 0:1] * dl_ref[...])
    h_fwd = hf[:, :HY_WIDTH] * decay
    h_bwd = hf[:, HY_WIDTH:] * decay
    row = lax.broadcasted_iota(jnp.int32, h_bwd.shape, 0) + pl.program_id(0) * f.shape[0]
    h_bwd = jnp.where(row == 0, 0.0, h_bwd)
    hs_ref[...] = (h_fwd + h_bwd).astype(BF16)
    hd_ref[...] = (h_bwd - h_fwd).astype(BF16)


def _hy_spectrum_kernel(c_ref, s_ref, hs_ref, hd_ref, gre_ref, gim_ref, *, scale):
    gre_ref[...] = _dot(c_ref[...], hs_ref[...]) * scale
    gim_ref[...] = _dot(s_ref[...], hd_ref[...]) * scale


def _hy_pre_kernel(z_ref, zp_ref, zn_ref, cw_ref, cb_ref, wb_ref, wf_ref, x0_ref, *, nt):
    ti = pl.program_id(1)
    z = z_ref[0]
    tl = z.shape[0]
    row = lax.broadcasted_iota(jnp.int32, z.shape, 0)
    prev_row = jnp.where(ti > 0, zp_ref[0, 7:8, :], 0.0)
    next_row = jnp.where(ti < nt - 1, zn_ref[0, 0:1, :], 0.0)
    z_prev = jnp.where(row == 0, prev_row, pltpu.roll(z, 1, 0))
    z_next = jnp.where(row == tl - 1, next_row, pltpu.roll(z, tl - 1, 0))
    cw = cw_ref[...]
    y = cb_ref[...] + z_prev * cw[0:1] + z * cw[1:2] + z_next * cw[2:3]
    x0, x1, v = y[:, :HY_WIDTH], y[:, HY_WIDTH:2 * HY_WIDTH], y[:, 2 * HY_WIDTH:]
    w = v * x1
    wb_ref[...] = w.astype(BF16)
    wf_ref[0] = w
    x0_ref[0] = x0


def _hy_fwd_kernel(c_ref, s_ref, x_ref, gre_ref, gim_ref, yre_ref, yim_ref, *, reps):
    x = x_ref[...]
    a = _dot(c_ref[...], x)
    b = _dot(s_ref[...], x)
    gre = jnp.concatenate([gre_ref[...]] * reps, axis=1)
    gim = jnp.concatenate([gim_ref[...]] * reps, axis=1)
    yre_ref[...] = (a * gre + b * gim).astype(BF16)
    yim_ref[...] = (a * gim - b * gre).astype(BF16)


def _hy_inv_kernel(ci_ref, si_ref, yre_ref, yim_ref, wf_ref, x0_ref, skip_ref, o_ref, *, reps):
    y = _dot(ci_ref[...], yre_ref[...]) - _dot(si_ref[...], yim_ref[...])
    for r in range(reps):
        yr = y[:, r * HY_WIDTH:(r + 1) * HY_WIDTH]
        o_ref[r] = ((yr + wf_ref[r] * skip_ref[...]) * x0_ref[r]).astype(BF16)


def _hyena(zhy, p, tabs):
    conv_w, conv_b, w1, b1, w2, b2, w3, freq, skip = p
    c_tab, s_tab, ci_tab, si_tab, feats, deltas = tabs
    b, l, _ = zhy.shape
    c = HY_WIDTH
    tl = min(l, 512)
    nt = l // tl
    hid = w2.shape[0]
    const1 = lambda i: (0, 0)
    w1p = jnp.zeros((LANES, hid), F32).at[:HY_EMB].set(w1)
    hs, hd = pl.pallas_call(
        _hy_filter_kernel, name="hy_filter",
        grid=(nt,),
        in_specs=[pl.BlockSpec((tl, LANES), lambda i: (i, 0)),
                  pl.BlockSpec((LANES, hid), const1), pl.BlockSpec((1, hid), const1),
                  pl.BlockSpec((hid, hid), const1), pl.BlockSpec((1, hid), const1),
                  pl.BlockSpec((hid, 2 * c), const1), pl.BlockSpec((1, hid), const1),
                  pl.BlockSpec((1, c), const1)],
        out_specs=[pl.BlockSpec((tl, c), lambda i: (i, 0))] * 2,
        out_shape=[jax.ShapeDtypeStruct((l, c), BF16)] * 2,
        compiler_params=_params("arbitrary"),
    )(feats, w1p, b1.reshape(1, hid), w2, b2.reshape(1, hid), w3, freq.reshape(1, hid), deltas)

    gre, gim = pl.pallas_call(
        functools.partial(_hy_spectrum_kernel, scale=1.0 / l), name="hy_spectrum",
        grid=(nt,),
        in_specs=[pl.BlockSpec((tl, l), lambda i: (i, 0)), pl.BlockSpec((tl, l), lambda i: (i, 0)),
                  pl.BlockSpec((l, c), const1), pl.BlockSpec((l, c), const1)],
        out_specs=[pl.BlockSpec((tl, c), lambda i: (i, 0))] * 2,
        out_shape=[jax.ShapeDtypeStruct((l, c), F32)] * 2,
        compiler_params=_params("arbitrary"),
    )(c_tab, s_tab, hs, hd)

    halo = 8
    wb, wf, x0 = pl.pallas_call(
        functools.partial(_hy_pre_kernel, nt=nt), name="hy_pre",
        grid=(b, nt),
        in_specs=[pl.BlockSpec((1, tl, 3 * c), lambda bi, i: (bi, i, 0)),
                  pl.BlockSpec((1, halo, 3 * c), lambda bi, i: (bi, jnp.maximum(i * (tl // halo) - 1, 0), 0)),
                  pl.BlockSpec((1, halo, 3 * c),
                               lambda bi, i: (bi, jnp.minimum((i + 1) * (tl // halo), l // halo - 1), 0)),
                  pl.BlockSpec((3, 3 * c), lambda bi, i: (0, 0)),
                  pl.BlockSpec((1, 3 * c), lambda bi, i: (0, 0))],
        out_specs=[pl.BlockSpec((tl, c), lambda bi, i: (i, bi)),
                   pl.BlockSpec((1, tl, c), lambda bi, i: (bi, i, 0)),
                   pl.BlockSpec((1, tl, c), lambda bi, i: (bi, i, 0))],
        out_shape=[jax.ShapeDtypeStruct((l, b * c), BF16),
                   jax.ShapeDtypeStruct((b, l, c), F32),
                   jax.ShapeDtypeStruct((b, l, c), F32)],
        compiler_params=_params("arbitrary", "arbitrary"),
    )(zhy, zhy, zhy, conv_w, conv_b.reshape(1, 3 * c))

    reps = 2
    tn = reps * c
    nj = b * c // tn
    yre, yim = pl.pallas_call(
        functools.partial(_hy_fwd_kernel, reps=reps), name="hy_fwd",
        grid=(nj, nt),
        in_specs=[pl.BlockSpec((tl, l), lambda j, i: (i, 0)), pl.BlockSpec((tl, l), lambda j, i: (i, 0)),
                  pl.BlockSpec((l, tn), lambda j, i: (0, j)),
                  pl.BlockSpec((tl, c), lambda j, i: (i, 0)), pl.BlockSpec((tl, c), lambda j, i: (i, 0))],
        out_specs=[pl.BlockSpec((tl, tn), lambda j, i: (i, j))] * 2,
        out_shape=[jax.ShapeDtypeStruct((l, b * c), BF16)] * 2,
        compiler_params=_params("arbitrary", "arbitrary"),
    )(c_tab, s_tab, wb, gre, gim)

    return pl.pallas_call(
        functools.partial(_hy_inv_kernel, reps=reps), name="hy_inv",
        grid=(nj, nt),
        in_specs=[pl.BlockSpec((tl, l), lambda j, i: (i, 0)), pl.BlockSpec((tl, l), lambda j, i: (i, 0)),
                  pl.BlockSpec((l, tn), lambda j, i: (0, j)), pl.BlockSpec((l, tn), lambda j, i: (0, j)),
                  pl.BlockSpec((reps, tl, c), lambda j, i: (j, i, 0)),
                  pl.BlockSpec((reps, tl, c), lambda j, i: (j, i, 0)),
                  pl.BlockSpec((1, c), lambda j, i: (0, 0))],
        out_specs=pl.BlockSpec((reps, tl, c), lambda j, i: (j, i, 0)),
        out_shape=jax.ShapeDtypeStruct((b, l, c), BF16),
        compiler_params=_params("arbitrary", "arbitrary"),
    )(ci_tab, si_tab, yre, yim, wf, x0, skip.reshape(1, c))


def _lane_mask(width, shift, idx):
    lane = lax.broadcasted_iota(jnp.int32, (1, width), 1)
    return jnp.where((lane >> shift) == idx, 1.0, 0.0).astype(BF16)


ATTN_WIDTH = 256
ATTN_CHUNK = 512


def _key_chunks(lk):
    return [(c0, min(ATTN_CHUNK, lk - c0)) for c0 in range(0, lk, ATTN_CHUNK)]


ATTN_STACK = 4
ATTN_TQ = 128
ATTN_STEP_ROWS = 512


def _qk_stage(q, shift, first, kt_refs, qs_scr, s_scr):
    tq = q.shape[0]
    for i in range(ATTN_STACK):
        qs_scr[i * tq:(i + 1) * tq, :] = q * _lane_mask(ATTN_WIDTH, shift, first + i)
    qs = qs_scr[...]
    m = jnp.full((ATTN_STACK * tq, LANES), -jnp.inf, F32)
    base = 0
    for kt_ref in kt_refs:
        for c0, ck in _key_chunks(kt_ref.shape[-1]):
            s = _dot(qs, kt_ref[:, c0:c0 + ck])
            s_scr[:, base + c0:base + c0 + ck] = s
            for j in range(ck // LANES):
                m = jnp.maximum(m, s[:, j * LANES:(j + 1) * LANES])
        base += kt_ref.shape[-1]
    return jnp.max(m, axis=-1, keepdims=True)


def _exp_stage(s_scr, mx, p_scr):
    l = jnp.zeros((s_scr.shape[0], LANES), F32)
    for c0, ck in _key_chunks(s_scr.shape[-1]):
        e = jnp.exp2(s_scr[:, c0:c0 + ck] - mx)
        p_scr[:, c0:c0 + ck] = e.astype(BF16)
        for j in range(ck // LANES):
            l = l + e[:, j * LANES:(j + 1) * LANES]
    return 1.0 / jnp.sum(l, axis=-1, keepdims=True)


def _attention_passes(passes, consume, kt_refs, v_refs, scratch):
    qs, ss, ps = scratch[0:2], scratch[2:4], scratch[4:6]
    tq = passes[0][0].shape[0]

    def qk(i):
        q, shift, first, g = passes[i]
        return _qk_stage(q, shift, first, [kt.at[0, g] for kt in kt_refs], qs[i % 2], ss[i % 2])

    mx = qk(0)
    for i in range(len(passes)):
        mx_next = qk(i + 1) if i + 1 < len(passes) else None
        inv = _exp_stage(ss[i % 2], mx, ps[i % 2])
        pv, base = None, 0
        for v_ref in v_refs:
            lk = v_ref.shape[2]
            part = _dot(ps[i % 2][:, base:base + lk], v_ref[0, passes[i][3]])
            pv = part if pv is None else pv + part
            base += lk
        pv = pv * inv
        consume(i, [pv[r * tq:(r + 1) * tq] for r in range(ATTN_STACK)])
        mx = mx_next


def _diff_kernel(lqk_ref, q_ref, *refs, n_kv, lam_init):
    kt_refs, v_refs, o_ref, scratch = refs[:n_kv], refs[n_kv:2 * n_kv], refs[2 * n_kv], refs[2 * n_kv + 1:]
    x = lqk_ref[...]
    lam = (jnp.exp(jnp.sum(x[0:1] * x[1:2], axis=-1, keepdims=True))
           - jnp.exp(jnp.sum(x[2:3] * x[3:4], axis=-1, keepdims=True)) + lam_init)
    tq = min(ATTN_TQ, q_ref.shape[1])
    group = lax.broadcasted_iota(jnp.int32, (tq, ATTN_WIDTH), 1) >> 6
    halves = 2 * DIFF_HEADS // ATTN_STACK
    passes = []
    for t in range(q_ref.shape[1] // tq):
        q = q_ref[0, t * tq:(t + 1) * tq, :]
        passes += [(q, 5, half * ATTN_STACK, 0) for half in range(halves)]
    acc = {}

    def consume(i, pvs):
        t, half = divmod(i, halves)
        o1, o2 = acc.pop(t, (jnp.zeros(group.shape, F32),) * 2)
        for r, pv in enumerate(pvs):
            head = (half * ATTN_STACK + r) >> 1
            if r & 1 == 0:
                o1 = jnp.where(group == head, pv, o1)
            else:
                o2 = jnp.where(group == head, pv, o2)
        if half + 1 < halves:
            acc[t] = (o1, o2)
            return
        o_ref[0, t * tq:(t + 1) * tq, :] = (o1 - lam * o2).astype(BF16)

    _attention_passes(passes, consume, kt_refs, v_refs, scratch)


def _gqa_kernel(q_ref, *refs, n_kv):
    kt_refs, v_refs, o_ref, scratch = refs[:n_kv], refs[n_kv:2 * n_kv], refs[2 * n_kv], refs[2 * n_kv + 1:]
    tq = min(ATTN_TQ, q_ref.shape[1])
    group = lax.broadcasted_iota(jnp.int32, (tq, ATTN_WIDTH), 1) >> 6
    groups = q_ref.shape[2] // ATTN_WIDTH
    passes = []
    for t in range(q_ref.shape[1] // tq):
        for g in range(groups):
            passes.append((q_ref[0, t * tq:(t + 1) * tq, g * ATTN_WIDTH:(g + 1) * ATTN_WIDTH], 6, 0, g))

    def consume(i, pvs):
        t, g = divmod(i, groups)
        o = jnp.zeros(group.shape, F32)
        for r, pv in enumerate(pvs):
            o = jnp.where(group == r, pv, o)
        o_ref[0, t * tq:(t + 1) * tq, g * ATTN_WIDTH:(g + 1) * ATTN_WIDTH] = o.astype(BF16)

    _attention_passes(passes, consume, kt_refs, v_refs, scratch)


def _attention(kernel_fn, name, q, kts, vs, extra=()):
    b, lq, width = q.shape
    groups, w = kts[0].shape[1:3]
    lk = sum(kt.shape[3] for kt in kts)
    rows = min(lq, ATTN_STEP_ROWS)
    pass_rows = ATTN_STACK * min(ATTN_TQ, rows)
    kv_spec = lambda a: pl.BlockSpec((1,) + a.shape[1:], lambda bi, i: (bi, 0, 0, 0), pipeline_mode=pl.Buffered(1))
    return pl.pallas_call(
        functools.partial(kernel_fn, n_kv=len(kts)), name=name,
        grid=(b, lq // rows),
        in_specs=[pl.BlockSpec(a.shape, lambda bi, i: (0, 0)) for a in extra]
        + [pl.BlockSpec((1, rows, width), lambda bi, i: (bi, i, 0))]
        + [kv_spec(a) for a in kts] + [kv_spec(a) for a in vs],
        out_specs=pl.BlockSpec((1, rows, width), lambda bi, i: (bi, i, 0)),
        out_shape=jax.ShapeDtypeStruct((b, lq, width), BF16),
        scratch_shapes=[pltpu.VMEM((pass_rows, w), BF16)] * 2 + [pltpu.VMEM((pass_rows, lk), F32)] * 2
        + [pltpu.VMEM((pass_rows, lk), BF16)] * 2,
        compiler_params=_params("arbitrary", "arbitrary"),
    )(*extra, q, *kts, *vs)


def _diff_attention(lqk, q, kts, vs, lam_init):
    return _attention(functools.partial(_diff_kernel, lam_init=lam_init), "diff_attn", q, kts, vs, (lqk,))


def _gqa_attention(q, kts, vs):
    return _attention(_gqa_kernel, "gqa_attn", q, kts, vs)


def _top2_route(logits):
    lane = lax.broadcasted_iota(jnp.int32, logits.shape, 1).astype(F32)
    neg = jnp.float32(-jnp.inf)
    lg = jnp.where(lane < N_EXPERTS, logits, neg)
    m1 = jnp.max(lg, axis=-1, keepdims=True)
    i1 = jnp.min(jnp.where(lg == m1, lane, float(LANES)), axis=-1, keepdims=True)
    lg2 = jnp.where(lane == i1, neg, lg)
    m2 = jnp.max(lg2, axis=-1, keepdims=True)
    i2 = jnp.min(jnp.where(lg2 == m2, lane, float(LANES)), axis=-1, keepdims=True)
    e2 = jnp.exp(m2 - m1)
    w1 = 1.0 / (1.0 + e2)
    return (jnp.where(lane == 0, i1, 0.0) + jnp.where(lane == 1, i2, 0.0)
            + jnp.where(lane == 2, w1, 0.0) + jnp.where(lane == 3, e2 * w1, 0.0))


def _outproj_kernel(*refs, moe):
    yhy_ref, od_ref, og_ref, h_ref, gt_ref, w_ref, gf_ref, sc_ref, sh_ref, sg_ref, bd_ref = refs[:11]
    if moe:
        rt_ref = refs[11]
        hn_ref, u_ref, gates_ref = refs[-3:]
    else:
        hn_ref, u_ref = refs[11:]
    od = od_ref[0].astype(F32)
    od = od * lax.rsqrt(_group_mean_sq(od, bd_ref[...], DIFF_V_DIM) + NORM_EPS) * sg_ref[...]
    y = (_dot(yhy_ref[0], w_ref[0:HY_WIDTH, :])
         + _dot(od.astype(BF16), w_ref[HY_WIDTH:HY_WIDTH + DIFF_WIDTH, :])
         + _dot(og_ref[0], w_ref[HY_WIDTH + DIFF_WIDTH:, :]))
    hn = h_ref[0] + gt_ref[0] * y
    hn_ref[...] = hn.reshape(hn_ref.shape)
    u = _rms(hn, gf_ref[...]) * (1.0 + sc_ref[0]) + sh_ref[0]
    u_ref[...] = u.astype(u_ref.dtype).reshape(u_ref.shape)
    if moe:
        u_hi = u.astype(BF16)
        u_lo = (u - u_hi.astype(F32)).astype(BF16)
        logits = _dot(u_hi, rt_ref[0]) + (_dot(u_lo, rt_ref[0]) + _dot(u_hi, rt_ref[1]))
        gates_ref[...] = _top2_route(logits)


def _outproj(yhy, od, og, h, gt, w_bf, gf, sc, sh, sub_g, bd256, router_pad=None, into=None):
    b, l, d = h.shape
    tm = min(l, 512)
    row = lambda bi, i: (bi, i, 0)
    vec = lambda bi, i: (bi, 0, 0)
    const = lambda bi, i: (0, 0)
    moe = router_pad is not None
    in_specs = [pl.BlockSpec((1, tm, HY_WIDTH), row), pl.BlockSpec((1, tm, DIFF_WIDTH), row),
                pl.BlockSpec((1, tm, GQA_WIDTH), row), pl.BlockSpec((1, tm, d), row),
                pl.BlockSpec((1, 1, d), vec), pl.BlockSpec((d, d), const), pl.BlockSpec((1, d), const),
                pl.BlockSpec((1, 1, d), vec), pl.BlockSpec((1, 1, d), vec),
                pl.BlockSpec((1, DIFF_WIDTH), const), pl.BlockSpec((DIFF_WIDTH, DIFF_WIDTH), const)]
    args = [yhy, od, og, h, gt, w_bf, gf, sc, sh, sub_g, bd256]
    out_specs = [pl.BlockSpec((1, tm, d), row), pl.BlockSpec((1, tm, d), row)]
    out_shape = [jax.ShapeDtypeStruct((b, l, d), F32), jax.ShapeDtypeStruct((b, l, d), F32 if moe else BF16)]
    aliases = {}
    if moe:
        total_rows, row_offset, filled = into
        in_specs.append(pl.BlockSpec((2, d, LANES), lambda bi, i: (0, 0, 0)))
        args.append(router_pad)
        flat_row = lambda bi, i: (row_offset // tm + bi * (l // tm) + i, 0)
        widths = (d, d, LANES)
        out_specs = [pl.BlockSpec((tm, w), flat_row) for w in widths]
        out_shape = [jax.ShapeDtypeStruct((total_rows, w), F32) for w in widths]
        if filled is not None:
            aliases = {len(args) + j: j for j in range(len(filled))}
            in_specs += [pl.BlockSpec(memory_space=pl.ANY)] * len(filled)
            args += list(filled)
    return pl.pallas_call(
        functools.partial(_outproj_kernel, moe=moe), name="outproj",
        grid=(b, l // tm), in_specs=in_specs, out_specs=out_specs, out_shape=out_shape,
        input_output_aliases=aliases,
        compiler_params=_params("arbitrary", "arbitrary"),
    )(*args)


def _ffn_kernel(u_ref, wg_ref, wu_ref, wd_ref, h_ref, gt_ref, o_ref, acc_ref):
    f = pl.program_id(2)

    @pl.when(f == 0)
    def _():
        acc_ref[...] = jnp.zeros_like(acc_ref)

    u = u_ref[0]
    hid = _silu(_dot(u, wg_ref[...])) * _dot(u, wu_ref[...])
    acc_ref[...] += _dot(hid.astype(BF16), wd_ref[...])

    @pl.when(f == pl.num_programs(2) - 1)
    def _():
        o_ref[0] = h_ref[0] + gt_ref[0] * acc_ref[...]


def _ffn(u, wg, wu, wd, h, gt):
    b, l, d = h.shape
    dff = wg.shape[1]
    tm = min(l, 512)
    tf = dff // 2
    row = lambda bi, i, f: (bi, i, 0)
    return pl.pallas_call(
        _ffn_kernel, name="ffn",
        grid=(b, l // tm, dff // tf),
        in_specs=[pl.BlockSpec((1, tm, d), row),
                  pl.BlockSpec((d, tf), lambda bi, i, f: (0, f)), pl.BlockSpec((d, tf), lambda bi, i, f: (0, f)),
                  pl.BlockSpec((tf, d), lambda bi, i, f: (f, 0)),
                  pl.BlockSpec((1, tm, d), row), pl.BlockSpec((1, 1, d), lambda bi, i, f: (bi, 0, 0))],
        out_specs=pl.BlockSpec((1, tm, d), row),
        out_shape=jax.ShapeDtypeStruct((b, l, d), F32),
        scratch_shapes=[pltpu.VMEM((tm, d), F32)],
        compiler_params=_params("arbitrary", "arbitrary", "arbitrary"),
    )(u, wg, wu, wd, h, gt)


MOE_TILE = 512
SC_CHUNK = 64


def _sc_gather(table, idx):
    info = plsc.get_sparse_core_info()
    nc, ns = info.num_cores, info.num_subcores
    n, d = idx.shape[0], table.shape[1]
    per_worker = n // (nc * ns)
    assert per_worker * nc * ns == n and per_worker % SC_CHUNK == 0
    mesh = plsc.VectorSubcoreMesh(core_axis_name="c", subcore_axis_name="s")

    @functools.partial(
        pl.kernel, mesh=mesh,
        out_type=jax.ShapeDtypeStruct((n, d), table.dtype),
        scratch_types=[pltpu.VMEM((SC_CHUNK,), jnp.int32), pltpu.VMEM((SC_CHUNK, d), table.dtype),
                       pltpu.SemaphoreType.DMA],
    )
    def gather_kernel(table_hbm, idx_hbm, out_hbm, idx_v, rows_v, sem):
        base = (lax.axis_index("s") * nc + lax.axis_index("c")) * per_worker

        @pl.loop(0, per_worker // SC_CHUNK)
        def _(j):
            off = pl.multiple_of(base + j * SC_CHUNK, 8)
            pltpu.sync_copy(idx_hbm.at[pl.ds(off, SC_CHUNK)], idx_v)
            pltpu.async_copy(table_hbm.at[idx_v], rows_v, sem).wait()
            pltpu.sync_copy(rows_v, out_hbm.at[pl.ds(off, SC_CHUNK)])

    return gather_kernel(table, idx)


def _route_plan(route, n_rows):
    t = route.shape[0]
    e = route[:, :2].astype(jnp.int32).T.reshape(2 * t)
    onehot = (e[:, None] == jnp.arange(N_EXPERTS, dtype=jnp.int32)[None, :]).astype(jnp.int32)
    csum = jnp.cumsum(onehot, axis=0)
    counts = csum[-1]
    rank = jnp.take_along_axis(csum, e[:, None], axis=1)[:, 0] - 1
    padded = (counts + MOE_TILE - 1) // MOE_TILE * MOE_TILE
    ends = jnp.cumsum(padded)
    pos = (ends - padded)[e] + rank
    row_token = jnp.zeros((n_rows,), jnp.int32).at[pos].set(jnp.arange(2 * t, dtype=jnp.int32) % t)
    tile_start = jnp.arange(n_rows // MOE_TILE, dtype=jnp.int32) * MOE_TILE
    tile_expert = jnp.minimum(jnp.sum((tile_start[:, None] >= ends[None, :]).astype(jnp.int32), axis=1),
                              N_EXPERTS - 1)
    n_used = (ends[-1] // MOE_TILE).reshape(1)
    return pos, row_token, tile_expert, n_used


def _gmm_kernel(te_ref, nu_ref, x_ref, wg_ref, wu_ref, wd_ref, o_ref, acc_ref):
    t, f = pl.program_id(0), pl.program_id(1)

    @pl.when(t < nu_ref[0])
    def _():
        @pl.when(f == 0)
        def _():
            acc_ref[...] = jnp.zeros_like(acc_ref)

        x = x_ref[...].astype(BF16)
        hid = _silu(_dot(x, wg_ref[0])) * _dot(x, wu_ref[0])
        acc_ref[...] += _dot(hid.astype(BF16), wd_ref[0])

        @pl.when(f == pl.num_programs(1) - 1)
        def _():
            o_ref[...] = acc_ref[...]


def _grouped_swiglu(x_sorted, tile_expert, n_used, wg, wu, wd):
    r, d = x_sorted.shape
    dff = wg.shape[2]
    n_f = 2
    tf = dff // n_f
    fidx = lambda t, f, te, nu: jnp.where(t < nu[0], f, n_f - 1)
    return pl.pallas_call(
        _gmm_kernel, name="grouped_swiglu",
        grid_spec=pltpu.PrefetchScalarGridSpec(
            num_scalar_prefetch=2,
            grid=(r // MOE_TILE, n_f),
            in_specs=[pl.BlockSpec((MOE_TILE, d), lambda t, f, te, nu: (t, 0)),
                      pl.BlockSpec((1, d, tf), lambda t, f, te, nu: (te[t], 0, fidx(t, f, te, nu))),
                      pl.BlockSpec((1, d, tf), lambda t, f, te, nu: (te[t], 0, fidx(t, f, te, nu))),
                      pl.BlockSpec((1, tf, d), lambda t, f, te, nu: (te[t], fidx(t, f, te, nu), 0))],
            out_specs=pl.BlockSpec((MOE_TILE, d), lambda t, f, te, nu: (t, 0)),
            scratch_shapes=[pltpu.VMEM((MOE_TILE, d), F32)]),
        out_shape=jax.ShapeDtypeStruct((r, d), F32),
        compiler_params=_params("arbitrary", "arbitrary"),
    )(tile_expert, n_used, x_sorted, wg, wu, wd)


def _moe_combine_kernel(y1_ref, y2_ref, rt_ref, h_ref, gt_ref, o_ref):
    rt = rt_ref[...]
    out = rt[:, 2:3] * y1_ref[...] + rt[:, 3:4] * y2_ref[...]
    o_ref[...] = h_ref[...] + gt_ref[0] * out


def _moe_combine(y_pair, route, h, gt_tab, tile_vec, row0, n):
    t, d = h.shape
    tm = 512
    first = row0 // tm
    return pl.pallas_call(
        _moe_combine_kernel, name="moe_combine",
        grid=(n // tm,),
        in_specs=[pl.BlockSpec((tm, d), lambda i: (first + i, 0)),
                  pl.BlockSpec((tm, d), lambda i: (first + i + t // tm, 0)),
                  pl.BlockSpec((tm, LANES), lambda i: (first + i, 0)),
                  pl.BlockSpec((tm, d), lambda i: (first + i, 0)),
                  pl.BlockSpec((1, 1, d), lambda i: (tile_vec(first + i, tm), 0, 0))],
        out_specs=pl.BlockSpec((tm, d), lambda i: (i, 0)),
        out_shape=jax.ShapeDtypeStruct((n, d), F32),
        compiler_params=_params("arbitrary"),
    )(y_pair, y_pair, route, h, gt_tab)


def _moe(u, route, wg, wu, wd, h, gt_tab, tile_vec, splits):
    t, d = u.shape
    n_rows = 2 * t + N_EXPERTS * MOE_TILE
    pos, row_token, tile_expert, n_used = _route_plan(route, n_rows)
    x_sorted = _sc_gather(u, row_token)
    y_sorted = _grouped_swiglu(x_sorted, tile_expert, n_used, wg, wu, wd)
    y_pair = _sc_gather(y_sorted, pos)
    return [_moe_combine(y_pair, route, h, gt_tab, tile_vec, row0, n) for row0, n in splits]


def _final_norm_kernel(h_ref, g_ref, o_ref):
    o_ref[0] = _rms(h_ref[0], g_ref[...])


def _final_norm(h, g):
    b, l, d = h.shape
    tm = min(l, 1024)
    return pl.pallas_call(
        _final_norm_kernel, name="final_norm",
        grid=(b, l // tm),
        in_specs=[pl.BlockSpec((1, tm, d), lambda bi, i: (bi, i, 0)), pl.BlockSpec((1, d), lambda bi, i: (0, 0))],
        out_specs=pl.BlockSpec((1, tm, d), lambda bi, i: (bi, i, 0)),
        out_shape=jax.ShapeDtypeStruct((b, l, d), F32),
        compiler_params=_params("arbitrary", "arbitrary"),
    )(h, g.reshape(1, d))


def _rope_tables(length, head_dim):
    t = jnp.arange(length)
    row = (t // GRID_W).astype(F32)
    col = (t % GRID_W).astype(F32)
    n = head_dim // 4
    inv = ROPE_THETA ** (-jnp.arange(n, dtype=F32) / n)
    ang = jnp.concatenate([row[:, None] * inv, col[:, None] * inv], axis=-1)
    cos = jnp.concatenate([jnp.cos(ang)] * 2, axis=-1)
    sin = jnp.concatenate([-jnp.sin(ang), jnp.sin(ang)], axis=-1)
    reps = LANES // head_dim
    return jnp.tile(cos, (1, reps)), jnp.tile(sin, (1, reps))


def _hyena_tables(length):
    k = jnp.arange(length, dtype=jnp.int32)
    period = 4 * length
    step = 2.0 * math.pi / period
    lo = jnp.arange(64, dtype=jnp.int32)
    hi = jnp.arange(length // 64, dtype=jnp.int32)

    def tabs(x, y_hi, y_lo):
        def cs(y):
            ang = ((x[:, None] * y[None, :]) % period).astype(F32) * step
            return jnp.cos(ang)[:, :, None], jnp.sin(ang)[:, :, None]
        (ca, sa), (cb, sb) = cs(y_hi), cs(y_lo)
        cb, sb = jnp.swapaxes(cb, 1, 2), jnp.swapaxes(sb, 1, 2)
        c = (ca * cb - sa * sb).reshape(length, length)
        s = (sa * cb + ca * sb).reshape(length, length)
        return c.astype(BF16), s.astype(BF16)

    c_tab, s_tab = tabs(2 * k + 1, 64 * hi, lo)
    ci_tab, si_tab = tabs(k, 128 * hi, 2 * lo + 1)
    t = jnp.linspace(0.0, 1.0, length, dtype=F32)[:, None]
    bands = (HY_EMB - 1) // 2
    ang = (2.0 * math.pi / length) * jnp.arange(length, dtype=F32)[:, None] \
        * jnp.linspace(1e-4, bands - 1, bands, dtype=F32)
    feats = jnp.concatenate([t, jnp.cos(ang), -jnp.sin(ang),
                             jnp.zeros((length, LANES - HY_EMB), F32)], axis=-1)
    max_decay = math.log(HY_DECAY_TARGET) / HY_FAST_DECAY
    min_decay = math.log(HY_DECAY_TARGET) / HY_SLOW_DECAY
    deltas = jnp.abs(jnp.linspace(min_decay, max_decay, HY_WIDTH, dtype=F32)).reshape(1, HY_WIDTH)
    return c_tab, s_tab, ci_tab, si_tab, feats, deltas


def _block_diag_ones(n, group):
    i = jnp.arange(n) // group
    return (i[:, None] == i[None, :]).astype(BF16)


def kernel(x, c, ctx, c_ctx, w_ada, b_ada, g_mix, g_ffn, w_in, w_out, hy_conv_w, hy_conv_b, hy_w1, hy_b1, hy_w2, hy_b2, hy_w3, hy_freq, hy_skip, diff_lq1, diff_lk1, diff_lq2, diff_lk2, diff_subln, gqa_qnorm, gqa_knorm, ffn_wg, ffn_wu, ffn_wd, moe_router, moe_wg, moe_wu, moe_wd, g_final):
    b, seq, d = x.shape
    n_ctx = ctx.shape[1]
    depth = w_ada.shape[0]

    rope_tabs = _rope_tables(seq, DIFF_QK_DIM) + _rope_tables(seq, GQA_HEAD_DIM)
    hy_tabs = _hyena_tables(seq)
    hy_tabs_c = _hyena_tables(n_ctx)
    bd128 = _block_diag_ones(LANES, GQA_HEAD_DIM)
    bd256 = _block_diag_ones(DIFF_WIDTH, DIFF_V_DIM)

    rows = 16
    cc = jnp.zeros((rows, d), F32).at[:b].set(c).at[b].set(c_ctx)
    mods = _mods(cc, w_ada, b_ada)

    def mod_vecs(i):
        lat = [mods[i, :b, j * d:(j + 1) * d].reshape(b, 1, d) for j in range(6)]
        cx = [jnp.broadcast_to(mods[i, b, j * d:(j + 1) * d].reshape(1, 1, d), (b, 1, d)) for j in range(6)]
        return lat, cx

    h, hc = x, ctx
    for i in range(depth):
        last = i == depth - 1
        lam_init = 0.8 - 0.6 * math.exp(-0.3 * i)
        (sh_m, sc_m, gt_m, sh_f, sc_f, gt_f), (csh_m, csc_m, cgt_m, csh_f, csc_f, cgt_f) = mod_vecs(i)
        w_in_bf = w_in[i].astype(BF16)
        w_out_bf = w_out[i].astype(BF16)
        g_m = g_mix[i].reshape(1, d)
        g_f = g_ffn[i].reshape(1, d)
        qn = jnp.tile(gqa_qnorm[i], LANES // GQA_HEAD_DIM).reshape(1, LANES)
        kn = jnp.tile(gqa_knorm[i], LANES // GQA_HEAD_DIM).reshape(1, LANES)
        subln = jnp.tile(diff_subln[i], DIFF_HEADS).reshape(1, DIFF_WIDTH) * (1.0 - lam_init)
        lqk = jnp.stack([diff_lq1[i], diff_lk1[i], diff_lq2[i], diff_lk2[i]]).astype(F32)
        hy_p = (hy_conv_w[i], hy_conv_b[i], hy_w1[i], hy_b1[i], hy_w2[i], hy_b2[i], hy_w3[i],
                hy_freq[i], hy_skip[i])
        moe = i % 2 == 1
        router_pad = None
        if moe:
            router = jnp.zeros((d, LANES), F32).at[:, :N_EXPERTS].set(moe_router[i // 2])
            router_hi = router.astype(BF16)
            router_pad = jnp.stack([router_hi, (router - router_hi.astype(F32)).astype(BF16)])

        zhy, qd, kdt, vdx, qg, kgt, vgx = _inproj(h, sc_m, sh_m, g_m, w_in_bf, qn, kn, bd128, rope_tabs)
        czhy, cqd, ckdt, cvdx, cqg, ckgt, cvgx = _inproj(hc, csc_m, csh_m, g_m, w_in_bf, qn, kn, bd128, None)

        y_hy = _hyena(zhy, hy_p, hy_tabs)
        o_d = _diff_attention(lqk, qd, [kdt, ckdt], [vdx, cvdx], lam_init)
        o_g = _gqa_attention(qg, [kgt, ckgt], [vgx, cvgx])
        if not last:
            yc_hy = _hyena(czhy, hy_p, hy_tabs_c)
            oc_d = _diff_attention(lqk, cqd, [ckdt], [cvdx], lam_init)
            oc_g = _gqa_attention(cqg, [ckgt], [cvgx])

        if moe:
            n_c = 0 if last else b * n_ctx
            total = n_c + b * seq
            filled = None
            if not last:
                filled = _outproj(yc_hy, oc_d, oc_g, hc, cgt_m, w_out_bf, g_f, csc_f, csh_f, subln, bd256,
                                  router_pad, (total, 0, None))
            h_all, u_all, route = _outproj(y_hy, o_d, o_g, h, gt_m, w_out_bf, g_f, sc_f, sh_f, subln, bd256,
                                           router_pad, (total, n_c, filled))
            wg, wu, wd = (w[i // 2].astype(BF16) for w in (moe_wg, moe_wu, moe_wd))
            gt_tab = jnp.concatenate([gt_f, cgt_f[:1]], axis=0)
            tile_vec = lambda t, tm: jnp.where(t < n_c // tm, b, (t - n_c // tm) // (seq // tm))
            parts = _moe(u_all, route, wg, wu, wd, h_all, gt_tab, tile_vec,
                         [(n_c, b * seq)] + ([] if last else [(0, n_c)]))
            h = parts[0].reshape(b, seq, d)
            if not last:
                hc = parts[1].reshape(b, n_ctx, d)
        else:
            wg, wu, wd = (w[i // 2].astype(BF16) for w in (ffn_wg, ffn_wu, ffn_wd))
            mixed = _outproj(y_hy, o_d, o_g, h, gt_m, w_out_bf, g_f, sc_f, sh_f, subln, bd256)
            h = _ffn(mixed[1], wg, wu, wd, mixed[0], gt_f)
            if not last:
                cmixed = _outproj(yc_hy, oc_d, oc_g, hc, cgt_m, w_out_bf, g_f, csc_f, csh_f, subln, bd256)
                hc = _ffn(cmixed[1], wg, wu, wd, cmixed[0], cgt_f)
    return _final_norm(h, g_final)
```

```python
import functools
import math

import jax
import jax.numpy as jnp
from jax import lax
from jax.experimental import pallas as pl
from jax.experimental.pallas import tpu as pltpu
from jax.experimental.pallas import tpu_sc as plsc

F32 = jnp.float32
BF16 = jnp.bfloat16
HIGHEST = lax.Precision.HIGHEST

D_MODEL = 1024
GRID_W = 64
ROPE_THETA = 10000.0
NORM_EPS = 1e-6

HY_WIDTH = 256
HY_EMB = 33
HY_FAST_DECAY = 0.3
HY_SLOW_DECAY = 1.5
HY_DECAY_TARGET = 1e-2

DIFF_HEADS = 4
DIFF_QK_DIM = 32
DIFF_V_DIM = 64
DIFF_WIDTH = 256
GQA_HEAD_DIM = 64
GQA_WIDTH = 512
GQA_KV_WIDTH = 128
GQA_REP = 4

OFF_DQ = 3 * HY_WIDTH
OFF_DK = OFF_DQ + DIFF_WIDTH
OFF_DV = OFF_DK + DIFF_WIDTH
OFF_GQ = OFF_DV + DIFF_WIDTH
OFF_GK = OFF_GQ + GQA_WIDTH
OFF_GV = OFF_GK + GQA_KV_WIDTH
IN_COLS = OFF_GV + GQA_KV_WIDTH

N_EXPERTS = 8
LOG2_E = math.log2(math.e)
LANES = 128
VMEM_LIMIT = 56 * 1024 * 1024


def _params(*sem):
    return pltpu.CompilerParams(dimension_semantics=sem, vmem_limit_bytes=VMEM_LIMIT)


def _dot(a, b):
    return jnp.dot(a, b, preferred_element_type=F32)


def _dot_hi(a, b):
    return jnp.dot(a, b, preferred_element_type=F32, precision=HIGHEST)


def _rms(x, g):
    ms = jnp.mean(x * x, axis=-1, keepdims=True)
    return x * lax.rsqrt(ms + NORM_EPS) * g


def _silu(x):
    return x * jax.nn.sigmoid(x)


def _group_mean_sq(x, ones_bd, width):
    return _dot((x * x).astype(BF16), ones_bd) * (1.0 / width)


def _mods_kernel(c_ref, w_ref, b_ref, o_ref):
    o_ref[0] = _dot_hi(_silu(c_ref[...]), w_ref[0]) + b_ref[0]


def _mods(cc, w_ada, b_ada):
    depth, d, n = w_ada.shape
    tn = 1536
    return pl.pallas_call(
        _mods_kernel, name="mods",
        grid=(depth, n // tn),
        in_specs=[pl.BlockSpec(cc.shape, lambda i, j: (0, 0)),
                  pl.BlockSpec((1, d, tn), lambda i, j: (i, 0, j)),
                  pl.BlockSpec((1, 1, tn), lambda i, j: (i, 0, j))],
        out_specs=pl.BlockSpec((1, cc.shape[0], tn), lambda i, j: (i, 0, j)),
        out_shape=jax.ShapeDtypeStruct((depth, cc.shape[0], n), F32),
        compiler_params=_params("arbitrary", "arbitrary"),
    )(cc, w_ada, b_ada.reshape(depth, 1, n))


def _rope128(x, cos, sin_signed, half):
    lane = lax.broadcasted_iota(jnp.int32, x.shape, 1)
    first = (lane & (2 * half - 1)) < half
    swapped = jnp.where(first, pltpu.roll(x, LANES - half, 1), pltpu.roll(x, half, 1))
    return x * cos + swapped * sin_signed


def _inproj_kernel(*refs, rope):
    h_ref, sc_ref, sh_ref, g_ref, w_ref, qn_ref, kn_ref, bd_ref = refs[:8]
    if rope:
        cd_ref, sd_ref, cg_ref, sg_ref = refs[8:12]
    zhy_ref, qd_ref, kd_ref, vd_ref, qg_ref, kg_ref, vg_ref = refs[-7:]
    u = _rms(h_ref[0], g_ref[...]) * (1.0 + sc_ref[0]) + sh_ref[0]
    z = _dot(u.astype(BF16), w_ref[...])
    zhy_ref[0] = z[:, :OFF_DQ]

    def piece(off, j):
        return z[:, off + LANES * j: off + LANES * (j + 1)]

    for j in range(DIFF_WIDTH // LANES):
        q, k = piece(OFF_DQ, j), piece(OFF_DK, j)
        if rope:
            q = _rope128(q, cd_ref[...], sd_ref[...], DIFF_QK_DIM // 2)
            k = _rope128(k, cd_ref[...], sd_ref[...], DIFF_QK_DIM // 2)
        qd_ref[0, :, LANES * j: LANES * (j + 1)] = (q * (LOG2_E * DIFF_QK_DIM ** -0.5)).astype(BF16)
        kd_ref[0, 0, LANES * j: LANES * (j + 1), :] = k.T.astype(BF16)
    vd_ref[0, 0] = z[:, OFF_DV:OFF_GQ].astype(BF16)

    def gqa_piece(x, gain):
        ms = _group_mean_sq(x, bd_ref[...], GQA_HEAD_DIM)
        x = x * lax.rsqrt(ms + NORM_EPS) * gain
        if rope:
            x = _rope128(x, cg_ref[...], sg_ref[...], GQA_HEAD_DIM // 2)
        return x

    for j in range(GQA_WIDTH // LANES):
        q = gqa_piece(piece(OFF_GQ, j), qn_ref[...])
        qg_ref[0, :, LANES * j: LANES * (j + 1)] = (q * (LOG2_E * GQA_HEAD_DIM ** -0.5)).astype(BF16)
    kt = gqa_piece(piece(OFF_GK, 0), kn_ref[...]).T.astype(BF16)
    v = z[:, OFF_GV:]
    v_swapped = pltpu.roll(v, GQA_HEAD_DIM, 1)
    low = lax.broadcasted_iota(jnp.int32, v.shape, 1) < GQA_HEAD_DIM
    v_rep = (jnp.where(low, v, v_swapped), jnp.where(low, v_swapped, v))
    for g in range(GQA_KV_WIDTH // GQA_HEAD_DIM):
        for r in range(GQA_REP):
            kg_ref[0, g, GQA_HEAD_DIM * r: GQA_HEAD_DIM * (r + 1), :] = kt[GQA_HEAD_DIM * g: GQA_HEAD_DIM * (g + 1), :]
        for half in range(GQA_REP * GQA_HEAD_DIM // LANES):
            vg_ref[0, g, :, LANES * half: LANES * (half + 1)] = v_rep[g].astype(BF16)


def _inproj(h, sc, sh, g, w_bf, qn, kn, bd, rope_tabs):
    b, l, d = h.shape
    tm = min(l, 512)
    row = lambda bi, i: (bi, i, 0)
    vec = lambda bi, i: (bi, 0, 0)
    const = lambda bi, i: (0, 0)
    in_specs = [pl.BlockSpec((1, tm, d), row), pl.BlockSpec((1, 1, d), vec), pl.BlockSpec((1, 1, d), vec),
                pl.BlockSpec((1, d), const), pl.BlockSpec((d, IN_COLS), const),
                pl.BlockSpec((1, LANES), const), pl.BlockSpec((1, LANES), const),
                pl.BlockSpec((LANES, LANES), const)]
    args = [h, sc, sh, g, w_bf, qn, kn, bd]
    if rope_tabs is not None:
        in_specs += [pl.BlockSpec((tm, LANES), lambda bi, i: (i, 0))] * 4
        args += list(rope_tabs)
    aw = ATTN_WIDTH
    kv_groups = GQA_KV_WIDTH // GQA_HEAD_DIM
    rows_spec = lambda w: pl.BlockSpec((1, tm, w), row)
    kt_spec = lambda g: pl.BlockSpec((1, g, aw, tm), lambda bi, i: (bi, 0, 0, i))
    v_spec = lambda g: pl.BlockSpec((1, g, tm, aw), lambda bi, i: (bi, 0, i, 0))
    sds = jax.ShapeDtypeStruct
    return pl.pallas_call(
        functools.partial(_inproj_kernel, rope=rope_tabs is not None), name="inproj",
        grid=(b, l // tm),
        in_specs=in_specs,
        out_specs=[rows_spec(OFF_DQ), rows_spec(DIFF_WIDTH), kt_spec(1), v_spec(1),
                   rows_spec(GQA_WIDTH), kt_spec(kv_groups), v_spec(kv_groups)],
        out_shape=[sds((b, l, OFF_DQ), F32), sds((b, l, DIFF_WIDTH), BF16), sds((b, 1, aw, l), BF16),
                   sds((b, 1, l, aw), BF16), sds((b, l, GQA_WIDTH), BF16), sds((b, kv_groups, aw, l), BF16),
                   sds((b, kv_groups, l, aw), BF16)],
        compiler_params=_params("arbitrary", "arbitrary"),
    )(*args)


def _hy_filter_kernel(f_ref, w1_ref, b1_ref, w2_ref, b2_ref, w3_ref, fr_ref, dl_ref, hs_ref, hd_ref):
    f = f_ref[...]
    fr = fr_ref[...]
    a = jnp.sin(fr * (_dot_hi(f, w1_ref[...]) + b1_ref[...]))
    a = jnp.sin(fr * (_dot_hi(a, w2_ref[...]) + b2_ref[...]))
    hf = _dot_hi(a, w3_ref[...])
    decay = jnp.exp(-f[:, 0:1] * dl_ref[...])
    h_fwd = hf[:, :HY_WIDTH] * decay
    h_bwd = hf[:, HY_WIDTH:] * decay
    row = lax.broadcasted_iota(jnp.int32, h_bwd.shape, 0) + pl.program_id(0) * f.shape[0]
    h_bwd = jnp.where(row == 0, 0.0, h_bwd)
    hs_ref[...] = (h_fwd + h_bwd).astype(BF16)
    hd_ref[...] = (h_bwd - h_fwd).astype(BF16)


def _hy_spectrum_kernel(c_ref, s_ref, hs_ref, hd_ref, gre_ref, gim_ref, *, scale):
    gre_ref[...] = _dot(c_ref[...], hs_ref[...]) * scale
    gim_ref[...] = _dot(s_ref[...], hd_ref[...]) * scale


def _hy_pre_kernel(z_ref, zp_ref, zn_ref, cw_ref, cb_ref, wb_ref, wf_ref, x0_ref, *, nt):
    ti = pl.program_id(1)
    z = z_ref[0]
    tl = z.shape[0]
    row = lax.broadcasted_iota(jnp.int32, z.shape, 0)
    prev_row = jnp.where(ti > 0, zp_ref[0, 7:8, :], 0.0)
    next_row = jnp.where(ti < nt - 1, zn_ref[0, 0:1, :], 0.0)
    z_prev = jnp.where(row == 0, prev_row, pltpu.roll(z, 1, 0))
    z_next = jnp.where(row == tl - 1, next_row, pltpu.roll(z, tl - 1, 0))
    cw = cw_ref[...]
    y = cb_ref[...] + z_prev * cw[0:1] + z * cw[1:2] + z_next * cw[2:3]
    x0, x1, v = y[:, :HY_WIDTH], y[:, HY_WIDTH:2 * HY_WIDTH], y[:, 2 * HY_WIDTH:]
    w = v * x1
    wb_ref[...] = w.astype(BF16)
    wf_ref[0] = w
    x0_ref[0] = x0


def _hy_fwd_kernel(c_ref, s_ref, x_ref, gre_ref, gim_ref, yre_ref, yim_ref, *, reps):
    x = x_ref[...]
    a = _dot(c_ref[...], x)
    b = _dot(s_ref[...], x)
    gre = jnp.concatenate([gre_ref[...]] * reps, axis=1)
    gim = jnp.concatenate([gim_ref[...]] * reps, axis=1)
    yre_ref[...] = (a * gre + b * gim).astype(BF16)
    yim_ref[...] = (a * gim - b * gre).astype(BF16)


def _hy_inv_kernel(ci_ref, si_ref, yre_ref, yim_ref, wf_ref, x0_ref, skip_ref, o_ref, *, reps):
    y = _dot(ci_ref[...], yre_ref[...]) - _dot(si_ref[...], yim_ref[...])
    for r in range(reps):
        yr = y[:, r * HY_WIDTH:(r + 1) * HY_WIDTH]
        o_ref[r] = ((yr + wf_ref[r] * skip_ref[...]) * x0_ref[r]).astype(BF16)


def _hyena(zhy, p, tabs):
    conv_w, conv_b, w1, b1, w2, b2, w3, freq, skip = p
    c_tab, s_tab, ci_tab, si_tab, feats, deltas = tabs
    b, l, _ = zhy.shape
    c = HY_WIDTH
    tl = min(l, 512)
    nt = l // tl
    hid = w2.shape[0]
    const1 = lambda i: (0, 0)
    w1p = jnp.zeros((LANES, hid), F32).at[:HY_EMB].set(w1)
    hs, hd = pl.pallas_call(
        _hy_filter_kernel, name="hy_filter",
        grid=(nt,),
        in_specs=[pl.BlockSpec((tl, LANES), lambda i: (i, 0)),
                  pl.BlockSpec((LANES, hid), const1), pl.BlockSpec((1, hid), const1),
                  pl.BlockSpec((hid, hid), const1), pl.BlockSpec((1, hid), const1),
                  pl.BlockSpec((hid, 2 * c), const1), pl.BlockSpec((1, hid), const1),
                  pl.BlockSpec((1, c), const1)],
        out_specs=[pl.BlockSpec((tl, c), lambda i: (i, 0))] * 2,
        out_shape=[jax.ShapeDtypeStruct((l, c), BF16)] * 2,
        compiler_params=_params("arbitrary"),
    )(feats, w1p, b1.reshape(1, hid), w2, b2.reshape(1, hid), w3, freq.reshape(1, hid), deltas)

    gre, gim = pl.pallas_call(
        functools.partial(_hy_spectrum_kernel, scale=1.0 / l), name="hy_spectrum",
        grid=(nt,),
        in_specs=[pl.BlockSpec((tl, l), lambda i: (i, 0)), pl.BlockSpec((tl, l), lambda i: (i, 0)),
                  pl.BlockSpec((l, c), const1), pl.BlockSpec((l, c), const1)],
        out_specs=[pl.BlockSpec((tl, c), lambda i: (i, 0))] * 2,
        out_shape=[jax.ShapeDtypeStruct((l, c), F32)] * 2,
        compiler_params=_params("arbitrary"),
    )(c_tab, s_tab, hs, hd)

    halo = 8
    wb, wf, x0 = pl.pallas_call(
        functools.partial(_hy_pre_kernel, nt=nt), name="hy_pre",
        grid=(b, nt),
        in_specs=[pl.BlockSpec((1, tl, 3 * c), lambda bi, i: (bi, i, 0)),
                  pl.BlockSpec((1, halo, 3 * c), lambda bi, i: (bi, jnp.maximum(i * (tl // halo) - 1, 0), 0)),
                  pl.BlockSpec((1, halo, 3 * c),
                               lambda bi, i: (bi, jnp.minimum((i + 1) * (tl // halo), l // halo - 1), 0)),
                  pl.BlockSpec((3, 3 * c), lambda bi, i: (0, 0)),
                  pl.BlockSpec((1, 3 * c), lambda bi, i: (0, 0))],
        out_specs=[pl.BlockSpec((tl, c), lambda bi, i: (i, bi)),
                   pl.BlockSpec((1, tl, c), lambda bi, i: (bi, i, 0)),
                   pl.BlockSpec((1, tl, c), lambda bi, i: (bi, i, 0))],
        out_shape=[jax.ShapeDtypeStruct((l, b * c), BF16),
                   jax.ShapeDtypeStruct((b, l, c), F32),
                   jax.ShapeDtypeStruct((b, l, c), F32)],
        compiler_params=_params("arbitrary", "arbitrary"),
    )(zhy, zhy, zhy, conv_w, conv_b.reshape(1, 3 * c))

    reps = 2
    tn = reps * c
    nj = b * c // tn
    yre, yim = pl.pallas_call(
        functools.partial(_hy_fwd_kernel, reps=reps), name="hy_fwd",
        grid=(nj, nt),
        in_specs=[pl.BlockSpec((tl, l), lambda j, i: (i, 0)), pl.BlockSpec((tl, l), lambda j, i: (i, 0)),
                  pl.BlockSpec((l, tn), lambda j, i: (0, j)),
                  pl.BlockSpec((tl, c), lambda j, i: (i, 0)), pl.BlockSpec((tl, c), lambda j, i: (i, 0))],
        out_specs=[pl.BlockSpec((tl, tn), lambda j, i: (i, j))] * 2,
        out_shape=[jax.ShapeDtypeStruct((l, b * c), BF16)] * 2,
        compiler_params=_params("arbitrary", "arbitrary"),
    )(c_tab, s_tab, wb, gre, gim)

    return pl.pallas_call(
        functools.partial(_hy_inv_kernel, reps=reps), name="hy_inv",
        grid=(nj, nt),
        in_specs=[pl.BlockSpec((tl, l), lambda j, i: (i, 0)), pl.BlockSpec((tl, l), lambda j, i: (i, 0)),
                  pl.BlockSpec((l, tn), lambda j, i: (0, j)), pl.BlockSpec((l, tn), lambda j, i: (0, j)),
                  pl.BlockSpec((reps, tl, c), lambda j, i: (j, i, 0)),
                  pl.BlockSpec((reps, tl, c), lambda j, i: (j, i, 0)),
                  pl.BlockSpec((1, c), lambda j, i: (0, 0))],
        out_specs=pl.BlockSpec((reps, tl, c), lambda j, i: (j, i, 0)),
        out_shape=jax.ShapeDtypeStruct((b, l, c), BF16),
        compiler_params=_params("arbitrary", "arbitrary"),
    )(ci_tab, si_tab, yre, yim, wf, x0, skip.reshape(1, c))


def _lane_mask(width, shift, idx):
    lane = lax.broadcasted_iota(jnp.int32, (1, width), 1)
    return jnp.where((lane >> shift) == idx, 1.0, 0.0).astype(BF16)


ATTN_WIDTH = 256
ATTN_CHUNK = 512


def _key_chunks(lk):
    return [(c0, min(ATTN_CHUNK, lk - c0)) for c0 in range(0, lk, ATTN_CHUNK)]


ATTN_STACK = 4
ATTN_TQ = 128
ATTN_STEP_ROWS = 512


def _qk_stage(q, shift, first, kt_refs, qs_scr, s_scr):
    tq = q.shape[0]
    for i in range(ATTN_STACK):
        qs_scr[i * tq:(i + 1) * tq, :] = q * _lane_mask(ATTN_WIDTH, shift, first + i)
    qs = qs_scr[...]
    m = jnp.full((ATTN_STACK * tq, LANES), -jnp.inf, F32)
    base = 0
    for kt_ref in kt_refs:
        for c0, ck in _key_chunks(kt_ref.shape[-1]):
            s = _dot(qs, kt_ref[:, c0:c0 + ck])
            s_scr[:, base + c0:base + c0 + ck] = s
            for j in range(ck // LANES):
                m = jnp.maximum(m, s[:, j * LANES:(j + 1) * LANES])
        base += kt_ref.shape[-1]
    return jnp.max(m, axis=-1, keepdims=True)


def _exp_stage(s_scr, mx, p_scr):
    l = jnp.zeros((s_scr.shape[0], LANES), F32)
    for c0, ck in _key_chunks(s_scr.shape[-1]):
        e = jnp.exp2(s_scr[:, c0:c0 + ck] - mx)
        p_scr[:, c0:c0 + ck] = e.astype(BF16)
        for j in range(ck // LANES):
            l = l + e[:, j * LANES:(j + 1) * LANES]
    return 1.0 / jnp.sum(l, axis=-1, keepdims=True)


def _attention_passes(passes, consume, kt_refs, v_refs, scratch):
    qs, ss, ps = scratch[0:2], scratch[2:4], scratch[4:6]
    tq = passes[0][0].shape[0]

    def qk(i):
        q, shift, first, g = passes[i]
        return _qk_stage(q, shift, first, [kt.at[0, g] for kt in kt_refs], qs[i % 2], ss[i % 2])

    mx = qk(0)
    for i in range(len(passes)):
        mx_next = qk(i + 1) if i + 1 < len(passes) else None
        inv = _exp_stage(ss[i % 2], mx, ps[i % 2])
        pv, base = None, 0
        for v_ref in v_refs:
            lk = v_ref.shape[2]
            part = _dot(ps[i % 2][:, base:base + lk], v_ref[0, passes[i][3]])
            pv = part if pv is None else pv + part
            base += lk
        pv = pv * inv
        consume(i, [pv[r * tq:(r + 1) * tq] for r in range(ATTN_STACK)])
        mx = mx_next


def _diff_kernel(lqk_ref, q_ref, *refs, n_kv, lam_init):
    kt_refs, v_refs, o_ref, scratch = refs[:n_kv], refs[n_kv:2 * n_kv], refs[2 * n_kv], refs[2 * n_kv + 1:]
    x = lqk_ref[...]
    lam = (jnp.exp(jnp.sum(x[0:1] * x[1:2], axis=-1, keepdims=True))
           - jnp.exp(jnp.sum(x[2:3] * x[3:4], axis=-1, keepdims=True)) + lam_init)
    tq = min(ATTN_TQ, q_ref.shape[1])
    group = lax.broadcasted_iota(jnp.int32, (tq, ATTN_WIDTH), 1) >> 6
    halves = 2 * DIFF_HEADS // ATTN_STACK
    passes = []
    for t in range(q_ref.shape[1] // tq):
        q = q_ref[0, t * tq:(t + 1) * tq, :]
        passes += [(q, 5, half * ATTN_STACK, 0) for half in range(halves)]
    acc = {}

    def consume(i, pvs):
        t, half = divmod(i, halves)
        o1, o2 = acc.pop(t, (jnp.zeros(group.shape, F32),) * 2)
        for r, pv in enumerate(pvs):
            head = (half * ATTN_STACK + r) >> 1
            if r & 1 == 0:
                o1 = jnp.where(group == head, pv, o1)
            else:
                o2 = jnp.where(group == head, pv, o2)
        if half + 1 < halves:
            acc[t] = (o1, o2)
            return
        o_ref[0, t * tq:(t + 1) * tq, :] = (o1 - lam * o2).astype(BF16)

    _attention_passes(passes, consume, kt_refs, v_refs, scratch)


def _gqa_kernel(q_ref, *refs, n_kv):
    kt_refs, v_refs, o_ref, scratch = refs[:n_kv], refs[n_kv:2 * n_kv], refs[2 * n_kv], refs[2 * n_kv + 1:]
    tq = min(ATTN_TQ, q_ref.shape[1])
    group = lax.broadcasted_iota(jnp.int32, (tq, ATTN_WIDTH), 1) >> 6
    groups = q_ref.shape[2] // ATTN_WIDTH
    passes = []
    for t in range(q_ref.shape[1] // tq):
        for g in range(groups):
            passes.append((q_ref[0, t * tq:(t + 1) * tq, g * ATTN_WIDTH:(g + 1) * ATTN_WIDTH], 6, 0, g))

    def consume(i, pvs):
        t, g = divmod(i, groups)
        o = jnp.zeros(group.shape, F32)
        for r, pv in enumerate(pvs):
            o = jnp.where(group == r, pv, o)
        o_ref[0, t * tq:(t + 1) * tq, g * ATTN_WIDTH:(g + 1) * ATTN_WIDTH] = o.astype(BF16)

    _attention_passes(passes, consume, kt_refs, v_refs, scratch)


def _attention(kernel_fn, name, q, kts, vs, extra=()):
    b, lq, width = q.shape
    groups, w = kts[0].shape[1:3]
    lk = sum(kt.shape[3] for kt in kts)
    rows = min(lq, ATTN_STEP_ROWS)
    pass_rows = ATTN_STACK * min(ATTN_TQ, rows)
    kv_spec = lambda a: pl.BlockSpec((1,) + a.shape[1:], lambda bi, i: (bi, 0, 0, 0), pipeline_mode=pl.Buffered(1))
    return pl.pallas_call(
        functools.partial(kernel_fn, n_kv=len(kts)), name=name,
        grid=(b, lq // rows),
        in_specs=[pl.BlockSpec(a.shape, lambda bi, i: (0, 0)) for a in extra]
        + [pl.BlockSpec((1, rows, width), lambda bi, i: (bi, i, 0))]
        + [kv_spec(a) for a in kts] + [kv_spec(a) for a in vs],
        out_specs=pl.BlockSpec((1, rows, width), lambda bi, i: (bi, i, 0)),
        out_shape=jax.ShapeDtypeStruct((b, lq, width), BF16),
        scratch_shapes=[pltpu.VMEM((pass_rows, w), BF16)] * 2 + [pltpu.VMEM((pass_rows, lk), F32)] * 2
        + [pltpu.VMEM((pass_rows, lk), BF16)] * 2,
        compiler_params=_params("arbitrary", "arbitrary"),
    )(*extra, q, *kts, *vs)


def _diff_attention(lqk, q, kts, vs, lam_init):
    return _attention(functools.partial(_diff_kernel, lam_init=lam_init), "diff_attn", q, kts, vs, (lqk,))


def _gqa_attention(q, kts, vs):
    return _attention(_gqa_kernel, "gqa_attn", q, kts, vs)


def _top2_route(logits):
    lane = lax.broadcasted_iota(jnp.int32, logits.shape, 1).astype(F32)
    neg = jnp.float32(-jnp.inf)
    lg = jnp.where(lane < N_EXPERTS, logits, neg)
    m1 = jnp.max(lg, axis=-1, keepdims=True)
    i1 = jnp.min(jnp.where(lg == m1, lane, float(LANES)), axis=-1, keepdims=True)
    lg2 = jnp.where(lane == i1, neg, lg)
    m2 = jnp.max(lg2, axis=-1, keepdims=True)
    i2 = jnp.min(jnp.where(lg2 == m2, lane, float(LANES)), axis=-1, keepdims=True)
    e2 = jnp.exp(m2 - m1)
    w1 = 1.0 / (1.0 + e2)
    return (jnp.where(lane == 0, i1, 0.0) + jnp.where(lane == 1, i2, 0.0)
            + jnp.where(lane == 2, w1, 0.0) + jnp.where(lane == 3, e2 * w1, 0.0))


def _outproj_kernel(*refs, moe):
    yhy_ref, od_ref, og_ref, h_ref, gt_ref, w_ref, gf_ref, sc_ref, sh_ref, sg_ref, bd_ref = refs[:11]
    if moe:
        rt_ref = refs[11]
        hn_ref, u_ref, gates_ref = refs[-3:]
    else:
        hn_ref, u_ref = refs[11:]
    od = od_ref[0].astype(F32)
    od = od * lax.rsqrt(_group_mean_sq(od, bd_ref[...], DIFF_V_DIM) + NORM_EPS) * sg_ref[...]
    y = (_dot(yhy_ref[0], w_ref[0:HY_WIDTH, :])
         + _dot(od.astype(BF16), w_ref[HY_WIDTH:HY_WIDTH + DIFF_WIDTH, :])
         + _dot(og_ref[0], w_ref[HY_WIDTH + DIFF_WIDTH:, :]))
    hn = h_ref[0] + gt_ref[0] * y
    hn_ref[...] = hn.reshape(hn_ref.shape)
    u = _rms(hn, gf_ref[...]) * (1.0 + sc_ref[0]) + sh_ref[0]
    u_ref[...] = u.astype(u_ref.dtype).reshape(u_ref.shape)
    if moe:
        u_hi = u.astype(BF16)
        u_lo = (u - u_hi.astype(F32)).astype(BF16)
        logits = _dot(u_hi, rt_ref[0]) + (_dot(u_lo, rt_ref[0]) + _dot(u_hi, rt_ref[1]))
        gates_ref[...] = _top2_route(logits)


def _outproj(yhy, od, og, h, gt, w_bf, gf, sc, sh, sub_g, bd256, router_pad=None, into=None):
    b, l, d = h.shape
    tm = min(l, 512)
    row = lambda bi, i: (bi, i, 0)
    vec = lambda bi, i: (bi, 0, 0)
    const = lambda bi, i: (0, 0)
    moe = router_pad is not None
    in_specs = [pl.BlockSpec((1, tm, HY_WIDTH), row), pl.BlockSpec((1, tm, DIFF_WIDTH), row),
                pl.BlockSpec((1, tm, GQA_WIDTH), row), pl.BlockSpec((1, tm, d), row),
                pl.BlockSpec((1, 1, d), vec), pl.BlockSpec((d, d), const), pl.BlockSpec((1, d), const),
                pl.BlockSpec((1, 1, d), vec), pl.BlockSpec((1, 1, d), vec),
                pl.BlockSpec((1, DIFF_WIDTH), const), pl.BlockSpec((DIFF_WIDTH, DIFF_WIDTH), const)]
    args = [yhy, od, og, h, gt, w_bf, gf, sc, sh, sub_g, bd256]
    out_specs = [pl.BlockSpec((1, tm, d), row), pl.BlockSpec((1, tm, d), row)]
    out_shape = [jax.ShapeDtypeStruct((b, l, d), F32), jax.ShapeDtypeStruct((b, l, d), F32 if moe else BF16)]
    aliases = {}
    if moe:
        total_rows, row_offset, filled = into
        in_specs.append(pl.BlockSpec((2, d, LANES), lambda bi, i: (0, 0, 0)))
        args.append(router_pad)
        flat_row = lambda bi, i: (row_offset // tm + bi * (l // tm) + i, 0)
        widths = (d, d, LANES)
        out_specs = [pl.BlockSpec((tm, w), flat_row) for w in widths]
        out_shape = [jax.ShapeDtypeStruct((total_rows, w), F32) for w in widths]
        if filled is not None:
            aliases = {len(args) + j: j for j in range(len(filled))}
            in_specs += [pl.BlockSpec(memory_space=pl.ANY)] * len(filled)
            args += list(filled)
    return pl.pallas_call(
        functools.partial(_outproj_kernel, moe=moe), name="outproj",
        grid=(b, l // tm), in_specs=in_specs, out_specs=out_specs, out_shape=out_shape,
        input_output_aliases=aliases,
        compiler_params=_params("arbitrary", "arbitrary"),
    )(*args)


def _ffn_kernel(u_ref, wg_ref, wu_ref, wd_ref, h_ref, gt_ref, o_ref, acc_ref):
    f = pl.program_id(2)

    @pl.when(f == 0)
    def _():
        acc_ref[...] = jnp.zeros_like(acc_ref)

    u = u_ref[0]
    hid = _silu(_dot(u, wg_ref[...])) * _dot(u, wu_ref[...])
    acc_ref[...] += _dot(hid.astype(BF16), wd_ref[...])

    @pl.when(f == pl.num_programs(2) - 1)
    def _():
        o_ref[0] = h_ref[0] + gt_ref[0] * acc_ref[...]


def _ffn(u, wg, wu, wd, h, gt):
    b, l, d = h.shape
    dff = wg.shape[1]
    tm = min(l, 512)
    tf = dff // 2
    row = lambda bi, i, f: (bi, i, 0)
    return pl.pallas_call(
        _ffn_kernel, name="ffn",
        grid=(b, l // tm, dff // tf),
        in_specs=[pl.BlockSpec((1, tm, d), row),
                  pl.BlockSpec((d, tf), lambda bi, i, f: (0, f)), pl.BlockSpec((d, tf), lambda bi, i, f: (0, f)),
                  pl.BlockSpec((tf, d), lambda bi, i, f: (f, 0)),
                  pl.BlockSpec((1, tm, d), row), pl.BlockSpec((1, 1, d), lambda bi, i, f: (bi, 0, 0))],
        out_specs=pl.BlockSpec((1, tm, d), row),
        out_shape=jax.ShapeDtypeStruct((b, l, d), F32),
        scratch_shapes=[pltpu.VMEM((tm, d), F32)],
        compiler_params=_params("arbitrary", "arbitrary", "arbitrary"),
    )(u, wg, wu, wd, h, gt)


MOE_TILE = 512
SC_CHUNK = 64


def _sc_gather(table, idx):
    info = plsc.get_sparse_core_info()
    nc, ns = info.num_cores, info.num_subcores
    n, d = idx.shape[0], table.shape[1]
    per_worker = n // (nc * ns)
    assert per_worker * nc * ns == n and per_worker % SC_CHUNK == 0
    mesh = plsc.VectorSubcoreMesh(core_axis_name="c", subcore_axis_name="s")

    @functools.partial(
        pl.kernel, mesh=mesh,
        out_type=jax.ShapeDtypeStruct((n, d), table.dtype),
        scratch_types=[pltpu.VMEM((SC_CHUNK,), jnp.int32), pltpu.VMEM((SC_CHUNK, d), table.dtype),
                       pltpu.SemaphoreType.DMA],
    )
    def gather_kernel(table_hbm, idx_hbm, out_hbm, idx_v, rows_v, sem):
        base = (lax.axis_index("s") * nc + lax.axis_index("c")) * per_worker

        @pl.loop(0, per_worker // SC_CHUNK)
        def _(j):
            off = pl.multiple_of(base + j * SC_CHUNK, 8)
            pltpu.sync_copy(idx_hbm.at[pl.ds(off, SC_CHUNK)], idx_v)
            pltpu.async_copy(table_hbm.at[idx_v], rows_v, sem).wait()
            pltpu.sync_copy(rows_v, out_hbm.at[pl.ds(off, SC_CHUNK)])

    return gather_kernel(table, idx)


def _route_plan(route, n_rows):
    t = route.shape[0]
    e = route[:, :2].astype(jnp.int32).T.reshape(2 * t)
    onehot = (e[:, None] == jnp.arange(N_EXPERTS, dtype=jnp.int32)[None, :]).astype(jnp.int32)
    csum = jnp.cumsum(onehot, axis=0)
    counts = csum[-1]
    rank = jnp.take_along_axis(csum, e[:, None], axis=1)[:, 0] - 1
    padded = (counts + MOE_TILE - 1) // MOE_TILE * MOE_TILE
    ends = jnp.cumsum(padded)
    pos = (ends - padded)[e] + rank
    row_token = jnp.zeros((n_rows,), jnp.int32).at[pos].set(
        jnp.arange(2 * t, dtype=jnp.int32) % t, unique_indices=True, mode="promise_in_bounds")
    tile_start = jnp.arange(n_rows // MOE_TILE, dtype=jnp.int32) * MOE_TILE
    tile_expert = jnp.minimum(jnp.sum((tile_start[:, None] >= ends[None, :]).astype(jnp.int32), axis=1),
                              N_EXPERTS - 1)
    n_used = (ends[-1] // MOE_TILE).reshape(1)
    return pos, row_token, tile_expert, n_used


def _gmm_kernel(te_ref, nu_ref, x_ref, wg_ref, wu_ref, wd_ref, o_ref, acc_ref):
    t, f = pl.program_id(0), pl.program_id(1)

    @pl.when(t < nu_ref[0])
    def _():
        @pl.when(f == 0)
        def _():
            acc_ref[...] = jnp.zeros_like(acc_ref)

        x = x_ref[...].astype(BF16)
        hid = _silu(_dot(x, wg_ref[0])) * _dot(x, wu_ref[0])
        acc_ref[...] += _dot(hid.astype(BF16), wd_ref[0])

        @pl.when(f == pl.num_programs(1) - 1)
        def _():
            o_ref[...] = acc_ref[...]


def _grouped_swiglu(x_sorted, tile_expert, n_used, wg, wu, wd):
    r, d = x_sorted.shape
    dff = wg.shape[2]
    n_f = 2
    tf = dff // n_f
    fidx = lambda t, f, te, nu: jnp.where(t < nu[0], f, n_f - 1)
    return pl.pallas_call(
        _gmm_kernel, name="grouped_swiglu",
        grid_spec=pltpu.PrefetchScalarGridSpec(
            num_scalar_prefetch=2,
            grid=(r // MOE_TILE, n_f),
            in_specs=[pl.BlockSpec((MOE_TILE, d), lambda t, f, te, nu: (t, 0)),
                      pl.BlockSpec((1, d, tf), lambda t, f, te, nu: (te[t], 0, fidx(t, f, te, nu))),
                      pl.BlockSpec((1, d, tf), lambda t, f, te, nu: (te[t], 0, fidx(t, f, te, nu))),
                      pl.BlockSpec((1, tf, d), lambda t, f, te, nu: (te[t], fidx(t, f, te, nu), 0))],
            out_specs=pl.BlockSpec((MOE_TILE, d), lambda t, f, te, nu: (t, 0)),
            scratch_shapes=[pltpu.VMEM((MOE_TILE, d), F32)]),
        out_shape=jax.ShapeDtypeStruct((r, d), F32),
        compiler_params=_params("arbitrary", "arbitrary"),
    )(tile_expert, n_used, x_sorted, wg, wu, wd)


def _moe_combine_kernel(y1_ref, y2_ref, rt_ref, h_ref, gt_ref, o_ref):
    rt = rt_ref[...]
    out = rt[:, 2:3] * y1_ref[...] + rt[:, 3:4] * y2_ref[...]
    o_ref[...] = h_ref[...] + gt_ref[0] * out


def _moe_combine(y_pair, route, h, gt_tab, tile_vec, row0, n):
    t, d = h.shape
    tm = 512
    first = row0 // tm
    return pl.pallas_call(
        _moe_combine_kernel, name="moe_combine",
        grid=(n // tm,),
        in_specs=[pl.BlockSpec((tm, d), lambda i: (first + i, 0)),
                  pl.BlockSpec((tm, d), lambda i: (first + i + t // tm, 0)),
                  pl.BlockSpec((tm, LANES), lambda i: (first + i, 0)),
                  pl.BlockSpec((tm, d), lambda i: (first + i, 0)),
                  pl.BlockSpec((1, 1, d), lambda i: (tile_vec(first + i, tm), 0, 0))],
        out_specs=pl.BlockSpec((tm, d), lambda i: (i, 0)),
        out_shape=jax.ShapeDtypeStruct((n, d), F32),
        compiler_params=_params("arbitrary"),
    )(y_pair, y_pair, route, h, gt_tab)


def _moe(u, route, wg, wu, wd, h, gt_tab, tile_vec, splits):
    t, d = u.shape
    n_rows = 2 * t + N_EXPERTS * MOE_TILE
    pos, row_token, tile_expert, n_used = _route_plan(route, n_rows)
    x_sorted = _sc_gather(u, row_token)
    y_sorted = _grouped_swiglu(x_sorted, tile_expert, n_used, wg, wu, wd)
    y_pair = _sc_gather(y_sorted, pos)
    return [_moe_combine(y_pair, route, h, gt_tab, tile_vec, row0, n) for row0, n in splits]


def _final_norm_kernel(h_ref, g_ref, o_ref):
    o_ref[0] = _rms(h_ref[0], g_ref[...])


def _final_norm(h, g):
    b, l, d = h.shape
    tm = min(l, 1024)
    return pl.pallas_call(
        _final_norm_kernel, name="final_norm",
        grid=(b, l // tm),
        in_specs=[pl.BlockSpec((1, tm, d), lambda bi, i: (bi, i, 0)), pl.BlockSpec((1, d), lambda bi, i: (0, 0))],
        out_specs=pl.BlockSpec((1, tm, d), lambda bi, i: (bi, i, 0)),
        out_shape=jax.ShapeDtypeStruct((b, l, d), F32),
        compiler_params=_params("arbitrary", "arbitrary"),
    )(h, g.reshape(1, d))


def _rope_tables(length, head_dim):
    t = jnp.arange(length)
    row = (t // GRID_W).astype(F32)
    col = (t % GRID_W).astype(F32)
    n = head_dim // 4
    inv = ROPE_THETA ** (-jnp.arange(n, dtype=F32) / n)
    ang = jnp.concatenate([row[:, None] * inv, col[:, None] * inv], axis=-1)
    cos = jnp.concatenate([jnp.cos(ang)] * 2, axis=-1)
    sin = jnp.concatenate([-jnp.sin(ang), jnp.sin(ang)], axis=-1)
    reps = LANES // head_dim
    return jnp.tile(cos, (1, reps)), jnp.tile(sin, (1, reps))


def _hyena_tables(length):
    k = jnp.arange(length, dtype=jnp.int32)
    period = 4 * length
    step = 2.0 * math.pi / period
    lo = jnp.arange(64, dtype=jnp.int32)
    hi = jnp.arange(length // 64, dtype=jnp.int32)

    def tabs(x, y_hi, y_lo):
        def cs(y):
            ang = ((x[:, None] * y[None, :]) % period).astype(F32) * step
            return jnp.cos(ang)[:, :, None], jnp.sin(ang)[:, :, None]
        (ca, sa), (cb, sb) = cs(y_hi), cs(y_lo)
        cb, sb = jnp.swapaxes(cb, 1, 2), jnp.swapaxes(sb, 1, 2)
        c = (ca * cb - sa * sb).reshape(length, length)
        s = (sa * cb + ca * sb).reshape(length, length)
        return c.astype(BF16), s.astype(BF16)

    c_tab, s_tab = tabs(2 * k + 1, 64 * hi, lo)
    ci_tab, si_tab = tabs(k, 128 * hi, 2 * lo + 1)
    t = jnp.linspace(0.0, 1.0, length, dtype=F32)[:, None]
    bands = (HY_EMB - 1) // 2
    ang = (2.0 * math.pi / length) * jnp.arange(length, dtype=F32)[:, None] \
        * jnp.linspace(1e-4, bands - 1, bands, dtype=F32)
    feats = jnp.concatenate([t, jnp.cos(ang), -jnp.sin(ang),
                             jnp.zeros((length, LANES - HY_EMB), F32)], axis=-1)
    max_decay = math.log(HY_DECAY_TARGET) / HY_FAST_DECAY
    min_decay = math.log(HY_DECAY_TARGET) / HY_SLOW_DECAY
    deltas = jnp.abs(jnp.linspace(min_decay, max_decay, HY_WIDTH, dtype=F32)).reshape(1, HY_WIDTH)
    return c_tab, s_tab, ci_tab, si_tab, feats, deltas


def _block_diag_ones(n, group):
    i = jnp.arange(n) // group
    return (i[:, None] == i[None, :]).astype(BF16)


def kernel(x, c, ctx, c_ctx, w_ada, b_ada, g_mix, g_ffn, w_in, w_out, hy_conv_w, hy_conv_b, hy_w1, hy_b1, hy_w2, hy_b2, hy_w3, hy_freq, hy_skip, diff_lq1, diff_lk1, diff_lq2, diff_lk2, diff_subln, gqa_qnorm, gqa_knorm, ffn_wg, ffn_wu, ffn_wd, moe_router, moe_wg, moe_wu, moe_wd, g_final):
    b, seq, d = x.shape
    n_ctx = ctx.shape[1]
    depth = w_ada.shape[0]

    rope_tabs = _rope_tables(seq, DIFF_QK_DIM) + _rope_tables(seq, GQA_HEAD_DIM)
    hy_tabs = _hyena_tables(seq)
    hy_tabs_c = _hyena_tables(n_ctx)
    bd128 = _block_diag_ones(LANES, GQA_HEAD_DIM)
    bd256 = _block_diag_ones(DIFF_WIDTH, DIFF_V_DIM)

    rows = 16
    cc = jnp.zeros((rows, d), F32).at[:b].set(c).at[b].set(c_ctx)
    mods = _mods(cc, w_ada, b_ada)

    def mod_vecs(i):
        lat = [mods[i, :b, j * d:(j + 1) * d].reshape(b, 1, d) for j in range(6)]
        cx = [jnp.broadcast_to(mods[i, b, j * d:(j + 1) * d].reshape(1, 1, d), (b, 1, d)) for j in range(6)]
        return lat, cx

    h, hc = x, ctx
    for i in range(depth):
        last = i == depth - 1
        lam_init = 0.8 - 0.6 * math.exp(-0.3 * i)
        (sh_m, sc_m, gt_m, sh_f, sc_f, gt_f), (csh_m, csc_m, cgt_m, csh_f, csc_f, cgt_f) = mod_vecs(i)
        w_in_bf = w_in[i].astype(BF16)
        w_out_bf = w_out[i].astype(BF16)
        g_m = g_mix[i].reshape(1, d)
        g_f = g_ffn[i].reshape(1, d)
        qn = jnp.tile(gqa_qnorm[i], LANES // GQA_HEAD_DIM).reshape(1, LANES)
        kn = jnp.tile(gqa_knorm[i], LANES // GQA_HEAD_DIM).reshape(1, LANES)
        subln = jnp.tile(diff_subln[i], DIFF_HEADS).reshape(1, DIFF_WIDTH) * (1.0 - lam_init)
        lqk = jnp.stack([diff_lq1[i], diff_lk1[i], diff_lq2[i], diff_lk2[i]]).astype(F32)
        hy_p = (hy_conv_w[i], hy_conv_b[i], hy_w1[i], hy_b1[i], hy_w2[i], hy_b2[i], hy_w3[i],
                hy_freq[i], hy_skip[i])
        moe = i % 2 == 1
        router_pad = None
        if moe:
            router = jnp.zeros((d, LANES), F32).at[:, :N_EXPERTS].set(moe_router[i // 2])
            router_hi = router.astype(BF16)
            router_pad = jnp.stack([router_hi, (router - router_hi.astype(F32)).astype(BF16)])

        zhy, qd, kdt, vdx, qg, kgt, vgx = _inproj(h, sc_m, sh_m, g_m, w_in_bf, qn, kn, bd128, rope_tabs)
        czhy, cqd, ckdt, cvdx, cqg, ckgt, cvgx = _inproj(hc, csc_m, csh_m, g_m, w_in_bf, qn, kn, bd128, None)

        y_hy = _hyena(zhy, hy_p, hy_tabs)
        o_d = _diff_attention(lqk, qd, [kdt, ckdt], [vdx, cvdx], lam_init)
        o_g = _gqa_attention(qg, [kgt, ckgt], [vgx, cvgx])
        if not last:
            yc_hy = _hyena(czhy, hy_p, hy_tabs_c)
            oc_d = _diff_attention(lqk, cqd, [ckdt], [cvdx], lam_init)
            oc_g = _gqa_attention(cqg, [ckgt], [cvgx])

        if moe:
            n_c = 0 if last else b * n_ctx
            total = n_c + b * seq
            filled = None
            if not last:
                filled = _outproj(yc_hy, oc_d, oc_g, hc, cgt_m, w_out_bf, g_f, csc_f, csh_f, subln, bd256,
                                  router_pad, (total, 0, None))
            h_all, u_all, route = _outproj(y_hy, o_d, o_g, h, gt_m, w_out_bf, g_f, sc_f, sh_f, subln, bd256,
                                           router_pad, (total, n_c, filled))
            wg, wu, wd = (w[i // 2].astype(BF16) for w in (moe_wg, moe_wu, moe_wd))
            gt_tab = jnp.concatenate([gt_f, cgt_f[:1]], axis=0)
            tile_vec = lambda t, tm: jnp.where(t < n_c // tm, b, (t - n_c // tm) // (seq // tm))
            parts = _moe(u_all, route, wg, wu, wd, h_all, gt_tab, tile_vec,
                         [(n_c, b * seq)] + ([] if last else [(0, n_c)]))
            h = parts[0].reshape(b, seq, d)
            if not last:
                hc = parts[1].reshape(b, n_ctx, d)
        else:
            wg, wu, wd = (w[i // 2].astype(BF16) for w in (ffn_wg, ffn_wu, ffn_wd))
            mixed = _outproj(y_hy, o_d, o_g, h, gt_m, w_out_bf, g_f, sc_f, sh_f, subln, bd256)
            h = _ffn(mixed[1], wg, wu, wd, mixed[0], gt_f)
            if not last:
                cmixed = _outproj(yc_hy, oc_d, oc_g, hc, cgt_m, w_out_bf, g_f, csc_f, csh_f, subln, bd256)
                hc = _ffn(cmixed[1], wg, wu, wd, cmixed[0], cgt_f)
    return _final_norm(h, g_final)
```

```python
import functools
import math

import jax
import jax.numpy as jnp
from jax import lax
from jax.experimental import pallas as pl
from jax.experimental.pallas import tpu as pltpu
from jax.experimental.pallas import tpu_sc as plsc

F32 = jnp.float32
BF16 = jnp.bfloat16
HIGHEST = lax.Precision.HIGHEST

D_MODEL = 1024
GRID_W = 64
ROPE_THETA = 10000.0
NORM_EPS = 1e-6

HY_WIDTH = 256
HY_EMB = 33
HY_FAST_DECAY = 0.3
HY_SLOW_DECAY = 1.5
HY_DECAY_TARGET = 1e-2

DIFF_HEADS = 4
DIFF_QK_DIM = 32
DIFF_V_DIM = 64
DIFF_WIDTH = 256
GQA_HEAD_DIM = 64
GQA_WIDTH = 512
GQA_KV_WIDTH = 128
GQA_REP = 4

OFF_DQ = 3 * HY_WIDTH
OFF_DK = OFF_DQ + DIFF_WIDTH
OFF_DV = OFF_DK + DIFF_WIDTH
OFF_GQ = OFF_DV + DIFF_WIDTH
OFF_GK = OFF_GQ + GQA_WIDTH
OFF_GV = OFF_GK + GQA_KV_WIDTH
IN_COLS = OFF_GV + GQA_KV_WIDTH

N_EXPERTS = 8
LOG2_E = math.log2(math.e)
LANES = 128
VMEM_LIMIT = 56 * 1024 * 1024


def _params(*sem):
    return pltpu.CompilerParams(dimension_semantics=sem, vmem_limit_bytes=VMEM_LIMIT)


def _dot(a, b):
    return jnp.dot(a, b, preferred_element_type=F32)


def _dot_hi(a, b):
    return jnp.dot(a, b, preferred_element_type=F32, precision=HIGHEST)


def _rms(x, g):
    ms = jnp.mean(x * x, axis=-1, keepdims=True)
    return x * lax.rsqrt(ms + NORM_EPS) * g


def _silu(x):
    return x * jax.nn.sigmoid(x)


def _group_mean_sq(x, ones_bd, width):
    return _dot((x * x).astype(BF16), ones_bd) * (1.0 / width)


def _mods_kernel(c_ref, w_ref, b_ref, o_ref):
    o_ref[0] = _dot_hi(_silu(c_ref[...]), w_ref[0]) + b_ref[0]


def _mods(cc, w_ada, b_ada):
    depth, d, n = w_ada.shape
    tn = 1536
    return pl.pallas_call(
        _mods_kernel, name="mods",
        grid=(depth, n // tn),
        in_specs=[pl.BlockSpec(cc.shape, lambda i, j: (0, 0)),
                  pl.BlockSpec((1, d, tn), lambda i, j: (i, 0, j)),
                  pl.BlockSpec((1, 1, tn), lambda i, j: (i, 0, j))],
        out_specs=pl.BlockSpec((1, cc.shape[0], tn), lambda i, j: (i, 0, j)),
        out_shape=jax.ShapeDtypeStruct((depth, cc.shape[0], n), F32),
        compiler_params=_params("arbitrary", "arbitrary"),
    )(cc, w_ada, b_ada.reshape(depth, 1, n))


def _rope128(x, cos, sin_signed, half):
    lane = lax.broadcasted_iota(jnp.int32, x.shape, 1)
    first = (lane & (2 * half - 1)) < half
    swapped = jnp.where(first, pltpu.roll(x, LANES - half, 1), pltpu.roll(x, half, 1))
    return x * cos + swapped * sin_signed


def _inproj_kernel(*refs, rope):
    h_ref, sc_ref, sh_ref, g_ref, w_ref, qn_ref, kn_ref, bd_ref = refs[:8]
    if rope:
        cd_ref, sd_ref, cg_ref, sg_ref = refs[8:12]
    zhy_ref, qd_ref, kd_ref, vd_ref, qg_ref, kg_ref, vg_ref = refs[-7:]
    u = _rms(h_ref[0], g_ref[...]) * (1.0 + sc_ref[0]) + sh_ref[0]
    z = _dot(u.astype(BF16), w_ref[...])
    zhy_ref[0] = z[:, :OFF_DQ]

    def piece(off, j):
        return z[:, off + LANES * j: off + LANES * (j + 1)]

    for j in range(DIFF_WIDTH // LANES):
        q, k = piece(OFF_DQ, j), piece(OFF_DK, j)
        if rope:
            q = _rope128(q, cd_ref[...], sd_ref[...], DIFF_QK_DIM // 2)
            k = _rope128(k, cd_ref[...], sd_ref[...], DIFF_QK_DIM // 2)
        qd_ref[0, :, LANES * j: LANES * (j + 1)] = (q * (LOG2_E * DIFF_QK_DIM ** -0.5)).astype(BF16)
        kd_ref[0, 0, LANES * j: LANES * (j + 1), :] = k.T.astype(BF16)
    vd_ref[0, 0] = z[:, OFF_DV:OFF_GQ].astype(BF16)

    def gqa_piece(x, gain):
        ms = _group_mean_sq(x, bd_ref[...], GQA_HEAD_DIM)
        x = x * lax.rsqrt(ms + NORM_EPS) * gain
        if rope:
            x = _rope128(x, cg_ref[...], sg_ref[...], GQA_HEAD_DIM // 2)
        return x

    for j in range(GQA_WIDTH // LANES):
        q = gqa_piece(piece(OFF_GQ, j), qn_ref[...])
        qg_ref[0, :, LANES * j: LANES * (j + 1)] = (q * (LOG2_E * GQA_HEAD_DIM ** -0.5)).astype(BF16)
    kt = gqa_piece(piece(OFF_GK, 0), kn_ref[...]).T.astype(BF16)
    v = z[:, OFF_GV:]
    v_swapped = pltpu.roll(v, GQA_HEAD_DIM, 1)
    low = lax.broadcasted_iota(jnp.int32, v.shape, 1) < GQA_HEAD_DIM
    v_rep = (jnp.where(low, v, v_swapped), jnp.where(low, v_swapped, v))
    for g in range(GQA_KV_WIDTH // GQA_HEAD_DIM):
        for r in range(GQA_REP):
            kg_ref[0, g, GQA_HEAD_DIM * r: GQA_HEAD_DIM * (r + 1), :] = kt[GQA_HEAD_DIM * g: GQA_HEAD_DIM * (g + 1), :]
        for half in range(GQA_REP * GQA_HEAD_DIM // LANES):
            vg_ref[0, g, :, LANES * half: LANES * (half + 1)] = v_rep[g].astype(BF16)


def _inproj(h, sc, sh, g, w_bf, qn, kn, bd, rope_tabs):
    b, l, d = h.shape
    tm = min(l, 512)
    row = lambda bi, i: (bi, i, 0)
    vec = lambda bi, i: (bi, 0, 0)
    const = lambda bi, i: (0, 0)
    in_specs = [pl.BlockSpec((1, tm, d), row), pl.BlockSpec((1, 1, d), vec), pl.BlockSpec((1, 1, d), vec),
                pl.BlockSpec((1, d), const), pl.BlockSpec((d, IN_COLS), const),
                pl.BlockSpec((1, LANES), const), pl.BlockSpec((1, LANES), const),
                pl.BlockSpec((LANES, LANES), const)]
    args = [h, sc, sh, g, w_bf, qn, kn, bd]
    if rope_tabs is not None:
        in_specs += [pl.BlockSpec((tm, LANES), lambda bi, i: (i, 0))] * 4
        args += list(rope_tabs)
    aw = ATTN_WIDTH
    kv_groups = GQA_KV_WIDTH // GQA_HEAD_DIM
    rows_spec = lambda w: pl.BlockSpec((1, tm, w), row)
    kt_spec = lambda g: pl.BlockSpec((1, g, aw, tm), lambda bi, i: (bi, 0, 0, i))
    v_spec = lambda g: pl.BlockSpec((1, g, tm, aw), lambda bi, i: (bi, 0, i, 0))
    sds = jax.ShapeDtypeStruct
    return pl.pallas_call(
        functools.partial(_inproj_kernel, rope=rope_tabs is not None), name="inproj",
        grid=(b, l // tm),
        in_specs=in_specs,
        out_specs=[rows_spec(OFF_DQ), rows_spec(DIFF_WIDTH), kt_spec(1), v_spec(1),
                   rows_spec(GQA_WIDTH), kt_spec(kv_groups), v_spec(kv_groups)],
        out_shape=[sds((b, l, OFF_DQ), F32), sds((b, l, DIFF_WIDTH), BF16), sds((b, 1, aw, l), BF16),
                   sds((b, 1, l, aw), BF16), sds((b, l, GQA_WIDTH), BF16), sds((b, kv_groups, aw, l), BF16),
                   sds((b, kv_groups, l, aw), BF16)],
        compiler_params=_params("arbitrary", "arbitrary"),
    )(*args)


def _hy_filter_kernel(f_ref, w1_ref, b1_ref, w2_ref, b2_ref, w3_ref, fr_ref, dl_ref, hs_ref, hd_ref):
    f = f_ref[...]
    fr = fr_ref[...]
    a = jnp.sin(fr * (_dot_hi(f, w1_ref[...]) + b1_ref[...]))
    a = jnp.sin(fr * (_dot_hi(a, w2_ref[...]) + b2_ref[...]))
    hf = _dot_hi(a, w3_ref[...])
    decay = jnp.exp(-f[:, 0:1] * dl_ref[...])
    h_fwd = hf[:, :HY_WIDTH] * decay
    h_bwd = hf[:, HY_WIDTH:] * decay
    row = lax.broadcasted_iota(jnp.int32, h_bwd.shape, 0) + pl.program_id(0) * f.shape[0]
    h_bwd = jnp.where(row == 0, 0.0, h_bwd)
    hs_ref[...] = (h_fwd + h_bwd).astype(BF16)
    hd_ref[...] = (h_bwd - h_fwd).astype(BF16)


def _hy_spectrum_kernel(c_ref, s_ref, hs_ref, hd_ref, gre_ref, gim_ref, *, scale):
    gre_ref[...] = _dot(c_ref[...], hs_ref[...]) * scale
    gim_ref[...] = _dot(s_ref[...], hd_ref[...]) * scale


def _hy_pre_kernel(z_ref, zp_ref, zn_ref, cw_ref, cb_ref, wb_ref, wf_ref, x0_ref, *, nt):
    ti = pl.program_id(1)
    z = z_ref[0]
    tl = z.shape[0]
    row = lax.broadcasted_iota(jnp.int32, z.shape, 0)
    prev_row = jnp.where(ti > 0, zp_ref[0, 7:8, :], 0.0)
    next_row = jnp.where(ti < nt - 1, zn_ref[0, 0:1, :], 0.0)
    z_prev = jnp.where(row == 0, prev_row, pltpu.roll(z, 1, 0))
    z_next = jnp.where(row == tl - 1, next_row, pltpu.roll(z, tl - 1, 0))
    cw = cw_ref[...]
    y = cb_ref[...] + z_prev * cw[0:1] + z * cw[1:2] + z_next * cw[2:3]
    x0, x1, v = y[:, :HY_WIDTH], y[:, HY_WIDTH:2 * HY_WIDTH], y[:, 2 * HY_WIDTH:]
    w = v * x1
    wb_ref[...] = w.astype(BF16)
    wf_ref[0] = w
    x0_ref[0] = x0


def _hy_fwd_kernel(c_ref, s_ref, x_ref, gre_ref, gim_ref, yre_ref, yim_ref, *, reps):
    x = x_ref[...]
    a = _dot(c_ref[...], x)
    b = _dot(s_ref[...], x)
    gre = jnp.concatenate([gre_ref[...]] * reps, axis=1)
    gim = jnp.concatenate([gim_ref[...]] * reps, axis=1)
    yre_ref[...] = (a * gre + b * gim).astype(BF16)
    yim_ref[...] = (a * gim - b * gre).astype(BF16)


def _hy_inv_kernel(ci_ref, si_ref, yre_ref, yim_ref, wf_ref, x0_ref, skip_ref, o_ref, *, reps):
    y = _dot(ci_ref[...], yre_ref[...]) - _dot(si_ref[...], yim_ref[...])
    for r in range(reps):
        yr = y[:, r * HY_WIDTH:(r + 1) * HY_WIDTH]
        o_ref[r] = ((yr + wf_ref[r] * skip_ref[...]) * x0_ref[r]).astype(BF16)


def _hyena(zhy, p, tabs):
    conv_w, conv_b, w1, b1, w2, b2, w3, freq, skip = p
    c_tab, s_tab, ci_tab, si_tab, feats, deltas = tabs
    b, l, _ = zhy.shape
    c = HY_WIDTH
    tl = min(l, 512)
    nt = l // tl
    hid = w2.shape[0]
    const1 = lambda i: (0, 0)
    w1p = jnp.zeros((LANES, hid), F32).at[:HY_EMB].set(w1)
    hs, hd = pl.pallas_call(
        _hy_filter_kernel, name="hy_filter",
        grid=(nt,),
        in_specs=[pl.BlockSpec((tl, LANES), lambda i: (i, 0)),
                  pl.BlockSpec((LANES, hid), const1), pl.BlockSpec((1, hid), const1),
                  pl.BlockSpec((hid, hid), const1), pl.BlockSpec((1, hid), const1),
                  pl.BlockSpec((hid, 2 * c), const1), pl.BlockSpec((1, hid), const1),
                  pl.BlockSpec((1, c), const1)],
        out_specs=[pl.BlockSpec((tl, c), lambda i: (i, 0))] * 2,
        out_shape=[jax.ShapeDtypeStruct((l, c), BF16)] * 2,
        compiler_params=_params("arbitrary"),
    )(feats, w1p, b1.reshape(1, hid), w2, b2.reshape(1, hid), w3, freq.reshape(1, hid), deltas)

    gre, gim = pl.pallas_call(
        functools.partial(_hy_spectrum_kernel, scale=1.0 / l), name="hy_spectrum",
        grid=(nt,),
        in_specs=[pl.BlockSpec((tl, l), lambda i: (i, 0)), pl.BlockSpec((tl, l), lambda i: (i, 0)),
                  pl.BlockSpec((l, c), const1), pl.BlockSpec((l, c), const1)],
        out_specs=[pl.BlockSpec((tl, c), lambda i: (i, 0))] * 2,
        out_shape=[jax.ShapeDtypeStruct((l, c), F32)] * 2,
        compiler_params=_params("arbitrary"),
    )(c_tab, s_tab, hs, hd)

    halo = 8
    wb, wf, x0 = pl.pallas_call(
        functools.partial(_hy_pre_kernel, nt=nt), name="hy_pre",
        grid=(b, nt),
        in_specs=[pl.BlockSpec((1, tl, 3 * c), lambda bi, i: (bi, i, 0)),
                  pl.BlockSpec((1, halo, 3 * c), lambda bi, i: (bi, jnp.maximum(i * (tl // halo) - 1, 0), 0)),
                  pl.BlockSpec((1, halo, 3 * c),
                               lambda bi, i: (bi, jnp.minimum((i + 1) * (tl // halo), l // halo - 1), 0)),
                  pl.BlockSpec((3, 3 * c), lambda bi, i: (0, 0)),
                  pl.BlockSpec((1, 3 * c), lambda bi, i: (0, 0))],
        out_specs=[pl.BlockSpec((tl, c), lambda bi, i: (i, bi)),
                   pl.BlockSpec((1, tl, c), lambda bi, i: (bi, i, 0)),
                   pl.BlockSpec((1, tl, c), lambda bi, i: (bi, i, 0))],
        out_shape=[jax.ShapeDtypeStruct((l, b * c), BF16),
                   jax.ShapeDtypeStruct((b, l, c), F32),
                   jax.ShapeDtypeStruct((b, l, c), F32)],
        compiler_params=_params("arbitrary", "arbitrary"),
    )(zhy, zhy, zhy, conv_w, conv_b.reshape(1, 3 * c))

    reps = 2
    tn = reps * c
    nj = b * c // tn
    yre, yim = pl.pallas_call(
        functools.partial(_hy_fwd_kernel, reps=reps), name="hy_fwd",
        grid=(nj, nt),
        in_specs=[pl.BlockSpec((tl, l), lambda j, i: (i, 0)), pl.BlockSpec((tl, l), lambda j, i: (i, 0)),
                  pl.BlockSpec((l, tn), lambda j, i: (0, j)),
                  pl.BlockSpec((tl, c), lambda j, i: (i, 0)), pl.BlockSpec((tl, c), lambda j, i: (i, 0))],
        out_specs=[pl.BlockSpec((tl, tn), lambda j, i: (i, j))] * 2,
        out_shape=[jax.ShapeDtypeStruct((l, b * c), BF16)] * 2,
        compiler_params=_params("arbitrary", "arbitrary"),
    )(c_tab, s_tab, wb, gre, gim)

    return pl.pallas_call(
        functools.partial(_hy_inv_kernel, reps=reps), name="hy_inv",
        grid=(nj, nt),
        in_specs=[pl.BlockSpec((tl, l), lambda j, i: (i, 0)), pl.BlockSpec((tl, l), lambda j, i: (i, 0)),
                  pl.BlockSpec((l, tn), lambda j, i: (0, j)), pl.BlockSpec((l, tn), lambda j, i: (0, j)),
                  pl.BlockSpec((reps, tl, c), lambda j, i: (j, i, 0)),
                  pl.BlockSpec((reps, tl, c), lambda j, i: (j, i, 0)),
                  pl.BlockSpec((1, c), lambda j, i: (0, 0))],
        out_specs=pl.BlockSpec((reps, tl, c), lambda j, i: (j, i, 0)),
        out_shape=jax.ShapeDtypeStruct((b, l, c), BF16),
        compiler_params=_params("arbitrary", "arbitrary"),
    )(ci_tab, si_tab, yre, yim, wf, x0, skip.reshape(1, c))


def _lane_mask(width, shift, idx):
    lane = lax.broadcasted_iota(jnp.int32, (1, width), 1)
    return jnp.where((lane >> shift) == idx, 1.0, 0.0).astype(BF16)


ATTN_WIDTH = 256
ATTN_CHUNK = 512


def _key_chunks(lk):
    return [(c0, min(ATTN_CHUNK, lk - c0)) for c0 in range(0, lk, ATTN_CHUNK)]


ATTN_STACK = 4
ATTN_TQ = 128
ATTN_STEP_ROWS = 512


def _qk_stage(q, shift, first, kt_refs, qs_scr, s_scr):
    tq = q.shape[0]
    for i in range(ATTN_STACK):
        qs_scr[i * tq:(i + 1) * tq, :] = q * _lane_mask(ATTN_WIDTH, shift, first + i)
    qs = qs_scr[...]
    m = jnp.full((ATTN_STACK * tq, LANES), -jnp.inf, F32)
    base = 0
    for kt_ref in kt_refs:
        for c0, ck in _key_chunks(kt_ref.shape[-1]):
            s = _dot(qs, kt_ref[:, c0:c0 + ck])
            s_scr[:, base + c0:base + c0 + ck] = s
            for j in range(ck // LANES):
                m = jnp.maximum(m, s[:, j * LANES:(j + 1) * LANES])
        base += kt_ref.shape[-1]
    return jnp.max(m, axis=-1, keepdims=True)


def _exp_stage(s_scr, mx, p_scr):
    l = jnp.zeros((s_scr.shape[0], LANES), F32)
    for c0, ck in _key_chunks(s_scr.shape[-1]):
        e = jnp.exp2(s_scr[:, c0:c0 + ck] - mx)
        p_scr[:, c0:c0 + ck] = e.astype(BF16)
        for j in range(ck // LANES):
            l = l + e[:, j * LANES:(j + 1) * LANES]
    return 1.0 / jnp.sum(l, axis=-1, keepdims=True)


def _attention_passes(passes, consume, kt_refs, v_refs, scratch):
    qs, ss, ps = scratch[0:2], scratch[2:4], scratch[4:6]
    tq = passes[0][0].shape[0]

    def qk(i):
        q, shift, first, g = passes[i]
        return _qk_stage(q, shift, first, [kt.at[0, g] for kt in kt_refs], qs[i % 2], ss[i % 2])

    mx = qk(0)
    for i in range(len(passes)):
        mx_next = qk(i + 1) if i + 1 < len(passes) else None
        inv = _exp_stage(ss[i % 2], mx, ps[i % 2])
        pv, base = None, 0
        for v_ref in v_refs:
            lk = v_ref.shape[2]
            part = _dot(ps[i % 2][:, base:base + lk], v_ref[0, passes[i][3]])
            pv = part if pv is None else pv + part
            base += lk
        pv = pv * inv
        consume(i, [pv[r * tq:(r + 1) * tq] for r in range(ATTN_STACK)])
        mx = mx_next


def _diff_kernel(lqk_ref, q_ref, *refs, n_kv, lam_init):
    kt_refs, v_refs, o_ref, scratch = refs[:n_kv], refs[n_kv:2 * n_kv], refs[2 * n_kv], refs[2 * n_kv + 1:]
    x = lqk_ref[...]
    lam = (jnp.exp(jnp.sum(x[0:1] * x[1:2], axis=-1, keepdims=True))
           - jnp.exp(jnp.sum(x[2:3] * x[3:4], axis=-1, keepdims=True)) + lam_init)
    tq = min(ATTN_TQ, q_ref.shape[1])
    group = lax.broadcasted_iota(jnp.int32, (tq, ATTN_WIDTH), 1) >> 6
    halves = 2 * DIFF_HEADS // ATTN_STACK
    passes = []
    for t in range(q_ref.shape[1] // tq):
        q = q_ref[0, t * tq:(t + 1) * tq, :]
        passes += [(q, 5, half * ATTN_STACK, 0) for half in range(halves)]
    acc = {}

    def consume(i, pvs):
        t, half = divmod(i, halves)
        o1, o2 = acc.pop(t, (jnp.zeros(group.shape, F32),) * 2)
        for r, pv in enumerate(pvs):
            head = (half * ATTN_STACK + r) >> 1
            if r & 1 == 0:
                o1 = jnp.where(group == head, pv, o1)
            else:
                o2 = jnp.where(group == head, pv, o2)
        if half + 1 < halves:
            acc[t] = (o1, o2)
            return
        o_ref[0, t * tq:(t + 1) * tq, :] = (o1 - lam * o2).astype(BF16)

    _attention_passes(passes, consume, kt_refs, v_refs, scratch)


def _gqa_kernel(q_ref, *refs, n_kv):
    kt_refs, v_refs, o_ref, scratch = refs[:n_kv], refs[n_kv:2 * n_kv], refs[2 * n_kv], refs[2 * n_kv + 1:]
    tq = min(ATTN_TQ, q_ref.shape[1])
    group = lax.broadcasted_iota(jnp.int32, (tq, ATTN_WIDTH), 1) >> 6
    groups = q_ref.shape[2] // ATTN_WIDTH
    passes = []
    for t in range(q_ref.shape[1] // tq):
        for g in range(groups):
            passes.append((q_ref[0, t * tq:(t + 1) * tq, g * ATTN_WIDTH:(g + 1) * ATTN_WIDTH], 6, 0, g))

    def consume(i, pvs):
        t, g = divmod(i, groups)
        o = jnp.zeros(group.shape, F32)
        for r, pv in enumerate(pvs):
            o = jnp.where(group == r, pv, o)
        o_ref[0, t * tq:(t + 1) * tq, g * ATTN_WIDTH:(g + 1) * ATTN_WIDTH] = o.astype(BF16)

    _attention_passes(passes, consume, kt_refs, v_refs, scratch)


def _attention(kernel_fn, name, q, kts, vs, extra=()):
    b, lq, width = q.shape
    groups, w = kts[0].shape[1:3]
    lk = sum(kt.shape[3] for kt in kts)
    rows = min(lq, ATTN_STEP_ROWS)
    pass_rows = ATTN_STACK * min(ATTN_TQ, rows)
    kv_spec = lambda a: pl.BlockSpec((1,) + a.shape[1:], lambda bi, i: (bi, 0, 0, 0), pipeline_mode=pl.Buffered(1))
    return pl.pallas_call(
        functools.partial(kernel_fn, n_kv=len(kts)), name=name,
        grid=(b, lq // rows),
        in_specs=[pl.BlockSpec(a.shape, lambda bi, i: (0, 0)) for a in extra]
        + [pl.BlockSpec((1, rows, width), lambda bi, i: (bi, i, 0))]
        + [kv_spec(a) for a in kts] + [kv_spec(a) for a in vs],
        out_specs=pl.BlockSpec((1, rows, width), lambda bi, i: (bi, i, 0)),
        out_shape=jax.ShapeDtypeStruct((b, lq, width), BF16),
        scratch_shapes=[pltpu.VMEM((pass_rows, w), BF16)] * 2 + [pltpu.VMEM((pass_rows, lk), F32)] * 2
        + [pltpu.VMEM((pass_rows, lk), BF16)] * 2,
        compiler_params=_params("arbitrary", "arbitrary"),
    )(*extra, q, *kts, *vs)


def _diff_attention(lqk, q, kts, vs, lam_init):
    return _attention(functools.partial(_diff_kernel, lam_init=lam_init), "diff_attn", q, kts, vs, (lqk,))


def _gqa_attention(q, kts, vs):
    return _attention(_gqa_kernel, "gqa_attn", q, kts, vs)


def _top2_route(logits):
    lane = lax.broadcasted_iota(jnp.int32, logits.shape, 1).astype(F32)
    neg = jnp.float32(-jnp.inf)
    lg = jnp.where(lane < N_EXPERTS, logits, neg)
    m1 = jnp.max(lg, axis=-1, keepdims=True)
    i1 = jnp.min(jnp.where(lg == m1, lane, float(LANES)), axis=-1, keepdims=True)
    lg2 = jnp.where(lane == i1, neg, lg)
    m2 = jnp.max(lg2, axis=-1, keepdims=True)
    i2 = jnp.min(jnp.where(lg2 == m2, lane, float(LANES)), axis=-1, keepdims=True)
    e2 = jnp.exp(m2 - m1)
    w1 = 1.0 / (1.0 + e2)
    return (jnp.where(lane == 0, i1, 0.0) + jnp.where(lane == 1, i2, 0.0)
            + jnp.where(lane == 2, w1, 0.0) + jnp.where(lane == 3, e2 * w1, 0.0))


def _outproj_kernel(*refs, moe):
    yhy_ref, od_ref, og_ref, h_ref, gt_ref, w_ref, gf_ref, sc_ref, sh_ref, sg_ref, bd_ref = refs[:11]
    if moe:
        rt_ref = refs[11]
        hn_ref, u_ref, gates_ref = refs[-3:]
    else:
        hn_ref, u_ref = refs[11:]
    od = od_ref[0].astype(F32)
    od = od * lax.rsqrt(_group_mean_sq(od, bd_ref[...], DIFF_V_DIM) + NORM_EPS) * sg_ref[...]
    y = (_dot(yhy_ref[0], w_ref[0:HY_WIDTH, :])
         + _dot(od.astype(BF16), w_ref[HY_WIDTH:HY_WIDTH + DIFF_WIDTH, :])
         + _dot(og_ref[0], w_ref[HY_WIDTH + DIFF_WIDTH:, :]))
    hn = h_ref[0] + gt_ref[0] * y
    hn_ref[...] = hn.reshape(hn_ref.shape)
    u = _rms(hn, gf_ref[...]) * (1.0 + sc_ref[0]) + sh_ref[0]
    u_ref[...] = u.astype(u_ref.dtype).reshape(u_ref.shape)
    if moe:
        u_hi = u.astype(BF16)
        u_lo = (u - u_hi.astype(F32)).astype(BF16)
        logits = _dot(u_hi, rt_ref[0]) + (_dot(u_lo, rt_ref[0]) + _dot(u_hi, rt_ref[1]))
        gates_ref[...] = _top2_route(logits)


def _outproj(yhy, od, og, h, gt, w_bf, gf, sc, sh, sub_g, bd256, router_pad=None, into=None):
    b, l, d = h.shape
    tm = min(l, 512)
    row = lambda bi, i: (bi, i, 0)
    vec = lambda bi, i: (bi, 0, 0)
    const = lambda bi, i: (0, 0)
    moe = router_pad is not None
    in_specs = [pl.BlockSpec((1, tm, HY_WIDTH), row), pl.BlockSpec((1, tm, DIFF_WIDTH), row),
                pl.BlockSpec((1, tm, GQA_WIDTH), row), pl.BlockSpec((1, tm, d), row),
                pl.BlockSpec((1, 1, d), vec), pl.BlockSpec((d, d), const), pl.BlockSpec((1, d), const),
                pl.BlockSpec((1, 1, d), vec), pl.BlockSpec((1, 1, d), vec),
                pl.BlockSpec((1, DIFF_WIDTH), const), pl.BlockSpec((DIFF_WIDTH, DIFF_WIDTH), const)]
    args = [yhy, od, og, h, gt, w_bf, gf, sc, sh, sub_g, bd256]
    out_specs = [pl.BlockSpec((1, tm, d), row), pl.BlockSpec((1, tm, d), row)]
    out_shape = [jax.ShapeDtypeStruct((b, l, d), F32), jax.ShapeDtypeStruct((b, l, d), F32 if moe else BF16)]
    aliases = {}
    if moe:
        total_rows, row_offset, filled = into
        in_specs.append(pl.BlockSpec((2, d, LANES), lambda bi, i: (0, 0, 0)))
        args.append(router_pad)
        flat_row = lambda bi, i: (row_offset // tm + bi * (l // tm) + i, 0)
        widths = (d, d, LANES)
        out_specs = [pl.BlockSpec((tm, w), flat_row) for w in widths]
        out_shape = [jax.ShapeDtypeStruct((total_rows, w), F32) for w in widths]
        if filled is not None:
            aliases = {len(args) + j: j for j in range(len(filled))}
            in_specs += [pl.BlockSpec(memory_space=pl.ANY)] * len(filled)
            args += list(filled)
    return pl.pallas_call(
        functools.partial(_outproj_kernel, moe=moe), name="outproj",
        grid=(b, l // tm), in_specs=in_specs, out_specs=out_specs, out_shape=out_shape,
        input_output_aliases=aliases,
        compiler_params=_params("arbitrary", "arbitrary"),
    )(*args)


def _ffn_kernel(u_ref, wg_ref, wu_ref, wd_ref, h_ref, gt_ref, o_ref):
    u = u_ref[0]
    hid = _silu(_dot(u, wg_ref[...])) * _dot(u, wu_ref[...])
    o_ref[0] = h_ref[0] + gt_ref[0] * _dot(hid.astype(BF16), wd_ref[...])


def _ffn(u, wg, wu, wd, h, gt):
    b, l, d = h.shape
    tm = min(l, 512)
    row = lambda bi, i: (bi, i, 0)
    resident = lambda a: pl.BlockSpec(a.shape, lambda bi, i: (0, 0), pipeline_mode=pl.Buffered(1))
    return pl.pallas_call(
        _ffn_kernel, name="ffn",
        grid=(b, l // tm),
        in_specs=[pl.BlockSpec((1, tm, d), row), resident(wg), resident(wu), resident(wd),
                  pl.BlockSpec((1, tm, d), row), pl.BlockSpec((1, 1, d), lambda bi, i: (bi, 0, 0))],
        out_specs=pl.BlockSpec((1, tm, d), row),
        out_shape=jax.ShapeDtypeStruct((b, l, d), F32),
        compiler_params=_params("arbitrary", "arbitrary"),
    )(u, wg, wu, wd, h, gt)


MOE_TILE = 512
SC_CHUNK = 64


def _sc_gather(table, idx):
    info = plsc.get_sparse_core_info()
    nc, ns = info.num_cores, info.num_subcores
    n, d = idx.shape[0], table.shape[1]
    per_worker = n // (nc * ns)
    assert per_worker * nc * ns == n and per_worker % SC_CHUNK == 0
    mesh = plsc.VectorSubcoreMesh(core_axis_name="c", subcore_axis_name="s")

    @functools.partial(
        pl.kernel, mesh=mesh,
        out_type=jax.ShapeDtypeStruct((n, d), table.dtype),
        scratch_types=[pltpu.VMEM((SC_CHUNK,), jnp.int32), pltpu.VMEM((SC_CHUNK, d), table.dtype),
                       pltpu.SemaphoreType.DMA],
    )
    def gather_kernel(table_hbm, idx_hbm, out_hbm, idx_v, rows_v, sem):
        base = (lax.axis_index("s") * nc + lax.axis_index("c")) * per_worker

        @pl.loop(0, per_worker // SC_CHUNK)
        def _(j):
            off = pl.multiple_of(base + j * SC_CHUNK, 8)
            pltpu.sync_copy(idx_hbm.at[pl.ds(off, SC_CHUNK)], idx_v)
            pltpu.async_copy(table_hbm.at[idx_v], rows_v, sem).wait()
            pltpu.sync_copy(rows_v, out_hbm.at[pl.ds(off, SC_CHUNK)])

    return gather_kernel(table, idx)


def _route_plan(route, n_rows):
    t = route.shape[0]
    e = route[:, :2].astype(jnp.int32).T.reshape(2 * t)
    onehot = (e[:, None] == jnp.arange(N_EXPERTS, dtype=jnp.int32)[None, :]).astype(jnp.int32)
    csum = jnp.cumsum(onehot, axis=0)
    counts = csum[-1]
    rank = jnp.take_along_axis(csum, e[:, None], axis=1)[:, 0] - 1
    padded = (counts + MOE_TILE - 1) // MOE_TILE * MOE_TILE
    ends = jnp.cumsum(padded)
    pos = (ends - padded)[e] + rank
    row_token = jnp.zeros((n_rows,), jnp.int32).at[pos].set(jnp.arange(2 * t, dtype=jnp.int32) % t)
    tile_start = jnp.arange(n_rows // MOE_TILE, dtype=jnp.int32) * MOE_TILE
    tile_expert = jnp.minimum(jnp.sum((tile_start[:, None] >= ends[None, :]).astype(jnp.int32), axis=1),
                              N_EXPERTS - 1)
    n_used = (ends[-1] // MOE_TILE).reshape(1)
    return pos, row_token, tile_expert, n_used


def _gmm_kernel(te_ref, nu_ref, x_ref, wg_ref, wu_ref, wd_ref, o_ref, acc_ref):
    t, f = pl.program_id(0), pl.program_id(1)

    @pl.when(t < nu_ref[0])
    def _():
        @pl.when(f == 0)
        def _():
            acc_ref[...] = jnp.zeros_like(acc_ref)

        x = x_ref[...].astype(BF16)
        hid = _silu(_dot(x, wg_ref[0])) * _dot(x, wu_ref[0])
        acc_ref[...] += _dot(hid.astype(BF16), wd_ref[0])

        @pl.when(f == pl.num_programs(1) - 1)
        def _():
            o_ref[...] = acc_ref[...]


def _grouped_swiglu(x_sorted, tile_expert, n_used, wg, wu, wd):
    r, d = x_sorted.shape
    dff = wg.shape[2]
    n_f = 2
    tf = dff // n_f
    fidx = lambda t, f, te, nu: jnp.where(t < nu[0], f, n_f - 1)
    return pl.pallas_call(
        _gmm_kernel, name="grouped_swiglu",
        grid_spec=pltpu.PrefetchScalarGridSpec(
            num_scalar_prefetch=2,
            grid=(r // MOE_TILE, n_f),
            in_specs=[pl.BlockSpec((MOE_TILE, d), lambda t, f, te, nu: (t, 0)),
                      pl.BlockSpec((1, d, tf), lambda t, f, te, nu: (te[t], 0, fidx(t, f, te, nu))),
                      pl.BlockSpec((1, d, tf), lambda t, f, te, nu: (te[t], 0, fidx(t, f, te, nu))),
                      pl.BlockSpec((1, tf, d), lambda t, f, te, nu: (te[t], fidx(t, f, te, nu), 0))],
            out_specs=pl.BlockSpec((MOE_TILE, d), lambda t, f, te, nu: (t, 0)),
            scratch_shapes=[pltpu.VMEM((MOE_TILE, d), F32)]),
        out_shape=jax.ShapeDtypeStruct((r, d), F32),
        compiler_params=_params("arbitrary", "arbitrary"),
    )(tile_expert, n_used, x_sorted, wg, wu, wd)


def _moe_combine_kernel(y1_ref, y2_ref, rt_ref, h_ref, gt_ref, o_ref):
    rt = rt_ref[...]
    out = rt[:, 2:3] * y1_ref[...] + rt[:, 3:4] * y2_ref[...]
    o_ref[...] = h_ref[...] + gt_ref[0] * out


def _moe_combine(y_pair, route, h, gt_tab, tile_vec, row0, n):
    t, d = h.shape
    tm = 512
    first = row0 // tm
    return pl.pallas_call(
        _moe_combine_kernel, name="moe_combine",
        grid=(n // tm,),
        in_specs=[pl.BlockSpec((tm, d), lambda i: (first + i, 0)),
                  pl.BlockSpec((tm, d), lambda i: (first + i + t // tm, 0)),
                  pl.BlockSpec((tm, LANES), lambda i: (first + i, 0)),
                  pl.BlockSpec((tm, d), lambda i: (first + i, 0)),
                  pl.BlockSpec((1, 1, d), lambda i: (tile_vec(first + i, tm), 0, 0))],
        out_specs=pl.BlockSpec((tm, d), lambda i: (i, 0)),
        out_shape=jax.ShapeDtypeStruct((n, d), F32),
        compiler_params=_params("arbitrary"),
    )(y_pair, y_pair, route, h, gt_tab)


def _moe(u, route, wg, wu, wd, h, gt_tab, tile_vec, splits):
    t, d = u.shape
    n_rows = 2 * t + N_EXPERTS * MOE_TILE
    pos, row_token, tile_expert, n_used = _route_plan(route, n_rows)
    x_sorted = _sc_gather(u, row_token)
    y_sorted = _grouped_swiglu(x_sorted, tile_expert, n_used, wg, wu, wd)
    y_pair = _sc_gather(y_sorted, pos)
    return [_moe_combine(y_pair, route, h, gt_tab, tile_vec, row0, n) for row0, n in splits]


def _final_norm_kernel(h_ref, g_ref, o_ref):
    o_ref[0] = _rms(h_ref[0], g_ref[...])


def _final_norm(h, g):
    b, l, d = h.shape
    tm = min(l, 1024)
    return pl.pallas_call(
        _final_norm_kernel, name="final_norm",
        grid=(b, l // tm),
        in_specs=[pl.BlockSpec((1, tm, d), lambda bi, i: (bi, i, 0)), pl.BlockSpec((1, d), lambda bi, i: (0, 0))],
        out_specs=pl.BlockSpec((1, tm, d), lambda bi, i: (bi, i, 0)),
        out_shape=jax.ShapeDtypeStruct((b, l, d), F32),
        compiler_params=_params("arbitrary", "arbitrary"),
    )(h, g.reshape(1, d))


def _rope_tables(length, head_dim):
    t = jnp.arange(length)
    row = (t // GRID_W).astype(F32)
    col = (t % GRID_W).astype(F32)
    n = head_dim // 4
    inv = ROPE_THETA ** (-jnp.arange(n, dtype=F32) / n)
    ang = jnp.concatenate([row[:, None] * inv, col[:, None] * inv], axis=-1)
    cos = jnp.concatenate([jnp.cos(ang)] * 2, axis=-1)
    sin = jnp.concatenate([-jnp.sin(ang), jnp.sin(ang)], axis=-1)
    reps = LANES // head_dim
    return jnp.tile(cos, (1, reps)), jnp.tile(sin, (1, reps))


def _hyena_tables(length):
    k = jnp.arange(length, dtype=jnp.int32)
    period = 4 * length
    step = 2.0 * math.pi / period
    lo = jnp.arange(64, dtype=jnp.int32)
    hi = jnp.arange(length // 64, dtype=jnp.int32)

    def tabs(x, y_hi, y_lo):
        def cs(y):
            ang = ((x[:, None] * y[None, :]) % period).astype(F32) * step
            return jnp.cos(ang)[:, :, None], jnp.sin(ang)[:, :, None]
        (ca, sa), (cb, sb) = cs(y_hi), cs(y_lo)
        cb, sb = jnp.swapaxes(cb, 1, 2), jnp.swapaxes(sb, 1, 2)
        c = (ca * cb - sa * sb).reshape(length, length)
        s = (sa * cb + ca * sb).reshape(length, length)
        return c.astype(BF16), s.astype(BF16)

    c_tab, s_tab = tabs(2 * k + 1, 64 * hi, lo)
    ci_tab, si_tab = tabs(k, 128 * hi, 2 * lo + 1)
    t = jnp.linspace(0.0, 1.0, length, dtype=F32)[:, None]
    bands = (HY_EMB - 1) // 2
    ang = (2.0 * math.pi / length) * jnp.arange(length, dtype=F32)[:, None] \
        * jnp.linspace(1e-4, bands - 1, bands, dtype=F32)
    feats = jnp.concatenate([t, jnp.cos(ang), -jnp.sin(ang),
                             jnp.zeros((length, LANES - HY_EMB), F32)], axis=-1)
    max_decay = math.log(HY_DECAY_TARGET) / HY_FAST_DECAY
    min_decay = math.log(HY_DECAY_TARGET) / HY_SLOW_DECAY
    deltas = jnp.abs(jnp.linspace(min_decay, max_decay, HY_WIDTH, dtype=F32)).reshape(1, HY_WIDTH)
    return c_tab, s_tab, ci_tab, si_tab, feats, deltas


def _block_diag_ones(n, group):
    i = jnp.arange(n) // group
    return (i[:, None] == i[None, :]).astype(BF16)


def kernel(x, c, ctx, c_ctx, w_ada, b_ada, g_mix, g_ffn, w_in, w_out, hy_conv_w, hy_conv_b, hy_w1, hy_b1, hy_w2, hy_b2, hy_w3, hy_freq, hy_skip, diff_lq1, diff_lk1, diff_lq2, diff_lk2, diff_subln, gqa_qnorm, gqa_knorm, ffn_wg, ffn_wu, ffn_wd, moe_router, moe_wg, moe_wu, moe_wd, g_final):
    b, seq, d = x.shape
    n_ctx = ctx.shape[1]
    depth = w_ada.shape[0]

    rope_tabs = _rope_tables(seq, DIFF_QK_DIM) + _rope_tables(seq, GQA_HEAD_DIM)
    hy_tabs = _hyena_tables(seq)
    hy_tabs_c = _hyena_tables(n_ctx)
    bd128 = _block_diag_ones(LANES, GQA_HEAD_DIM)
    bd256 = _block_diag_ones(DIFF_WIDTH, DIFF_V_DIM)

    rows = 16
    cc = jnp.zeros((rows, d), F32).at[:b].set(c).at[b].set(c_ctx)
    mods = _mods(cc, w_ada, b_ada)

    def mod_vecs(i):
        lat = [mods[i, :b, j * d:(j + 1) * d].reshape(b, 1, d) for j in range(6)]
        cx = [jnp.broadcast_to(mods[i, b, j * d:(j + 1) * d].reshape(1, 1, d), (b, 1, d)) for j in range(6)]
        return lat, cx

    h, hc = x, ctx
    for i in range(depth):
        last = i == depth - 1
        lam_init = 0.8 - 0.6 * math.exp(-0.3 * i)
        (sh_m, sc_m, gt_m, sh_f, sc_f, gt_f), (csh_m, csc_m, cgt_m, csh_f, csc_f, cgt_f) = mod_vecs(i)
        w_in_bf = w_in[i].astype(BF16)
        w_out_bf = w_out[i].astype(BF16)
        g_m = g_mix[i].reshape(1, d)
        g_f = g_ffn[i].reshape(1, d)
        qn = jnp.tile(gqa_qnorm[i], LANES // GQA_HEAD_DIM).reshape(1, LANES)
        kn = jnp.tile(gqa_knorm[i], LANES // GQA_HEAD_DIM).reshape(1, LANES)
        subln = jnp.tile(diff_subln[i], DIFF_HEADS).reshape(1, DIFF_WIDTH) * (1.0 - lam_init)
        lqk = jnp.stack([diff_lq1[i], diff_lk1[i], diff_lq2[i], diff_lk2[i]]).astype(F32)
        hy_p = (hy_conv_w[i], hy_conv_b[i], hy_w1[i], hy_b1[i], hy_w2[i], hy_b2[i], hy_w3[i],
                hy_freq[i], hy_skip[i])
        moe = i % 2 == 1
        router_pad = None
        if moe:
            router = jnp.zeros((d, LANES), F32).at[:, :N_EXPERTS].set(moe_router[i // 2])
            router_hi = router.astype(BF16)
            router_pad = jnp.stack([router_hi, (router - router_hi.astype(F32)).astype(BF16)])

        zhy, qd, kdt, vdx, qg, kgt, vgx = _inproj(h, sc_m, sh_m, g_m, w_in_bf, qn, kn, bd128, rope_tabs)
        czhy, cqd, ckdt, cvdx, cqg, ckgt, cvgx = _inproj(hc, csc_m, csh_m, g_m, w_in_bf, qn, kn, bd128, None)

        y_hy = _hyena(zhy, hy_p, hy_tabs)
        o_d = _diff_attention(lqk, qd, [kdt, ckdt], [vdx, cvdx], lam_init)
        o_g = _gqa_attention(qg, [kgt, ckgt], [vgx, cvgx])
        if not last:
            yc_hy = _hyena(czhy, hy_p, hy_tabs_c)
            oc_d = _diff_attention(lqk, cqd, [ckdt], [cvdx], lam_init)
            oc_g = _gqa_attention(cqg, [ckgt], [cvgx])

        if moe:
            n_c = 0 if last else b * n_ctx
            total = n_c + b * seq
            filled = None
            if not last:
                filled = _outproj(yc_hy, oc_d, oc_g, hc, cgt_m, w_out_bf, g_f, csc_f, csh_f, subln, bd256,
                                  router_pad, (total, 0, None))
            h_all, u_all, route = _outproj(y_hy, o_d, o_g, h, gt_m, w_out_bf, g_f, sc_f, sh_f, subln, bd256,
                                           router_pad, (total, n_c, filled))
            wg, wu, wd = (w[i // 2].astype(BF16) for w in (moe_wg, moe_wu, moe_wd))
            gt_tab = jnp.concatenate([gt_f, cgt_f[:1]], axis=0)
            tile_vec = lambda t, tm: jnp.where(t < n_c // tm, b, (t - n_c // tm) // (seq // tm))
            parts = _moe(u_all, route, wg, wu, wd, h_all, gt_tab, tile_vec,
                         [(n_c, b * seq)] + ([] if last else [(0, n_c)]))
            h = parts[0].reshape(b, seq, d)
            if not last:
                hc = parts[1].reshape(b, n_ctx, d)
        else:
            wg, wu, wd = (w[i // 2].astype(BF16) for w in (ffn_wg, ffn_wu, ffn_wd))
            mixed = _outproj(y_hy, o_d, o_g, h, gt_m, w_out_bf, g_f, sc_f, sh_f, subln, bd256)
            h = _ffn(mixed[1], wg, wu, wd, mixed[0], gt_f)
            if not last:
                cmixed = _outproj(yc_hy, oc_d, oc_g, hc, cgt_m, w_out_bf, g_f, csc_f, csh_f, subln, bd256)
                hc = _ffn(cmixed[1], wg, wu, wd, cmixed[0], cgt_f)
    return _final_norm(h, g_final)
```

```python
import functools
import math

import jax
import jax.numpy as jnp
from jax import lax
from jax.experimental import pallas as pl
from jax.experimental.pallas import tpu as pltpu
from jax.experimental.pallas import tpu_sc as plsc

F32 = jnp.float32
BF16 = jnp.bfloat16
HIGHEST = lax.Precision.HIGHEST

D_MODEL = 1024
GRID_W = 64
ROPE_THETA = 10000.0
NORM_EPS = 1e-6

HY_WIDTH = 256
HY_EMB = 33
HY_FAST_DECAY = 0.3
HY_SLOW_DECAY = 1.5
HY_DECAY_TARGET = 1e-2

DIFF_HEADS = 4
DIFF_QK_DIM = 32
DIFF_V_DIM = 64
DIFF_WIDTH = 256
GQA_HEAD_DIM = 64
GQA_WIDTH = 512
GQA_KV_WIDTH = 128
GQA_REP = 4

OFF_DQ = 3 * HY_WIDTH
OFF_DK = OFF_DQ + DIFF_WIDTH
OFF_DV = OFF_DK + DIFF_WIDTH
OFF_GQ = OFF_DV + DIFF_WIDTH
OFF_GK = OFF_GQ + GQA_WIDTH
OFF_GV = OFF_GK + GQA_KV_WIDTH
IN_COLS = OFF_GV + GQA_KV_WIDTH

N_EXPERTS = 8
LOG2_E = math.log2(math.e)
LANES = 128
VMEM_LIMIT = 56 * 1024 * 1024


def _params(*sem):
    return pltpu.CompilerParams(dimension_semantics=sem, vmem_limit_bytes=VMEM_LIMIT)


def _dot(a, b):
    return jnp.dot(a, b, preferred_element_type=F32)


def _dot_hi(a, b):
    return jnp.dot(a, b, preferred_element_type=F32, precision=HIGHEST)


def _rms(x, g):
    ms = jnp.mean(x * x, axis=-1, keepdims=True)
    return x * lax.rsqrt(ms + NORM_EPS) * g


def _silu(x):
    return x * jax.nn.sigmoid(x)


def _group_mean_sq(x, ones_bd, width):
    return _dot((x * x).astype(BF16), ones_bd) * (1.0 / width)


def _mods_kernel(c_ref, w_ref, b_ref, o_ref):
    o_ref[0] = _dot_hi(_silu(c_ref[...]), w_ref[0]) + b_ref[0]


def _mods(cc, w_ada, b_ada):
    depth, d, n = w_ada.shape
    tn = 1536
    return pl.pallas_call(
        _mods_kernel, name="mods",
        grid=(depth, n // tn),
        in_specs=[pl.BlockSpec(cc.shape, lambda i, j: (0, 0)),
                  pl.BlockSpec((1, d, tn), lambda i, j: (i, 0, j)),
                  pl.BlockSpec((1, 1, tn), lambda i, j: (i, 0, j))],
        out_specs=pl.BlockSpec((1, cc.shape[0], tn), lambda i, j: (i, 0, j)),
        out_shape=jax.ShapeDtypeStruct((depth, cc.shape[0], n), F32),
        compiler_params=_params("arbitrary", "arbitrary"),
    )(cc, w_ada, b_ada.reshape(depth, 1, n))


def _rope128(x, cos, sin_signed, half):
    lane = lax.broadcasted_iota(jnp.int32, x.shape, 1)
    first = (lane & (2 * half - 1)) < half
    swapped = jnp.where(first, pltpu.roll(x, LANES - half, 1), pltpu.roll(x, half, 1))
    return x * cos + swapped * sin_signed


def _inproj_kernel(*refs, rope):
    h_ref, sc_ref, sh_ref, g_ref, w_ref, qn_ref, kn_ref, bd_ref = refs[:8]
    if rope:
        cd_ref, sd_ref, cg_ref, sg_ref = refs[8:12]
    zhy_ref, qd_ref, kd_ref, vd_ref, qg_ref, kg_ref, vg_ref = refs[-7:]
    u = _rms(h_ref[0], g_ref[...]) * (1.0 + sc_ref[0]) + sh_ref[0]
    z = _dot(u.astype(BF16), w_ref[...])
    zhy_ref[0] = z[:, :OFF_DQ]

    def piece(off, j):
        return z[:, off + LANES * j: off + LANES * (j + 1)]

    for j in range(DIFF_WIDTH // LANES):
        q, k = piece(OFF_DQ, j), piece(OFF_DK, j)
        if rope:
            q = _rope128(q, cd_ref[...], sd_ref[...], DIFF_QK_DIM // 2)
            k = _rope128(k, cd_ref[...], sd_ref[...], DIFF_QK_DIM // 2)
        qd_ref[0, :, LANES * j: LANES * (j + 1)] = (q * (LOG2_E * DIFF_QK_DIM ** -0.5)).astype(BF16)
        kd_ref[0, 0, LANES * j: LANES * (j + 1), :] = k.T.astype(BF16)
    vd_ref[0, 0] = z[:, OFF_DV:OFF_GQ].astype(BF16)

    def gqa_piece(x, gain):
        ms = _group_mean_sq(x, bd_ref[...], GQA_HEAD_DIM)
        x = x * lax.rsqrt(ms + NORM_EPS) * gain
        if rope:
            x = _rope128(x, cg_ref[...], sg_ref[...], GQA_HEAD_DIM // 2)
        return x

    for j in range(GQA_WIDTH // LANES):
        q = gqa_piece(piece(OFF_GQ, j), qn_ref[...])
        qg_ref[0, :, LANES * j: LANES * (j + 1)] = (q * (LOG2_E * GQA_HEAD_DIM ** -0.5)).astype(BF16)
    kt = gqa_piece(piece(OFF_GK, 0), kn_ref[...]).T.astype(BF16)
    v = z[:, OFF_GV:]
    v_swapped = pltpu.roll(v, GQA_HEAD_DIM, 1)
    low = lax.broadcasted_iota(jnp.int32, v.shape, 1) < GQA_HEAD_DIM
    v_rep = (jnp.where(low, v, v_swapped), jnp.where(low, v_swapped, v))
    for g in range(GQA_KV_WIDTH // GQA_HEAD_DIM):
        for r in range(GQA_REP):
            kg_ref[0, g, GQA_HEAD_DIM * r: GQA_HEAD_DIM * (r + 1), :] = kt[GQA_HEAD_DIM * g: GQA_HEAD_DIM * (g + 1), :]
        for half in range(GQA_REP * GQA_HEAD_DIM // LANES):
            vg_ref[0, g, :, LANES * half: LANES * (half + 1)] = v_rep[g].astype(BF16)


def _inproj(h, sc, sh, g, w_bf, qn, kn, bd, rope_tabs):
    b, l, d = h.shape
    tm = min(l, 512)
    row = lambda bi, i: (bi, i, 0)
    vec = lambda bi, i: (bi, 0, 0)
    const = lambda bi, i: (0, 0)
    in_specs = [pl.BlockSpec((1, tm, d), row), pl.BlockSpec((1, 1, d), vec), pl.BlockSpec((1, 1, d), vec),
                pl.BlockSpec((1, d), const), pl.BlockSpec((d, IN_COLS), const),
                pl.BlockSpec((1, LANES), const), pl.BlockSpec((1, LANES), const),
                pl.BlockSpec((LANES, LANES), const)]
    args = [h, sc, sh, g, w_bf, qn, kn, bd]
    if rope_tabs is not None:
        in_specs += [pl.BlockSpec((tm, LANES), lambda bi, i: (i, 0))] * 4
        args += list(rope_tabs)
    aw = ATTN_WIDTH
    kv_groups = GQA_KV_WIDTH // GQA_HEAD_DIM
    rows_spec = lambda w: pl.BlockSpec((1, tm, w), row)
    kt_spec = lambda g: pl.BlockSpec((1, g, aw, tm), lambda bi, i: (bi, 0, 0, i))
    v_spec = lambda g: pl.BlockSpec((1, g, tm, aw), lambda bi, i: (bi, 0, i, 0))
    sds = jax.ShapeDtypeStruct
    return pl.pallas_call(
        functools.partial(_inproj_kernel, rope=rope_tabs is not None), name="inproj",
        grid=(b, l // tm),
        in_specs=in_specs,
        out_specs=[rows_spec(OFF_DQ), rows_spec(DIFF_WIDTH), kt_spec(1), v_spec(1),
                   rows_spec(GQA_WIDTH), kt_spec(kv_groups), v_spec(kv_groups)],
        out_shape=[sds((b, l, OFF_DQ), F32), sds((b, l, DIFF_WIDTH), BF16), sds((b, 1, aw, l), BF16),
                   sds((b, 1, l, aw), BF16), sds((b, l, GQA_WIDTH), BF16), sds((b, kv_groups, aw, l), BF16),
                   sds((b, kv_groups, l, aw), BF16)],
        compiler_params=_params("arbitrary", "arbitrary"),
    )(*args)


def _hy_filter_kernel(f_ref, w1_ref, b1_ref, w2_ref, b2_ref, w3_ref, fr_ref, dl_ref, hs_ref, hd_ref):
    f = f_ref[...]
    fr = fr_ref[...]
    a = jnp.sin(fr * (_dot_hi(f, w1_ref[...]) + b1_ref[...]))
    a = jnp.sin(fr * (_dot_hi(a, w2_ref[...]) + b2_ref[...]))
    hf = _dot_hi(a, w3_ref[...])
    decay = jnp.exp(-f[:, 0:1] * dl_ref[...])
    h_fwd = hf[:, :HY_WIDTH] * decay
    h_bwd = hf[:, HY_WIDTH:] * decay
    row = lax.broadcasted_iota(jnp.int32, h_bwd.shape, 0) + pl.program_id(0) * f.shape[0]
    h_bwd = jnp.where(row == 0, 0.0, h_bwd)
    hs_ref[...] = (h_fwd + h_bwd).astype(BF16)
    hd_ref[...] = (h_bwd - h_fwd).astype(BF16)


def _hy_spectrum_kernel(c_ref, s_ref, hs_ref, hd_ref, gre_ref, gim_ref, *, scale):
    gre_ref[...] = _dot(c_ref[...], hs_ref[...]) * scale
    gim_ref[...] = _dot(s_ref[...], hd_ref[...]) * scale


def _hy_pre_kernel(z_ref, zp_ref, zn_ref, cw_ref, cb_ref, wb_ref, wf_ref, x0_ref, *, nt):
    ti = pl.program_id(1)
    z = z_ref[0]
    tl = z.shape[0]
    row = lax.broadcasted_iota(jnp.int32, z.shape, 0)
    prev_row = jnp.where(ti > 0, zp_ref[0, 7:8, :], 0.0)
    next_row = jnp.where(ti < nt - 1, zn_ref[0, 0:1, :], 0.0)
    z_prev = jnp.where(row == 0, prev_row, pltpu.roll(z, 1, 0))
    z_next = jnp.where(row == tl - 1, next_row, pltpu.roll(z, tl - 1, 0))
    cw = cw_ref[...]
    y = cb_ref[...] + z_prev * cw[0:1] + z * cw[1:2] + z_next * cw[2:3]
    x0, x1, v = y[:, :HY_WIDTH], y[:, HY_WIDTH:2 * HY_WIDTH], y[:, 2 * HY_WIDTH:]
    w = v * x1
    wb_ref[...] = w.astype(BF16)
    wf_ref[0] = w
    x0_ref[0] = x0


def _hy_fwd_kernel(c_ref, s_ref, x_ref, gre_ref, gim_ref, yre_ref, yim_ref, *, reps):
    x = x_ref[...]
    a = _dot(c_ref[...], x)
    b = _dot(s_ref[...], x)
    gre = jnp.concatenate([gre_ref[...]] * reps, axis=1)
    gim = jnp.concatenate([gim_ref[...]] * reps, axis=1)
    yre_ref[...] = (a * gre + b * gim).astype(BF16)
    yim_ref[...] = (a * gim - b * gre).astype(BF16)


def _hy_inv_kernel(ci_ref, si_ref, yre_ref, yim_ref, wf_ref, x0_ref, skip_ref, o_ref, *, reps):
    y = _dot(ci_ref[...], yre_ref[...]) - _dot(si_ref[...], yim_ref[...])
    for r in range(reps):
        yr = y[:, r * HY_WIDTH:(r + 1) * HY_WIDTH]
        o_ref[r] = ((yr + wf_ref[r] * skip_ref[...]) * x0_ref[r]).astype(BF16)


def _hyena(zhy, p, tabs):
    conv_w, conv_b, w1, b1, w2, b2, w3, freq, skip = p
    c_tab, s_tab, ci_tab, si_tab, feats, deltas = tabs
    b, l, _ = zhy.shape
    c = HY_WIDTH
    tl = min(l, 512)
    nt = l // tl
    hid = w2.shape[0]
    const1 = lambda i: (0, 0)
    w1p = jnp.zeros((LANES, hid), F32).at[:HY_EMB].set(w1)
    hs, hd = pl.pallas_call(
        _hy_filter_kernel, name="hy_filter",
        grid=(nt,),
        in_specs=[pl.BlockSpec((tl, LANES), lambda i: (i, 0)),
                  pl.BlockSpec((LANES, hid), const1), pl.BlockSpec((1, hid), const1),
                  pl.BlockSpec((hid, hid), const1), pl.BlockSpec((1, hid), const1),
                  pl.BlockSpec((hid, 2 * c), const1), pl.BlockSpec((1, hid), const1),
                  pl.BlockSpec((1, c), const1)],
        out_specs=[pl.BlockSpec((tl, c), lambda i: (i, 0))] * 2,
        out_shape=[jax.ShapeDtypeStruct((l, c), BF16)] * 2,
        compiler_params=_params("arbitrary"),
    )(feats, w1p, b1.reshape(1, hid), w2, b2.reshape(1, hid), w3, freq.reshape(1, hid), deltas)

    gre, gim = pl.pallas_call(
        functools.partial(_hy_spectrum_kernel, scale=1.0 / l), name="hy_spectrum",
        grid=(nt,),
        in_specs=[pl.BlockSpec((tl, l), lambda i: (i, 0)), pl.BlockSpec((tl, l), lambda i: (i, 0)),
                  pl.BlockSpec((l, c), const1), pl.BlockSpec((l, c), const1)],
        out_specs=[pl.BlockSpec((tl, c), lambda i: (i, 0))] * 2,
        out_shape=[jax.ShapeDtypeStruct((l, c), F32)] * 2,
        compiler_params=_params("arbitrary"),
    )(c_tab, s_tab, hs, hd)

    halo = 8
    wb, wf, x0 = pl.pallas_call(
        functools.partial(_hy_pre_kernel, nt=nt), name="hy_pre",
        grid=(b, nt),
        in_specs=[pl.BlockSpec((1, tl, 3 * c), lambda bi, i: (bi, i, 0)),
                  pl.BlockSpec((1, halo, 3 * c), lambda bi, i: (bi, jnp.maximum(i * (tl // halo) - 1, 0), 0)),
                  pl.BlockSpec((1, halo, 3 * c),
                               lambda bi, i: (bi, jnp.minimum((i + 1) * (tl // halo), l // halo - 1), 0)),
                  pl.BlockSpec((3, 3 * c), lambda bi, i: (0, 0)),
                  pl.BlockSpec((1, 3 * c), lambda bi, i: (0, 0))],
        out_specs=[pl.BlockSpec((tl, c), lambda bi, i: (i, bi)),
                   pl.BlockSpec((1, tl, c), lambda bi, i: (bi, i, 0)),
                   pl.BlockSpec((1, tl, c), lambda bi, i: (bi, i, 0))],
        out_shape=[jax.ShapeDtypeStruct((l, b * c), BF16),
                   jax.ShapeDtypeStruct((b, l, c), F32),
                   jax.ShapeDtypeStruct((b, l, c), F32)],
        compiler_params=_params("arbitrary", "arbitrary"),
    )(zhy, zhy, zhy, conv_w, conv_b.reshape(1, 3 * c))

    reps = 2
    tn = reps * c
    nj = b * c // tn
    yre, yim = pl.pallas_call(
        functools.partial(_hy_fwd_kernel, reps=reps), name="hy_fwd",
        grid=(nj, nt),
        in_specs=[pl.BlockSpec((tl, l), lambda j, i: (i, 0)), pl.BlockSpec((tl, l), lambda j, i: (i, 0)),
                  pl.BlockSpec((l, tn), lambda j, i: (0, j)),
                  pl.BlockSpec((tl, c), lambda j, i: (i, 0)), pl.BlockSpec((tl, c), lambda j, i: (i, 0))],
        out_specs=[pl.BlockSpec((tl, tn), lambda j, i: (i, j))] * 2,
        out_shape=[jax.ShapeDtypeStruct((l, b * c), BF16)] * 2,
        compiler_params=_params("arbitrary", "arbitrary"),
    )(c_tab, s_tab, wb, gre, gim)

    return pl.pallas_call(
        functools.partial(_hy_inv_kernel, reps=reps), name="hy_inv",
        grid=(nj, nt),
        in_specs=[pl.BlockSpec((tl, l), lambda j, i: (i, 0)), pl.BlockSpec((tl, l), lambda j, i: (i, 0)),
                  pl.BlockSpec((l, tn), lambda j, i: (0, j)), pl.BlockSpec((l, tn), lambda j, i: (0, j)),
                  pl.BlockSpec((reps, tl, c), lambda j, i: (j, i, 0)),
                  pl.BlockSpec((reps, tl, c), lambda j, i: (j, i, 0)),
                  pl.BlockSpec((1, c), lambda j, i: (0, 0))],
        out_specs=pl.BlockSpec((reps, tl, c), lambda j, i: (j, i, 0)),
        out_shape=jax.ShapeDtypeStruct((b, l, c), BF16),
        compiler_params=_params("arbitrary", "arbitrary"),
    )(ci_tab, si_tab, yre, yim, wf, x0, skip.reshape(1, c))


def _lane_mask(width, shift, idx):
    lane = lax.broadcasted_iota(jnp.int32, (1, width), 1)
    return jnp.where((lane >> shift) == idx, 1.0, 0.0).astype(BF16)


ATTN_WIDTH = 256
ATTN_CHUNK = 512


def _key_chunks(lk):
    return [(c0, min(ATTN_CHUNK, lk - c0)) for c0 in range(0, lk, ATTN_CHUNK)]


ATTN_STACK = 4
ATTN_TQ = 128
ATTN_STEP_ROWS = 512


def _qk_stage(q, shift, first, kt_refs, qs_scr, s_scr):
    tq = q.shape[0]
    for i in range(ATTN_STACK):
        qs_scr[i * tq:(i + 1) * tq, :] = q * _lane_mask(ATTN_WIDTH, shift, first + i)
    qs = qs_scr[...]
    m = jnp.full((ATTN_STACK * tq, LANES), -jnp.inf, F32)
    base = 0
    for kt_ref in kt_refs:
        for c0, ck in _key_chunks(kt_ref.shape[-1]):
            s = _dot(qs, kt_ref[:, c0:c0 + ck])
            s_scr[:, base + c0:base + c0 + ck] = s
            for j in range(ck // LANES):
                m = jnp.maximum(m, s[:, j * LANES:(j + 1) * LANES])
        base += kt_ref.shape[-1]
    return jnp.max(m, axis=-1, keepdims=True)


def _exp_stage(s_scr, mx, p_scr):
    l = jnp.zeros((s_scr.shape[0], LANES), F32)
    for c0, ck in _key_chunks(s_scr.shape[-1]):
        e = jnp.exp2(s_scr[:, c0:c0 + ck] - mx)
        p_scr[:, c0:c0 + ck] = e.astype(BF16)
        for j in range(ck // LANES):
            l = l + e[:, j * LANES:(j + 1) * LANES]
    return 1.0 / jnp.sum(l, axis=-1, keepdims=True)


def _attention_passes(passes, consume, kt_refs, v_refs, scratch):
    qs, ss, ps = scratch[0:2], scratch[2:4], scratch[4:6]
    tq = passes[0][0].shape[0]

    def qk(i):
        q, shift, first, g = passes[i]
        return _qk_stage(q, shift, first, [kt.at[0, g] for kt in kt_refs], qs[i % 2], ss[i % 2])

    mx = qk(0)
    for i in range(len(passes)):
        mx_next = qk(i + 1) if i + 1 < len(passes) else None
        inv = _exp_stage(ss[i % 2], mx, ps[i % 2])
        pv, base = None, 0
        for v_ref in v_refs:
            lk = v_ref.shape[2]
            part = _dot(ps[i % 2][:, base:base + lk], v_ref[0, passes[i][3]])
            pv = part if pv is None else pv + part
            base += lk
        pv = pv * inv
        consume(i, [pv[r * tq:(r + 1) * tq] for r in range(ATTN_STACK)])
        mx = mx_next


def _diff_kernel(lqk_ref, q_ref, *refs, n_kv, lam_init):
    kt_refs, v_refs, o_ref, scratch = refs[:n_kv], refs[n_kv:2 * n_kv], refs[2 * n_kv], refs[2 * n_kv + 1:]
    x = lqk_ref[...]
    lam = (jnp.exp(jnp.sum(x[0:1] * x[1:2], axis=-1, keepdims=True))
           - jnp.exp(jnp.sum(x[2:3] * x[3:4], axis=-1, keepdims=True)) + lam_init)
    tq = min(ATTN_TQ, q_ref.shape[1])
    group = lax.broadcasted_iota(jnp.int32, (tq, ATTN_WIDTH), 1) >> 6
    halves = 2 * DIFF_HEADS // ATTN_STACK
    passes = []
    for t in range(q_ref.shape[1] // tq):
        q = q_ref[0, t * tq:(t + 1) * tq, :]
        passes += [(q, 5, half * ATTN_STACK, 0) for half in range(halves)]
    acc = {}

    def consume(i, pvs):
        t, half = divmod(i, halves)
        o1, o2 = acc.pop(t, (jnp.zeros(group.shape, F32),) * 2)
        for r, pv in enumerate(pvs):
            head = (half * ATTN_STACK + r) >> 1
            if r & 1 == 0:
                o1 = jnp.where(group == head, pv, o1)
            else:
                o2 = jnp.where(group == head, pv, o2)
        if half + 1 < halves:
            acc[t] = (o1, o2)
            return
        o_ref[0, t * tq:(t + 1) * tq, :] = (o1 - lam * o2).astype(BF16)

    _attention_passes(passes, consume, kt_refs, v_refs, scratch)


def _gqa_kernel(q_ref, *refs, n_kv):
    kt_refs, v_refs, o_ref, scratch = refs[:n_kv], refs[n_kv:2 * n_kv], refs[2 * n_kv], refs[2 * n_kv + 1:]
    tq = min(ATTN_TQ, q_ref.shape[1])
    group = lax.broadcasted_iota(jnp.int32, (tq, ATTN_WIDTH), 1) >> 6
    groups = q_ref.shape[2] // ATTN_WIDTH
    passes = []
    for t in range(q_ref.shape[1] // tq):
        for g in range(groups):
            passes.append((q_ref[0, t * tq:(t + 1) * tq, g * ATTN_WIDTH:(g + 1) * ATTN_WIDTH], 6, 0, g))

    def consume(i, pvs):
        t, g = divmod(i, groups)
        o = jnp.zeros(group.shape, F32)
        for r, pv in enumerate(pvs):
            o = jnp.where(group == r, pv, o)
        o_ref[0, t * tq:(t + 1) * tq, g * ATTN_WIDTH:(g + 1) * ATTN_WIDTH] = o.astype(BF16)

    _attention_passes(passes, consume, kt_refs, v_refs, scratch)


def _attention(kernel_fn, name, q, kts, vs, extra=()):
    b, lq, width = q.shape
    groups, w = kts[0].shape[1:3]
    lk = sum(kt.shape[3] for kt in kts)
    rows = min(lq, ATTN_STEP_ROWS)
    pass_rows = ATTN_STACK * min(ATTN_TQ, rows)
    kv_spec = lambda a: pl.BlockSpec((1,) + a.shape[1:], lambda bi, i: (bi, 0, 0, 0), pipeline_mode=pl.Buffered(1))
    return pl.pallas_call(
        functools.partial(kernel_fn, n_kv=len(kts)), name=name,
        grid=(b, lq // rows),
        in_specs=[pl.BlockSpec(a.shape, lambda bi, i: (0, 0)) for a in extra]
        + [pl.BlockSpec((1, rows, width), lambda bi, i: (bi, i, 0))]
        + [kv_spec(a) for a in kts] + [kv_spec(a) for a in vs],
        out_specs=pl.BlockSpec((1, rows, width), lambda bi, i: (bi, i, 0)),
        out_shape=jax.ShapeDtypeStruct((b, lq, width), BF16),
        scratch_shapes=[pltpu.VMEM((pass_rows, w), BF16)] * 2 + [pltpu.VMEM((pass_rows, lk), F32)] * 2
        + [pltpu.VMEM((pass_rows, lk), BF16)] * 2,
        compiler_params=_params("arbitrary", "arbitrary"),
    )(*extra, q, *kts, *vs)


def _diff_attention(lqk, q, kts, vs, lam_init):
    return _attention(functools.partial(_diff_kernel, lam_init=lam_init), "diff_attn", q, kts, vs, (lqk,))


def _gqa_attention(q, kts, vs):
    return _attention(_gqa_kernel, "gqa_attn", q, kts, vs)


def _top2_route(logits):
    lane = lax.broadcasted_iota(jnp.int32, logits.shape, 1).astype(F32)
    neg = jnp.float32(-jnp.inf)
    lg = jnp.where(lane < N_EXPERTS, logits, neg)
    m1 = jnp.max(lg, axis=-1, keepdims=True)
    i1 = jnp.min(jnp.where(lg == m1, lane, float(LANES)), axis=-1, keepdims=True)
    lg2 = jnp.where(lane == i1, neg, lg)
    m2 = jnp.max(lg2, axis=-1, keepdims=True)
    i2 = jnp.min(jnp.where(lg2 == m2, lane, float(LANES)), axis=-1, keepdims=True)
    e2 = jnp.exp(m2 - m1)
    w1 = 1.0 / (1.0 + e2)
    return (jnp.where(lane == 0, i1, 0.0) + jnp.where(lane == 1, i2, 0.0)
            + jnp.where(lane == 2, w1, 0.0) + jnp.where(lane == 3, e2 * w1, 0.0))


def _outproj_kernel(*refs, moe):
    yhy_ref, od_ref, og_ref, h_ref, gt_ref, w_ref, gf_ref, sc_ref, sh_ref, sg_ref, bd_ref = refs[:11]
    if moe:
        rt_ref = refs[11]
        hn_ref, u_ref, gates_ref = refs[-3:]
    else:
        hn_ref, u_ref = refs[11:]
    od = od_ref[0].astype(F32)
    od = od * lax.rsqrt(_group_mean_sq(od, bd_ref[...], DIFF_V_DIM) + NORM_EPS) * sg_ref[...]
    y = (_dot(yhy_ref[0], w_ref[0:HY_WIDTH, :])
         + _dot(od.astype(BF16), w_ref[HY_WIDTH:HY_WIDTH + DIFF_WIDTH, :])
         + _dot(og_ref[0], w_ref[HY_WIDTH + DIFF_WIDTH:, :]))
    hn = h_ref[0] + gt_ref[0] * y
    hn_ref[...] = hn.reshape(hn_ref.shape)
    u = _rms(hn, gf_ref[...]) * (1.0 + sc_ref[0]) + sh_ref[0]
    u_ref[...] = u.astype(u_ref.dtype).reshape(u_ref.shape)
    if moe:
        u_hi = u.astype(BF16)
        u_lo = (u - u_hi.astype(F32)).astype(BF16)
        logits = _dot(u_hi, rt_ref[0]) + (_dot(u_lo, rt_ref[0]) + _dot(u_hi, rt_ref[1]))
        gates_ref[...] = _top2_route(logits)


def _outproj(yhy, od, og, h, gt, w_bf, gf, sc, sh, sub_g, bd256, router_pad=None, into=None):
    b, l, d = h.shape
    tm = min(l, 512)
    row = lambda bi, i: (bi, i, 0)
    vec = lambda bi, i: (bi, 0, 0)
    const = lambda bi, i: (0, 0)
    moe = router_pad is not None
    in_specs = [pl.BlockSpec((1, tm, HY_WIDTH), row), pl.BlockSpec((1, tm, DIFF_WIDTH), row),
                pl.BlockSpec((1, tm, GQA_WIDTH), row), pl.BlockSpec((1, tm, d), row),
                pl.BlockSpec((1, 1, d), vec), pl.BlockSpec((d, d), const), pl.BlockSpec((1, d), const),
                pl.BlockSpec((1, 1, d), vec), pl.BlockSpec((1, 1, d), vec),
                pl.BlockSpec((1, DIFF_WIDTH), const), pl.BlockSpec((DIFF_WIDTH, DIFF_WIDTH), const)]
    args = [yhy, od, og, h, gt, w_bf, gf, sc, sh, sub_g, bd256]
    out_specs = [pl.BlockSpec((1, tm, d), row), pl.BlockSpec((1, tm, d), row)]
    out_shape = [jax.ShapeDtypeStruct((b, l, d), F32), jax.ShapeDtypeStruct((b, l, d), F32 if moe else BF16)]
    aliases = {}
    if moe:
        total_rows, row_offset, filled = into
        in_specs.append(pl.BlockSpec((2, d, LANES), lambda bi, i: (0, 0, 0)))
        args.append(router_pad)
        flat_row = lambda bi, i: (row_offset // tm + bi * (l // tm) + i, 0)
        widths = (d, d, LANES)
        out_specs = [pl.BlockSpec((tm, w), flat_row) for w in widths]
        out_shape = [jax.ShapeDtypeStruct((total_rows, w), F32) for w in widths]
        if filled is not None:
            aliases = {len(args) + j: j for j in range(len(filled))}
            in_specs += [pl.BlockSpec(memory_space=pl.ANY)] * len(filled)
            args += list(filled)
    return pl.pallas_call(
        functools.partial(_outproj_kernel, moe=moe), name="outproj",
        grid=(b, l // tm), in_specs=in_specs, out_specs=out_specs, out_shape=out_shape,
        input_output_aliases=aliases,
        compiler_params=_params("arbitrary", "arbitrary"),
    )(*args)


def _ffn_kernel(u_ref, wg_ref, wu_ref, wd_ref, h_ref, gt_ref, o_ref):
    u = u_ref[0]
    hid = _silu(_dot(u, wg_ref[...])) * _dot(u, wu_ref[...])
    o_ref[0] = h_ref[0] + gt_ref[0] * _dot(hid.astype(BF16), wd_ref[...])


def _ffn(u, wg, wu, wd, h, gt):
    b, l, d = h.shape
    tm = min(l, 512)
    row = lambda bi, i: (bi, i, 0)
    resident = lambda a: pl.BlockSpec(a.shape, lambda bi, i: (0, 0), pipeline_mode=pl.Buffered(1))
    return pl.pallas_call(
        _ffn_kernel, name="ffn",
        grid=(b, l // tm),
        in_specs=[pl.BlockSpec((1, tm, d), row), resident(wg), resident(wu), resident(wd),
                  pl.BlockSpec((1, tm, d), row), pl.BlockSpec((1, 1, d), lambda bi, i: (bi, 0, 0))],
        out_specs=pl.BlockSpec((1, tm, d), row),
        out_shape=jax.ShapeDtypeStruct((b, l, d), F32),
        compiler_params=_params("arbitrary", "arbitrary"),
    )(u, wg, wu, wd, h, gt)


MOE_TILE = 512
SC_CHUNK = 64


def _sc_gather(table, idx):
    info = plsc.get_sparse_core_info()
    nc, ns = info.num_cores, info.num_subcores
    n, d = idx.shape[0], table.shape[1]
    per_worker = n // (nc * ns)
    assert per_worker * nc * ns == n and per_worker % SC_CHUNK == 0
    mesh = plsc.VectorSubcoreMesh(core_axis_name="c", subcore_axis_name="s")

    @functools.partial(
        pl.kernel, mesh=mesh,
        out_type=jax.ShapeDtypeStruct((n, d), table.dtype),
        scratch_types=[pltpu.VMEM((SC_CHUNK,), jnp.int32), pltpu.VMEM((SC_CHUNK, d), table.dtype),
                       pltpu.SemaphoreType.DMA],
    )
    def gather_kernel(table_hbm, idx_hbm, out_hbm, idx_v, rows_v, sem):
        base = (lax.axis_index("s") * nc + lax.axis_index("c")) * per_worker

        @pl.loop(0, per_worker // SC_CHUNK)
        def _(j):
            off = pl.multiple_of(base + j * SC_CHUNK, 8)
            pltpu.sync_copy(idx_hbm.at[pl.ds(off, SC_CHUNK)], idx_v)
            pltpu.async_copy(table_hbm.at[idx_v], rows_v, sem).wait()
            pltpu.sync_copy(rows_v, out_hbm.at[pl.ds(off, SC_CHUNK)])

    return gather_kernel(table, idx)


def _route_plan(route, n_rows):
    t = route.shape[0]
    e = route[:, :2].astype(jnp.int32).T.reshape(2 * t)
    onehot = (e[:, None] == jnp.arange(N_EXPERTS, dtype=jnp.int32)[None, :]).astype(jnp.int32)
    csum = jnp.cumsum(onehot, axis=0)
    counts = csum[-1]
    rank = jnp.take_along_axis(csum, e[:, None], axis=1)[:, 0] - 1
    padded = (counts + MOE_TILE - 1) // MOE_TILE * MOE_TILE
    ends = jnp.cumsum(padded)
    pos = (ends - padded)[e] + rank
    row_token = jnp.zeros((n_rows,), jnp.int32).at[pos].set(jnp.arange(2 * t, dtype=jnp.int32) % t)
    tile_start = jnp.arange(n_rows // MOE_TILE, dtype=jnp.int32) * MOE_TILE
    tile_expert = jnp.minimum(jnp.sum((tile_start[:, None] >= ends[None, :]).astype(jnp.int32), axis=1),
                              N_EXPERTS - 1)
    n_used = (ends[-1] // MOE_TILE).reshape(1)
    return pos, row_token, tile_expert, n_used


def _gmm_kernel(te_ref, nu_ref, x_ref, wg_ref, wu_ref, wd_ref, o_ref):
    @pl.when(pl.program_id(0) < nu_ref[0])
    def _():
        x = x_ref[...].astype(BF16)
        half = wg_ref.shape[2] // 2
        out = None
        for f0 in (0, half):
            hid = _silu(_dot(x, wg_ref[0, :, f0:f0 + half])) * _dot(x, wu_ref[0, :, f0:f0 + half])
            part = _dot(hid.astype(BF16), wd_ref[0, f0:f0 + half, :])
            out = part if out is None else out + part
        o_ref[...] = out


def _grouped_swiglu(x_sorted, tile_expert, n_used, wg, wu, wd):
    r, d = x_sorted.shape
    weights = lambda a: pl.BlockSpec((1,) + a.shape[1:], lambda t, te, nu: (te[t], 0, 0),
                                     pipeline_mode=pl.Buffered(1))
    return pl.pallas_call(
        _gmm_kernel, name="grouped_swiglu",
        grid_spec=pltpu.PrefetchScalarGridSpec(
            num_scalar_prefetch=2,
            grid=(r // MOE_TILE,),
            in_specs=[pl.BlockSpec((MOE_TILE, d), lambda t, te, nu: (t, 0)),
                      weights(wg), weights(wu), weights(wd)],
            out_specs=pl.BlockSpec((MOE_TILE, d), lambda t, te, nu: (t, 0))),
        out_shape=jax.ShapeDtypeStruct((r, d), F32),
        compiler_params=_params("arbitrary"),
    )(tile_expert, n_used, x_sorted, wg, wu, wd)


def _moe_combine_kernel(y1_ref, y2_ref, rt_ref, h_ref, gt_ref, o_ref):
    rt = rt_ref[...]
    out = rt[:, 2:3] * y1_ref[...] + rt[:, 3:4] * y2_ref[...]
    o_ref[...] = h_ref[...] + gt_ref[0] * out


def _moe_combine(y_pair, route, h, gt_tab, tile_vec, row0, n):
    t, d = h.shape
    tm = 512
    first = row0 // tm
    return pl.pallas_call(
        _moe_combine_kernel, name="moe_combine",
        grid=(n // tm,),
        in_specs=[pl.BlockSpec((tm, d), lambda i: (first + i, 0)),
                  pl.BlockSpec((tm, d), lambda i: (first + i + t // tm, 0)),
                  pl.BlockSpec((tm, LANES), lambda i: (first + i, 0)),
                  pl.BlockSpec((tm, d), lambda i: (first + i, 0)),
                  pl.BlockSpec((1, 1, d), lambda i: (tile_vec(first + i, tm), 0, 0))],
        out_specs=pl.BlockSpec((tm, d), lambda i: (i, 0)),
        out_shape=jax.ShapeDtypeStruct((n, d), F32),
        compiler_params=_params("arbitrary"),
    )(y_pair, y_pair, route, h, gt_tab)


def _moe(u, route, wg, wu, wd, h, gt_tab, tile_vec, splits):
    t, d = u.shape
    n_rows = 2 * t + N_EXPERTS * MOE_TILE
    pos, row_token, tile_expert, n_used = _route_plan(route, n_rows)
    x_sorted = _sc_gather(u, row_token)
    y_sorted = _grouped_swiglu(x_sorted, tile_expert, n_used, wg, wu, wd)
    y_pair = _sc_gather(y_sorted, pos)
    return [_moe_combine(y_pair, route, h, gt_tab, tile_vec, row0, n) for row0, n in splits]


def _final_norm_kernel(h_ref, g_ref, o_ref):
    o_ref[0] = _rms(h_ref[0], g_ref[...])


def _final_norm(h, g):
    b, l, d = h.shape
    tm = min(l, 1024)
    return pl.pallas_call(
        _final_norm_kernel, name="final_norm",
        grid=(b, l // tm),
        in_specs=[pl.BlockSpec((1, tm, d), lambda bi, i: (bi, i, 0)), pl.BlockSpec((1, d), lambda bi, i: (0, 0))],
        out_specs=pl.BlockSpec((1, tm, d), lambda bi, i: (bi, i, 0)),
        out_shape=jax.ShapeDtypeStruct((b, l, d), F32),
        compiler_params=_params("arbitrary", "arbitrary"),
    )(h, g.reshape(1, d))


def _rope_tables(length, head_dim):
    t = jnp.arange(length)
    row = (t // GRID_W).astype(F32)
    col = (t % GRID_W).astype(F32)
    n = head_dim // 4
    inv = ROPE_THETA ** (-jnp.arange(n, dtype=F32) / n)
    ang = jnp.concatenate([row[:, None] * inv, col[:, None] * inv], axis=-1)
    cos = jnp.concatenate([jnp.cos(ang)] * 2, axis=-1)
    sin = jnp.concatenate([-jnp.sin(ang), jnp.sin(ang)], axis=-1)
    reps = LANES // head_dim
    return jnp.tile(cos, (1, reps)), jnp.tile(sin, (1, reps))


def _hyena_tables(length):
    k = jnp.arange(length, dtype=jnp.int32)
    period = 4 * length
    step = 2.0 * math.pi / period
    lo = jnp.arange(64, dtype=jnp.int32)
    hi = jnp.arange(length // 64, dtype=jnp.int32)

    def tabs(x, y_hi, y_lo):
        def cs(y):
            ang = ((x[:, None] * y[None, :]) % period).astype(F32) * step
            return jnp.cos(ang)[:, :, None], jnp.sin(ang)[:, :, None]
        (ca, sa), (cb, sb) = cs(y_hi), cs(y_lo)
        cb, sb = jnp.swapaxes(cb, 1, 2), jnp.swapaxes(sb, 1, 2)
        c = (ca * cb - sa * sb).reshape(length, length)
        s = (sa * cb + ca * sb).reshape(length, length)
        return c.astype(BF16), s.astype(BF16)

    c_tab, s_tab = tabs(2 * k + 1, 64 * hi, lo)
    ci_tab, si_tab = tabs(k, 128 * hi, 2 * lo + 1)
    t = jnp.linspace(0.0, 1.0, length, dtype=F32)[:, None]
    bands = (HY_EMB - 1) // 2
    ang = (2.0 * math.pi / length) * jnp.arange(length, dtype=F32)[:, None] \
        * jnp.linspace(1e-4, bands - 1, bands, dtype=F32)
    feats = jnp.concatenate([t, jnp.cos(ang), -jnp.sin(ang),
                             jnp.zeros((length, LANES - HY_EMB), F32)], axis=-1)
    max_decay = math.log(HY_DECAY_TARGET) / HY_FAST_DECAY
    min_decay = math.log(HY_DECAY_TARGET) / HY_SLOW_DECAY
    deltas = jnp.abs(jnp.linspace(min_decay, max_decay, HY_WIDTH, dtype=F32)).reshape(1, HY_WIDTH)
    return c_tab, s_tab, ci_tab, si_tab, feats, deltas


def _block_diag_ones(n, group):
    i = jnp.arange(n) // group
    return (i[:, None] == i[None, :]).astype(BF16)


def kernel(x, c, ctx, c_ctx, w_ada, b_ada, g_mix, g_ffn, w_in, w_out, hy_conv_w, hy_conv_b, hy_w1, hy_b1, hy_w2, hy_b2, hy_w3, hy_freq, hy_skip, diff_lq1, diff_lk1, diff_lq2, diff_lk2, diff_subln, gqa_qnorm, gqa_knorm, ffn_wg, ffn_wu, ffn_wd, moe_router, moe_wg, moe_wu, moe_wd, g_final):
    b, seq, d = x.shape
    n_ctx = ctx.shape[1]
    depth = w_ada.shape[0]

    rope_tabs = _rope_tables(seq, DIFF_QK_DIM) + _rope_tables(seq, GQA_HEAD_DIM)
    hy_tabs = _hyena_tables(seq)
    hy_tabs_c = _hyena_tables(n_ctx)
    bd128 = _block_diag_ones(LANES, GQA_HEAD_DIM)
    bd256 = _block_diag_ones(DIFF_WIDTH, DIFF_V_DIM)

    rows = 16
    cc = jnp.zeros((rows, d), F32).at[:b].set(c).at[b].set(c_ctx)
    mods = _mods(cc, w_ada, b_ada)

    def mod_vecs(i):
        lat = [mods[i, :b, j * d:(j + 1) * d].reshape(b, 1, d) for j in range(6)]
        cx = [jnp.broadcast_to(mods[i, b, j * d:(j + 1) * d].reshape(1, 1, d), (b, 1, d)) for j in range(6)]
        return lat, cx

    h, hc = x, ctx
    for i in range(depth):
        last = i == depth - 1
        lam_init = 0.8 - 0.6 * math.exp(-0.3 * i)
        (sh_m, sc_m, gt_m, sh_f, sc_f, gt_f), (csh_m, csc_m, cgt_m, csh_f, csc_f, cgt_f) = mod_vecs(i)
        w_in_bf = w_in[i].astype(BF16)
        w_out_bf = w_out[i].astype(BF16)
        g_m = g_mix[i].reshape(1, d)
        g_f = g_ffn[i].reshape(1, d)
        qn = jnp.tile(gqa_qnorm[i], LANES // GQA_HEAD_DIM).reshape(1, LANES)
        kn = jnp.tile(gqa_knorm[i], LANES // GQA_HEAD_DIM).reshape(1, LANES)
        subln = jnp.tile(diff_subln[i], DIFF_HEADS).reshape(1, DIFF_WIDTH) * (1.0 - lam_init)
        lqk = jnp.stack([diff_lq1[i], diff_lk1[i], diff_lq2[i], diff_lk2[i]]).astype(F32)
        hy_p = (hy_conv_w[i], hy_conv_b[i], hy_w1[i], hy_b1[i], hy_w2[i], hy_b2[i], hy_w3[i],
                hy_freq[i], hy_skip[i])
        moe = i % 2 == 1
        router_pad = None
        if moe:
            router = jnp.zeros((d, LANES), F32).at[:, :N_EXPERTS].set(moe_router[i // 2])
            router_hi = router.astype(BF16)
            router_pad = jnp.stack([router_hi, (router - router_hi.astype(F32)).astype(BF16)])

        zhy, qd, kdt, vdx, qg, kgt, vgx = _inproj(h, sc_m, sh_m, g_m, w_in_bf, qn, kn, bd128, rope_tabs)
        czhy, cqd, ckdt, cvdx, cqg, ckgt, cvgx = _inproj(hc, csc_m, csh_m, g_m, w_in_bf, qn, kn, bd128, None)

        y_hy = _hyena(zhy, hy_p, hy_tabs)
        o_d = _diff_attention(lqk, qd, [kdt, ckdt], [vdx, cvdx], lam_init)
        o_g = _gqa_attention(qg, [kgt, ckgt], [vgx, cvgx])
        if not last:
            yc_hy = _hyena(czhy, hy_p, hy_tabs_c)
            oc_d = _diff_attention(lqk, cqd, [ckdt], [cvdx], lam_init)
            oc_g = _gqa_attention(cqg, [ckgt], [cvgx])

        if moe:
            n_c = 0 if last else b * n_ctx
            total = n_c + b * seq
            filled = None
            if not last:
                filled = _outproj(yc_hy, oc_d, oc_g, hc, cgt_m, w_out_bf, g_f, csc_f, csh_f, subln, bd256,
                                  router_pad, (total, 0, None))
            h_all, u_all, route = _outproj(y_hy, o_d, o_g, h, gt_m, w_out_bf, g_f, sc_f, sh_f, subln, bd256,
                                           router_pad, (total, n_c, filled))
            wg, wu, wd = (w[i // 2].astype(BF16) for w in (moe_wg, moe_wu, moe_wd))
            gt_tab = jnp.concatenate([gt_f, cgt_f[:1]], axis=0)
            tile_vec = lambda t, tm: jnp.where(t < n_c // tm, b, (t - n_c // tm) // (seq // tm))
            parts = _moe(u_all, route, wg, wu, wd, h_all, gt_tab, tile_vec,
                         [(n_c, b * seq)] + ([] if last else [(0, n_c)]))
            h = parts[0].reshape(b, seq, d)
            if not last:
                hc = parts[1].reshape(b, n_ctx, d)
        else:
            wg, wu, wd = (w[i // 2].astype(BF16) for w in (ffn_wg, ffn_wu, ffn_wd))
            mixed = _outproj(y_hy, o_d, o_g, h, gt_m, w_out_bf, g_f, sc_f, sh_f, subln, bd256)
            h = _ffn(mixed[1], wg, wu, wd, mixed[0], gt_f)
            if not last:
                cmixed = _outproj(yc_hy, oc_d, oc_g, hc, cgt_m, w_out_bf, g_f, csc_f, csh_f, subln, bd256)
                hc = _ffn(cmixed[1], wg, wu, wd, cmixed[0], cgt_f)
    return _final_norm(h, g_final)
```

```python
import functools
import math

import jax
import jax.numpy as jnp
from jax import lax
from jax.experimental import pallas as pl
from jax.experimental.pallas import tpu as pltpu
from jax.experimental.pallas import tpu_sc as plsc

F32 = jnp.float32
BF16 = jnp.bfloat16
HIGHEST = lax.Precision.HIGHEST

D_MODEL = 1024
GRID_W = 64
ROPE_THETA = 10000.0
NORM_EPS = 1e-6

HY_WIDTH = 256
HY_EMB = 33
HY_FAST_DECAY = 0.3
HY_SLOW_DECAY = 1.5
HY_DECAY_TARGET = 1e-2

DIFF_HEADS = 4
DIFF_QK_DIM = 32
DIFF_V_DIM = 64
DIFF_WIDTH = 256
GQA_HEAD_DIM = 64
GQA_WIDTH = 512
GQA_KV_WIDTH = 128
GQA_REP = 4

OFF_DQ = 3 * HY_WIDTH
OFF_DK = OFF_DQ + DIFF_WIDTH
OFF_DV = OFF_DK + DIFF_WIDTH
OFF_GQ = OFF_DV + DIFF_WIDTH
OFF_GK = OFF_GQ + GQA_WIDTH
OFF_GV = OFF_GK + GQA_KV_WIDTH
IN_COLS = OFF_GV + GQA_KV_WIDTH

N_EXPERTS = 8
LOG2_E = math.log2(math.e)
LANES = 128
VMEM_LIMIT = 56 * 1024 * 1024


def _params(*sem):
    return pltpu.CompilerParams(dimension_semantics=sem, vmem_limit_bytes=VMEM_LIMIT)


def _dot(a, b):
    return jnp.dot(a, b, preferred_element_type=F32)


def _dot_hi(a, b):
    return jnp.dot(a, b, preferred_element_type=F32, precision=HIGHEST)


def _rms(x, g):
    ms = jnp.mean(x * x, axis=-1, keepdims=True)
    return x * lax.rsqrt(ms + NORM_EPS) * g


def _silu(x):
    return x * jax.nn.sigmoid(x)


def _group_mean_sq(x, ones_bd, width):
    return _dot((x * x).astype(BF16), ones_bd) * (1.0 / width)


def _mods_kernel(c_ref, w_ref, b_ref, o_ref):
    o_ref[0] = _dot_hi(_silu(c_ref[...]), w_ref[0]) + b_ref[0]


def _mods(cc, w_ada, b_ada):
    depth, d, n = w_ada.shape
    tn = 1536
    return pl.pallas_call(
        _mods_kernel, name="mods",
        grid=(depth, n // tn),
        in_specs=[pl.BlockSpec(cc.shape, lambda i, j: (0, 0)),
                  pl.BlockSpec((1, d, tn), lambda i, j: (i, 0, j)),
                  pl.BlockSpec((1, 1, tn), lambda i, j: (i, 0, j))],
        out_specs=pl.BlockSpec((1, cc.shape[0], tn), lambda i, j: (i, 0, j)),
        out_shape=jax.ShapeDtypeStruct((depth, cc.shape[0], n), F32),
        compiler_params=_params("arbitrary", "arbitrary"),
    )(cc, w_ada, b_ada.reshape(depth, 1, n))


def _rope128(x, cos, sin_signed, half):
    lane = lax.broadcasted_iota(jnp.int32, x.shape, 1)
    first = (lane & (2 * half - 1)) < half
    swapped = jnp.where(first, pltpu.roll(x, LANES - half, 1), pltpu.roll(x, half, 1))
    return x * cos + swapped * sin_signed


def _inproj_kernel(*refs, rope):
    h_ref, sc_ref, sh_ref, g_ref, w_ref, qn_ref, kn_ref, bd_ref = refs[:8]
    if rope:
        cd_ref, sd_ref, cg_ref, sg_ref = refs[8:12]
    zhy_ref, qd_ref, kd_ref, vd_ref, qg_ref, kg_ref, vg_ref = refs[-7:]
    u = _rms(h_ref[0], g_ref[...]) * (1.0 + sc_ref[0]) + sh_ref[0]
    z = _dot(u.astype(BF16), w_ref[...])
    zhy_ref[0] = z[:, :OFF_DQ]

    def piece(off, j):
        return z[:, off + LANES * j: off + LANES * (j + 1)]

    for j in range(DIFF_WIDTH // LANES):
        q, k = piece(OFF_DQ, j), piece(OFF_DK, j)
        if rope:
            q = _rope128(q, cd_ref[...], sd_ref[...], DIFF_QK_DIM // 2)
            k = _rope128(k, cd_ref[...], sd_ref[...], DIFF_QK_DIM // 2)
        qd_ref[0, :, LANES * j: LANES * (j + 1)] = (q * (LOG2_E * DIFF_QK_DIM ** -0.5)).astype(BF16)
        kd_ref[0, 0, LANES * j: LANES * (j + 1), :] = k.T.astype(BF16)
    vd_ref[0, 0] = z[:, OFF_DV:OFF_GQ].astype(BF16)

    def gqa_piece(x, gain):
        ms = _group_mean_sq(x, bd_ref[...], GQA_HEAD_DIM)
        x = x * lax.rsqrt(ms + NORM_EPS) * gain
        if rope:
            x = _rope128(x, cg_ref[...], sg_ref[...], GQA_HEAD_DIM // 2)
        return x

    for j in range(GQA_WIDTH // LANES):
        q = gqa_piece(piece(OFF_GQ, j), qn_ref[...])
        qg_ref[0, :, LANES * j: LANES * (j + 1)] = (q * (LOG2_E * GQA_HEAD_DIM ** -0.5)).astype(BF16)
    kt = gqa_piece(piece(OFF_GK, 0), kn_ref[...]).T.astype(BF16)
    v = z[:, OFF_GV:]
    v_swapped = pltpu.roll(v, GQA_HEAD_DIM, 1)
    low = lax.broadcasted_iota(jnp.int32, v.shape, 1) < GQA_HEAD_DIM
    v_rep = (jnp.where(low, v, v_swapped), jnp.where(low, v_swapped, v))
    for g in range(GQA_KV_WIDTH // GQA_HEAD_DIM):
        for r in range(GQA_REP):
            kg_ref[0, g, GQA_HEAD_DIM * r: GQA_HEAD_DIM * (r + 1), :] = kt[GQA_HEAD_DIM * g: GQA_HEAD_DIM * (g + 1), :]
        for half in range(GQA_REP * GQA_HEAD_DIM // LANES):
            vg_ref[0, g, :, LANES * half: LANES * (half + 1)] = v_rep[g].astype(BF16)


def _inproj(h, sc, sh, g, w_bf, qn, kn, bd, rope_tabs):
    b, l, d = h.shape
    tm = min(l, 512)
    row = lambda bi, i: (bi, i, 0)
    vec = lambda bi, i: (bi, 0, 0)
    const = lambda bi, i: (0, 0)
    in_specs = [pl.BlockSpec((1, tm, d), row), pl.BlockSpec((1, 1, d), vec), pl.BlockSpec((1, 1, d), vec),
                pl.BlockSpec((1, d), const), pl.BlockSpec((d, IN_COLS), const),
                pl.BlockSpec((1, LANES), const), pl.BlockSpec((1, LANES), const),
                pl.BlockSpec((LANES, LANES), const)]
    args = [h, sc, sh, g, w_bf, qn, kn, bd]
    if rope_tabs is not None:
        in_specs += [pl.BlockSpec((tm, LANES), lambda bi, i: (i, 0))] * 4
        args += list(rope_tabs)
    aw = ATTN_WIDTH
    kv_groups = GQA_KV_WIDTH // GQA_HEAD_DIM
    rows_spec = lambda w: pl.BlockSpec((1, tm, w), row)
    kt_spec = lambda g: pl.BlockSpec((1, g, aw, tm), lambda bi, i: (bi, 0, 0, i))
    v_spec = lambda g: pl.BlockSpec((1, g, tm, aw), lambda bi, i: (bi, 0, i, 0))
    sds = jax.ShapeDtypeStruct
    return pl.pallas_call(
        functools.partial(_inproj_kernel, rope=rope_tabs is not None), name="inproj",
        grid=(b, l // tm),
        in_specs=in_specs,
        out_specs=[rows_spec(OFF_DQ), rows_spec(DIFF_WIDTH), kt_spec(1), v_spec(1),
                   rows_spec(GQA_WIDTH), kt_spec(kv_groups), v_spec(kv_groups)],
        out_shape=[sds((b, l, OFF_DQ), F32), sds((b, l, DIFF_WIDTH), BF16), sds((b, 1, aw, l), BF16),
                   sds((b, 1, l, aw), BF16), sds((b, l, GQA_WIDTH), BF16), sds((b, kv_groups, aw, l), BF16),
                   sds((b, kv_groups, l, aw), BF16)],
        compiler_params=_params("arbitrary", "arbitrary"),
    )(*args)


def _hy_filter_kernel(f_ref, w1_ref, b1_ref, w2_ref, b2_ref, w3_ref, fr_ref, dl_ref, hs_ref, hd_ref):
    f = f_ref[...]
    fr = fr_ref[...]
    a = jnp.sin(fr * (_dot_hi(f, w1_ref[...]) + b1_ref[...]))
    a = jnp.sin(fr * (_dot_hi(a, w2_ref[...]) + b2_ref[...]))
    hf = _dot_hi(a, w3_ref[...])
    decay = jnp.exp(-f[:, 0:1] * dl_ref[...])
    h_fwd = hf[:, :HY_WIDTH] * decay
    h_bwd = hf[:, HY_WIDTH:] * decay
    row = lax.broadcasted_iota(jnp.int32, h_bwd.shape, 0) + pl.program_id(0) * f.shape[0]
    h_bwd = jnp.where(row == 0, 0.0, h_bwd)
    hs_ref[...] = (h_fwd + h_bwd).astype(BF16)
    hd_ref[...] = (h_bwd - h_fwd).astype(BF16)


def _hy_spectrum_kernel(c_ref, s_ref, hs_ref, hd_ref, gre_ref, gim_ref, *, scale):
    gre_ref[...] = _dot(c_ref[...], hs_ref[...]) * scale
    gim_ref[...] = _dot(s_ref[...], hd_ref[...]) * scale


def _hy_pre_kernel(z_ref, zp_ref, zn_ref, cw_ref, cb_ref, wb_ref, wf_ref, x0_ref, *, nt):
    ti = pl.program_id(1)
    z = z_ref[0]
    tl = z.shape[0]
    row = lax.broadcasted_iota(jnp.int32, z.shape, 0)
    prev_row = jnp.where(ti > 0, zp_ref[0, 7:8, :], 0.0)
    next_row = jnp.where(ti < nt - 1, zn_ref[0, 0:1, :], 0.0)
    z_prev = jnp.where(row == 0, prev_row, pltpu.roll(z, 1, 0))
    z_next = jnp.where(row == tl - 1, next_row, pltpu.roll(z, tl - 1, 0))
    cw = cw_ref[...]
    y = cb_ref[...] + z_prev * cw[0:1] + z * cw[1:2] + z_next * cw[2:3]
    x0, x1, v = y[:, :HY_WIDTH], y[:, HY_WIDTH:2 * HY_WIDTH], y[:, 2 * HY_WIDTH:]
    w = v * x1
    wb_ref[...] = w.astype(BF16)
    wf_ref[0] = w
    x0_ref[0] = x0


def _hy_fwd_kernel(c_ref, s_ref, x_ref, gre_ref, gim_ref, yre_ref, yim_ref, *, reps):
    x = x_ref[...]
    a = _dot(c_ref[...], x)
    b = _dot(s_ref[...], x)
    gre = jnp.concatenate([gre_ref[...]] * reps, axis=1)
    gim = jnp.concatenate([gim_ref[...]] * reps, axis=1)
    yre_ref[...] = (a * gre + b * gim).astype(BF16)
    yim_ref[...] = (a * gim - b * gre).astype(BF16)


def _hy_inv_kernel(ci_ref, si_ref, yre_ref, yim_ref, wf_ref, x0_ref, skip_ref, o_ref, *, reps):
    y = _dot(ci_ref[...], yre_ref[...]) - _dot(si_ref[...], yim_ref[...])
    for r in range(reps):
        yr = y[:, r * HY_WIDTH:(r + 1) * HY_WIDTH]
        o_ref[r] = ((yr + wf_ref[r] * skip_ref[...]) * x0_ref[r]).astype(BF16)


def _hyena(zhy, p, tabs):
    conv_w, conv_b, w1, b1, w2, b2, w3, freq, skip = p
    c_tab, s_tab, ci_tab, si_tab, feats, deltas = tabs
    b, l, _ = zhy.shape
    c = HY_WIDTH
    tl = min(l, 512)
    nt = l // tl
    hid = w2.shape[0]
    const1 = lambda i: (0, 0)
    w1p = jnp.zeros((LANES, hid), F32).at[:HY_EMB].set(w1)
    hs, hd = pl.pallas_call(
        _hy_filter_kernel, name="hy_filter",
        grid=(nt,),
        in_specs=[pl.BlockSpec((tl, LANES), lambda i: (i, 0)),
                  pl.BlockSpec((LANES, hid), const1), pl.BlockSpec((1, hid), const1),
                  pl.BlockSpec((hid, hid), const1), pl.BlockSpec((1, hid), const1),
                  pl.BlockSpec((hid, 2 * c), const1), pl.BlockSpec((1, hid), const1),
                  pl.BlockSpec((1, c), const1)],
        out_specs=[pl.BlockSpec((tl, c), lambda i: (i, 0))] * 2,
        out_shape=[jax.ShapeDtypeStruct((l, c), BF16)] * 2,
        compiler_params=_params("arbitrary"),
    )(feats, w1p, b1.reshape(1, hid), w2, b2.reshape(1, hid), w3, freq.reshape(1, hid), deltas)

    gre, gim = pl.pallas_call(
        functools.partial(_hy_spectrum_kernel, scale=1.0 / l), name="hy_spectrum",
        grid=(nt,),
        in_specs=[pl.BlockSpec((tl, l), lambda i: (i, 0)), pl.BlockSpec((tl, l), lambda i: (i, 0)),
                  pl.BlockSpec((l, c), const1), pl.BlockSpec((l, c), const1)],
        out_specs=[pl.BlockSpec((tl, c), lambda i: (i, 0))] * 2,
        out_shape=[jax.ShapeDtypeStruct((l, c), F32)] * 2,
        compiler_params=_params("arbitrary"),
    )(c_tab, s_tab, hs, hd)

    halo = 8
    wb, wf, x0 = pl.pallas_call(
        functools.partial(_hy_pre_kernel, nt=nt), name="hy_pre",
        grid=(b, nt),
        in_specs=[pl.BlockSpec((1, tl, 3 * c), lambda bi, i: (bi, i, 0)),
                  pl.BlockSpec((1, halo, 3 * c), lambda bi, i: (bi, jnp.maximum(i * (tl // halo) - 1, 0), 0)),
                  pl.BlockSpec((1, halo, 3 * c),
                               lambda bi, i: (bi, jnp.minimum((i + 1) * (tl // halo), l // halo - 1), 0)),
                  pl.BlockSpec((3, 3 * c), lambda bi, i: (0, 0)),
                  pl.BlockSpec((1, 3 * c), lambda bi, i: (0, 0))],
        out_specs=[pl.BlockSpec((tl, c), lambda bi, i: (i, bi)),
                   pl.BlockSpec((1, tl, c), lambda bi, i: (bi, i, 0)),
                   pl.BlockSpec((1, tl, c), lambda bi, i: (bi, i, 0))],
        out_shape=[jax.ShapeDtypeStruct((l, b * c), BF16),
                   jax.ShapeDtypeStruct((b, l, c), F32),
                   jax.ShapeDtypeStruct((b, l, c), F32)],
        compiler_params=_params("arbitrary", "arbitrary"),
    )(zhy, zhy, zhy, conv_w, conv_b.reshape(1, 3 * c))

    reps = 2
    tn = reps * c
    nj = b * c // tn
    yre, yim = pl.pallas_call(
        functools.partial(_hy_fwd_kernel, reps=reps), name="hy_fwd",
        grid=(nj, nt),
        in_specs=[pl.BlockSpec((tl, l), lambda j, i: (i, 0)), pl.BlockSpec((tl, l), lambda j, i: (i, 0)),
                  pl.BlockSpec((l, tn), lambda j, i: (0, j)),
                  pl.BlockSpec((tl, c), lambda j, i: (i, 0)), pl.BlockSpec((tl, c), lambda j, i: (i, 0))],
        out_specs=[pl.BlockSpec((tl, tn), lambda j, i: (i, j))] * 2,
        out_shape=[jax.ShapeDtypeStruct((l, b * c), BF16)] * 2,
        compiler_params=_params("arbitrary", "arbitrary"),
    )(c_tab, s_tab, wb, gre, gim)

    return pl.pallas_call(
        functools.partial(_hy_inv_kernel, reps=reps), name="hy_inv",
        grid=(nj, nt),
        in_specs=[pl.BlockSpec((tl, l), lambda j, i: (i, 0)), pl.BlockSpec((tl, l), lambda j, i: (i, 0)),
                  pl.BlockSpec((l, tn), lambda j, i: (0, j)), pl.BlockSpec((l, tn), lambda j, i: (0, j)),
                  pl.BlockSpec((reps, tl, c), lambda j, i: (j, i, 0)),
                  pl.BlockSpec((reps, tl, c), lambda j, i: (j, i, 0)),
                  pl.BlockSpec((1, c), lambda j, i: (0, 0))],
        out_specs=pl.BlockSpec((reps, tl, c), lambda j, i: (j, i, 0)),
        out_shape=jax.ShapeDtypeStruct((b, l, c), BF16),
        compiler_params=_params("arbitrary", "arbitrary"),
    )(ci_tab, si_tab, yre, yim, wf, x0, skip.reshape(1, c))


def _lane_mask(width, shift, idx):
    lane = lax.broadcasted_iota(jnp.int32, (1, width), 1)
    return jnp.where((lane >> shift) == idx, 1.0, 0.0).astype(BF16)


ATTN_WIDTH = 256
ATTN_CHUNK = 512


def _key_chunks(lk):
    return [(c0, min(ATTN_CHUNK, lk - c0)) for c0 in range(0, lk, ATTN_CHUNK)]


ATTN_STACK = 4
ATTN_TQ = 128
ATTN_STEP_ROWS = 512


def _qk_stage(q, shift, first, kt_refs, qs_scr, s_scr):
    tq = q.shape[0]
    for i in range(ATTN_STACK):
        qs_scr[i * tq:(i + 1) * tq, :] = q * _lane_mask(ATTN_WIDTH, shift, first + i)
    qs = qs_scr[...]
    m = jnp.full((ATTN_STACK * tq, LANES), -jnp.inf, F32)
    base = 0
    for kt_ref in kt_refs:
        for c0, ck in _key_chunks(kt_ref.shape[-1]):
            s = _dot(qs, kt_ref[:, c0:c0 + ck])
            s_scr[:, base + c0:base + c0 + ck] = s
            for j in range(ck // LANES):
                m = jnp.maximum(m, s[:, j * LANES:(j + 1) * LANES])
        base += kt_ref.shape[-1]
    return jnp.max(m, axis=-1, keepdims=True)


def _exp_stage(s_scr, mx, p_scr):
    l = jnp.zeros((s_scr.shape[0], LANES), F32)
    for c0, ck in _key_chunks(s_scr.shape[-1]):
        e = jnp.exp2(s_scr[:, c0:c0 + ck] - mx)
        p_scr[:, c0:c0 + ck] = e.astype(BF16)
        for j in range(ck // LANES):
            l = l + e[:, j * LANES:(j + 1) * LANES]
    return 1.0 / jnp.sum(l, axis=-1, keepdims=True)


def _attention_passes(passes, consume, kt_refs, v_refs, scratch):
    qs, ss, ps = scratch[0:2], scratch[2:4], scratch[4:6]
    tq = passes[0][0].shape[0]

    def qk(i):
        q, shift, first, g = passes[i]
        return _qk_stage(q, shift, first, [kt.at[0, g] for kt in kt_refs], qs[i % 2], ss[i % 2])

    mx = qk(0)
    for i in range(len(passes)):
        mx_next = qk(i + 1) if i + 1 < len(passes) else None
        inv = _exp_stage(ss[i % 2], mx, ps[i % 2])
        pv, base = None, 0
        for v_ref in v_refs:
            lk = v_ref.shape[2]
            part = _dot(ps[i % 2][:, base:base + lk], v_ref[0, passes[i][3]])
            pv = part if pv is None else pv + part
            base += lk
        pv = pv * inv
        consume(i, [pv[r * tq:(r + 1) * tq] for r in range(ATTN_STACK)])
        mx = mx_next


def _diff_kernel(lqk_ref, q_ref, *refs, n_kv, lam_init):
    kt_refs, v_refs, o_ref, scratch = refs[:n_kv], refs[n_kv:2 * n_kv], refs[2 * n_kv], refs[2 * n_kv + 1:]
    x = lqk_ref[...]
    lam = (jnp.exp(jnp.sum(x[0:1] * x[1:2], axis=-1, keepdims=True))
           - jnp.exp(jnp.sum(x[2:3] * x[3:4], axis=-1, keepdims=True)) + lam_init)
    tq = min(ATTN_TQ, q_ref.shape[1])
    group = lax.broadcasted_iota(jnp.int32, (tq, ATTN_WIDTH), 1) >> 6
    halves = 2 * DIFF_HEADS // ATTN_STACK
    passes = []
    for t in range(q_ref.shape[1] // tq):
        q = q_ref[0, t * tq:(t + 1) * tq, :]
        passes += [(q, 5, half * ATTN_STACK, 0) for half in range(halves)]
    acc = {}

    def consume(i, pvs):
        t, half = divmod(i, halves)
        o1, o2 = acc.pop(t, (jnp.zeros(group.shape, F32),) * 2)
        for r, pv in enumerate(pvs):
            head = (half * ATTN_STACK + r) >> 1
            if r & 1 == 0:
                o1 = jnp.where(group == head, pv, o1)
            else:
                o2 = jnp.where(group == head, pv, o2)
        if half + 1 < halves:
            acc[t] = (o1, o2)
            return
        o_ref[0, t * tq:(t + 1) * tq, :] = (o1 - lam * o2).astype(BF16)

    _attention_passes(passes, consume, kt_refs, v_refs, scratch)


def _gqa_kernel(q_ref, *refs, n_kv):
    kt_refs, v_refs, o_ref, scratch = refs[:n_kv], refs[n_kv:2 * n_kv], refs[2 * n_kv], refs[2 * n_kv + 1:]
    tq = min(ATTN_TQ, q_ref.shape[1])
    group = lax.broadcasted_iota(jnp.int32, (tq, ATTN_WIDTH), 1) >> 6
    groups = q_ref.shape[2] // ATTN_WIDTH
    passes = []
    for t in range(q_ref.shape[1] // tq):
        for g in range(groups):
            passes.append((q_ref[0, t * tq:(t + 1) * tq, g * ATTN_WIDTH:(g + 1) * ATTN_WIDTH], 6, 0, g))

    def consume(i, pvs):
        t, g = divmod(i, groups)
        o = jnp.zeros(group.shape, F32)
        for r, pv in enumerate(pvs):
            o = jnp.where(group == r, pv, o)
        o_ref[0, t * tq:(t + 1) * tq, g * ATTN_WIDTH:(g + 1) * ATTN_WIDTH] = o.astype(BF16)

    _attention_passes(passes, consume, kt_refs, v_refs, scratch)


def _attention(kernel_fn, name, q, kts, vs, extra=()):
    b, lq, width = q.shape
    groups, w = kts[0].shape[1:3]
    lk = sum(kt.shape[3] for kt in kts)
    rows = min(lq, ATTN_STEP_ROWS)
    pass_rows = ATTN_STACK * min(ATTN_TQ, rows)
    kv_spec = lambda a: pl.BlockSpec((1,) + a.shape[1:], lambda bi, i: (bi, 0, 0, 0))
    return pl.pallas_call(
        functools.partial(kernel_fn, n_kv=len(kts)), name=name,
        grid=(b, lq // rows),
        in_specs=[pl.BlockSpec(a.shape, lambda bi, i: (0, 0)) for a in extra]
        + [pl.BlockSpec((1, rows, width), lambda bi, i: (bi, i, 0))]
        + [kv_spec(a) for a in kts] + [kv_spec(a) for a in vs],
        out_specs=pl.BlockSpec((1, rows, width), lambda bi, i: (bi, i, 0)),
        out_shape=jax.ShapeDtypeStruct((b, lq, width), BF16),
        scratch_shapes=[pltpu.VMEM((pass_rows, w), BF16)] * 2 + [pltpu.VMEM((pass_rows, lk), F32)] * 2
        + [pltpu.VMEM((pass_rows, lk), BF16)] * 2,
        compiler_params=_params("arbitrary", "arbitrary"),
    )(*extra, q, *kts, *vs)


def _diff_attention(lqk, q, kts, vs, lam_init):
    return _attention(functools.partial(_diff_kernel, lam_init=lam_init), "diff_attn", q, kts, vs, (lqk,))


def _gqa_attention(q, kts, vs):
    return _attention(_gqa_kernel, "gqa_attn", q, kts, vs)


def _top2_route(logits):
    lane = lax.broadcasted_iota(jnp.int32, logits.shape, 1).astype(F32)
    neg = jnp.float32(-jnp.inf)
    lg = jnp.where(lane < N_EXPERTS, logits, neg)
    m1 = jnp.max(lg, axis=-1, keepdims=True)
    i1 = jnp.min(jnp.where(lg == m1, lane, float(LANES)), axis=-1, keepdims=True)
    lg2 = jnp.where(lane == i1, neg, lg)
    m2 = jnp.max(lg2, axis=-1, keepdims=True)
    i2 = jnp.min(jnp.where(lg2 == m2, lane, float(LANES)), axis=-1, keepdims=True)
    e2 = jnp.exp(m2 - m1)
    w1 = 1.0 / (1.0 + e2)
    return (jnp.where(lane == 0, i1, 0.0) + jnp.where(lane == 1, i2, 0.0)
            + jnp.where(lane == 2, w1, 0.0) + jnp.where(lane == 3, e2 * w1, 0.0))


def _outproj_kernel(*refs, moe):
    yhy_ref, od_ref, og_ref, h_ref, gt_ref, w_ref, gf_ref, sc_ref, sh_ref, sg_ref, bd_ref = refs[:11]
    if moe:
        rt_ref = refs[11]
        hn_ref, u_ref, gates_ref = refs[-3:]
    else:
        hn_ref, u_ref = refs[11:]
    od = od_ref[0].astype(F32)
    od = od * lax.rsqrt(_group_mean_sq(od, bd_ref[...], DIFF_V_DIM) + NORM_EPS) * sg_ref[...]
    y = (_dot(yhy_ref[0], w_ref[0:HY_WIDTH, :])
         + _dot(od.astype(BF16), w_ref[HY_WIDTH:HY_WIDTH + DIFF_WIDTH, :])
         + _dot(og_ref[0], w_ref[HY_WIDTH + DIFF_WIDTH:, :]))
    hn = h_ref[0] + gt_ref[0] * y
    hn_ref[...] = hn.reshape(hn_ref.shape)
    u = _rms(hn, gf_ref[...]) * (1.0 + sc_ref[0]) + sh_ref[0]
    u_ref[...] = u.astype(u_ref.dtype).reshape(u_ref.shape)
    if moe:
        u_hi = u.astype(BF16)
        u_lo = (u - u_hi.astype(F32)).astype(BF16)
        logits = _dot(u_hi, rt_ref[0]) + (_dot(u_lo, rt_ref[0]) + _dot(u_hi, rt_ref[1]))
        gates_ref[...] = _top2_route(logits)


def _outproj(yhy, od, og, h, gt, w_bf, gf, sc, sh, sub_g, bd256, router_pad=None, into=None):
    b, l, d = h.shape
    tm = min(l, 512)
    row = lambda bi, i: (bi, i, 0)
    vec = lambda bi, i: (bi, 0, 0)
    const = lambda bi, i: (0, 0)
    moe = router_pad is not None
    in_specs = [pl.BlockSpec((1, tm, HY_WIDTH), row), pl.BlockSpec((1, tm, DIFF_WIDTH), row),
                pl.BlockSpec((1, tm, GQA_WIDTH), row), pl.BlockSpec((1, tm, d), row),
                pl.BlockSpec((1, 1, d), vec), pl.BlockSpec((d, d), const), pl.BlockSpec((1, d), const),
                pl.BlockSpec((1, 1, d), vec), pl.BlockSpec((1, 1, d), vec),
                pl.BlockSpec((1, DIFF_WIDTH), const), pl.BlockSpec((DIFF_WIDTH, DIFF_WIDTH), const)]
    args = [yhy, od, og, h, gt, w_bf, gf, sc, sh, sub_g, bd256]
    out_specs = [pl.BlockSpec((1, tm, d), row), pl.BlockSpec((1, tm, d), row)]
    out_shape = [jax.ShapeDtypeStruct((b, l, d), F32), jax.ShapeDtypeStruct((b, l, d), F32 if moe else BF16)]
    aliases = {}
    if moe:
        total_rows, row_offset, filled = into
        in_specs.append(pl.BlockSpec((2, d, LANES), lambda bi, i: (0, 0, 0)))
        args.append(router_pad)
        flat_row = lambda bi, i: (row_offset // tm + bi * (l // tm) + i, 0)
        widths = (d, d, LANES)
        out_specs = [pl.BlockSpec((tm, w), flat_row) for w in widths]
        out_shape = [jax.ShapeDtypeStruct((total_rows, w), F32) for w in widths]
        if filled is not None:
            aliases = {len(args) + j: j for j in range(len(filled))}
            in_specs += [pl.BlockSpec(memory_space=pl.ANY)] * len(filled)
            args += list(filled)
    return pl.pallas_call(
        functools.partial(_outproj_kernel, moe=moe), name="outproj",
        grid=(b, l // tm), in_specs=in_specs, out_specs=out_specs, out_shape=out_shape,
        input_output_aliases=aliases,
        compiler_params=_params("arbitrary", "arbitrary"),
    )(*args)


def _ffn_kernel(u_ref, wg_ref, wu_ref, wd_ref, h_ref, gt_ref, o_ref):
    u = u_ref[0]
    hid = _silu(_dot(u, wg_ref[...])) * _dot(u, wu_ref[...])
    o_ref[0] = h_ref[0] + gt_ref[0] * _dot(hid.astype(BF16), wd_ref[...])


def _ffn(u, wg, wu, wd, h, gt):
    b, l, d = h.shape
    tm = min(l, 512)
    row = lambda bi, i: (bi, i, 0)
    resident = lambda a: pl.BlockSpec(a.shape, lambda bi, i: (0, 0), pipeline_mode=pl.Buffered(1))
    return pl.pallas_call(
        _ffn_kernel, name="ffn",
        grid=(b, l // tm),
        in_specs=[pl.BlockSpec((1, tm, d), row), resident(wg), resident(wu), resident(wd),
                  pl.BlockSpec((1, tm, d), row), pl.BlockSpec((1, 1, d), lambda bi, i: (bi, 0, 0))],
        out_specs=pl.BlockSpec((1, tm, d), row),
        out_shape=jax.ShapeDtypeStruct((b, l, d), F32),
        compiler_params=_params("arbitrary", "arbitrary"),
    )(u, wg, wu, wd, h, gt)


MOE_TILE = 512
SC_CHUNK = 64


def _sc_gather(table, idx):
    info = plsc.get_sparse_core_info()
    nc, ns = info.num_cores, info.num_subcores
    n, d = idx.shape[0], table.shape[1]
    per_worker = n // (nc * ns)
    assert per_worker * nc * ns == n and per_worker % SC_CHUNK == 0
    mesh = plsc.VectorSubcoreMesh(core_axis_name="c", subcore_axis_name="s")

    @functools.partial(
        pl.kernel, mesh=mesh,
        out_type=jax.ShapeDtypeStruct((n, d), table.dtype),
        scratch_types=[pltpu.VMEM((SC_CHUNK,), jnp.int32), pltpu.VMEM((SC_CHUNK, d), table.dtype),
                       pltpu.SemaphoreType.DMA],
    )
    def gather_kernel(table_hbm, idx_hbm, out_hbm, idx_v, rows_v, sem):
        base = (lax.axis_index("s") * nc + lax.axis_index("c")) * per_worker

        @pl.loop(0, per_worker // SC_CHUNK)
        def _(j):
            off = pl.multiple_of(base + j * SC_CHUNK, 8)
            pltpu.sync_copy(idx_hbm.at[pl.ds(off, SC_CHUNK)], idx_v)
            pltpu.async_copy(table_hbm.at[idx_v], rows_v, sem).wait()
            pltpu.sync_copy(rows_v, out_hbm.at[pl.ds(off, SC_CHUNK)])

    return gather_kernel(table, idx)


def _route_plan(route, n_rows):
    t = route.shape[0]
    e = route[:, :2].astype(jnp.int32).T.reshape(2 * t)
    onehot = (e[:, None] == jnp.arange(N_EXPERTS, dtype=jnp.int32)[None, :]).astype(jnp.int32)
    csum = jnp.cumsum(onehot, axis=0)
    counts = csum[-1]
    rank = jnp.take_along_axis(csum, e[:, None], axis=1)[:, 0] - 1
    padded = (counts + MOE_TILE - 1) // MOE_TILE * MOE_TILE
    ends = jnp.cumsum(padded)
    pos = (ends - padded)[e] + rank
    row_token = jnp.zeros((n_rows,), jnp.int32).at[pos].set(jnp.arange(2 * t, dtype=jnp.int32) % t)
    tile_start = jnp.arange(n_rows // MOE_TILE, dtype=jnp.int32) * MOE_TILE
    tile_expert = jnp.minimum(jnp.sum((tile_start[:, None] >= ends[None, :]).astype(jnp.int32), axis=1),
                              N_EXPERTS - 1)
    n_used = (ends[-1] // MOE_TILE).reshape(1)
    return pos, row_token, tile_expert, n_used


def _gmm_kernel(te_ref, nu_ref, x_ref, wg_ref, wu_ref, wd_ref, o_ref):
    @pl.when(pl.program_id(0) < nu_ref[0])
    def _():
        x = x_ref[...].astype(BF16)
        half = wg_ref.shape[2] // 2
        out = None
        for f0 in (0, half):
            hid = _silu(_dot(x, wg_ref[0, :, f0:f0 + half])) * _dot(x, wu_ref[0, :, f0:f0 + half])
            part = _dot(hid.astype(BF16), wd_ref[0, f0:f0 + half, :])
            out = part if out is None else out + part
        o_ref[...] = out


def _grouped_swiglu(x_sorted, tile_expert, n_used, wg, wu, wd):
    r, d = x_sorted.shape
    weights = lambda a: pl.BlockSpec((1,) + a.shape[1:], lambda t, te, nu: (te[t], 0, 0),
                                     pipeline_mode=pl.Buffered(1))
    return pl.pallas_call(
        _gmm_kernel, name="grouped_swiglu",
        grid_spec=pltpu.PrefetchScalarGridSpec(
            num_scalar_prefetch=2,
            grid=(r // MOE_TILE,),
            in_specs=[pl.BlockSpec((MOE_TILE, d), lambda t, te, nu: (t, 0)),
                      weights(wg), weights(wu), weights(wd)],
            out_specs=pl.BlockSpec((MOE_TILE, d), lambda t, te, nu: (t, 0))),
        out_shape=jax.ShapeDtypeStruct((r, d), F32),
        compiler_params=_params("arbitrary"),
    )(tile_expert, n_used, x_sorted, wg, wu, wd)


def _moe_combine_kernel(y1_ref, y2_ref, rt_ref, h_ref, gt_ref, o_ref):
    rt = rt_ref[...]
    out = rt[:, 2:3] * y1_ref[...] + rt[:, 3:4] * y2_ref[...]
    o_ref[...] = h_ref[...] + gt_ref[0] * out


def _moe_combine(y_pair, route, h, gt_tab, tile_vec, row0, n):
    t, d = h.shape
    tm = 512
    first = row0 // tm
    return pl.pallas_call(
        _moe_combine_kernel, name="moe_combine",
        grid=(n // tm,),
        in_specs=[pl.BlockSpec((tm, d), lambda i: (first + i, 0)),
                  pl.BlockSpec((tm, d), lambda i: (first + i + t // tm, 0)),
                  pl.BlockSpec((tm, LANES), lambda i: (first + i, 0)),
                  pl.BlockSpec((tm, d), lambda i: (first + i, 0)),
                  pl.BlockSpec((1, 1, d), lambda i: (tile_vec(first + i, tm), 0, 0))],
        out_specs=pl.BlockSpec((tm, d), lambda i: (i, 0)),
        out_shape=jax.ShapeDtypeStruct((n, d), F32),
        compiler_params=_params("arbitrary"),
    )(y_pair, y_pair, route, h, gt_tab)


def _moe(u, route, wg, wu, wd, h, gt_tab, tile_vec, splits):
    t, d = u.shape
    n_rows = 2 * t + N_EXPERTS * MOE_TILE
    pos, row_token, tile_expert, n_used = _route_plan(route, n_rows)
    x_sorted = _sc_gather(u, row_token)
    y_sorted = _grouped_swiglu(x_sorted, tile_expert, n_used, wg, wu, wd)
    y_pair = _sc_gather(y_sorted, pos)
    return [_moe_combine(y_pair, route, h, gt_tab, tile_vec, row0, n) for row0, n in splits]


def _final_norm_kernel(h_ref, g_ref, o_ref):
    o_ref[0] = _rms(h_ref[0], g_ref[...])


def _final_norm(h, g):
    b, l, d = h.shape
    tm = min(l, 1024)
    return pl.pallas_call(
        _final_norm_kernel, name="final_norm",
        grid=(b, l // tm),
        in_specs=[pl.BlockSpec((1, tm, d), lambda bi, i: (bi, i, 0)), pl.BlockSpec((1, d), lambda bi, i: (0, 0))],
        out_specs=pl.BlockSpec((1, tm, d), lambda bi, i: (bi, i, 0)),
        out_shape=jax.ShapeDtypeStruct((b, l, d), F32),
        compiler_params=_params("arbitrary", "arbitrary"),
    )(h, g.reshape(1, d))


def _rope_tables(length, head_dim):
    t = jnp.arange(length)
    row = (t // GRID_W).astype(F32)
    col = (t % GRID_W).astype(F32)
    n = head_dim // 4
    inv = ROPE_THETA ** (-jnp.arange(n, dtype=F32) / n)
    ang = jnp.concatenate([row[:, None] * inv, col[:, None] * inv], axis=-1)
    cos = jnp.concatenate([jnp.cos(ang)] * 2, axis=-1)
    sin = jnp.concatenate([-jnp.sin(ang), jnp.sin(ang)], axis=-1)
    reps = LANES // head_dim
    return jnp.tile(cos, (1, reps)), jnp.tile(sin, (1, reps))


def _hyena_tables(length):
    k = jnp.arange(length, dtype=jnp.int32)
    period = 4 * length
    step = 2.0 * math.pi / period
    lo = jnp.arange(64, dtype=jnp.int32)
    hi = jnp.arange(length // 64, dtype=jnp.int32)

    def tabs(x, y_hi, y_lo):
        def cs(y):
            ang = ((x[:, None] * y[None, :]) % period).astype(F32) * step
            return jnp.cos(ang)[:, :, None], jnp.sin(ang)[:, :, None]
        (ca, sa), (cb, sb) = cs(y_hi), cs(y_lo)
        cb, sb = jnp.swapaxes(cb, 1, 2), jnp.swapaxes(sb, 1, 2)
        c = (ca * cb - sa * sb).reshape(length, length)
        s = (sa * cb + ca * sb).reshape(length, length)
        return c.astype(BF16), s.astype(BF16)

    c_tab, s_tab = tabs(2 * k + 1, 64 * hi, lo)
    ci_tab, si_tab = tabs(k, 128 * hi, 2 * lo + 1)
    t = jnp.linspace(0.0, 1.0, length, dtype=F32)[:, None]
    bands = (HY_EMB - 1) // 2
    ang = (2.0 * math.pi / length) * jnp.arange(length, dtype=F32)[:, None] \
        * jnp.linspace(1e-4, bands - 1, bands, dtype=F32)
    feats = jnp.concatenate([t, jnp.cos(ang), -jnp.sin(ang),
                             jnp.zeros((length, LANES - HY_EMB), F32)], axis=-1)
    max_decay = math.log(HY_DECAY_TARGET) / HY_FAST_DECAY
    min_decay = math.log(HY_DECAY_TARGET) / HY_SLOW_DECAY
    deltas = jnp.abs(jnp.linspace(min_decay, max_decay, HY_WIDTH, dtype=F32)).reshape(1, HY_WIDTH)
    return c_tab, s_tab, ci_tab, si_tab, feats, deltas


def _block_diag_ones(n, group):
    i = jnp.arange(n) // group
    return (i[:, None] == i[None, :]).astype(BF16)


def kernel(x, c, ctx, c_ctx, w_ada, b_ada, g_mix, g_ffn, w_in, w_out, hy_conv_w, hy_conv_b, hy_w1, hy_b1, hy_w2, hy_b2, hy_w3, hy_freq, hy_skip, diff_lq1, diff_lk1, diff_lq2, diff_lk2, diff_subln, gqa_qnorm, gqa_knorm, ffn_wg, ffn_wu, ffn_wd, moe_router, moe_wg, moe_wu, moe_wd, g_final):
    b, seq, d = x.shape
    n_ctx = ctx.shape[1]
    depth = w_ada.shape[0]

    rope_tabs = _rope_tables(seq, DIFF_QK_DIM) + _rope_tables(seq, GQA_HEAD_DIM)
    hy_tabs = _hyena_tables(seq)
    hy_tabs_c = _hyena_tables(n_ctx)
    bd128 = _block_diag_ones(LANES, GQA_HEAD_DIM)
    bd256 = _block_diag_ones(DIFF_WIDTH, DIFF_V_DIM)

    rows = 16
    cc = jnp.zeros((rows, d), F32).at[:b].set(c).at[b].set(c_ctx)
    mods = _mods(cc, w_ada, b_ada)

    def mod_vecs(i):
        lat = [mods[i, :b, j * d:(j + 1) * d].reshape(b, 1, d) for j in range(6)]
        cx = [jnp.broadcast_to(mods[i, b, j * d:(j + 1) * d].reshape(1, 1, d), (b, 1, d)) for j in range(6)]
        return lat, cx

    h, hc = x, ctx
    for i in range(depth):
        last = i == depth - 1
        lam_init = 0.8 - 0.6 * math.exp(-0.3 * i)
        (sh_m, sc_m, gt_m, sh_f, sc_f, gt_f), (csh_m, csc_m, cgt_m, csh_f, csc_f, cgt_f) = mod_vecs(i)
        w_in_bf = w_in[i].astype(BF16)
        w_out_bf = w_out[i].astype(BF16)
        g_m = g_mix[i].reshape(1, d)
        g_f = g_ffn[i].reshape(1, d)
        qn = jnp.tile(gqa_qnorm[i], LANES // GQA_HEAD_DIM).reshape(1, LANES)
        kn = jnp.tile(gqa_knorm[i], LANES // GQA_HEAD_DIM).reshape(1, LANES)
        subln = jnp.tile(diff_subln[i], DIFF_HEADS).reshape(1, DIFF_WIDTH) * (1.0 - lam_init)
        lqk = jnp.stack([diff_lq1[i], diff_lk1[i], diff_lq2[i], diff_lk2[i]]).astype(F32)
        hy_p = (hy_conv_w[i], hy_conv_b[i], hy_w1[i], hy_b1[i], hy_w2[i], hy_b2[i], hy_w3[i],
                hy_freq[i], hy_skip[i])
        moe = i % 2 == 1
        router_pad = None
        if moe:
            router = jnp.zeros((d, LANES), F32).at[:, :N_EXPERTS].set(moe_router[i // 2])
            router_hi = router.astype(BF16)
            router_pad = jnp.stack([router_hi, (router - router_hi.astype(F32)).astype(BF16)])

        zhy, qd, kdt, vdx, qg, kgt, vgx = _inproj(h, sc_m, sh_m, g_m, w_in_bf, qn, kn, bd128, rope_tabs)
        czhy, cqd, ckdt, cvdx, cqg, ckgt, cvgx = _inproj(hc, csc_m, csh_m, g_m, w_in_bf, qn, kn, bd128, None)

        y_hy = _hyena(zhy, hy_p, hy_tabs)
        o_d = _diff_attention(lqk, qd, [kdt, ckdt], [vdx, cvdx], lam_init)
        o_g = _gqa_attention(qg, [kgt, ckgt], [vgx, cvgx])
        if not last:
            yc_hy = _hyena(czhy, hy_p, hy_tabs_c)
            oc_d = _diff_attention(lqk, cqd, [ckdt], [cvdx], lam_init)
            oc_g = _gqa_attention(cqg, [ckgt], [cvgx])

        if moe:
            n_c = 0 if last else b * n_ctx
            total = n_c + b * seq
            filled = None
            if not last:
                filled = _outproj(yc_hy, oc_d, oc_g, hc, cgt_m, w_out_bf, g_f, csc_f, csh_f, subln, bd256,
                                  router_pad, (total, 0, None))
            h_all, u_all, route = _outproj(y_hy, o_d, o_g, h, gt_m, w_out_bf, g_f, sc_f, sh_f, subln, bd256,
                                           router_pad, (total, n_c, filled))
            wg, wu, wd = (w[i // 2].astype(BF16) for w in (moe_wg, moe_wu, moe_wd))
            gt_tab = jnp.concatenate([gt_f, cgt_f[:1]], axis=0)
            tile_vec = lambda t, tm: jnp.where(t < n_c // tm, b, (t - n_c // tm) // (seq // tm))
            parts = _moe(u_all, route, wg, wu, wd, h_all, gt_tab, tile_vec,
                         [(n_c, b * seq)] + ([] if last else [(0, n_c)]))
            h = parts[0].reshape(b, seq, d)
            if not last:
                hc = parts[1].reshape(b, n_ctx, d)
        else:
            wg, wu, wd = (w[i // 2].astype(BF16) for w in (ffn_wg, ffn_wu, ffn_wd))
            mixed = _outproj(y_hy, o_d, o_g, h, gt_m, w_out_bf, g_f, sc_f, sh_f, subln, bd256)
            h = _ffn(mixed[1], wg, wu, wd, mixed[0], gt_f)
            if not last:
                cmixed = _outproj(yc_hy, oc_d, oc_g, hc, cgt_m, w_out_bf, g_f, csc_f, csh_f, subln, bd256)
                hc = _ffn(cmixed[1], wg, wu, wd, cmixed[0], cgt_f)
    return _final_norm(h, g_final)
```

```python
import functools
import math

import jax
import jax.numpy as jnp
from jax import lax
from jax.experimental import pallas as pl
from jax.experimental.pallas import tpu as pltpu
from jax.experimental.pallas import tpu_sc as plsc

F32 = jnp.float32
BF16 = jnp.bfloat16
HIGHEST = lax.Precision.HIGHEST

D_MODEL = 1024
GRID_W = 64
ROPE_THETA = 10000.0
NORM_EPS = 1e-6

HY_WIDTH = 256
HY_EMB = 33
HY_FAST_DECAY = 0.3
HY_SLOW_DECAY = 1.5
HY_DECAY_TARGET = 1e-2

DIFF_HEADS = 4
DIFF_QK_DIM = 32
DIFF_V_DIM = 64
DIFF_WIDTH = 256
GQA_HEAD_DIM = 64
GQA_WIDTH = 512
GQA_KV_WIDTH = 128
GQA_REP = 4

OFF_DQ = 3 * HY_WIDTH
OFF_DK = OFF_DQ + DIFF_WIDTH
OFF_DV = OFF_DK + DIFF_WIDTH
OFF_GQ = OFF_DV + DIFF_WIDTH
OFF_GK = OFF_GQ + GQA_WIDTH
OFF_GV = OFF_GK + GQA_KV_WIDTH
IN_COLS = OFF_GV + GQA_KV_WIDTH

N_EXPERTS = 8
LOG2_E = math.log2(math.e)
LANES = 128
VMEM_LIMIT = 56 * 1024 * 1024


def _params(*sem):
    return pltpu.CompilerParams(dimension_semantics=sem, vmem_limit_bytes=VMEM_LIMIT)


def _dot(a, b):
    return jnp.dot(a, b, preferred_element_type=F32)


def _dot_hi(a, b):
    return jnp.dot(a, b, preferred_element_type=F32, precision=HIGHEST)


def _rms(x, g):
    ms = jnp.mean(x * x, axis=-1, keepdims=True)
    return x * lax.rsqrt(ms + NORM_EPS) * g


def _silu(x):
    return x * jax.nn.sigmoid(x)


def _group_mean_sq(x, ones_bd, width):
    return _dot((x * x).astype(BF16), ones_bd) * (1.0 / width)


def _mods_kernel(c_ref, w_ref, b_ref, o_ref):
    o_ref[0] = _dot_hi(_silu(c_ref[...]), w_ref[0]) + b_ref[0]


def _mods(cc, w_ada, b_ada):
    depth, d, n = w_ada.shape
    tn = 1536
    return pl.pallas_call(
        _mods_kernel, name="mods",
        grid=(depth, n // tn),
        in_specs=[pl.BlockSpec(cc.shape, lambda i, j: (0, 0)),
                  pl.BlockSpec((1, d, tn), lambda i, j: (i, 0, j)),
                  pl.BlockSpec((1, 1, tn), lambda i, j: (i, 0, j))],
        out_specs=pl.BlockSpec((1, cc.shape[0], tn), lambda i, j: (i, 0, j)),
        out_shape=jax.ShapeDtypeStruct((depth, cc.shape[0], n), F32),
        compiler_params=_params("arbitrary", "arbitrary"),
    )(cc, w_ada, b_ada.reshape(depth, 1, n))


def _rope128(x, cos, sin_signed, half):
    lane = lax.broadcasted_iota(jnp.int32, x.shape, 1)
    first = (lane & (2 * half - 1)) < half
    swapped = jnp.where(first, pltpu.roll(x, LANES - half, 1), pltpu.roll(x, half, 1))
    return x * cos + swapped * sin_signed


def _inproj_kernel(*refs, rope):
    h_ref, sc_ref, sh_ref, g_ref, w_ref, qn_ref, kn_ref, bd_ref = refs[:8]
    if rope:
        cd_ref, sd_ref, cg_ref, sg_ref = refs[8:12]
    zhy_ref, qd_ref, kd_ref, vd_ref, qg_ref, kg_ref, vg_ref = refs[-7:]
    u = _rms(h_ref[0], g_ref[...]) * (1.0 + sc_ref[0]) + sh_ref[0]
    z = _dot(u.astype(BF16), w_ref[...])
    zhy_ref[0] = z[:, :OFF_DQ]

    def piece(off, j):
        return z[:, off + LANES * j: off + LANES * (j + 1)]

    for j in range(DIFF_WIDTH // LANES):
        q, k = piece(OFF_DQ, j), piece(OFF_DK, j)
        if rope:
            q = _rope128(q, cd_ref[...], sd_ref[...], DIFF_QK_DIM // 2)
            k = _rope128(k, cd_ref[...], sd_ref[...], DIFF_QK_DIM // 2)
        qd_ref[0, :, LANES * j: LANES * (j + 1)] = (q * (LOG2_E * DIFF_QK_DIM ** -0.5)).astype(BF16)
        kd_ref[0, 0, LANES * j: LANES * (j + 1), :] = k.T.astype(BF16)
    vd_ref[0, 0] = z[:, OFF_DV:OFF_GQ].astype(BF16)

    def gqa_piece(x, gain):
        ms = _group_mean_sq(x, bd_ref[...], GQA_HEAD_DIM)
        x = x * lax.rsqrt(ms + NORM_EPS) * gain
        if rope:
            x = _rope128(x, cg_ref[...], sg_ref[...], GQA_HEAD_DIM // 2)
        return x

    for j in range(GQA_WIDTH // LANES):
        q = gqa_piece(piece(OFF_GQ, j), qn_ref[...])
        qg_ref[0, :, LANES * j: LANES * (j + 1)] = (q * (LOG2_E * GQA_HEAD_DIM ** -0.5)).astype(BF16)
    kt = gqa_piece(piece(OFF_GK, 0), kn_ref[...]).T.astype(BF16)
    v = z[:, OFF_GV:]
    v_swapped = pltpu.roll(v, GQA_HEAD_DIM, 1)
    low = lax.broadcasted_iota(jnp.int32, v.shape, 1) < GQA_HEAD_DIM
    v_rep = (jnp.where(low, v, v_swapped), jnp.where(low, v_swapped, v))
    for g in range(GQA_KV_WIDTH // GQA_HEAD_DIM):
        for r in range(GQA_REP):
            kg_ref[0, g, GQA_HEAD_DIM * r: GQA_HEAD_DIM * (r + 1), :] = kt[GQA_HEAD_DIM * g: GQA_HEAD_DIM * (g + 1), :]
        for half in range(GQA_REP * GQA_HEAD_DIM // LANES):
            vg_ref[0, g, :, LANES * half: LANES * (half + 1)] = v_rep[g].astype(BF16)


def _inproj(h, sc, sh, g, w_bf, qn, kn, bd, rope_tabs):
    b, l, d = h.shape
    tm = min(l, 512)
    row = lambda bi, i: (bi, i, 0)
    vec = lambda bi, i: (bi, 0, 0)
    const = lambda bi, i: (0, 0)
    in_specs = [pl.BlockSpec((1, tm, d), row), pl.BlockSpec((1, 1, d), vec), pl.BlockSpec((1, 1, d), vec),
                pl.BlockSpec((1, d), const), pl.BlockSpec((d, IN_COLS), const),
                pl.BlockSpec((1, LANES), const), pl.BlockSpec((1, LANES), const),
                pl.BlockSpec((LANES, LANES), const)]
    args = [h, sc, sh, g, w_bf, qn, kn, bd]
    if rope_tabs is not None:
        in_specs += [pl.BlockSpec((tm, LANES), lambda bi, i: (i, 0))] * 4
        args += list(rope_tabs)
    aw = ATTN_WIDTH
    kv_groups = GQA_KV_WIDTH // GQA_HEAD_DIM
    rows_spec = lambda w: pl.BlockSpec((1, tm, w), row)
    kt_spec = lambda g: pl.BlockSpec((1, g, aw, tm), lambda bi, i: (bi, 0, 0, i))
    v_spec = lambda g: pl.BlockSpec((1, g, tm, aw), lambda bi, i: (bi, 0, i, 0))
    sds = jax.ShapeDtypeStruct
    return pl.pallas_call(
        functools.partial(_inproj_kernel, rope=rope_tabs is not None), name="inproj",
        grid=(b, l // tm),
        in_specs=in_specs,
        out_specs=[rows_spec(OFF_DQ), rows_spec(DIFF_WIDTH), kt_spec(1), v_spec(1),
                   rows_spec(GQA_WIDTH), kt_spec(kv_groups), v_spec(kv_groups)],
        out_shape=[sds((b, l, OFF_DQ), F32), sds((b, l, DIFF_WIDTH), BF16), sds((b, 1, aw, l), BF16),
                   sds((b, 1, l, aw), BF16), sds((b, l, GQA_WIDTH), BF16), sds((b, kv_groups, aw, l), BF16),
                   sds((b, kv_groups, l, aw), BF16)],
        compiler_params=_params("arbitrary", "arbitrary"),
    )(*args)


def _hy_filter_kernel(f_ref, w1_ref, b1_ref, w2_ref, b2_ref, w3_ref, fr_ref, dl_ref, hs_ref, hd_ref):
    f = f_ref[...]
    fr = fr_ref[...]
    a = jnp.sin(fr * (_dot_hi(f, w1_ref[...]) + b1_ref[...]))
    a = jnp.sin(fr * (_dot_hi(a, w2_ref[...]) + b2_ref[...]))
    hf = _dot_hi(a, w3_ref[...])
    decay = jnp.exp(-f[:, 0:1] * dl_ref[...])
    h_fwd = hf[:, :HY_WIDTH] * decay
    h_bwd = hf[:, HY_WIDTH:] * decay
    row = lax.broadcasted_iota(jnp.int32, h_bwd.shape, 0) + pl.program_id(0) * f.shape[0]
    h_bwd = jnp.where(row == 0, 0.0, h_bwd)
    hs_ref[...] = (h_fwd + h_bwd).astype(BF16)
    hd_ref[...] = (h_bwd - h_fwd).astype(BF16)


def _hy_spectrum_kernel(c_ref, s_ref, hs_ref, hd_ref, gre_ref, gim_ref, *, scale):
    gre_ref[...] = _dot(c_ref[...], hs_ref[...]) * scale
    gim_ref[...] = _dot(s_ref[...], hd_ref[...]) * scale


def _hy_pre_kernel(z_ref, zp_ref, zn_ref, cw_ref, cb_ref, wb_ref, wf_ref, x0_ref, *, nt):
    ti = pl.program_id(1)
    z = z_ref[0]
    tl = z.shape[0]
    row = lax.broadcasted_iota(jnp.int32, z.shape, 0)
    prev_row = jnp.where(ti > 0, zp_ref[0, 7:8, :], 0.0)
    next_row = jnp.where(ti < nt - 1, zn_ref[0, 0:1, :], 0.0)
    z_prev = jnp.where(row == 0, prev_row, pltpu.roll(z, 1, 0))
    z_next = jnp.where(row == tl - 1, next_row, pltpu.roll(z, tl - 1, 0))
    cw = cw_ref[...]
    y = cb_ref[...] + z_prev * cw[0:1] + z * cw[1:2] + z_next * cw[2:3]
    x0, x1, v = y[:, :HY_WIDTH], y[:, HY_WIDTH:2 * HY_WIDTH], y[:, 2 * HY_WIDTH:]
    w = v * x1
    wb_ref[...] = w.astype(BF16)
    wf_ref[0] = w
    x0_ref[0] = x0


def _hy_fwd_kernel(c_ref, s_ref, x_ref, gre_ref, gim_ref, yre_ref, yim_ref, *, reps):
    x = x_ref[...]
    a = _dot(c_ref[...], x)
    b = _dot(s_ref[...], x)
    gre = jnp.concatenate([gre_ref[...]] * reps, axis=1)
    gim = jnp.concatenate([gim_ref[...]] * reps, axis=1)
    yre_ref[...] = (a * gre + b * gim).astype(BF16)
    yim_ref[...] = (a * gim - b * gre).astype(BF16)


def _hy_inv_kernel(ci_ref, si_ref, yre_ref, yim_ref, wf_ref, x0_ref, skip_ref, o_ref, *, reps):
    y = _dot(ci_ref[...], yre_ref[...]) - _dot(si_ref[...], yim_ref[...])
    for r in range(reps):
        yr = y[:, r * HY_WIDTH:(r + 1) * HY_WIDTH]
        o_ref[r] = ((yr + wf_ref[r] * skip_ref[...]) * x0_ref[r]).astype(BF16)


def _hyena(zhy, p, tabs):
    conv_w, conv_b, w1, b1, w2, b2, w3, freq, skip = p
    c_tab, s_tab, ci_tab, si_tab, feats, deltas = tabs
    b, l, _ = zhy.shape
    c = HY_WIDTH
    tl = min(l, 512)
    nt = l // tl
    hid = w2.shape[0]
    const1 = lambda i: (0, 0)
    w1p = jnp.zeros((LANES, hid), F32).at[:HY_EMB].set(w1)
    hs, hd = pl.pallas_call(
        _hy_filter_kernel, name="hy_filter",
        grid=(nt,),
        in_specs=[pl.BlockSpec((tl, LANES), lambda i: (i, 0)),
                  pl.BlockSpec((LANES, hid), const1), pl.BlockSpec((1, hid), const1),
                  pl.BlockSpec((hid, hid), const1), pl.BlockSpec((1, hid), const1),
                  pl.BlockSpec((hid, 2 * c), const1), pl.BlockSpec((1, hid), const1),
                  pl.BlockSpec((1, c), const1)],
        out_specs=[pl.BlockSpec((tl, c), lambda i: (i, 0))] * 2,
        out_shape=[jax.ShapeDtypeStruct((l, c), BF16)] * 2,
        compiler_params=_params("arbitrary"),
    )(feats, w1p, b1.reshape(1, hid), w2, b2.reshape(1, hid), w3, freq.reshape(1, hid), deltas)

    gre, gim = pl.pallas_call(
        functools.partial(_hy_spectrum_kernel, scale=1.0 / l), name="hy_spectrum",
        grid=(nt,),
        in_specs=[pl.BlockSpec((tl, l), lambda i: (i, 0)), pl.BlockSpec((tl, l), lambda i: (i, 0)),
                  pl.BlockSpec((l, c), const1), pl.BlockSpec((l, c), const1)],
        out_specs=[pl.BlockSpec((tl, c), lambda i: (i, 0))] * 2,
        out_shape=[jax.ShapeDtypeStruct((l, c), F32)] * 2,
        compiler_params=_params("arbitrary"),
    )(c_tab, s_tab, hs, hd)

    halo = 8
    wb, wf, x0 = pl.pallas_call(
        functools.partial(_hy_pre_kernel, nt=nt), name="hy_pre",
        grid=(b, nt),
        in_specs=[pl.BlockSpec((1, tl, 3 * c), lambda bi, i: (bi, i, 0)),
                  pl.BlockSpec((1, halo, 3 * c), lambda bi, i: (bi, jnp.maximum(i * (tl // halo) - 1, 0), 0)),
                  pl.BlockSpec((1, halo, 3 * c),
                               lambda bi, i: (bi, jnp.minimum((i + 1) * (tl // halo), l // halo - 1), 0)),
                  pl.BlockSpec((3, 3 * c), lambda bi, i: (0, 0)),
                  pl.BlockSpec((1, 3 * c), lambda bi, i: (0, 0))],
        out_specs=[pl.BlockSpec((tl, c), lambda bi, i: (i, bi)),
                   pl.BlockSpec((1, tl, c), lambda bi, i: (bi, i, 0)),
                   pl.BlockSpec((1, tl, c), lambda bi, i: (bi, i, 0))],
        out_shape=[jax.ShapeDtypeStruct((l, b * c), BF16),
                   jax.ShapeDtypeStruct((b, l, c), F32),
                   jax.ShapeDtypeStruct((b, l, c), F32)],
        compiler_params=_params("arbitrary", "arbitrary"),
    )(zhy, zhy, zhy, conv_w, conv_b.reshape(1, 3 * c))

    reps = 2
    tn = reps * c
    nj = b * c // tn
    yre, yim = pl.pallas_call(
        functools.partial(_hy_fwd_kernel, reps=reps), name="hy_fwd",
        grid=(nj, nt),
        in_specs=[pl.BlockSpec((tl, l), lambda j, i: (i, 0)), pl.BlockSpec((tl, l), lambda j, i: (i, 0)),
                  pl.BlockSpec((l, tn), lambda j, i: (0, j)),
                  pl.BlockSpec((tl, c), lambda j, i: (i, 0)), pl.BlockSpec((tl, c), lambda j, i: (i, 0))],
        out_specs=[pl.BlockSpec((tl, tn), lambda j, i: (i, j))] * 2,
        out_shape=[jax.ShapeDtypeStruct((l, b * c), BF16)] * 2,
        compiler_params=_params("arbitrary", "arbitrary"),
    )(c_tab, s_tab, wb, gre, gim)

    return pl.pallas_call(
        functools.partial(_hy_inv_kernel, reps=reps), name="hy_inv",
        grid=(nj, nt),
        in_specs=[pl.BlockSpec((tl, l), lambda j, i: (i, 0)), pl.BlockSpec((tl, l), lambda j, i: (i, 0)),
                  pl.BlockSpec((l, tn), lambda j, i: (0, j)), pl.BlockSpec((l, tn), lambda j, i: (0, j)),
                  pl.BlockSpec((reps, tl, c), lambda j, i: (j, i, 0)),
                  pl.BlockSpec((reps, tl, c), lambda j, i: (j, i, 0)),
                  pl.BlockSpec((1, c), lambda j, i: (0, 0))],
        out_specs=pl.BlockSpec((reps, tl, c), lambda j, i: (j, i, 0)),
        out_shape=jax.ShapeDtypeStruct((b, l, c), BF16),
        compiler_params=_params("arbitrary", "arbitrary"),
    )(ci_tab, si_tab, yre, yim, wf, x0, skip.reshape(1, c))


def _lane_mask(width, shift, idx):
    lane = lax.broadcasted_iota(jnp.int32, (1, width), 1)
    return jnp.where((lane >> shift) == idx, 1.0, 0.0).astype(BF16)


ATTN_WIDTH = 256
ATTN_CHUNK = 512


def _key_chunks(lk):
    return [(c0, min(ATTN_CHUNK, lk - c0)) for c0 in range(0, lk, ATTN_CHUNK)]


ATTN_STACK = 4
ATTN_TQ = 128
ATTN_STEP_ROWS = 512


def _qk_stage(q, shift, first, kt_refs, qs_scr, s_scr):
    tq = q.shape[0]
    for i in range(ATTN_STACK):
        qs_scr[i * tq:(i + 1) * tq, :] = q * _lane_mask(ATTN_WIDTH, shift, first + i)
    qs = qs_scr[...]
    m = jnp.full((ATTN_STACK * tq, LANES), -jnp.inf, F32)
    base = 0
    for kt_ref in kt_refs:
        for c0, ck in _key_chunks(kt_ref.shape[-1]):
            s = _dot(qs, kt_ref[:, c0:c0 + ck])
            s_scr[:, base + c0:base + c0 + ck] = s
            for j in range(ck // LANES):
                m = jnp.maximum(m, s[:, j * LANES:(j + 1) * LANES])
        base += kt_ref.shape[-1]
    return jnp.max(m, axis=-1, keepdims=True)


def _exp_stage(s_scr, mx, p_scr):
    l = jnp.zeros((s_scr.shape[0], LANES), F32)
    for c0, ck in _key_chunks(s_scr.shape[-1]):
        e = jnp.exp2(s_scr[:, c0:c0 + ck] - mx)
        p_scr[:, c0:c0 + ck] = e.astype(BF16)
        for j in range(ck // LANES):
            l = l + e[:, j * LANES:(j + 1) * LANES]
    return 1.0 / jnp.sum(l, axis=-1, keepdims=True)


def _attention_passes(passes, consume, kt_refs, v_refs, scratch):
    qs, ss, ps = scratch[0:2], scratch[2:4], scratch[4:6]
    tq = passes[0][0].shape[0]

    def qk(i):
        q, shift, first, g = passes[i]
        return _qk_stage(q, shift, first, [kt.at[0, g] for kt in kt_refs], qs[i % 2], ss[i % 2])

    mx = qk(0)
    for i in range(len(passes)):
        mx_next = qk(i + 1) if i + 1 < len(passes) else None
        inv = _exp_stage(ss[i % 2], mx, ps[i % 2])
        pv, base = None, 0
        for v_ref in v_refs:
            lk = v_ref.shape[2]
            part = _dot(ps[i % 2][:, base:base + lk], v_ref[0, passes[i][3]])
            pv = part if pv is None else pv + part
            base += lk
        pv = pv * inv
        consume(i, [pv[r * tq:(r + 1) * tq] for r in range(ATTN_STACK)])
        mx = mx_next


def _diff_kernel(lqk_ref, q_ref, *refs, n_kv, lam_init):
    kt_refs, v_refs, o_ref, scratch = refs[:n_kv], refs[n_kv:2 * n_kv], refs[2 * n_kv], refs[2 * n_kv + 1:]
    x = lqk_ref[...]
    lam = (jnp.exp(jnp.sum(x[0:1] * x[1:2], axis=-1, keepdims=True))
           - jnp.exp(jnp.sum(x[2:3] * x[3:4], axis=-1, keepdims=True)) + lam_init)
    tq = min(ATTN_TQ, q_ref.shape[1])
    group = lax.broadcasted_iota(jnp.int32, (tq, ATTN_WIDTH), 1) >> 6
    halves = 2 * DIFF_HEADS // ATTN_STACK
    passes = []
    for t in range(q_ref.shape[1] // tq):
        q = q_ref[0, t * tq:(t + 1) * tq, :]
        passes += [(q, 5, half * ATTN_STACK, 0) for half in range(halves)]
    acc = {}

    def consume(i, pvs):
        t, half = divmod(i, halves)
        o1, o2 = acc.pop(t, (jnp.zeros(group.shape, F32),) * 2)
        for r, pv in enumerate(pvs):
            head = (half * ATTN_STACK + r) >> 1
            if r & 1 == 0:
                o1 = jnp.where(group == head, pv, o1)
            else:
                o2 = jnp.where(group == head, pv, o2)
        if half + 1 < halves:
            acc[t] = (o1, o2)
            return
        o_ref[0, t * tq:(t + 1) * tq, :] = (o1 - lam * o2).astype(BF16)

    _attention_passes(passes, consume, kt_refs, v_refs, scratch)


def _gqa_kernel(q_ref, *refs, n_kv):
    kt_refs, v_refs, o_ref, scratch = refs[:n_kv], refs[n_kv:2 * n_kv], refs[2 * n_kv], refs[2 * n_kv + 1:]
    tq = min(ATTN_TQ, q_ref.shape[1])
    group = lax.broadcasted_iota(jnp.int32, (tq, ATTN_WIDTH), 1) >> 6
    groups = q_ref.shape[2] // ATTN_WIDTH
    passes = []
    for t in range(q_ref.shape[1] // tq):
        for g in range(groups):
            passes.append((q_ref[0, t * tq:(t + 1) * tq, g * ATTN_WIDTH:(g + 1) * ATTN_WIDTH], 6, 0, g))

    def consume(i, pvs):
        t, g = divmod(i, groups)
        o = jnp.zeros(group.shape, F32)
        for r, pv in enumerate(pvs):
            o = jnp.where(group == r, pv, o)
        o_ref[0, t * tq:(t + 1) * tq, g * ATTN_WIDTH:(g + 1) * ATTN_WIDTH] = o.astype(BF16)

    _attention_passes(passes, consume, kt_refs, v_refs, scratch)


def _attention(kernel_fn, name, q, kts, vs, extra=()):
    b, lq, width = q.shape
    groups, w = kts[0].shape[1:3]
    lk = sum(kt.shape[3] for kt in kts)
    rows = min(lq, ATTN_STEP_ROWS)
    pass_rows = ATTN_STACK * min(ATTN_TQ, rows)
    kv_spec = lambda a: pl.BlockSpec((1,) + a.shape[1:], lambda bi, i: (bi, 0, 0, 0))
    return pl.pallas_call(
        functools.partial(kernel_fn, n_kv=len(kts)), name=name,
        grid=(b, lq // rows),
        in_specs=[pl.BlockSpec(a.shape, lambda bi, i: (0, 0)) for a in extra]
        + [pl.BlockSpec((1, rows, width), lambda bi, i: (bi, i, 0))]
        + [kv_spec(a) for a in kts] + [kv_spec(a) for a in vs],
        out_specs=pl.BlockSpec((1, rows, width), lambda bi, i: (bi, i, 0)),
        out_shape=jax.ShapeDtypeStruct((b, lq, width), BF16),
        scratch_shapes=[pltpu.VMEM((pass_rows, w), BF16)] * 2 + [pltpu.VMEM((pass_rows, lk), F32)] * 2
        + [pltpu.VMEM((pass_rows, lk), BF16)] * 2,
        compiler_params=_params("arbitrary", "arbitrary"),
    )(*extra, q, *kts, *vs)


def _diff_attention(lqk, q, kts, vs, lam_init):
    return _attention(functools.partial(_diff_kernel, lam_init=lam_init), "diff_attn", q, kts, vs, (lqk,))


def _gqa_attention(q, kts, vs):
    return _attention(_gqa_kernel, "gqa_attn", q, kts, vs)


def _top2_route(logits):
    lane = lax.broadcasted_iota(jnp.int32, logits.shape, 1).astype(F32)
    neg = jnp.float32(-jnp.inf)
    lg = jnp.where(lane < N_EXPERTS, logits, neg)
    m1 = jnp.max(lg, axis=-1, keepdims=True)
    i1 = jnp.min(jnp.where(lg == m1, lane, float(LANES)), axis=-1, keepdims=True)
    lg2 = jnp.where(lane == i1, neg, lg)
    m2 = jnp.max(lg2, axis=-1, keepdims=True)
    i2 = jnp.min(jnp.where(lg2 == m2, lane, float(LANES)), axis=-1, keepdims=True)
    e2 = jnp.exp(m2 - m1)
    w1 = 1.0 / (1.0 + e2)
    return (jnp.where(lane == 0, i1, 0.0) + jnp.where(lane == 1, i2, 0.0)
            + jnp.where(lane == 2, w1, 0.0) + jnp.where(lane == 3, e2 * w1, 0.0))


def _outproj_kernel(*refs, moe):
    yhy_ref, od_ref, og_ref, h_ref, gt_ref, w_ref, gf_ref, sc_ref, sh_ref, sg_ref, bd_ref = refs[:11]
    if moe:
        rt_ref = refs[11]
        hn_ref, u_ref, gates_ref = refs[-3:]
    else:
        hn_ref, u_ref = refs[11:]
    od = od_ref[0].astype(F32)
    od = od * lax.rsqrt(_group_mean_sq(od, bd_ref[...], DIFF_V_DIM) + NORM_EPS) * sg_ref[...]
    y = (_dot(yhy_ref[0], w_ref[0:HY_WIDTH, :])
         + _dot(od.astype(BF16), w_ref[HY_WIDTH:HY_WIDTH + DIFF_WIDTH, :])
         + _dot(og_ref[0], w_ref[HY_WIDTH + DIFF_WIDTH:, :]))
    hn = h_ref[0] + gt_ref[0] * y
    hn_ref[...] = hn.reshape(hn_ref.shape)
    u = _rms(hn, gf_ref[...]) * (1.0 + sc_ref[0]) + sh_ref[0]
    u_ref[...] = u.astype(u_ref.dtype).reshape(u_ref.shape)
    if moe:
        u_hi = u.astype(BF16)
        u_lo = (u - u_hi.astype(F32)).astype(BF16)
        logits = _dot(u_hi, rt_ref[0]) + (_dot(u_lo, rt_ref[0]) + _dot(u_hi, rt_ref[1]))
        gates_ref[...] = _top2_route(logits)


def _outproj(yhy, od, og, h, gt, w_bf, gf, sc, sh, sub_g, bd256, router_pad=None, into=None):
    b, l, d = h.shape
    tm = min(l, 512)
    row = lambda bi, i: (bi, i, 0)
    vec = lambda bi, i: (bi, 0, 0)
    const = lambda bi, i: (0, 0)
    moe = router_pad is not None
    in_specs = [pl.BlockSpec((1, tm, HY_WIDTH), row), pl.BlockSpec((1, tm, DIFF_WIDTH), row),
                pl.BlockSpec((1, tm, GQA_WIDTH), row), pl.BlockSpec((1, tm, d), row),
                pl.BlockSpec((1, 1, d), vec), pl.BlockSpec((d, d), const), pl.BlockSpec((1, d), const),
                pl.BlockSpec((1, 1, d), vec), pl.BlockSpec((1, 1, d), vec),
                pl.BlockSpec((1, DIFF_WIDTH), const), pl.BlockSpec((DIFF_WIDTH, DIFF_WIDTH), const)]
    args = [yhy, od, og, h, gt, w_bf, gf, sc, sh, sub_g, bd256]
    out_specs = [pl.BlockSpec((1, tm, d), row), pl.BlockSpec((1, tm, d), row)]
    out_shape = [jax.ShapeDtypeStruct((b, l, d), F32), jax.ShapeDtypeStruct((b, l, d), F32 if moe else BF16)]
    aliases = {}
    if moe:
        total_rows, row_offset, filled = into
        in_specs.append(pl.BlockSpec((2, d, LANES), lambda bi, i: (0, 0, 0)))
        args.append(router_pad)
        flat_row = lambda bi, i: (row_offset // tm + bi * (l // tm) + i, 0)
        widths = (d, d, LANES)
        out_specs = [pl.BlockSpec((tm, w), flat_row) for w in widths]
        out_shape = [jax.ShapeDtypeStruct((total_rows, w), F32) for w in widths]
        if filled is not None:
            aliases = {len(args) + j: j for j in range(len(filled))}
            in_specs += [pl.BlockSpec(memory_space=pl.ANY)] * len(filled)
            args += list(filled)
    return pl.pallas_call(
        functools.partial(_outproj_kernel, moe=moe), name="outproj",
        grid=(b, l // tm), in_specs=in_specs, out_specs=out_specs, out_shape=out_shape,
        input_output_aliases=aliases,
        compiler_params=_params("arbitrary", "arbitrary"),
    )(*args)


def _ffn_kernel(u_ref, wg_ref, wu_ref, wd_ref, h_ref, gt_ref, o_ref):
    u = u_ref[0]
    hid = _silu(_dot(u, wg_ref[...])) * _dot(u, wu_ref[...])
    o_ref[0] = h_ref[0] + gt_ref[0] * _dot(hid.astype(BF16), wd_ref[...])


def _ffn(u, wg, wu, wd, h, gt):
    b, l, d = h.shape
    tm = min(l, 512)
    row = lambda bi, i: (bi, i, 0)
    resident = lambda a: pl.BlockSpec(a.shape, lambda bi, i: (0, 0), pipeline_mode=pl.Buffered(1))
    return pl.pallas_call(
        _ffn_kernel, name="ffn",
        grid=(b, l // tm),
        in_specs=[pl.BlockSpec((1, tm, d), row), resident(wg), resident(wu), resident(wd),
                  pl.BlockSpec((1, tm, d), row), pl.BlockSpec((1, 1, d), lambda bi, i: (bi, 0, 0))],
        out_specs=pl.BlockSpec((1, tm, d), row),
        out_shape=jax.ShapeDtypeStruct((b, l, d), F32),
        compiler_params=_params("arbitrary", "arbitrary"),
    )(u, wg, wu, wd, h, gt)


MOE_TILE = 512
SC_CHUNK = 32


def _sc_gather(table, idx):
    info = plsc.get_sparse_core_info()
    nc, ns = info.num_cores, info.num_subcores
    n, d = idx.shape[0], table.shape[1]
    per_worker = n // (nc * ns)
    n_chunks = per_worker // SC_CHUNK
    assert per_worker * nc * ns == n and n_chunks * SC_CHUNK == per_worker and n_chunks % 2 == 0
    mesh = plsc.VectorSubcoreMesh(core_axis_name="c", subcore_axis_name="s")

    @functools.partial(
        pl.kernel, mesh=mesh,
        out_type=jax.ShapeDtypeStruct((n, d), table.dtype),
        scratch_types=[pltpu.VMEM((SC_CHUNK,), jnp.int32)] * 2 + [pltpu.VMEM((SC_CHUNK, d), table.dtype)] * 2
        + [pltpu.SemaphoreType.DMA] * 2,
    )
    def gather_kernel(table_hbm, idx_hbm, out_hbm, idx0, idx1, rows0, rows1, sem0, sem1):
        base = (lax.axis_index("s") * nc + lax.axis_index("c")) * per_worker
        bufs = ((idx0, rows0, sem0), (idx1, rows1, sem1))

        def rows_at(j):
            return pl.ds(pl.multiple_of(base + j * SC_CHUNK, 8), SC_CHUNK)

        def start(j, b):
            idx_v, rows_v, sem = bufs[b]
            pltpu.sync_copy(idx_hbm.at[rows_at(j)], idx_v)
            pltpu.make_async_copy(table_hbm.at[idx_v], rows_v, sem).start()

        def finish(j, b):
            idx_v, rows_v, sem = bufs[b]
            pltpu.make_async_copy(table_hbm.at[idx_v], rows_v, sem).wait()
            pltpu.sync_copy(rows_v, out_hbm.at[rows_at(j)])

        start(0, 0)

        @pl.loop(0, n_chunks - 2, step=2)
        def _(j):
            start(j + 1, 1)
            finish(j, 0)
            start(j + 2, 0)
            finish(j + 1, 1)

        start(n_chunks - 1, 1)
        finish(n_chunks - 2, 0)
        finish(n_chunks - 1, 1)

    return gather_kernel(table, idx)


def _route_plan(route, n_rows):
    t = route.shape[0]
    e = route[:, :2].astype(jnp.int32).T.reshape(2 * t)
    onehot = (e[:, None] == jnp.arange(N_EXPERTS, dtype=jnp.int32)[None, :]).astype(jnp.int32)
    csum = jnp.cumsum(onehot, axis=0)
    counts = csum[-1]
    rank = jnp.take_along_axis(csum, e[:, None], axis=1)[:, 0] - 1
    padded = (counts + MOE_TILE - 1) // MOE_TILE * MOE_TILE
    ends = jnp.cumsum(padded)
    pos = (ends - padded)[e] + rank
    row_token = jnp.zeros((n_rows,), jnp.int32).at[pos].set(jnp.arange(2 * t, dtype=jnp.int32) % t)
    tile_start = jnp.arange(n_rows // MOE_TILE, dtype=jnp.int32) * MOE_TILE
    tile_expert = jnp.minimum(jnp.sum((tile_start[:, None] >= ends[None, :]).astype(jnp.int32), axis=1),
                              N_EXPERTS - 1)
    n_used = (ends[-1] // MOE_TILE).reshape(1)
    return pos, row_token, tile_expert, n_used


def _gmm_kernel(te_ref, nu_ref, x_ref, wg_ref, wu_ref, wd_ref, o_ref):
    @pl.when(pl.program_id(0) < nu_ref[0])
    def _():
        x = x_ref[...].astype(BF16)
        half = wg_ref.shape[2] // 2
        out = None
        for f0 in (0, half):
            hid = _silu(_dot(x, wg_ref[0, :, f0:f0 + half])) * _dot(x, wu_ref[0, :, f0:f0 + half])
            part = _dot(hid.astype(BF16), wd_ref[0, f0:f0 + half, :])
            out = part if out is None else out + part
        o_ref[...] = out


def _grouped_swiglu(x_sorted, tile_expert, n_used, wg, wu, wd):
    r, d = x_sorted.shape
    weights = lambda a: pl.BlockSpec((1,) + a.shape[1:], lambda t, te, nu: (te[t], 0, 0),
                                     pipeline_mode=pl.Buffered(1))
    return pl.pallas_call(
        _gmm_kernel, name="grouped_swiglu",
        grid_spec=pltpu.PrefetchScalarGridSpec(
            num_scalar_prefetch=2,
            grid=(r // MOE_TILE,),
            in_specs=[pl.BlockSpec((MOE_TILE, d), lambda t, te, nu: (t, 0)),
                      weights(wg), weights(wu), weights(wd)],
            out_specs=pl.BlockSpec((MOE_TILE, d), lambda t, te, nu: (t, 0))),
        out_shape=jax.ShapeDtypeStruct((r, d), F32),
        compiler_params=_params("arbitrary"),
    )(tile_expert, n_used, x_sorted, wg, wu, wd)


def _moe_combine_kernel(y1_ref, y2_ref, rt_ref, h_ref, gt_ref, o_ref):
    rt = rt_ref[...]
    out = rt[:, 2:3] * y1_ref[...] + rt[:, 3:4] * y2_ref[...]
    o_ref[...] = h_ref[...] + gt_ref[0] * out


def _moe_combine(y_pair, route, h, gt_tab, tile_vec, row0, n):
    t, d = h.shape
    tm = 512
    first = row0 // tm
    return pl.pallas_call(
        _moe_combine_kernel, name="moe_combine",
        grid=(n // tm,),
        in_specs=[pl.BlockSpec((tm, d), lambda i: (first + i, 0)),
                  pl.BlockSpec((tm, d), lambda i: (first + i + t // tm, 0)),
                  pl.BlockSpec((tm, LANES), lambda i: (first + i, 0)),
                  pl.BlockSpec((tm, d), lambda i: (first + i, 0)),
                  pl.BlockSpec((1, 1, d), lambda i: (tile_vec(first + i, tm), 0, 0))],
        out_specs=pl.BlockSpec((tm, d), lambda i: (i, 0)),
        out_shape=jax.ShapeDtypeStruct((n, d), F32),
        compiler_params=_params("arbitrary"),
    )(y_pair, y_pair, route, h, gt_tab)


def _moe(u, route, wg, wu, wd, h, gt_tab, tile_vec, splits):
    t, d = u.shape
    n_rows = 2 * t + N_EXPERTS * MOE_TILE
    pos, row_token, tile_expert, n_used = _route_plan(route, n_rows)
    x_sorted = _sc_gather(u, row_token)
    y_sorted = _grouped_swiglu(x_sorted, tile_expert, n_used, wg, wu, wd)
    y_pair = _sc_gather(y_sorted, pos)
    return [_moe_combine(y_pair, route, h, gt_tab, tile_vec, row0, n) for row0, n in splits]


def _final_norm_kernel(h_ref, g_ref, o_ref):
    o_ref[0] = _rms(h_ref[0], g_ref[...])


def _final_norm(h, g):
    b, l, d = h.shape
    tm = min(l, 1024)
    return pl.pallas_call(
        _final_norm_kernel, name="final_norm",
        grid=(b, l // tm),
        in_specs=[pl.BlockSpec((1, tm, d), lambda bi, i: (bi, i, 0)), pl.BlockSpec((1, d), lambda bi, i: (0, 0))],
        out_specs=pl.BlockSpec((1, tm, d), lambda bi, i: (bi, i, 0)),
        out_shape=jax.ShapeDtypeStruct((b, l, d), F32),
        compiler_params=_params("arbitrary", "arbitrary"),
    )(h, g.reshape(1, d))


def _rope_tables(length, head_dim):
    t = jnp.arange(length)
    row = (t // GRID_W).astype(F32)
    col = (t % GRID_W).astype(F32)
    n = head_dim // 4
    inv = ROPE_THETA ** (-jnp.arange(n, dtype=F32) / n)
    ang = jnp.concatenate([row[:, None] * inv, col[:, None] * inv], axis=-1)
    cos = jnp.concatenate([jnp.cos(ang)] * 2, axis=-1)
    sin = jnp.concatenate([-jnp.sin(ang), jnp.sin(ang)], axis=-1)
    reps = LANES // head_dim
    return jnp.tile(cos, (1, reps)), jnp.tile(sin, (1, reps))


def _hyena_tables(length):
    k = jnp.arange(length, dtype=jnp.int32)
    period = 4 * length
    step = 2.0 * math.pi / period
    lo = jnp.arange(64, dtype=jnp.int32)
    hi = jnp.arange(length // 64, dtype=jnp.int32)

    def tabs(x, y_hi, y_lo):
        def cs(y):
            ang = ((x[:, None] * y[None, :]) % period).astype(F32) * step
            return jnp.cos(ang)[:, :, None], jnp.sin(ang)[:, :, None]
        (ca, sa), (cb, sb) = cs(y_hi), cs(y_lo)
        cb, sb = jnp.swapaxes(cb, 1, 2), jnp.swapaxes(sb, 1, 2)
        c = (ca * cb - sa * sb).reshape(length, length)
        s = (sa * cb + ca * sb).reshape(length, length)
        return c.astype(BF16), s.astype(BF16)

    c_tab, s_tab = tabs(2 * k + 1, 64 * hi, lo)
    ci_tab, si_tab = tabs(k, 128 * hi, 2 * lo + 1)
    t = jnp.linspace(0.0, 1.0, length, dtype=F32)[:, None]
    bands = (HY_EMB - 1) // 2
    ang = (2.0 * math.pi / length) * jnp.arange(length, dtype=F32)[:, None] \
        * jnp.linspace(1e-4, bands - 1, bands, dtype=F32)
    feats = jnp.concatenate([t, jnp.cos(ang), -jnp.sin(ang),
                             jnp.zeros((length, LANES - HY_EMB), F32)], axis=-1)
    max_decay = math.log(HY_DECAY_TARGET) / HY_FAST_DECAY
    min_decay = math.log(HY_DECAY_TARGET) / HY_SLOW_DECAY
    deltas = jnp.abs(jnp.linspace(min_decay, max_decay, HY_WIDTH, dtype=F32)).reshape(1, HY_WIDTH)
    return c_tab, s_tab, ci_tab, si_tab, feats, deltas


def _block_diag_ones(n, group):
    i = jnp.arange(n) // group
    return (i[:, None] == i[None, :]).astype(BF16)


def kernel(x, c, ctx, c_ctx, w_ada, b_ada, g_mix, g_ffn, w_in, w_out, hy_conv_w, hy_conv_b, hy_w1, hy_b1, hy_w2, hy_b2, hy_w3, hy_freq, hy_skip, diff_lq1, diff_lk1, diff_lq2, diff_lk2, diff_subln, gqa_qnorm, gqa_knorm, ffn_wg, ffn_wu, ffn_wd, moe_router, moe_wg, moe_wu, moe_wd, g_final):
    b, seq, d = x.shape
    n_ctx = ctx.shape[1]
    depth = w_ada.shape[0]

    rope_tabs = _rope_tables(seq, DIFF_QK_DIM) + _rope_tables(seq, GQA_HEAD_DIM)
    hy_tabs = _hyena_tables(seq)
    hy_tabs_c = _hyena_tables(n_ctx)
    bd128 = _block_diag_ones(LANES, GQA_HEAD_DIM)
    bd256 = _block_diag_ones(DIFF_WIDTH, DIFF_V_DIM)

    rows = 16
    cc = jnp.zeros((rows, d), F32).at[:b].set(c).at[b].set(c_ctx)
    mods = _mods(cc, w_ada, b_ada)

    def mod_vecs(i):
        lat = [mods[i, :b, j * d:(j + 1) * d].reshape(b, 1, d) for j in range(6)]
        cx = [jnp.broadcast_to(mods[i, b, j * d:(j + 1) * d].reshape(1, 1, d), (b, 1, d)) for j in range(6)]
        return lat, cx

    h, hc = x, ctx
    for i in range(depth):
        last = i == depth - 1
        lam_init = 0.8 - 0.6 * math.exp(-0.3 * i)
        (sh_m, sc_m, gt_m, sh_f, sc_f, gt_f), (csh_m, csc_m, cgt_m, csh_f, csc_f, cgt_f) = mod_vecs(i)
        w_in_bf = w_in[i].astype(BF16)
        w_out_bf = w_out[i].astype(BF16)
        g_m = g_mix[i].reshape(1, d)
        g_f = g_ffn[i].reshape(1, d)
        qn = jnp.tile(gqa_qnorm[i], LANES // GQA_HEAD_DIM).reshape(1, LANES)
        kn = jnp.tile(gqa_knorm[i], LANES // GQA_HEAD_DIM).reshape(1, LANES)
        subln = jnp.tile(diff_subln[i], DIFF_HEADS).reshape(1, DIFF_WIDTH) * (1.0 - lam_init)
        lqk = jnp.stack([diff_lq1[i], diff_lk1[i], diff_lq2[i], diff_lk2[i]]).astype(F32)
        hy_p = (hy_conv_w[i], hy_conv_b[i], hy_w1[i], hy_b1[i], hy_w2[i], hy_b2[i], hy_w3[i],
                hy_freq[i], hy_skip[i])
        moe = i % 2 == 1
        router_pad = None
        if moe:
            router = jnp.zeros((d, LANES), F32).at[:, :N_EXPERTS].set(moe_router[i // 2])
            router_hi = router.astype(BF16)
            router_pad = jnp.stack([router_hi, (router - router_hi.astype(F32)).astype(BF16)])

        zhy, qd, kdt, vdx, qg, kgt, vgx = _inproj(h, sc_m, sh_m, g_m, w_in_bf, qn, kn, bd128, rope_tabs)
        czhy, cqd, ckdt, cvdx, cqg, ckgt, cvgx = _inproj(hc, csc_m, csh_m, g_m, w_in_bf, qn, kn, bd128, None)

        y_hy = _hyena(zhy, hy_p, hy_tabs)
        o_d = _diff_attention(lqk, qd, [kdt, ckdt], [vdx, cvdx], lam_init)
        o_g = _gqa_attention(qg, [kgt, ckgt], [vgx, cvgx])
        if not last:
            yc_hy = _hyena(czhy, hy_p, hy_tabs_c)
            oc_d = _diff_attention(lqk, cqd, [ckdt], [cvdx], lam_init)
            oc_g = _gqa_attention(cqg, [ckgt], [cvgx])

        if moe:
            n_c = 0 if last else b * n_ctx
            total = n_c + b * seq
            filled = None
            if not last:
                filled = _outproj(yc_hy, oc_d, oc_g, hc, cgt_m, w_out_bf, g_f, csc_f, csh_f, subln, bd256,
                                  router_pad, (total, 0, None))
            h_all, u_all, route = _outproj(y_hy, o_d, o_g, h, gt_m, w_out_bf, g_f, sc_f, sh_f, subln, bd256,
                                           router_pad, (total, n_c, filled))
            wg, wu, wd = (w[i // 2].astype(BF16) for w in (moe_wg, moe_wu, moe_wd))
            gt_tab = jnp.concatenate([gt_f, cgt_f[:1]], axis=0)
            tile_vec = lambda t, tm: jnp.where(t < n_c // tm, b, (t - n_c // tm) // (seq // tm))
            parts = _moe(u_all, route, wg, wu, wd, h_all, gt_tab, tile_vec,
                         [(n_c, b * seq)] + ([] if last else [(0, n_c)]))
            h = parts[0].reshape(b, seq, d)
            if not last:
                hc = parts[1].reshape(b, n_ctx, d)
        else:
            wg, wu, wd = (w[i // 2].astype(BF16) for w in (ffn_wg, ffn_wu, ffn_wd))
            mixed = _outproj(y_hy, o_d, o_g, h, gt_m, w_out_bf, g_f, sc_f, sh_f, subln, bd256)
            h = _ffn(mixed[1], wg, wu, wd, mixed[0], gt_f)
            if not last:
                cmixed = _outproj(yc_hy, oc_d, oc_g, hc, cgt_m, w_out_bf, g_f, csc_f, csh_f, subln, bd256)
                hc = _ffn(cmixed[1], wg, wu, wd, cmixed[0], cgt_f)
    return _final_norm(h, g_final)
```
